```python
import math
import jax
import jax.numpy as jnp
from jax import lax
import numpy as np

D_MODEL = 1024
BATCH = 8
SEQ = 4096
DEPTH = 4
DEC_BATCH = 1
DEC_SEQ = 16384
PAST_LEN = 128

CHUNK = 128
SHORT_CONV = 3
ML_HEADS = 4
ML_HEAD_DIM = D_MODEL // 8
ML_WIDTH = ML_HEADS * ML_HEAD_DIM
RT_HEADS = 4
RT_HEAD_DIM = D_MODEL // 8
RT_WIDTH = RT_HEADS * RT_HEAD_DIM
HY_WIDTH = D_MODEL // 2
HY_ORDER = 2
HY_BANDS = 16
HY_POS_DIM = 1 + 2 * HY_BANDS
HY_FILTER_HIDDEN = 64
D_MIX = ML_WIDTH + RT_WIDTH + HY_WIDTH
IN_SIZES = (2 * ML_WIDTH, ML_WIDTH, ML_WIDTH, ML_WIDTH, 4 * ML_HEADS,
            3 * RT_WIDTH, RT_WIDTH, 3 * HY_WIDTH, HY_WIDTH)
IN_COLS = sum(IN_SIZES)
RT_LOG_GAMMA_FWD = tuple(math.log(1.0 - 2.0 ** (-5.0 - h)) for h in range(RT_HEADS))
RT_LOG_GAMMA_BWD = tuple(math.log(1.0 - 2.0 ** (-5.5 - h)) for h in range(RT_HEADS))
ROPE_BASE = 10000.0
RMS_EPS = 1e-6
HEAD_NORM_EPS = 1e-5
M_INIT = -1e30
HY_FAST_DECAY = math.log(1e-2) / 0.3
HY_SLOW_DECAY = math.log(1e-2) / 1.5

kernel_name = 'hybrid_mlstm_retention_hyena_encoder'


def rmsnorm(x, g):
    xf = x.astype(jnp.float32)
    y = xf * lax.rsqrt(jnp.mean(xf * xf, axis=-1, keepdims=True) + RMS_EPS) * g.astype(jnp.float32)
    return y.astype(x.dtype)


def head_norm(h, g):
    b, l = h.shape[0], h.shape[1]
    mu = jnp.mean(h, axis=-1, keepdims=True)
    hc = h - mu
    var = jnp.mean(hc * hc, axis=-1, keepdims=True)
    return (hc * lax.rsqrt(var + HEAD_NORM_EPS)).reshape(b, l, -1) * g.astype(jnp.float32)


def centred_conv3(u, w, bias):
    up = jnp.pad(u, ((0, 0), (1, 1), (0, 0)))
    return up[:, :-2] * w[0] + up[:, 1:-1] * w[1] + up[:, 2:] * w[2] + bias


def _to_heads(t, n_heads):
    b, l, _ = t.shape
    return t.reshape(b, l, n_heads, -1).transpose(0, 2, 1, 3)


def _flip(t):
    return jnp.flip(t, axis=2)


def rotary(x):
    l, dh = x.shape[1], x.shape[3]
    inv = ROPE_BASE ** (-jnp.arange(0, dh, 2, dtype=jnp.float32) / dh)
    ang = jnp.arange(l, dtype=jnp.float32)[:, None] * inv[None, :]
    cos = jnp.cos(ang)[None, :, None, :]
    sin = jnp.sin(ang)[None, :, None, :]
    x1, x2 = x[..., : dh // 2], x[..., dh // 2:]
    return jnp.concatenate([x1 * cos - x2 * sin, x1 * sin + x2 * cos], axis=-1)


def mlstm_chunkwise(q, k, v, log_i, log_f):
    b, h, l, dh = q.shape
    n = l // CHUNK
    qc = q.reshape(b, h, n, CHUNK, dh)
    kc = k.reshape(b, h, n, CHUNK, dh)
    vc = v.reshape(b, h, n, CHUNK, dh)
    lic = log_i.reshape(b, h, n, CHUNK)
    cum_f = jnp.cumsum(log_f.reshape(b, h, n, CHUNK), axis=-1)
    causal = jnp.tril(jnp.ones((CHUNK, CHUNK), dtype=bool))
    d_intra = jnp.where(causal, cum_f[..., :, None] - cum_f[..., None, :] + lic[..., None, :], -jnp.inf)
    m_intra = jnp.max(d_intra, axis=-1)
    f_total = cum_f[..., -1]
    g = f_total[..., None] - cum_f + lic
    m_chunk = jnp.max(g, axis=-1)
    kw = kc * jnp.exp(g - m_chunk[..., None])[..., None]
    kv_chunk = jnp.einsum('bhncd,bhnce->bhnde', kw, vc)
    k_chunk = jnp.sum(kw, axis=-2)

    def step(carry, xs):
        c_prev, n_prev, m_prev = carry
        kv_n, k_n, m_n, ft_n = xs
        m_new = jnp.maximum(ft_n + m_prev, m_n)
        a = jnp.exp(ft_n + m_prev - m_new)
        s = jnp.exp(m_n - m_new)
        c_new = a[..., None, None] * c_prev + s[..., None, None] * kv_n
        n_new = a[..., None] * n_prev + s[..., None] * k_n
        return (c_new, n_new, m_new), (c_prev, n_prev, m_prev)

    init = (jnp.zeros((b, h, dh, dh), jnp.float32), jnp.zeros((b, h, dh), jnp.float32),
            jnp.full((b, h), M_INIT, jnp.float32))
    xs = (jnp.moveaxis(kv_chunk, 2, 0), jnp.moveaxis(k_chunk, 2, 0),
          jnp.moveaxis(m_chunk, 2, 0), jnp.moveaxis(f_total, 2, 0))
    _, (c_st, n_st, m_st) = lax.scan(step, init, xs)
    c_st = jnp.moveaxis(c_st, 0, 2)
    n_st = jnp.moveaxis(n_st, 0, 2)
    m_st = jnp.moveaxis(m_st, 0, 2)
    m_inter = cum_f + m_st[..., None]
    m_tot = jnp.maximum(m_inter, m_intra)
    w_inter = jnp.exp(m_inter - m_tot)
    s = jnp.einsum('bhnid,bhnjd->bhnij', qc, kc) * jnp.exp(d_intra - m_tot[..., None])
    num = jnp.einsum('bhnij,bhnjd->bhnid', s, vc) + w_inter[..., None] * jnp.einsum('bhnid,bhnde->bhnie', qc, c_st)
    den = jnp.sum(s, axis=-1) + w_inter * jnp.einsum('bhnid,bhnd->bhni', qc, n_st)
    out = num / jnp.maximum(jnp.abs(den), jnp.exp(-m_tot))[..., None]
    return out.reshape(b, h, l, dh)


def retention_chunkwise(q, k, v, log_gamma):
    b, h, l, dh = q.shape
    n = l // CHUNK
    qc = q.reshape(b, h, n, CHUNK, dh)
    kc = k.reshape(b, h, n, CHUNK, dh)
    vc = v.reshape(b, h, n, CHUNK, dh)
    pos = jnp.arange(CHUNK, dtype=jnp.float32)
    rel = pos[:, None] - pos[None, :]
    lg = log_gamma[:, None, None]
    decay = jnp.where(rel >= 0, jnp.exp(jnp.where(rel >= 0, rel, 0.0) * lg), 0.0)
    scores = jnp.einsum('bhnid,bhnjd->bhnij', qc, kc) * decay[None, :, None]
    out = jnp.einsum('bhnij,bhnje->bhnie', scores, vc)
    k_w = kc * jnp.exp((CHUNK - 1 - pos)[None, :] * log_gamma[:, None])[None, :, None, :, None]
    kv_chunk = jnp.einsum('bhncd,bhnce->bhnde', k_w, vc)
    chunk_decay = jnp.exp(CHUNK * log_gamma)[None, :, None, None]

    def step(r_prev, kv_n):
        return chunk_decay * r_prev + kv_n, r_prev

    _, r_st = lax.scan(step, jnp.zeros((b, h, dh, dh), jnp.float32), jnp.moveaxis(kv_chunk, 2, 0))
    r_st = jnp.moveaxis(r_st, 0, 2)
    q_w = qc * jnp.exp((pos + 1.0)[None, :] * log_gamma[:, None])[None, :, None, :, None]
    out = out + jnp.einsum('bhnid,bhnde->bhnie', q_w, r_st)
    return out.reshape(b, h, l, dh)


def hyena_filter_spectrum(l, w1, b1, w2, b2, w3, freq, deltas):
    f32 = jnp.float32
    t = jnp.linspace(0.0, 1.0, l, dtype=f32)[:, None]
    w = (2.0 * math.pi / l) * jnp.arange(l, dtype=f32)[:, None]
    bands = jnp.linspace(1e-4, HY_BANDS - 1, HY_BANDS, dtype=f32)[None, :]
    feats = jnp.concatenate([t, jnp.cos(bands * w), -jnp.sin(bands * w)], axis=-1)
    fr = freq.astype(f32)
    hid = jnp.sin(fr * (feats @ w1.astype(f32) + b1.astype(f32)))
    hid = jnp.sin(fr * (hid @ w2.astype(f32) + b2.astype(f32)))
    filt = (hid @ w3.astype(f32)).reshape(l, HY_ORDER, 2, HY_WIDTH)
    filt = filt * jnp.exp(-t[:, :, None, None] * jnp.abs(deltas.astype(f32)))
    fwd, bwd = filt[:, :, 0], filt[:, :, 1]
    two_sided = jnp.concatenate([fwd[:1] + bwd[:1], fwd[1:], jnp.zeros_like(fwd[:1]),
                                 jnp.flip(bwd[1:], axis=0)], axis=0)
    two_sided = two_sided / jnp.sum(jnp.abs(two_sided), axis=0, keepdims=True)
    return jnp.fft.rfft(two_sided, axis=0)


def long_conv(u, spec, skip):
    l = u.shape[1]
    y = jnp.fft.irfft(jnp.fft.rfft(u, n=2 * l, axis=1) * spec[None], n=2 * l, axis=1)[:, :l]
    return y + u * skip


def mlstm_branch(ml_qk, ml_v, ml_o, ml_z, ml_gates, conv_w, conv_b, gate_b, norm_g):
    b, l, _ = ml_v.shape
    f32 = jnp.float32
    qk = jax.nn.silu(centred_conv3(ml_qk.astype(f32), conv_w.astype(f32), conv_b.astype(f32)))
    q = _to_heads(qk[..., :ML_WIDTH], ML_HEADS)
    k = _to_heads(qk[..., ML_WIDTH:], ML_HEADS) * (ML_HEAD_DIM ** -0.5)
    v = _to_heads(ml_v.astype(f32), ML_HEADS)
    pre = (ml_gates.astype(f32).reshape(b, l, 4, ML_HEADS) + gate_b.astype(f32)).transpose(2, 0, 3, 1)
    h_fwd = mlstm_chunkwise(q, k, v, pre[0], jax.nn.log_sigmoid(pre[1]))
    h_bwd = _flip(mlstm_chunkwise(_flip(q), _flip(k), _flip(v), _flip(pre[2]),
                                  _flip(jax.nn.log_sigmoid(pre[3]))))
    h = (h_fwd + h_bwd).transpose(0, 2, 1, 3)
    h = jax.nn.sigmoid(ml_o.astype(f32)).reshape(b, l, ML_HEADS, ML_HEAD_DIM) * h
    return head_norm(h, norm_g) * jax.nn.silu(ml_z.astype(f32))


def retention_branch(rt_qkv, rt_z, norm_g):
    b, l, _ = rt_z.shape
    f32 = jnp.float32
    qkv = rt_qkv.astype(f32)
    q = rotary(qkv[..., :RT_WIDTH].reshape(b, l, RT_HEADS, RT_HEAD_DIM)) * (RT_HEAD_DIM ** -0.5)
    k = rotary(qkv[..., RT_WIDTH:2 * RT_WIDTH].reshape(b, l, RT_HEADS, RT_HEAD_DIM))
    q = q.transpose(0, 2, 1, 3)
    k = k.transpose(0, 2, 1, 3)
    v = _to_heads(qkv[..., 2 * RT_WIDTH:], RT_HEADS)
    lg_f = jnp.asarray(RT_LOG_GAMMA_FWD, dtype=f32)
    lg_b = jnp.asarray(RT_LOG_GAMMA_BWD, dtype=f32)
    o = retention_chunkwise(q, k, v, lg_f) + _flip(retention_chunkwise(_flip(q), _flip(k), _flip(v), lg_b))
    return head_norm(o.transpose(0, 2, 1, 3), norm_g) * jax.nn.silu(rt_z.astype(f32))


def hyena_branch(hy_in, hy_z, conv_w, conv_b, spec, skip):
    f32 = jnp.float32
    u = centred_conv3(hy_in.astype(f32), conv_w.astype(f32), conv_b.astype(f32))
    v, x1, x2 = u[..., :HY_WIDTH], u[..., HY_WIDTH:2 * HY_WIDTH], u[..., 2 * HY_WIDTH:]
    sk = skip.astype(f32)
    z = x1 * long_conv(v, spec[:, 0], sk[0])
    z = x2 * long_conv(z, spec[:, 1], sk[1])
    return z * jax.nn.silu(hy_z.astype(f32))


def encoder_trunk(x, norm_g, w_in, ml_conv_w, ml_conv_b, ml_gate_b, ml_norm_g, rt_norm_g,
                  hy_conv_w, hy_conv_b, hy_w1, hy_b1, hy_w2, hy_b2, hy_w3, hy_freq, hy_deltas,
                  hy_skip, w_out, final_g):
    l = x.shape[1]
    split_points = np.cumsum(IN_SIZES)[:-1].tolist()
    for layer in range(DEPTH):
        h = rmsnorm(x, norm_g[layer])
        u = jnp.einsum('bld,de->ble', h, w_in[layer])
        ml_qk, ml_v, ml_o, ml_z, ml_gates, rt_qkv, rt_z, hy_in, hy_z = jnp.split(u, split_points, axis=-1)
        y_ml = mlstm_branch(ml_qk, ml_v, ml_o, ml_z, ml_gates, ml_conv_w[layer], ml_conv_b[layer],
                            ml_gate_b[layer], ml_norm_g[layer])
        y_rt = retention_branch(rt_qkv, rt_z, rt_norm_g[layer])
        spec = hyena_filter_spectrum(l, hy_w1[layer], hy_b1[layer], hy_w2[layer], hy_b2[layer],
                                     hy_w3[layer], hy_freq[layer], hy_deltas[layer])
        y_hy = hyena_branch(hy_in, hy_z, hy_conv_w[layer], hy_conv_b[layer], spec, hy_skip[layer])
        mixed = jnp.concatenate([y_ml, y_rt, y_hy], axis=-1).astype(x.dtype)
        x = x + jnp.einsum('ble,ed->bld', mixed, w_out[layer])
    return rmsnorm(x, final_g)


def setup_inputs(seed: int = 0) -> dict:
    key = jax.random.key(seed)
    ks = jax.random.split(key, 24)
    f32 = jnp.float32

    def nrm(k, shape, scale):
        return scale * jax.random.normal(k, shape, f32)

    x_prompt = nrm(ks[0], (BATCH, SEQ, D_MODEL), 1.0)
    x_sample = nrm(ks[1], (DEC_BATCH, DEC_SEQ, D_MODEL), 1.0)
    norm_g = 1.0 + nrm(ks[2], (DEPTH, D_MODEL), 0.01)
    w_in = nrm(ks[3], (DEPTH, D_MODEL, IN_COLS), D_MODEL ** -0.5)
    ml_conv_w = nrm(ks[4], (DEPTH, SHORT_CONV, 2 * ML_WIDTH), SHORT_CONV ** -0.5)
    ml_conv_b = nrm(ks[5], (DEPTH, 2 * ML_WIDTH), 0.01)
    f_bias = jnp.linspace(3.0, 6.0, ML_HEADS, dtype=f32)
    i_b = nrm(ks[6], (DEPTH, 2, ML_HEADS), 0.1)
    f_n = nrm(ks[7], (DEPTH, 2, ML_HEADS), 0.01)
    ml_gate_b = jnp.stack([i_b[:, 0], f_bias + f_n[:, 0], i_b[:, 1], f_bias + f_n[:, 1]], axis=1)
    ml_norm_g = 1.0 + nrm(ks[8], (DEPTH, ML_WIDTH), 0.01)
    rt_norm_g = 1.0 + nrm(ks[9], (DEPTH, RT_WIDTH), 0.01)
    hy_conv_w = nrm(ks[10], (DEPTH, SHORT_CONV, 3 * HY_WIDTH), SHORT_CONV ** -0.5)
    hy_conv_b = nrm(ks[11], (DEPTH, 3 * HY_WIDTH), 0.01)
    hy_w1 = nrm(ks[12], (DEPTH, HY_POS_DIM, HY_FILTER_HIDDEN), HY_POS_DIM ** -0.5)
    hy_b1 = nrm(ks[13], (DEPTH, HY_FILTER_HIDDEN), 0.01)
    hy_w2 = nrm(ks[14], (DEPTH, HY_FILTER_HIDDEN, HY_FILTER_HIDDEN), HY_FILTER_HIDDEN ** -0.5)
    hy_b2 = nrm(ks[15], (DEPTH, HY_FILTER_HIDDEN), 0.01)
    hy_w3 = nrm(ks[16], (DEPTH, HY_FILTER_HIDDEN, HY_ORDER * 2 * HY_WIDTH), HY_FILTER_HIDDEN ** -0.5)
    hy_freq = 1.0 + nrm(ks[17], (DEPTH, HY_FILTER_HIDDEN), 0.01)
    base_decay = jnp.abs(jnp.linspace(HY_FAST_DECAY, HY_SLOW_DECAY, HY_WIDTH, dtype=f32))
    hy_deltas = base_decay * (1.0 + nrm(ks[18], (DEPTH, HY_ORDER, 2, HY_WIDTH), 0.05))
    hy_skip = nrm(ks[19], (DEPTH, HY_ORDER, HY_WIDTH), 0.1)
    w_out = nrm(ks[20], (DEPTH, D_MIX, D_MODEL), D_MIX ** -0.5)
    final_g = 1.0 + nrm(ks[21], (D_MODEL,), 0.01)
    return {'x_prompt': x_prompt, 'x_sample': x_sample, 'norm_g': norm_g, 'w_in': w_in,
            'ml_conv_w': ml_conv_w, 'ml_conv_b': ml_conv_b, 'ml_gate_b': ml_gate_b,
            'ml_norm_g': ml_norm_g, 'rt_norm_g': rt_norm_g, 'hy_conv_w': hy_conv_w,
            'hy_conv_b': hy_conv_b, 'hy_w1': hy_w1, 'hy_b1': hy_b1, 'hy_w2': hy_w2, 'hy_b2': hy_b2,
            'hy_w3': hy_w3, 'hy_freq': hy_freq, 'hy_deltas': hy_deltas, 'hy_skip': hy_skip,
            'w_out': w_out, 'final_g': final_g}


def reference(x_prompt, x_sample, norm_g, w_in, ml_conv_w, ml_conv_b, ml_gate_b, ml_norm_g,
              rt_norm_g, hy_conv_w, hy_conv_b, hy_w1, hy_b1, hy_w2, hy_b2, hy_w3, hy_freq,
              hy_deltas, hy_skip, w_out, final_g):
    y_prompt = encoder_trunk(x_prompt, norm_g, w_in, ml_conv_w, ml_conv_b, ml_gate_b, ml_norm_g,
                             rt_norm_g, hy_conv_w, hy_conv_b, hy_w1, hy_b1, hy_w2, hy_b2, hy_w3,
                             hy_freq, hy_deltas, hy_skip, w_out, final_g)
    y_sample = encoder_trunk(x_sample, norm_g, w_in, ml_conv_w, ml_conv_b, ml_gate_b, ml_norm_g,
                             rt_norm_g, hy_conv_w, hy_conv_b, hy_w1, hy_b1, hy_w2, hy_b2, hy_w3,
                             hy_freq, hy_deltas, hy_skip, w_out, final_g)
    return (y_prompt, y_sample)
```

```python
import functools
import math

import numpy as np
import jax
import jax.numpy as jnp
from jax import lax
from jax.experimental import pallas as pl
from jax.experimental.pallas import tpu as pltpu

F32 = jnp.float32
BF16 = jnp.bfloat16

D_MODEL = 1024
DEPTH = 4
CHUNK = 128
HEADS = 4
HEAD_DIM = 128
WIDTH = 512
HY_ORDER = 2
HY_BANDS = 16
HY_HIDDEN = 64
FEAT_ROWS = 40
ROPE_BASE = 10000.0
RMS_EPS = 1e-6
HEAD_NORM_EPS = 1e-5
M_INIT = -1e30
RT_LOG_GAMMA_FWD = tuple(math.log(1.0 - 2.0 ** (-5.0 - h)) for h in range(HEADS))
RT_LOG_GAMMA_BWD = tuple(math.log(1.0 - 2.0 ** (-5.5 - h)) for h in range(HEADS))

LANES = 128
SUBLANES = 8
TM_IN = 256
TM_OUT = 512
TB_SEQ = 1024
TG = 1024
CB_HY = 8
LT_FILT = 2048
ML_COLS = 5 * WIDTH
RT_COLS = 4 * WIDTH
HY_ROWS = 4 * WIDTH
HIGHEST = lax.Precision.HIGHEST


def _dot(a, b):
    return jnp.dot(a, b, preferred_element_type=F32)


def _dot_nt(a, b):
    return lax.dot_general(a, b, (((1,), (1,)), ((), ())), preferred_element_type=F32)


def _dot_tn(a, b):
    return lax.dot_general(a, b, (((0,), (0,)), ((), ())), preferred_element_type=F32)


def _dot_f32(a, b):
    return jnp.dot(a, b, precision=HIGHEST, preferred_element_type=F32)


def _silu(y):
    return y * jax.nn.sigmoid(y)


def _log_sigmoid(x):
    return -(jnp.maximum(-x, 0.0) + jnp.log1p(jnp.exp(-jnp.abs(x))))


def _head_norm(h, gain):
    mu = jnp.mean(h, axis=-1, keepdims=True)
    hc = h - mu
    var = jnp.mean(hc * hc, axis=-1, keepdims=True)
    return hc * lax.rsqrt(var + HEAD_NORM_EPS) * gain


def _in_kernel(x_ref, g_ref, wtok_ref, wgt_ref, whyt_ref,
               oml_ref, og_ref, ort_ref, ogt_ref, ohyt_ref):
    x = x_ref[...]
    ms = jnp.mean(x * x, axis=-1, keepdims=True)
    h = (x * lax.rsqrt(ms + RMS_EPS) * g_ref[...]).astype(BF16)
    oml_ref[...] = _dot(h, wtok_ref[:, 0:ML_COLS])
    og_ref[...] = _dot(h, wtok_ref[:, ML_COLS:ML_COLS + LANES])
    ort_ref[...] = _dot(h, wtok_ref[:, ML_COLS + LANES:])
    ogt_ref[...] = _dot_nt(wgt_ref[...], h)
    ohyt_ref[...] = _dot_nt(whyt_ref[...], h)


def _in_proj(x, gain, w_tok, w_gt, w_hyt):
    t = x.shape[0]
    ncol = w_tok.shape[1]
    return pl.pallas_call(
        _in_kernel,
        grid=(t // TM_IN,),
        in_specs=[
            pl.BlockSpec((TM_IN, D_MODEL), lambda i: (i, 0)),
            pl.BlockSpec((1, D_MODEL), lambda i: (0, 0)),
            pl.BlockSpec((D_MODEL, ncol), lambda i: (0, 0)),
            pl.BlockSpec((4 * HEADS, D_MODEL), lambda i: (0, 0)),
            pl.BlockSpec((HY_ROWS, D_MODEL), lambda i: (0, 0)),
        ],
        out_specs=[
            pl.BlockSpec((TM_IN, ML_COLS), lambda i: (i, 0)),
            pl.BlockSpec((TM_IN, LANES), lambda i: (i, 0)),
            pl.BlockSpec((TM_IN, RT_COLS), lambda i: (i, 0)),
            pl.BlockSpec((4 * HEADS, TM_IN), lambda i: (0, i)),
            pl.BlockSpec((HY_ROWS, TM_IN), lambda i: (0, i)),
        ],
        out_shape=[
            jax.ShapeDtypeStruct((t, ML_COLS), F32),
            jax.ShapeDtypeStruct((t, LANES), F32),
            jax.ShapeDtypeStruct((t, RT_COLS), F32),
            jax.ShapeDtypeStruct((4 * HEADS, t), F32),
            jax.ShapeDtypeStruct((HY_ROWS, t), F32),
        ],
        name="in_proj",
    )(x, gain, w_tok, w_gt, w_hyt)


def _gates_kernel(g_ref, gt_ref, b_ref, bt_ref, gc_ref, gr_ref):
    row = lax.broadcasted_iota(jnp.int32, (CHUNK, CHUNK), 0)
    col = lax.broadcasted_iota(jnp.int32, (CHUNK, CHUNK), 1)
    lower = (row >= col).astype(F32)
    upper = (row <= col).astype(F32)
    gc_ref[...] = jnp.zeros_like(gc_ref)
    gr_ref[...] = jnp.zeros_like(gr_ref)
    for c in range(TG // CHUNK):
        sl = slice(c * CHUNK, (c + 1) * CHUNK)
        pre = g_ref[sl, :] + b_ref[...]
        lf = _log_sigmoid(pre)
        cum_f = _dot_f32(lower, lf)
        cum_b = _dot_f32(upper, lf)
        pre_t = gt_ref[:, sl] + bt_ref[...]
        lf_t = _log_sigmoid(pre_t)
        cum_f_t = _dot_f32(lf_t, upper)
        cum_b_t = _dot_f32(lf_t, lower)
        for h in range(HEADS):
            gc_ref[h, sl, 0:1] = pre[:, h:h + 1]
            gc_ref[h, sl, 1:2] = cum_f[:, HEADS + h:HEADS + h + 1]
            gc_ref[h, sl, 2:3] = pre[:, 2 * HEADS + h:2 * HEADS + h + 1]
            gc_ref[h, sl, 3:4] = cum_b[:, 3 * HEADS + h:3 * HEADS + h + 1]
            gr_ref[h, 0:1, sl] = pre_t[h:h + 1, :]
            gr_ref[h, 1:2, sl] = cum_f_t[HEADS + h:HEADS + h + 1, :]
            gr_ref[h, 2:3, sl] = pre_t[2 * HEADS + h:2 * HEADS + h + 1, :]
            gr_ref[h, 3:4, sl] = cum_b_t[3 * HEADS + h:3 * HEADS + h + 1, :]


def _gates(g, gt, bias_row, bias_col):
    t = g.shape[0]
    return pl.pallas_call(
        _gates_kernel,
        grid=(t // TG,),
        in_specs=[
            pl.BlockSpec((TG, LANES), lambda i: (i, 0)),
            pl.BlockSpec((4 * HEADS, TG), lambda i: (0, i)),
            pl.BlockSpec((1, LANES), lambda i: (0, 0)),
            pl.BlockSpec((4 * HEADS, 1), lambda i: (0, 0)),
        ],
        out_specs=[
            pl.BlockSpec((HEADS, TG, SUBLANES), lambda i: (0, i, 0)),
            pl.BlockSpec((HEADS, SUBLANES, TG), lambda i: (0, 0, i)),
        ],
        out_shape=[
            jax.ShapeDtypeStruct((HEADS, t, SUBLANES), F32),
            jax.ShapeDtypeStruct((HEADS, SUBLANES, t), F32),
        ],
        name="gates",
    )(g, gt, bias_row, bias_col)


def _seq_block(n, d, nb):
    return n + d * (nb - 1 - 2 * n)


def _ml_kernel(nb, q_ref, k_ref, v_ref, o_ref, z_ref, qp_ref, qn_ref, kp_ref, kn_ref,
               gc_ref, gr_ref, cwq_ref, cbq_ref, cwk_ref, cbk_ref, ng_ref,
               out_ref, hf_ref, st_ref, m_ref, stg_ref, qs_ref, ks_ref):
    tb = TB_SEQ
    d = pl.program_id(2)
    n = pl.program_id(3)
    sb = _seq_block(n, d, nb)

    @pl.when(n == 0)
    def _():
        st_ref[...] = jnp.zeros_like(st_ref)
        m_ref[...] = jnp.full_like(m_ref, M_INIT)

    def conv_silu(raw_ref, prev_ref, next_ref, cw_ref, cb_ref):
        stg_ref[SUBLANES:SUBLANES + tb, :] = raw_ref[...]
        stg_ref[SUBLANES - 1:SUBLANES, :] = jnp.where(sb > 0, prev_ref[SUBLANES - 1:SUBLANES, :], 0.0)
        stg_ref[SUBLANES + tb:SUBLANES + tb + 1, :] = jnp.where(sb < nb - 1, next_ref[0:1, :], 0.0)
        w = cw_ref[...]
        y = (stg_ref[SUBLANES - 1:SUBLANES - 1 + tb, :] * w[0:1, :]
             + stg_ref[SUBLANES:SUBLANES + tb, :] * w[1:2, :]
             + stg_ref[SUBLANES + 1:SUBLANES + 1 + tb, :] * w[2:3, :] + cb_ref[...])
        return _silu(y)

    qs_ref[...] = conv_silu(q_ref, qp_ref, qn_ref, cwq_ref, cbq_ref).astype(BF16)
    ks_ref[...] = conv_silu(k_ref, kp_ref, kn_ref, cwk_ref, cbk_ref) * (HEAD_DIM ** -0.5)

    row = lax.broadcasted_iota(jnp.int32, (CHUNK, CHUNK), 0)
    col = lax.broadcasted_iota(jnp.int32, (CHUNK, CHUNK), 1)
    unit = (col == 0).astype(BF16)

    def chunk(c, rev):
        sl = pl.ds(pl.multiple_of(c * CHUNK, CHUNK), CHUNK)
        a = 2 if rev else 0
        q = qs_ref[sl, :]
        k = ks_ref[sl, :]
        v2 = jnp.concatenate([v_ref[sl, :].astype(BF16), unit], axis=1)
        gc = gc_ref[sl, :]
        gr = gr_ref[:, sl]
        li_c, cum_c = gc[:, a:a + 1], gc[:, a + 1:a + 2]
        li_r, cum_r = gr[a:a + 1, :], gr[a + 1:a + 2, :]
        f_tot = cum_c[0:1, :] if rev else cum_c[CHUNK - 1:CHUNK, :]
        dm = cum_c - cum_r + li_r
        dm = jnp.where((row <= col) if rev else (row >= col), dm, -jnp.inf)
        m_intra = jnp.max(dm, axis=1, keepdims=True)
        m_prev = m_ref[0:1, 0:1]
        m_inter = cum_c + m_prev
        m_tot = jnp.maximum(m_inter, m_intra)
        w_inter = jnp.exp(m_inter - m_tot)
        s = _dot_nt(q, k.astype(BF16)) * jnp.exp(dm - m_tot)
        num2 = _dot(s.astype(BF16), v2) + w_inter * _dot(q, st_ref[...].astype(BF16))
        num = num2[:, 0:HEAD_DIM]
        den = num2[:, HEAD_DIM:HEAD_DIM + 1]
        out = num / jnp.maximum(jnp.abs(den), jnp.exp(-m_tot))
        g = f_tot - cum_c + li_c
        m_chunk = jnp.max(g, axis=0, keepdims=True)
        kw = (k * jnp.exp(g - m_chunk)).astype(BF16)
        kv2 = _dot_tn(kw, v2)
        m_new = jnp.maximum(f_tot + m_prev, m_chunk)
        st_ref[...] = jnp.exp(f_tot + m_prev - m_new) * st_ref[...] + jnp.exp(m_chunk - m_new) * kv2
        m_ref[...] = jnp.broadcast_to(m_new, m_ref.shape)
        return out

    def run(rev):
        def body(i, carry):
            c = (tb // CHUNK - 1 - i) if rev else i
            out = chunk(c, rev)
            pos = pl.ds(pl.multiple_of(sb * tb + c * CHUNK, CHUNK), CHUNK)
            if not rev:
                hf_ref[pos, :] = out
            else:
                sl = pl.ds(pl.multiple_of(c * CHUNK, CHUNK), CHUNK)
                hh = jax.nn.sigmoid(o_ref[sl, :]) * (out + hf_ref[pos, :])
                out_ref[sl, :] = _head_norm(hh, ng_ref[...]) * _silu(z_ref[sl, :])
            return carry
        lax.fori_loop(0, tb // CHUNK, body, 0)

    pl.when(d == 0)(lambda: run(False))
    pl.when(d == 1)(lambda: run(True))


def _mlstm(u_ml, gc, gr, conv_w, conv_b, norm_g, row0, bsz, seq):
    t = u_ml.shape[0]
    tb = TB_SEQ
    nb = seq // tb
    blk0 = row0 // tb
    t8 = tb // SUBLANES

    def rb(b, d, n):
        return blk0 + b * nb + _seq_block(n, d, nb)

    def rb_out(b, d, n):
        return blk0 + b * nb + nb - 1 - n * d

    def col(g):
        return lambda b, h, d, n: (rb(b, d, n), g * HEADS + h)

    def col_late(g):
        return lambda b, h, d, n: (rb_out(b, d, n), g * HEADS + h)

    def halo_prev(g):
        return lambda b, h, d, n: (jnp.maximum(rb(b, d, n) * t8 - 1, 0), g * HEADS + h)

    def halo_next(g):
        return lambda b, h, d, n: (jnp.minimum((rb(b, d, n) + 1) * t8, t // SUBLANES - 1), g * HEADS + h)

    blk = (tb, HEAD_DIM)
    halo = (SUBLANES, HEAD_DIM)
    return pl.pallas_call(
        functools.partial(_ml_kernel, nb),
        grid=(bsz, HEADS, 2, nb),
        in_specs=[
            pl.BlockSpec(blk, col(0)), pl.BlockSpec(blk, col(1)), pl.BlockSpec(blk, col(2)),
            pl.BlockSpec(blk, col_late(3)), pl.BlockSpec(blk, col_late(4)),
            pl.BlockSpec(halo, halo_prev(0)), pl.BlockSpec(halo, halo_next(0)),
            pl.BlockSpec(halo, halo_prev(1)), pl.BlockSpec(halo, halo_next(1)),
            pl.BlockSpec((None, tb, SUBLANES), lambda b, h, d, n: (h, rb(b, d, n), 0)),
            pl.BlockSpec((None, SUBLANES, tb), lambda b, h, d, n: (h, 0, rb(b, d, n))),
            pl.BlockSpec((3, HEAD_DIM), lambda b, h, d, n: (0, h)),
            pl.BlockSpec((1, HEAD_DIM), lambda b, h, d, n: (0, h)),
            pl.BlockSpec((3, HEAD_DIM), lambda b, h, d, n: (0, HEADS + h)),
            pl.BlockSpec((1, HEAD_DIM), lambda b, h, d, n: (0, HEADS + h)),
            pl.BlockSpec((1, HEAD_DIM), lambda b, h, d, n: (0, h)),
        ],
        out_specs=pl.BlockSpec(blk, lambda b, h, d, n: (rb_out(b, d, n) - blk0, h)),
        out_shape=jax.ShapeDtypeStruct((bsz * seq, WIDTH), F32),
        scratch_shapes=[
            pltpu.VMEM((seq, HEAD_DIM), F32),
            pltpu.VMEM((HEAD_DIM, 2 * HEAD_DIM), F32),
            pltpu.VMEM((SUBLANES, LANES), F32),
            pltpu.VMEM((tb + 2 * SUBLANES, HEAD_DIM), F32),
            pltpu.VMEM((tb, HEAD_DIM), BF16),
            pltpu.VMEM((tb, HEAD_DIM), F32),
        ],
        name="mlstm",
    )(u_ml, u_ml, u_ml, u_ml, u_ml, u_ml, u_ml, u_ml, u_ml, gc, gr,
      conv_w, conv_b, conv_w, conv_b, norm_g)


def _rt_kernel(nb, q_ref, k_ref, v_ref, z_ref, cc_ref, ss_ref, dec_ref, dv_ref, ng_ref,
               out_ref, hf_ref, st_ref, qs_ref, ks_ref):
    tb = TB_SEQ
    d = pl.program_id(2)
    n = pl.program_id(3)
    sb = _seq_block(n, d, nb)

    @pl.when(n == 0)
    def _():
        st_ref[...] = jnp.zeros_like(st_ref)

    cc = cc_ref[...]
    ss = ss_ref[...]
    qs_ref[...] = (q_ref[...] * cc + pltpu.roll(q_ref[...], HEAD_DIM // 2, 1) * ss) * (HEAD_DIM ** -0.5)
    ks_ref[...] = k_ref[...] * cc + pltpu.roll(k_ref[...], HEAD_DIM // 2, 1) * ss
    dec = dec_ref[...]
    dv = dv_ref[...]
    q_dec, k_dec, chunk_dec = dv[:, 0:1], dv[:, 1:2], dv[0:1, 2:3]

    def chunk(c):
        sl = pl.ds(pl.multiple_of(c * CHUNK, CHUNK), CHUNK)
        q = qs_ref[sl, :]
        k = ks_ref[sl, :]
        vb = v_ref[sl, :].astype(BF16)
        s = _dot_nt(q.astype(BF16), k.astype(BF16)) * dec
        out = _dot(s.astype(BF16), vb) + _dot((q * q_dec).astype(BF16), st_ref[...].astype(BF16))
        st_ref[...] = chunk_dec * st_ref[...] + _dot_tn((k * k_dec).astype(BF16), vb)
        return out

    def run(rev):
        def body(i, carry):
            c = (tb // CHUNK - 1 - i) if rev else i
            out = chunk(c)
            pos = pl.ds(pl.multiple_of(sb * tb + c * CHUNK, CHUNK), CHUNK)
            if not rev:
                hf_ref[pos, :] = out
            else:
                sl = pl.ds(pl.multiple_of(c * CHUNK, CHUNK), CHUNK)
                out_ref[sl, :] = _head_norm(out + hf_ref[pos, :], ng_ref[...]) * _silu(z_ref[sl, :])
            return carry
        lax.fori_loop(0, tb // CHUNK, body, 0)

    pl.when(d == 0)(lambda: run(False))
    pl.when(d == 1)(lambda: run(True))


def _retention(u_rt, rot_c, rot_s, dec, dvec, norm_g, row0, bsz, seq):
    tb = TB_SEQ
    nb = seq // tb
    blk0 = row0 // tb

    def rb(b, d, n):
        return blk0 + b * nb + _seq_block(n, d, nb)

    def rb_out(b, d, n):
        return blk0 + b * nb + nb - 1 - n * d

    def col(g):
        return lambda b, h, d, n: (rb(b, d, n), g * HEADS + h)

    blk = (tb, HEAD_DIM)
    return pl.pallas_call(
        functools.partial(_rt_kernel, nb),
        grid=(bsz, HEADS, 2, nb),
        in_specs=[
            pl.BlockSpec(blk, col(0)), pl.BlockSpec(blk, col(1)), pl.BlockSpec(blk, col(2)),
            pl.BlockSpec(blk, lambda b, h, d, n: (rb_out(b, d, n), 3 * HEADS + h)),
            pl.BlockSpec(blk, lambda b, h, d, n: (_seq_block(n, d, nb), 0)),
            pl.BlockSpec(blk, lambda b, h, d, n: (_seq_block(n, d, nb), 0)),
            pl.BlockSpec((None, None, CHUNK, CHUNK), lambda b, h, d, n: (d, h, 0, 0)),
            pl.BlockSpec((None, None, CHUNK, SUBLANES), lambda b, h, d, n: (d, h, 0, 0)),
            pl.BlockSpec((1, HEAD_DIM), lambda b, h, d, n: (0, h)),
        ],
        out_specs=pl.BlockSpec(blk, lambda b, h, d, n: (rb_out(b, d, n) - blk0, h)),
        out_shape=jax.ShapeDtypeStruct((bsz * seq, WIDTH), F32),
        scratch_shapes=[
            pltpu.VMEM((seq, HEAD_DIM), F32),
            pltpu.VMEM((HEAD_DIM, HEAD_DIM), F32),
            pltpu.VMEM((tb, HEAD_DIM), F32),
            pltpu.VMEM((tb, HEAD_DIM), F32),
        ],
        name="retention",
    )(u_rt, u_rt, u_rt, u_rt, rot_c, rot_s, dec, dvec, norm_g)


def _rotary_tables(seq):
    inv = ROPE_BASE ** (-jnp.arange(0, HEAD_DIM, 2, dtype=F32) / HEAD_DIM)
    ang = jnp.arange(seq, dtype=F32)[:, None] * inv[None, :]
    cos, sin = jnp.cos(ang), jnp.sin(ang)
    return jnp.concatenate([cos, cos], axis=1), jnp.concatenate([-sin, sin], axis=1)


def _retention_tables():
    pos = np.arange(CHUNK, dtype=np.float64)
    rel = pos[:, None] - pos[None, :]
    dec = np.zeros((2, HEADS, CHUNK, CHUNK))
    dvec = np.zeros((2, HEADS, CHUNK, SUBLANES))
    for h in range(HEADS):
        lf, lb = RT_LOG_GAMMA_FWD[h], RT_LOG_GAMMA_BWD[h]
        dec[0, h] = np.where(rel >= 0, np.exp(np.maximum(rel, 0.0) * lf), 0.0)
        dec[1, h] = np.where(rel <= 0, np.exp(np.maximum(-rel, 0.0) * lb), 0.0)
        dvec[0, h, :, 0] = np.exp((pos + 1.0) * lf)
        dvec[0, h, :, 1] = np.exp((CHUNK - 1.0 - pos) * lf)
        dvec[0, h, :, 2] = np.exp(CHUNK * lf)
        dvec[1, h, :, 0] = np.exp((CHUNK - pos) * lb)
        dvec[1, h, :, 1] = np.exp(pos * lb)
        dvec[1, h, :, 2] = np.exp(CHUNK * lb)
    return jnp.asarray(dec, F32), jnp.asarray(dvec, F32)


def _filt_kernel(ft_ref, fr_ref, f0_ref, w1_ref, b1_ref, w2_ref, b2_ref, fq_ref,
                 w3f_ref, w3b_ref, dlf_ref, dlb_ref, gb_ref, gf_ref, s_ref):
    lt = pl.program_id(3)
    fq = fq_ref[...]

    def hidden(ft):
        h1 = jnp.sin(fq * (_dot_f32(w1_ref[...], ft) + b1_ref[...]))
        return jnp.sin(fq * (_dot_f32(w2_ref[...], h1) + b2_ref[...]))

    ft = ft_ref[...]
    fr = fr_ref[...]
    dl_f = jnp.abs(dlf_ref[...])
    dl_b = jnp.abs(dlb_ref[...])
    fwd = _dot_f32(w3f_ref[...], hidden(ft)) * jnp.exp(-ft[0:1, :] * dl_f)
    bwd = _dot_f32(w3b_ref[...], hidden(fr)) * jnp.exp(-fr[0:1, :] * dl_b)
    f0 = f0_ref[...]
    bwd0 = (_dot_f32(w3b_ref[...], hidden(f0)) * jnp.exp(-f0[0:1, :] * dl_b))[:, 0:1]
    lag = lax.broadcasted_iota(jnp.int32, fwd.shape, 1) + lt * LT_FILT
    fwd = jnp.where(lag == 0, fwd + bwd0, fwd)
    bwd = jnp.where(lag == 0, 0.0, bwd)
    gb_ref[...] = bwd
    gf_ref[...] = fwd
    part = jnp.sum(jnp.abs(fwd) + jnp.abs(bwd), axis=1, keepdims=True)
    lane = lax.broadcasted_iota(jnp.int32, (LANES, LANES), 1)
    s_ref[...] = jnp.where(lane == 0, part, 0.0)


def _filters(seq, w1t, b1, w2t, b2, freq, w3t, deltas):
    t = jnp.linspace(0.0, 1.0, seq, dtype=F32)[None, :]
    w = (2.0 * math.pi / seq) * jnp.arange(seq, dtype=F32)[None, :]
    bands = jnp.linspace(1e-4, HY_BANDS - 1, HY_BANDS, dtype=F32)[:, None]
    feats = jnp.concatenate([t, jnp.cos(bands * w), -jnp.sin(bands * w),
                             jnp.zeros((FEAT_ROWS - 1 - 2 * HY_BANDS, seq), F32)], axis=0)
    feats_rev = jnp.concatenate([feats[:, :1], feats[:, :0:-1]], axis=1)
    feats0 = jnp.broadcast_to(feats[:, :1], (FEAT_ROWS, LANES))
    nlt = seq // LT_FILT
    ct = WIDTH // LANES
    lyr = lambda l, o, c, i: (l, 0, 0)
    return pl.pallas_call(
        _filt_kernel,
        grid=(DEPTH, HY_ORDER, ct, nlt),
        in_specs=[
            pl.BlockSpec((FEAT_ROWS, LT_FILT), lambda l, o, c, i: (0, i)),
            pl.BlockSpec((FEAT_ROWS, LT_FILT), lambda l, o, c, i: (0, i)),
            pl.BlockSpec((FEAT_ROWS, LANES), lambda l, o, c, i: (0, 0)),
            pl.BlockSpec((None, HY_HIDDEN, FEAT_ROWS), lyr),
            pl.BlockSpec((None, HY_HIDDEN, 1), lyr),
            pl.BlockSpec((None, HY_HIDDEN, HY_HIDDEN), lyr),
            pl.BlockSpec((None, HY_HIDDEN, 1), lyr),
            pl.BlockSpec((None, HY_HIDDEN, 1), lyr),
            pl.BlockSpec((None, LANES, HY_HIDDEN), lambda l, o, c, i: (l, o * 2 * ct + c, 0)),
            pl.BlockSpec((None, LANES, HY_HIDDEN), lambda l, o, c, i: (l, o * 2 * ct + ct + c, 0)),
            pl.BlockSpec((None, LANES, 1), lambda l, o, c, i: (l, o * 2 * ct + c, 0)),
            pl.BlockSpec((None, LANES, 1), lambda l, o, c, i: (l, o * 2 * ct + ct + c, 0)),
        ],
        out_specs=[
            pl.BlockSpec((None, None, LANES, LT_FILT), lambda l, o, c, i: (l, o, c, i)),
            pl.BlockSpec((None, None, LANES, LT_FILT), lambda l, o, c, i: (l, o, c, i)),
            pl.BlockSpec((None, None, LANES, LANES), lambda l, o, c, i: (l, o, c, i)),
        ],
        out_shape=[
            jax.ShapeDtypeStruct((DEPTH, HY_ORDER, WIDTH, seq), F32),
            jax.ShapeDtypeStruct((DEPTH, HY_ORDER, WIDTH, seq), F32),
            jax.ShapeDtypeStruct((DEPTH, HY_ORDER, WIDTH, nlt * LANES), F32),
        ],
        name="hyena_filters",
    )(feats, feats_rev, feats0, w1t, b1, w2t, b2, freq, w3t, w3t, deltas, deltas)


def _hy_kernel(bsz, nj, prm_ref, v_ref, x1_ref, x2_ref, z_ref, g_ref, s_ref, out_ref,
               tst_ref, upad_ref, ush_ref):
    ct = pl.program_id(0)
    rows = bsz * nj
    ntile = 2 * nj
    span = 3 * nj

    upad_ref[...] = jnp.zeros_like(upad_ref)
    tst_ref[(ntile - 1) * LANES:ntile * LANES, :] = jnp.zeros((LANES, LANES), BF16)

    lane = lax.broadcasted_iota(jnp.int32, (rows, LANES), 1)
    blk = lax.broadcasted_iota(jnp.int32, (rows, LANES), 0) & (nj - 1)
    first = (lane == 0) & (blk == 0)
    last = (lane == LANES - 1) & (blk == nj - 1)
    lane8 = lax.broadcasted_iota(jnp.int32, (SUBLANES, LANES), 1)
    sub8 = lax.broadcasted_iota(jnp.int32, (SUBLANES, LANES), 0)

    def conv3(x, base):
        r = pltpu.roll(x, 1, 1)
        xm1 = jnp.where(lane == 0, pltpu.roll(r, 1, 0), r)
        xm1 = jnp.where(first, 0.0, xm1)
        l = pltpu.roll(x, LANES - 1, 1)
        xp1 = jnp.where(lane == LANES - 1, pltpu.roll(l, rows - 1, 0), l)
        xp1 = jnp.where(last, 0.0, xp1)
        return prm_ref[base] * xm1 + prm_ref[base + 1] * x + prm_ref[base + 2] * xp1 + prm_ref[base + 3]

    def rolled(order, c, idx):
        x = jnp.broadcast_to(g_ref[order, c, pl.ds(idx, 1), :], (SUBLANES, LANES))
        return tuple(pltpu.roll(x, r0, 1, stride=1, stride_axis=0) for r0 in range(0, LANES, SUBLANES))

    def build_toeplitz(order, c):
        def body(e, cur):
            prv = rolled(order, c, ntile - 2 - e)
            for p in range(LANES // (2 * SUBLANES)):
                halves = []
                for q in (2 * p, 2 * p + 1):
                    halves.append(jnp.where(lane8 >= sub8 + q * SUBLANES, cur[q], prv[q]))
                r = pl.multiple_of(e * LANES + p * 2 * SUBLANES, 2 * SUBLANES)
                tst_ref[pl.ds(r, 2 * SUBLANES), :] = jnp.concatenate(halves, axis=0).astype(BF16)
            return prv
        lax.fori_loop(0, ntile - 1, body, rolled(order, c, ntile - 1))

    def long_conv(u, order, c):
        upad_ref[:, nj:2 * nj, :] = u.reshape(bsz, nj, LANES)
        for s in range(SUBLANES):
            ush_ref[s] = upad_ref[:, s + 1:s + 1 + span, :]
        build_toeplitz(order, c)

        def body(m, acc):
            base = pl.multiple_of(m * SUBLANES, SUBLANES)
            for s in range(0, SUBLANES, 2):
                lhs = jnp.concatenate(
                    [ush_ref[s, :, pl.ds(base, nj), :].reshape(rows, LANES),
                     ush_ref[s + 1, :, pl.ds(base, nj), :].reshape(rows, LANES)], axis=1)
                r = pl.multiple_of((m * SUBLANES + s) * LANES, 2 * LANES)
                acc = acc + _dot(lhs.astype(BF16), tst_ref[pl.ds(r, 2 * LANES), :])
            return acc
        acc = lax.fori_loop(0, ntile // SUBLANES, body, jnp.zeros((rows, LANES), F32))
        norm = jnp.sum(s_ref[order, pl.ds(c, 1), :], axis=1, keepdims=True)
        return acc / norm

    def channel(c, carry):
        base = (ct * CB_HY + c) * 16
        v = conv3(v_ref[c].reshape(rows, LANES), base)
        x1 = conv3(x1_ref[c].reshape(rows, LANES), base + 4)
        x2 = conv3(x2_ref[c].reshape(rows, LANES), base + 8)
        z1 = x1 * (long_conv(v, 0, c) + prm_ref[base + 12] * v)
        y2 = long_conv(z1, 1, c) + prm_ref[base + 13] * z1
        out = x2 * y2 * _silu(z_ref[c].reshape(rows, LANES))
        out_ref[c] = out.reshape(bsz, nj, LANES)
        return carry

    lax.fori_loop(0, CB_HY, channel, 0)


def _hyena(prm, u_t, g, s, bsz, seq):
    nj = seq // LANES
    nct = WIDTH // CB_HY
    blk = (CB_HY, bsz, nj, LANES)
    return pl.pallas_call(
        functools.partial(_hy_kernel, bsz, nj),
        grid=(nct,),
        in_specs=[
            pl.BlockSpec(memory_space=pltpu.SMEM),
            pl.BlockSpec(blk, lambda i: (i, 0, 0, 0)),
            pl.BlockSpec(blk, lambda i: (nct + i, 0, 0, 0)),
            pl.BlockSpec(blk, lambda i: (2 * nct + i, 0, 0, 0)),
            pl.BlockSpec(blk, lambda i: (3 * nct + i, 0, 0, 0)),
            pl.BlockSpec((HY_ORDER, CB_HY, 2 * nj, LANES), lambda i: (0, i, 0, 0)),
            pl.BlockSpec((HY_ORDER, CB_HY, s.shape[-1]), lambda i: (0, i, 0)),
        ],
        out_specs=pl.BlockSpec(blk, lambda i: (i, 0, 0, 0)),
        out_shape=jax.ShapeDtypeStruct((WIDTH, bsz, nj, LANES), F32),
        scratch_shapes=[
            pltpu.VMEM((2 * nj * LANES, LANES), BF16),
            pltpu.VMEM((bsz, 3 * nj + SUBLANES, LANES), F32),
            pltpu.VMEM((SUBLANES, bsz, 3 * nj, LANES), F32),
        ],
        name="hyena",
    )(prm, u_t, u_t, u_t, u_t, g, s)


def _out_kernel(final, x_ref, yml_ref, yrt_ref, yhyt_ref, w_ref, g_ref, o_ref):
    acc = _dot(yml_ref[...].astype(BF16), w_ref[0:WIDTH, :])
    acc = acc + _dot(yrt_ref[...].astype(BF16), w_ref[WIDTH:2 * WIDTH, :])
    acc = acc + _dot_tn(yhyt_ref[...].astype(BF16), w_ref[2 * WIDTH:, :])
    x = x_ref[...] + acc
    if final:
        ms = jnp.mean(x * x, axis=-1, keepdims=True)
        x = x * lax.rsqrt(ms + RMS_EPS) * g_ref[...]
    o_ref[...] = x


def _out_proj(x, y_ml, y_rt, y_hyt, w, gain, final):
    t = x.shape[0]
    return pl.pallas_call(
        functools.partial(_out_kernel, final),
        grid=(t // TM_OUT,),
        in_specs=[
            pl.BlockSpec((TM_OUT, D_MODEL), lambda i: (i, 0)),
            pl.BlockSpec((TM_OUT, WIDTH), lambda i: (i, 0)),
            pl.BlockSpec((TM_OUT, WIDTH), lambda i: (i, 0)),
            pl.BlockSpec((WIDTH, TM_OUT), lambda i: (0, i)),
            pl.BlockSpec((3 * WIDTH, D_MODEL), lambda i: (0, 0)),
            pl.BlockSpec((1, D_MODEL), lambda i: (0, 0)),
        ],
        out_specs=pl.BlockSpec((TM_OUT, D_MODEL), lambda i: (i, 0)),
        out_shape=jax.ShapeDtypeStruct((t, D_MODEL), F32),
        name="out_proj",
    )(x, y_ml, y_rt, y_hyt, w, gain)


def kernel(x_prompt, x_sample, norm_g, w_in, ml_conv_w, ml_conv_b, ml_gate_b, ml_norm_g, rt_norm_g, hy_conv_w, hy_conv_b, hy_w1, hy_b1, hy_w2, hy_b2, hy_w3, hy_freq, hy_deltas, hy_skip, w_out, final_g):
    groups = []
    row0 = 0
    for xg in (x_prompt, x_sample):
        bsz, seq, _ = xg.shape
        groups.append((row0, bsz, seq))
        row0 += bsz * seq
    x = jnp.concatenate([x_prompt.reshape(-1, D_MODEL), x_sample.reshape(-1, D_MODEL)], axis=0)

    c_gate = ML_COLS
    c_rt = c_gate + 4 * HEADS
    c_hy = c_rt + RT_COLS
    gate_pad = jnp.zeros((DEPTH, D_MODEL, LANES - 4 * HEADS), F32)
    w_tok = jnp.concatenate([w_in[:, :, :c_gate], w_in[:, :, c_gate:c_rt], gate_pad,
                             w_in[:, :, c_rt:c_hy]], axis=2).astype(BF16)
    w_gt = jnp.swapaxes(w_in[:, :, c_gate:c_rt], 1, 2).astype(BF16)
    w_hyt = jnp.swapaxes(w_in[:, :, c_hy:], 1, 2).astype(BF16)
    w_out_b = w_out.astype(BF16)
    gate_b = ml_gate_b.reshape(DEPTH, 1, 4 * HEADS)
    gate_b_row = jnp.concatenate([gate_b, jnp.zeros((DEPTH, 1, LANES - 4 * HEADS), F32)], axis=2)
    gate_b_col = ml_gate_b.reshape(DEPTH, 4 * HEADS, 1)

    cw = hy_conv_w.reshape(DEPTH, 3, 3, WIDTH)
    cbias = hy_conv_b.reshape(DEPTH, 1, 3, WIDTH)
    taps = jnp.concatenate([cw, cbias], axis=1)
    taps = jnp.transpose(taps, (0, 3, 2, 1)).reshape(DEPTH, WIDTH, 12)
    hy_prm = jnp.concatenate([taps, jnp.transpose(hy_skip, (0, 2, 1)),
                              jnp.zeros((DEPTH, WIDTH, 2), F32)], axis=2).reshape(DEPTH, WIDTH * 16)

    w1t = jnp.concatenate([jnp.swapaxes(hy_w1, 1, 2),
                           jnp.zeros((DEPTH, HY_HIDDEN, FEAT_ROWS - hy_w1.shape[1]), F32)], axis=2)
    w2t = jnp.swapaxes(hy_w2, 1, 2)
    w3t = jnp.swapaxes(hy_w3, 1, 2)
    b1 = hy_b1[:, :, None]
    b2 = hy_b2[:, :, None]
    freq = hy_freq[:, :, None]
    deltas = hy_deltas.reshape(DEPTH, HY_ORDER * 2 * WIDTH, 1)

    filters = []
    for (_, _, seq) in groups:
        gb, gf, s = _filters(seq, w1t, b1, w2t, b2, freq, w3t, deltas)
        g = jnp.concatenate([gb, gf], axis=-1)
        filters.append((g.reshape(DEPTH, HY_ORDER, WIDTH, 2 * seq // LANES, LANES), s))
    rot = [_rotary_tables(seq) for (_, _, seq) in groups]
    dec, dvec = _retention_tables()

    for layer in range(DEPTH):
        u_ml, u_g, u_rt, u_gt, u_hyt = _in_proj(x, norm_g[layer][None, :], w_tok[layer],
                                                w_gt[layer], w_hyt[layer])
        gc, gr = _gates(u_g, u_gt, gate_b_row[layer], gate_b_col[layer])
        y_ml, y_rt, y_hy = [], [], []
        for gi, (r0, bsz, seq) in enumerate(groups):
            y_ml.append(_mlstm(u_ml, gc, gr, ml_conv_w[layer], ml_conv_b[layer][None, :],
                               ml_norm_g[layer][None, :], r0, bsz, seq))
            y_rt.append(_retention(u_rt, rot[gi][0], rot[gi][1], dec, dvec,
                                   rt_norm_g[layer][None, :], r0, bsz, seq))
            u_t = u_hyt[:, r0:r0 + bsz * seq].reshape(HY_ROWS, bsz, seq // LANES, LANES)
            g, s = filters[gi]
            y = _hyena(hy_prm[layer], u_t, g[layer], s[layer], bsz, seq)
            y_hy.append(y.reshape(WIDTH, bsz * seq))
        x = _out_proj(x, jnp.concatenate(y_ml, axis=0), jnp.concatenate(y_rt, axis=0),
                      jnp.concatenate(y_hy, axis=1), w_out_b[layer], final_g[None, :],
                      layer == DEPTH - 1)

    n_prompt = x_prompt.shape[0] * x_prompt.shape[1]
    return x[:n_prompt].reshape(x_prompt.shape), x[n_prompt:].reshape(x_sample.shape)
```

```python
import functools
import math

import ml_dtypes
import numpy as np
import jax
import jax.numpy as jnp
from jax import lax
from jax.experimental import pallas as pl
from jax.experimental.pallas import tpu as pltpu

F32 = jnp.float32
BF16 = jnp.bfloat16

D_MODEL = 1024
DEPTH = 4
CHUNK = 128
HEADS = 4
HEAD_DIM = 128
WIDTH = 512
HY_ORDER = 2
HY_BANDS = 16
HY_HIDDEN = 64
FEAT_ROWS = 40
ROPE_BASE = 10000.0
RMS_EPS = 1e-6
HEAD_NORM_EPS = 1e-5
M_INIT = -1e30
RT_LOG_GAMMA_FWD = tuple(math.log(1.0 - 2.0 ** (-5.0 - h)) for h in range(HEADS))
RT_LOG_GAMMA_BWD = tuple(math.log(1.0 - 2.0 ** (-5.5 - h)) for h in range(HEADS))

LANES = 128
SUBLANES = 8
TM_IN = 256
TM_OUT = 512
TB_SEQ = 1024
TG = 1024
CB_HY = 8
CB_SPEC = 8
KH_PAD = 16
LT_FILT = 2048
ML_COLS = 5 * WIDTH
RT_COLS = 4 * WIDTH
HY_ROWS = 4 * WIDTH
HIGHEST = lax.Precision.HIGHEST


def _dot(a, b):
    return jnp.dot(a, b, preferred_element_type=F32)


def _dot_nt(a, b):
    return lax.dot_general(a, b, (((1,), (1,)), ((), ())), preferred_element_type=F32)


def _dot_tn(a, b):
    return lax.dot_general(a, b, (((0,), (0,)), ((), ())), preferred_element_type=F32)


def _dot_f32(a, b):
    return jnp.dot(a, b, precision=HIGHEST, preferred_element_type=F32)


def _silu(y):
    return y * jax.nn.sigmoid(y)


def _log_sigmoid(x):
    return -(jnp.maximum(-x, 0.0) + jnp.log1p(jnp.exp(-jnp.abs(x))))


def _head_norm(h, gain):
    mu = jnp.mean(h, axis=-1, keepdims=True)
    hc = h - mu
    var = jnp.mean(hc * hc, axis=-1, keepdims=True)
    return hc * lax.rsqrt(var + HEAD_NORM_EPS) * gain


def _in_kernel(x_ref, g_ref, wtok_ref, wgt_ref, whyt_ref,
               oml_ref, og_ref, ort_ref, ogt_ref, ohyt_ref):
    x = x_ref[...]
    ms = jnp.mean(x * x, axis=-1, keepdims=True)
    h = (x * lax.rsqrt(ms + RMS_EPS) * g_ref[...]).astype(BF16)
    oml_ref[...] = _dot(h, wtok_ref[:, 0:ML_COLS])
    og_ref[...] = _dot(h, wtok_ref[:, ML_COLS:ML_COLS + LANES])
    ort_ref[...] = _dot(h, wtok_ref[:, ML_COLS + LANES:])
    ogt_ref[...] = _dot_nt(wgt_ref[...], h)
    ohyt_ref[...] = _dot_nt(whyt_ref[...], h)


def _in_proj(x, gain, w_tok, w_gt, w_hyt):
    t = x.shape[0]
    ncol = w_tok.shape[1]
    return pl.pallas_call(
        _in_kernel,
        grid=(t // TM_IN,),
        in_specs=[
            pl.BlockSpec((TM_IN, D_MODEL), lambda i: (i, 0)),
            pl.BlockSpec((1, D_MODEL), lambda i: (0, 0)),
            pl.BlockSpec((D_MODEL, ncol), lambda i: (0, 0)),
            pl.BlockSpec((4 * HEADS, D_MODEL), lambda i: (0, 0)),
            pl.BlockSpec((HY_ROWS, D_MODEL), lambda i: (0, 0)),
        ],
        out_specs=[
            pl.BlockSpec((TM_IN, ML_COLS), lambda i: (i, 0)),
            pl.BlockSpec((TM_IN, LANES), lambda i: (i, 0)),
            pl.BlockSpec((TM_IN, RT_COLS), lambda i: (i, 0)),
            pl.BlockSpec((4 * HEADS, TM_IN), lambda i: (0, i)),
            pl.BlockSpec((HY_ROWS, TM_IN), lambda i: (0, i)),
        ],
        out_shape=[
            jax.ShapeDtypeStruct((t, ML_COLS), F32),
            jax.ShapeDtypeStruct((t, LANES), F32),
            jax.ShapeDtypeStruct((t, RT_COLS), F32),
            jax.ShapeDtypeStruct((4 * HEADS, t), F32),
            jax.ShapeDtypeStruct((HY_ROWS, t), F32),
        ],
        name="in_proj",
    )(x, gain, w_tok, w_gt, w_hyt)


def _gates_kernel(g_ref, gt_ref, b_ref, bt_ref, gc_ref, gr_ref):
    row = lax.broadcasted_iota(jnp.int32, (CHUNK, CHUNK), 0)
    col = lax.broadcasted_iota(jnp.int32, (CHUNK, CHUNK), 1)
    lower = (row >= col).astype(F32)
    upper = (row <= col).astype(F32)
    gc_ref[...] = jnp.zeros_like(gc_ref)
    gr_ref[...] = jnp.zeros_like(gr_ref)
    for c in range(TG // CHUNK):
        sl = slice(c * CHUNK, (c + 1) * CHUNK)
        pre = g_ref[sl, :] + b_ref[...]
        lf = _log_sigmoid(pre)
        cum_f = _dot_f32(lower, lf)
        cum_b = _dot_f32(upper, lf)
        pre_t = gt_ref[:, sl] + bt_ref[...]
        lf_t = _log_sigmoid(pre_t)
        cum_f_t = _dot_f32(lf_t, upper)
        cum_b_t = _dot_f32(lf_t, lower)
        for h in range(HEADS):
            gc_ref[h, sl, 0:1] = pre[:, h:h + 1]
            gc_ref[h, sl, 1:2] = cum_f[:, HEADS + h:HEADS + h + 1]
            gc_ref[h, sl, 2:3] = pre[:, 2 * HEADS + h:2 * HEADS + h + 1]
            gc_ref[h, sl, 3:4] = cum_b[:, 3 * HEADS + h:3 * HEADS + h + 1]
            gr_ref[h, 0:1, sl] = pre_t[h:h + 1, :]
            gr_ref[h, 1:2, sl] = cum_f_t[HEADS + h:HEADS + h + 1, :]
            gr_ref[h, 2:3, sl] = pre_t[2 * HEADS + h:2 * HEADS + h + 1, :]
            gr_ref[h, 3:4, sl] = cum_b_t[3 * HEADS + h:3 * HEADS + h + 1, :]


def _gates(g, gt, bias_row, bias_col):
    t = g.shape[0]
    return pl.pallas_call(
        _gates_kernel,
        grid=(t // TG,),
        in_specs=[
            pl.BlockSpec((TG, LANES), lambda i: (i, 0)),
            pl.BlockSpec((4 * HEADS, TG), lambda i: (0, i)),
            pl.BlockSpec((1, LANES), lambda i: (0, 0)),
            pl.BlockSpec((4 * HEADS, 1), lambda i: (0, 0)),
        ],
        out_specs=[
            pl.BlockSpec((HEADS, TG, SUBLANES), lambda i: (0, i, 0)),
            pl.BlockSpec((HEADS, SUBLANES, TG), lambda i: (0, 0, i)),
        ],
        out_shape=[
            jax.ShapeDtypeStruct((HEADS, t, SUBLANES), F32),
            jax.ShapeDtypeStruct((HEADS, SUBLANES, t), F32),
        ],
        name="gates",
    )(g, gt, bias_row, bias_col)


def _seq_block(n, d, nb):
    return n + d * (nb - 1 - 2 * n)


def _ml_kernel(nb, q_ref, k_ref, v_ref, o_ref, z_ref, qp_ref, qn_ref, kp_ref, kn_ref,
               gc_ref, gr_ref, cwq_ref, cbq_ref, cwk_ref, cbk_ref, ng_ref,
               out_ref, hf_ref, st_ref, m_ref, stg_ref, qs_ref, ks_ref):
    tb = TB_SEQ
    d = pl.program_id(2)
    n = pl.program_id(3)
    sb = _seq_block(n, d, nb)

    @pl.when(n == 0)
    def _():
        st_ref[...] = jnp.zeros_like(st_ref)
        m_ref[...] = jnp.full_like(m_ref, M_INIT)

    def conv_silu(raw_ref, prev_ref, next_ref, cw_ref, cb_ref):
        stg_ref[SUBLANES:SUBLANES + tb, :] = raw_ref[...]
        stg_ref[SUBLANES - 1:SUBLANES, :] = jnp.where(sb > 0, prev_ref[SUBLANES - 1:SUBLANES, :], 0.0)
        stg_ref[SUBLANES + tb:SUBLANES + tb + 1, :] = jnp.where(sb < nb - 1, next_ref[0:1, :], 0.0)
        w = cw_ref[...]
        y = (stg_ref[SUBLANES - 1:SUBLANES - 1 + tb, :] * w[0:1, :]
             + stg_ref[SUBLANES:SUBLANES + tb, :] * w[1:2, :]
             + stg_ref[SUBLANES + 1:SUBLANES + 1 + tb, :] * w[2:3, :] + cb_ref[...])
        return _silu(y)

    qs_ref[...] = conv_silu(q_ref, qp_ref, qn_ref, cwq_ref, cbq_ref).astype(BF16)
    ks_ref[...] = conv_silu(k_ref, kp_ref, kn_ref, cwk_ref, cbk_ref) * (HEAD_DIM ** -0.5)

    row = lax.broadcasted_iota(jnp.int32, (CHUNK, CHUNK), 0)
    col = lax.broadcasted_iota(jnp.int32, (CHUNK, CHUNK), 1)
    unit = (col == 0).astype(BF16)

    def chunk(c, rev):
        sl = pl.ds(pl.multiple_of(c * CHUNK, CHUNK), CHUNK)
        a = 2 if rev else 0
        q = qs_ref[sl, :]
        k = ks_ref[sl, :]
        v2 = jnp.concatenate([v_ref[sl, :].astype(BF16), unit], axis=1)
        gc = gc_ref[sl, :]
        gr = gr_ref[:, sl]
        li_c, cum_c = gc[:, a:a + 1], gc[:, a + 1:a + 2]
        li_r, cum_r = gr[a:a + 1, :], gr[a + 1:a + 2, :]
        f_tot = cum_c[0:1, :] if rev else cum_c[CHUNK - 1:CHUNK, :]
        dm = cum_c - cum_r + li_r
        dm = jnp.where((row <= col) if rev else (row >= col), dm, -jnp.inf)
        m_intra = jnp.max(dm, axis=1, keepdims=True)
        m_prev = m_ref[0:1, 0:1]
        m_inter = cum_c + m_prev
        m_tot = jnp.maximum(m_inter, m_intra)
        w_inter = jnp.exp(m_inter - m_tot)
        s = _dot_nt(q, k.astype(BF16)) * jnp.exp(dm - m_tot)
        num2 = _dot(s.astype(BF16), v2) + w_inter * _dot(q, st_ref[...].astype(BF16))
        num = num2[:, 0:HEAD_DIM]
        den = num2[:, HEAD_DIM:HEAD_DIM + 1]
        out = num / jnp.maximum(jnp.abs(den), jnp.exp(-m_tot))
        g = f_tot - cum_c + li_c
        m_chunk = jnp.max(g, axis=0, keepdims=True)
        kw = (k * jnp.exp(g - m_chunk)).astype(BF16)
        kv2 = _dot_tn(kw, v2)
        m_new = jnp.maximum(f_tot + m_prev, m_chunk)
        st_ref[...] = jnp.exp(f_tot + m_prev - m_new) * st_ref[...] + jnp.exp(m_chunk - m_new) * kv2
        m_ref[...] = jnp.broadcast_to(m_new, m_ref.shape)
        return out

    def run(rev):
        def body(i, carry):
            c = (tb // CHUNK - 1 - i) if rev else i
            out = chunk(c, rev)
            pos = pl.ds(pl.multiple_of(sb * tb + c * CHUNK, CHUNK), CHUNK)
            if not rev:
                hf_ref[pos, :] = out
            else:
                sl = pl.ds(pl.multiple_of(c * CHUNK, CHUNK), CHUNK)
                hh = jax.nn.sigmoid(o_ref[sl, :]) * (out + hf_ref[pos, :])
                out_ref[sl, :] = _head_norm(hh, ng_ref[...]) * _silu(z_ref[sl, :])
            return carry
        lax.fori_loop(0, tb // CHUNK, body, 0)

    pl.when(d == 0)(lambda: run(False))
    pl.when(d == 1)(lambda: run(True))


def _mlstm(u_ml, gc, gr, conv_w, conv_b, norm_g, row0, bsz, seq):
    t = u_ml.shape[0]
    tb = TB_SEQ
    nb = seq // tb
    blk0 = row0 // tb
    t8 = tb // SUBLANES

    def rb(b, d, n):
        return blk0 + b * nb + _seq_block(n, d, nb)

    def rb_out(b, d, n):
        return blk0 + b * nb + nb - 1 - n * d

    def col(g):
        return lambda b, h, d, n: (rb(b, d, n), g * HEADS + h)

    def col_late(g):
        return lambda b, h, d, n: (rb_out(b, d, n), g * HEADS + h)

    def halo_prev(g):
        return lambda b, h, d, n: (jnp.maximum(rb(b, d, n) * t8 - 1, 0), g * HEADS + h)

    def halo_next(g):
        return lambda b, h, d, n: (jnp.minimum((rb(b, d, n) + 1) * t8, t // SUBLANES - 1), g * HEADS + h)

    blk = (tb, HEAD_DIM)
    halo = (SUBLANES, HEAD_DIM)
    return pl.pallas_call(
        functools.partial(_ml_kernel, nb),
        grid=(bsz, HEADS, 2, nb),
        in_specs=[
            pl.BlockSpec(blk, col(0)), pl.BlockSpec(blk, col(1)), pl.BlockSpec(blk, col(2)),
            pl.BlockSpec(blk, col_late(3)), pl.BlockSpec(blk, col_late(4)),
            pl.BlockSpec(halo, halo_prev(0)), pl.BlockSpec(halo, halo_next(0)),
            pl.BlockSpec(halo, halo_prev(1)), pl.BlockSpec(halo, halo_next(1)),
            pl.BlockSpec((None, tb, SUBLANES), lambda b, h, d, n: (h, rb(b, d, n), 0)),
            pl.BlockSpec((None, SUBLANES, tb), lambda b, h, d, n: (h, 0, rb(b, d, n))),
            pl.BlockSpec((3, HEAD_DIM), lambda b, h, d, n: (0, h)),
            pl.BlockSpec((1, HEAD_DIM), lambda b, h, d, n: (0, h)),
            pl.BlockSpec((3, HEAD_DIM), lambda b, h, d, n: (0, HEADS + h)),
            pl.BlockSpec((1, HEAD_DIM), lambda b, h, d, n: (0, HEADS + h)),
            pl.BlockSpec((1, HEAD_DIM), lambda b, h, d, n: (0, h)),
        ],
        out_specs=pl.BlockSpec(blk, lambda b, h, d, n: (rb_out(b, d, n) - blk0, h)),
        out_shape=jax.ShapeDtypeStruct((bsz * seq, WIDTH), F32),
        scratch_shapes=[
            pltpu.VMEM((seq, HEAD_DIM), F32),
            pltpu.VMEM((HEAD_DIM, 2 * HEAD_DIM), F32),
            pltpu.VMEM((SUBLANES, LANES), F32),
            pltpu.VMEM((tb + 2 * SUBLANES, HEAD_DIM), F32),
            pltpu.VMEM((tb, HEAD_DIM), BF16),
            pltpu.VMEM((tb, HEAD_DIM), F32),
        ],
        name="mlstm",
    )(u_ml, u_ml, u_ml, u_ml, u_ml, u_ml, u_ml, u_ml, u_ml, gc, gr,
      conv_w, conv_b, conv_w, conv_b, norm_g)


def _rt_kernel(nb, q_ref, k_ref, v_ref, z_ref, cc_ref, ss_ref, dec_ref, dv_ref, ng_ref,
               out_ref, hf_ref, st_ref, qs_ref, ks_ref):
    tb = TB_SEQ
    d = pl.program_id(2)
    n = pl.program_id(3)
    sb = _seq_block(n, d, nb)

    @pl.when(n == 0)
    def _():
        st_ref[...] = jnp.zeros_like(st_ref)

    cc = cc_ref[...]
    ss = ss_ref[...]
    qs_ref[...] = (q_ref[...] * cc + pltpu.roll(q_ref[...], HEAD_DIM // 2, 1) * ss) * (HEAD_DIM ** -0.5)
    ks_ref[...] = k_ref[...] * cc + pltpu.roll(k_ref[...], HEAD_DIM // 2, 1) * ss
    dec = dec_ref[...]
    dv = dv_ref[...]
    q_dec, k_dec, chunk_dec = dv[:, 0:1], dv[:, 1:2], dv[0:1, 2:3]

    def chunk(c):
        sl = pl.ds(pl.multiple_of(c * CHUNK, CHUNK), CHUNK)
        q = qs_ref[sl, :]
        k = ks_ref[sl, :]
        vb = v_ref[sl, :].astype(BF16)
        s = _dot_nt(q.astype(BF16), k.astype(BF16)) * dec
        out = _dot(s.astype(BF16), vb) + _dot((q * q_dec).astype(BF16), st_ref[...].astype(BF16))
        st_ref[...] = chunk_dec * st_ref[...] + _dot_tn((k * k_dec).astype(BF16), vb)
        return out

    def run(rev):
        def body(i, carry):
            c = (tb // CHUNK - 1 - i) if rev else i
            out = chunk(c)
            pos = pl.ds(pl.multiple_of(sb * tb + c * CHUNK, CHUNK), CHUNK)
            if not rev:
                hf_ref[pos, :] = out
            else:
                sl = pl.ds(pl.multiple_of(c * CHUNK, CHUNK), CHUNK)
                out_ref[sl, :] = _head_norm(out + hf_ref[pos, :], ng_ref[...]) * _silu(z_ref[sl, :])
            return carry
        lax.fori_loop(0, tb // CHUNK, body, 0)

    pl.when(d == 0)(lambda: run(False))
    pl.when(d == 1)(lambda: run(True))


def _retention(u_rt, rot_c, rot_s, dec, dvec, norm_g, row0, bsz, seq):
    tb = TB_SEQ
    nb = seq // tb
    blk0 = row0 // tb

    def rb(b, d, n):
        return blk0 + b * nb + _seq_block(n, d, nb)

    def rb_out(b, d, n):
        return blk0 + b * nb + nb - 1 - n * d

    def col(g):
        return lambda b, h, d, n: (rb(b, d, n), g * HEADS + h)

    blk = (tb, HEAD_DIM)
    return pl.pallas_call(
        functools.partial(_rt_kernel, nb),
        grid=(bsz, HEADS, 2, nb),
        in_specs=[
            pl.BlockSpec(blk, col(0)), pl.BlockSpec(blk, col(1)), pl.BlockSpec(blk, col(2)),
            pl.BlockSpec(blk, lambda b, h, d, n: (rb_out(b, d, n), 3 * HEADS + h)),
            pl.BlockSpec(blk, lambda b, h, d, n: (_seq_block(n, d, nb), 0)),
            pl.BlockSpec(blk, lambda b, h, d, n: (_seq_block(n, d, nb), 0)),
            pl.BlockSpec((None, None, CHUNK, CHUNK), lambda b, h, d, n: (d, h, 0, 0)),
            pl.BlockSpec((None, None, CHUNK, SUBLANES), lambda b, h, d, n: (d, h, 0, 0)),
            pl.BlockSpec((1, HEAD_DIM), lambda b, h, d, n: (0, h)),
        ],
        out_specs=pl.BlockSpec(blk, lambda b, h, d, n: (rb_out(b, d, n) - blk0, h)),
        out_shape=jax.ShapeDtypeStruct((bsz * seq, WIDTH), F32),
        scratch_shapes=[
            pltpu.VMEM((seq, HEAD_DIM), F32),
            pltpu.VMEM((HEAD_DIM, HEAD_DIM), F32),
            pltpu.VMEM((tb, HEAD_DIM), F32),
            pltpu.VMEM((tb, HEAD_DIM), F32),
        ],
        name="retention",
    )(u_rt, u_rt, u_rt, u_rt, rot_c, rot_s, dec, dvec, norm_g)


def _rotary_tables(seq):
    inv = ROPE_BASE ** (-jnp.arange(0, HEAD_DIM, 2, dtype=F32) / HEAD_DIM)
    ang = jnp.arange(seq, dtype=F32)[:, None] * inv[None, :]
    cos, sin = jnp.cos(ang), jnp.sin(ang)
    return jnp.concatenate([cos, cos], axis=1), jnp.concatenate([-sin, sin], axis=1)


def _retention_tables():
    pos = np.arange(CHUNK, dtype=np.float64)
    rel = pos[:, None] - pos[None, :]
    dec = np.zeros((2, HEADS, CHUNK, CHUNK))
    dvec = np.zeros((2, HEADS, CHUNK, SUBLANES))
    for h in range(HEADS):
        lf, lb = RT_LOG_GAMMA_FWD[h], RT_LOG_GAMMA_BWD[h]
        dec[0, h] = np.where(rel >= 0, np.exp(np.maximum(rel, 0.0) * lf), 0.0)
        dec[1, h] = np.where(rel <= 0, np.exp(np.maximum(-rel, 0.0) * lb), 0.0)
        dvec[0, h, :, 0] = np.exp((pos + 1.0) * lf)
        dvec[0, h, :, 1] = np.exp((CHUNK - 1.0 - pos) * lf)
        dvec[0, h, :, 2] = np.exp(CHUNK * lf)
        dvec[1, h, :, 0] = np.exp((CHUNK - pos) * lb)
        dvec[1, h, :, 1] = np.exp(pos * lb)
        dvec[1, h, :, 2] = np.exp(CHUNK * lb)
    return jnp.asarray(dec, F32), jnp.asarray(dvec, F32)


def _filt_kernel(ft_ref, fr_ref, f0_ref, w1_ref, b1_ref, w2_ref, b2_ref, fq_ref,
                 w3f_ref, w3b_ref, dlf_ref, dlb_ref, gb_ref, gf_ref, s_ref):
    lt = pl.program_id(3)
    fq = fq_ref[...]

    def hidden(ft):
        h1 = jnp.sin(fq * (_dot_f32(w1_ref[...], ft) + b1_ref[...]))
        return jnp.sin(fq * (_dot_f32(w2_ref[...], h1) + b2_ref[...]))

    ft = ft_ref[...]
    fr = fr_ref[...]
    dl_f = jnp.abs(dlf_ref[...])
    dl_b = jnp.abs(dlb_ref[...])
    fwd = _dot_f32(w3f_ref[...], hidden(ft)) * jnp.exp(-ft[0:1, :] * dl_f)
    bwd = _dot_f32(w3b_ref[...], hidden(fr)) * jnp.exp(-fr[0:1, :] * dl_b)
    f0 = f0_ref[...]
    bwd0 = (_dot_f32(w3b_ref[...], hidden(f0)) * jnp.exp(-f0[0:1, :] * dl_b))[:, 0:1]
    lag = lax.broadcasted_iota(jnp.int32, fwd.shape, 1) + lt * LT_FILT
    fwd = jnp.where(lag == 0, fwd + bwd0, fwd)
    bwd = jnp.where(lag == 0, 0.0, bwd)
    gb_ref[...] = bwd
    gf_ref[...] = fwd
    part = jnp.sum(jnp.abs(fwd) + jnp.abs(bwd), axis=1, keepdims=True)
    lane = lax.broadcasted_iota(jnp.int32, (LANES, LANES), 1)
    s_ref[...] = jnp.where(lane == 0, part, 0.0)


def _filters(seq, w1t, b1, w2t, b2, freq, w3t, deltas):
    t = jnp.linspace(0.0, 1.0, seq, dtype=F32)[None, :]
    w = (2.0 * math.pi / seq) * jnp.arange(seq, dtype=F32)[None, :]
    bands = jnp.linspace(1e-4, HY_BANDS - 1, HY_BANDS, dtype=F32)[:, None]
    feats = jnp.concatenate([t, jnp.cos(bands * w), -jnp.sin(bands * w),
                             jnp.zeros((FEAT_ROWS - 1 - 2 * HY_BANDS, seq), F32)], axis=0)
    feats_rev = jnp.concatenate([feats[:, :1], feats[:, :0:-1]], axis=1)
    feats0 = jnp.broadcast_to(feats[:, :1], (FEAT_ROWS, LANES))
    nlt = seq // LT_FILT
    ct = WIDTH // LANES
    lyr = lambda l, o, c, i: (l, 0, 0)
    return pl.pallas_call(
        _filt_kernel,
        grid=(DEPTH, HY_ORDER, ct, nlt),
        in_specs=[
            pl.BlockSpec((FEAT_ROWS, LT_FILT), lambda l, o, c, i: (0, i)),
            pl.BlockSpec((FEAT_ROWS, LT_FILT), lambda l, o, c, i: (0, i)),
            pl.BlockSpec((FEAT_ROWS, LANES), lambda l, o, c, i: (0, 0)),
            pl.BlockSpec((None, HY_HIDDEN, FEAT_ROWS), lyr),
            pl.BlockSpec((None, HY_HIDDEN, 1), lyr),
            pl.BlockSpec((None, HY_HIDDEN, HY_HIDDEN), lyr),
            pl.BlockSpec((None, HY_HIDDEN, 1), lyr),
            pl.BlockSpec((None, HY_HIDDEN, 1), lyr),
            pl.BlockSpec((None, LANES, HY_HIDDEN), lambda l, o, c, i: (l, o * 2 * ct + c, 0)),
            pl.BlockSpec((None, LANES, HY_HIDDEN), lambda l, o, c, i: (l, o * 2 * ct + ct + c, 0)),
            pl.BlockSpec((None, LANES, 1), lambda l, o, c, i: (l, o * 2 * ct + c, 0)),
            pl.BlockSpec((None, LANES, 1), lambda l, o, c, i: (l, o * 2 * ct + ct + c, 0)),
        ],
        out_specs=[
            pl.BlockSpec((None, None, LANES, LT_FILT), lambda l, o, c, i: (l, o, c, i)),
            pl.BlockSpec((None, None, LANES, LT_FILT), lambda l, o, c, i: (l, o, c, i)),
            pl.BlockSpec((None, None, LANES, LANES), lambda l, o, c, i: (l, o, c, i)),
        ],
        out_shape=[
            jax.ShapeDtypeStruct((DEPTH, HY_ORDER, WIDTH, seq), F32),
            jax.ShapeDtypeStruct((DEPTH, HY_ORDER, WIDTH, seq), F32),
            jax.ShapeDtypeStruct((DEPTH, HY_ORDER, WIDTH, nlt * LANES), F32),
        ],
        name="hyena_filters",
    )(feats, feats_rev, feats0, w1t, b1, w2t, b2, freq, w3t, w3t, deltas, deltas)


def _hy_kernel(bsz, nj, prm_ref, v_ref, x1_ref, x2_ref, z_ref, g_ref, s_ref, out_ref,
               tst_ref, upad_ref, ush_ref):
    ct = pl.program_id(0)
    rows = bsz * nj
    ntile = 2 * nj
    span = 3 * nj

    upad_ref[...] = jnp.zeros_like(upad_ref)
    tst_ref[(ntile - 1) * LANES:ntile * LANES, :] = jnp.zeros((LANES, LANES), BF16)

    lane = lax.broadcasted_iota(jnp.int32, (rows, LANES), 1)
    blk = lax.broadcasted_iota(jnp.int32, (rows, LANES), 0) & (nj - 1)
    first = (lane == 0) & (blk == 0)
    last = (lane == LANES - 1) & (blk == nj - 1)
    lane8 = lax.broadcasted_iota(jnp.int32, (SUBLANES, LANES), 1)
    sub8 = lax.broadcasted_iota(jnp.int32, (SUBLANES, LANES), 0)

    def conv3(x, base):
        r = pltpu.roll(x, 1, 1)
        xm1 = jnp.where(lane == 0, pltpu.roll(r, 1, 0), r)
        xm1 = jnp.where(first, 0.0, xm1)
        l = pltpu.roll(x, LANES - 1, 1)
        xp1 = jnp.where(lane == LANES - 1, pltpu.roll(l, rows - 1, 0), l)
        xp1 = jnp.where(last, 0.0, xp1)
        return prm_ref[base] * xm1 + prm_ref[base + 1] * x + prm_ref[base + 2] * xp1 + prm_ref[base + 3]

    def rolled(order, c, idx):
        x = jnp.broadcast_to(g_ref[order, c, pl.ds(idx, 1), :], (SUBLANES, LANES))
        return tuple(pltpu.roll(x, r0, 1, stride=1, stride_axis=0) for r0 in range(0, LANES, SUBLANES))

    def build_toeplitz(order, c):
        def body(e, cur):
            prv = rolled(order, c, ntile - 2 - e)
            for p in range(LANES // (2 * SUBLANES)):
                halves = []
                for q in (2 * p, 2 * p + 1):
                    halves.append(jnp.where(lane8 >= sub8 + q * SUBLANES, cur[q], prv[q]))
                r = pl.multiple_of(e * LANES + p * 2 * SUBLANES, 2 * SUBLANES)
                tst_ref[pl.ds(r, 2 * SUBLANES), :] = jnp.concatenate(halves, axis=0).astype(BF16)
            return prv
        lax.fori_loop(0, ntile - 1, body, rolled(order, c, ntile - 1))

    def long_conv(u, order, c):
        upad_ref[:, nj:2 * nj, :] = u.reshape(bsz, nj, LANES)
        for s in range(SUBLANES):
            ush_ref[s] = upad_ref[:, s + 1:s + 1 + span, :]
        build_toeplitz(order, c)

        def body(m, acc):
            base = pl.multiple_of(m * SUBLANES, SUBLANES)
            for s in range(0, SUBLANES, 2):
                lhs = jnp.concatenate(
                    [ush_ref[s, :, pl.ds(base, nj), :].reshape(rows, LANES),
                     ush_ref[s + 1, :, pl.ds(base, nj), :].reshape(rows, LANES)], axis=1)
                r = pl.multiple_of((m * SUBLANES + s) * LANES, 2 * LANES)
                acc = acc + _dot(lhs.astype(BF16), tst_ref[pl.ds(r, 2 * LANES), :])
            return acc
        acc = lax.fori_loop(0, ntile // SUBLANES, body, jnp.zeros((rows, LANES), F32))
        norm = jnp.sum(s_ref[order, pl.ds(c, 1), :], axis=1, keepdims=True)
        return acc / norm

    def channel(c, carry):
        base = (ct * CB_HY + c) * 16
        v = conv3(v_ref[c].reshape(rows, LANES), base)
        x1 = conv3(x1_ref[c].reshape(rows, LANES), base + 4)
        x2 = conv3(x2_ref[c].reshape(rows, LANES), base + 8)
        z1 = x1 * (long_conv(v, 0, c) + prm_ref[base + 12] * v)
        y2 = long_conv(z1, 1, c) + prm_ref[base + 13] * z1
        out = x2 * y2 * _silu(z_ref[c].reshape(rows, LANES))
        out_ref[c] = out.reshape(bsz, nj, LANES)
        return carry

    lax.fori_loop(0, CB_HY, channel, 0)


def _hyena(prm, u_t, g, s, bsz, seq):
    nj = seq // LANES
    nct = WIDTH // CB_HY
    blk = (CB_HY, bsz, nj, LANES)
    return pl.pallas_call(
        functools.partial(_hy_kernel, bsz, nj),
        grid=(nct,),
        in_specs=[
            pl.BlockSpec(memory_space=pltpu.SMEM),
            pl.BlockSpec(blk, lambda i: (i, 0, 0, 0)),
            pl.BlockSpec(blk, lambda i: (nct + i, 0, 0, 0)),
            pl.BlockSpec(blk, lambda i: (2 * nct + i, 0, 0, 0)),
            pl.BlockSpec(blk, lambda i: (3 * nct + i, 0, 0, 0)),
            pl.BlockSpec((HY_ORDER, CB_HY, 2 * nj, LANES), lambda i: (0, i, 0, 0)),
            pl.BlockSpec((HY_ORDER, CB_HY, s.shape[-1]), lambda i: (0, i, 0)),
        ],
        out_specs=pl.BlockSpec(blk, lambda i: (i, 0, 0, 0)),
        out_shape=jax.ShapeDtypeStruct((WIDTH, bsz, nj, LANES), F32),
        scratch_shapes=[
            pltpu.VMEM((2 * nj * LANES, LANES), BF16),
            pltpu.VMEM((bsz, 3 * nj + SUBLANES, LANES), F32),
            pltpu.VMEM((SUBLANES, bsz, 3 * nj, LANES), F32),
        ],
        name="hyena",
    )(prm, u_t, u_t, u_t, u_t, g, s)


def _split_hi_lo(x):
    hi = x.astype(BF16)
    lo = (x - hi.astype(F32)).astype(BF16)
    return hi, lo


def _split3_rows(x):
    hi, lo = _split_hi_lo(x)
    return jnp.concatenate([hi, lo, hi], axis=0)


def _split3_cols(x):
    hi, lo = _split_hi_lo(x)
    return jnp.concatenate([hi, lo, hi], axis=1)


def _np_split(m):
    hi = m.astype(ml_dtypes.bfloat16)
    lo = (m - hi.astype(np.float64)).astype(ml_dtypes.bfloat16)
    return hi, lo


def _dft_tables(seq):
    nj = seq // LANES
    n1 = 2 * nj
    n = n1 * LANES
    khp = n1 // 2 + KH_PAD
    k1 = np.arange(khp, dtype=np.float64)[:, None]
    valid = (k1 <= n1 // 2).astype(np.float64)
    ang = 2.0 * np.pi * k1 * np.arange(n1, dtype=np.float64)[None, :] / n1
    fa_full = np.concatenate([valid * np.cos(ang), -valid * np.sin(ang)], axis=0)

    def lhs3(m):
        hi, lo = _np_split(m)
        return jnp.asarray(np.concatenate([hi, hi, lo], axis=1))

    def rhs3(m):
        hi, lo = _np_split(m)
        return jnp.asarray(np.concatenate([hi, hi, lo], axis=0))

    n2 = np.arange(LANES, dtype=np.float64)
    ang2 = 2.0 * np.pi * n2[:, None] * n2[None, :] / LANES
    c2, s2 = np.cos(ang2), np.sin(ang2)
    fb = np.block([[c2, -s2], [s2, c2]])
    fbi = np.block([[c2, s2], [-s2, c2]])
    w = np.where((k1 == 0) | (k1 == n1 // 2), 1.0, 2.0) * valid
    th = 2.0 * np.pi * np.arange(nj, dtype=np.float64)[:, None] * k1.T / n1
    fai = np.concatenate([(w.T / n) * np.cos(th), -(w.T / n) * np.sin(th)], axis=1)
    phi = 2.0 * np.pi * k1 * n2[None, :] / n
    return dict(
        nj=nj, n1=n1, khp=khp,
        fa=lhs3(fa_full[:, :nj]), fa_full=lhs3(fa_full), fb=rhs3(fb), fbi=rhs3(fbi), fai=lhs3(fai),
        twc=jnp.asarray(valid * np.cos(phi), F32), tws=jnp.asarray(valid * np.sin(phi), F32))


def _dft_rows(fa_ref, twc, tws, khp, cols):
    rhs = cols[0] if len(cols) == 1 else jnp.concatenate(cols, axis=1)
    y = _dot(fa_ref[...], _split3_rows(rhs))
    out = []
    for i in range(len(cols)):
        yre = y[0:khp, i * LANES:(i + 1) * LANES]
        yim = y[khp:2 * khp, i * LANES:(i + 1) * LANES]
        out.append((yre * twc + yim * tws, yim * twc - yre * tws))
    return out


def _spec_kernel(khp, g_ref, s_ref, fa_ref, fb_ref, twc_ref, tws_ref, h_ref, ybuf):
    twc = twc_ref[...]
    tws = tws_ref[...]
    for c in range(0, CB_SPEC, 2):
        res = _dft_rows(fa_ref, twc, tws, khp, [g_ref[c], g_ref[c + 1]])
        for i, (yre, yim) in enumerate(res):
            r0 = (c + i) * khp
            ybuf[r0:r0 + khp, 0:LANES] = yre
            ybuf[r0:r0 + khp, LANES:2 * LANES] = yim
    x = _dot(_split3_cols(ybuf[...]), fb_ref[...])
    for c in range(CB_SPEC):
        norm = jnp.sum(s_ref[c:c + 1, :], axis=1, keepdims=True)
        h_ref[c] = x[c * khp:(c + 1) * khp, :] / norm


def _spectrum(g, s, tab):
    nch = g.shape[0]
    n1, khp = tab["n1"], tab["khp"]
    const = lambda a: pl.BlockSpec(a.shape, lambda i: (0, 0))
    return pl.pallas_call(
        functools.partial(_spec_kernel, khp),
        grid=(nch // CB_SPEC,),
        in_specs=[
            pl.BlockSpec((CB_SPEC, n1, LANES), lambda i: (i, 0, 0)),
            pl.BlockSpec((CB_SPEC, s.shape[-1]), lambda i: (i, 0)),
            const(tab["fa_full"]), const(tab["fb"]), const(tab["twc"]), const(tab["tws"]),
        ],
        out_specs=pl.BlockSpec((CB_SPEC, khp, 2 * LANES), lambda i: (i, 0, 0)),
        out_shape=jax.ShapeDtypeStruct((nch, khp, 2 * LANES), F32),
        scratch_shapes=[pltpu.VMEM((CB_SPEC * khp, 2 * LANES), F32)],
        name="hyena_spectrum",
    )(g, s, tab["fa_full"], tab["fb"], tab["twc"], tab["tws"])


def _hyfft_kernel(bsz, nj, khp, prm_ref, v_ref, x1_ref, x2_ref, z_ref, h_ref,
                  fa_ref, fb_ref, fbi_ref, fai_ref, twc_ref, tws_ref, out_ref,
                  xs_ref, ybuf, pbuf):
    ct = pl.program_id(0)
    rows = bsz * nj
    pairs = [(b, b + 1) for b in range(0, bsz, 2)] if bsz > 1 else [(0,)]
    twc = twc_ref[...]
    tws = tws_ref[...]

    lane = lax.broadcasted_iota(jnp.int32, (rows, LANES), 1)
    blk = lax.broadcasted_iota(jnp.int32, (rows, LANES), 0) & (nj - 1)
    first = (lane == 0) & (blk == 0)
    last = (lane == LANES - 1) & (blk == nj - 1)

    def conv3(x, base):
        r = pltpu.roll(x, 1, 1)
        xm1 = jnp.where(lane == 0, pltpu.roll(r, 1, 0), r)
        xm1 = jnp.where(first, 0.0, xm1)
        l = pltpu.roll(x, LANES - 1, 1)
        xp1 = jnp.where(lane == LANES - 1, pltpu.roll(l, rows - 1, 0), l)
        xp1 = jnp.where(last, 0.0, xp1)
        return prm_ref[base] * xm1 + prm_ref[base + 1] * x + prm_ref[base + 2] * xp1 + prm_ref[base + 3]

    def row0(c, b):
        return (c * bsz + b) * khp

    def forward(c, x):
        for pr in pairs:
            res = _dft_rows(fa_ref, twc, tws, khp, [x[b * nj:(b + 1) * nj, :] for b in pr])
            for b, (yre, yim) in zip(pr, res):
                r0 = row0(c, b)
                ybuf[r0:r0 + khp, 0:LANES] = yre
                ybuf[r0:r0 + khp, LANES:2 * LANES] = yim

    def spectral(order):
        x = _dot(_split3_cols(ybuf[...]), fb_ref[...])
        for c in range(CB_HY):
            hre = h_ref[order, c, :, 0:LANES]
            him = h_ref[order, c, :, LANES:2 * LANES]
            for b in range(bsz):
                r0 = row0(c, b)
                xre = x[r0:r0 + khp, 0:LANES]
                xim = x[r0:r0 + khp, LANES:2 * LANES]
                pbuf[r0:r0 + khp, 0:LANES] = xre * hre - xim * him
                pbuf[r0:r0 + khp, LANES:2 * LANES] = xre * him + xim * hre
        r = _dot(_split3_cols(pbuf[...]), fbi_ref[...])
        for c in range(CB_HY):
            for b in range(bsz):
                r0 = row0(c, b)
                rre = r[r0:r0 + khp, 0:LANES]
                rim = r[r0:r0 + khp, LANES:2 * LANES]
                ybuf[r0:r0 + khp, 0:LANES] = rre * twc - rim * tws
                ybuf[r0:r0 + khp, LANES:2 * LANES] = rre * tws + rim * twc

    def inverse(c):
        outs = [None] * bsz
        for pr in pairs:
            rre = [ybuf[row0(c, b):row0(c, b) + khp, 0:LANES] for b in pr]
            rim = [ybuf[row0(c, b):row0(c, b) + khp, LANES:2 * LANES] for b in pr]
            if len(pr) > 1:
                rre, rim = [jnp.concatenate(rre, axis=1)], [jnp.concatenate(rim, axis=1)]
            y = _dot(fai_ref[...], _split3_rows(jnp.concatenate([rre[0], rim[0]], axis=0)))
            for i, b in enumerate(pr):
                outs[b] = y[:, i * LANES:(i + 1) * LANES]
        return outs[0] if bsz == 1 else jnp.concatenate(outs, axis=0)

    for c in range(CB_HY):
        base = (ct * CB_HY + c) * 16
        v = conv3(v_ref[c].reshape(rows, LANES), base)
        xs_ref[0, c] = v
        xs_ref[1, c] = conv3(x1_ref[c].reshape(rows, LANES), base + 4)
        xs_ref[2, c] = conv3(x2_ref[c].reshape(rows, LANES), base + 8)
        forward(c, v)
    spectral(0)
    for c in range(CB_HY):
        base = (ct * CB_HY + c) * 16
        v = xs_ref[0, c]
        z1 = xs_ref[1, c] * (inverse(c) + prm_ref[base + 12] * v)
        xs_ref[0, c] = z1
        forward(c, z1)
    spectral(1)
    for c in range(CB_HY):
        base = (ct * CB_HY + c) * 16
        z1 = xs_ref[0, c]
        y2 = inverse(c) + prm_ref[base + 13] * z1
        out = xs_ref[2, c] * y2 * _silu(z_ref[c].reshape(rows, LANES))
        out_ref[c] = out.reshape(bsz, nj, LANES)


def _hyena_fft(prm, u_t, h, tab, bsz, seq):
    nj, khp = tab["nj"], tab["khp"]
    nct = WIDTH // CB_HY
    blk = (CB_HY, bsz, nj, LANES)
    const = lambda a: pl.BlockSpec(a.shape, lambda i: (0, 0))
    return pl.pallas_call(
        functools.partial(_hyfft_kernel, bsz, nj, khp),
        grid=(nct,),
        in_specs=[
            pl.BlockSpec(memory_space=pltpu.SMEM),
            pl.BlockSpec(blk, lambda i: (i, 0, 0, 0)),
            pl.BlockSpec(blk, lambda i: (nct + i, 0, 0, 0)),
            pl.BlockSpec(blk, lambda i: (2 * nct + i, 0, 0, 0)),
            pl.BlockSpec(blk, lambda i: (3 * nct + i, 0, 0, 0)),
            pl.BlockSpec((HY_ORDER, CB_HY, khp, 2 * LANES), lambda i: (0, i, 0, 0)),
            const(tab["fa"]), const(tab["fb"]), const(tab["fbi"]), const(tab["fai"]),
            const(tab["twc"]), const(tab["tws"]),
        ],
        out_specs=pl.BlockSpec(blk, lambda i: (i, 0, 0, 0)),
        out_shape=jax.ShapeDtypeStruct((WIDTH, bsz, nj, LANES), F32),
        scratch_shapes=[
            pltpu.VMEM((3, CB_HY, bsz * nj, LANES), F32),
            pltpu.VMEM((CB_HY * bsz * khp, 2 * LANES), F32),
            pltpu.VMEM((CB_HY * bsz * khp, 2 * LANES), F32),
        ],
        name="hyena_fft",
    )(prm, u_t, u_t, u_t, u_t, h, tab["fa"], tab["fb"], tab["fbi"], tab["fai"],
      tab["twc"], tab["tws"])


def _out_kernel(final, x_ref, yml_ref, yrt_ref, yhyt_ref, w_ref, g_ref, o_ref):
    acc = _dot(yml_ref[...].astype(BF16), w_ref[0:WIDTH, :])
    acc = acc + _dot(yrt_ref[...].astype(BF16), w_ref[WIDTH:2 * WIDTH, :])
    acc = acc + _dot_tn(yhyt_ref[...].astype(BF16), w_ref[2 * WIDTH:, :])
    x = x_ref[...] + acc
    if final:
        ms = jnp.mean(x * x, axis=-1, keepdims=True)
        x = x * lax.rsqrt(ms + RMS_EPS) * g_ref[...]
    o_ref[...] = x


def _out_proj(x, y_ml, y_rt, y_hyt, w, gain, final):
    t = x.shape[0]
    return pl.pallas_call(
        functools.partial(_out_kernel, final),
        grid=(t // TM_OUT,),
        in_specs=[
            pl.BlockSpec((TM_OUT, D_MODEL), lambda i: (i, 0)),
            pl.BlockSpec((TM_OUT, WIDTH), lambda i: (i, 0)),
            pl.BlockSpec((TM_OUT, WIDTH), lambda i: (i, 0)),
            pl.BlockSpec((WIDTH, TM_OUT), lambda i: (0, i)),
            pl.BlockSpec((3 * WIDTH, D_MODEL), lambda i: (0, 0)),
            pl.BlockSpec((1, D_MODEL), lambda i: (0, 0)),
        ],
        out_specs=pl.BlockSpec((TM_OUT, D_MODEL), lambda i: (i, 0)),
        out_shape=jax.ShapeDtypeStruct((t, D_MODEL), F32),
        name="out_proj",
    )(x, y_ml, y_rt, y_hyt, w, gain)


def kernel(x_prompt, x_sample, norm_g, w_in, ml_conv_w, ml_conv_b, ml_gate_b, ml_norm_g, rt_norm_g, hy_conv_w, hy_conv_b, hy_w1, hy_b1, hy_w2, hy_b2, hy_w3, hy_freq, hy_deltas, hy_skip, w_out, final_g):
    groups = []
    row0 = 0
    for xg in (x_prompt, x_sample):
        bsz, seq, _ = xg.shape
        groups.append((row0, bsz, seq))
        row0 += bsz * seq
    x = jnp.concatenate([x_prompt.reshape(-1, D_MODEL), x_sample.reshape(-1, D_MODEL)], axis=0)

    c_gate = ML_COLS
    c_rt = c_gate + 4 * HEADS
    c_hy = c_rt + RT_COLS
    gate_pad = jnp.zeros((DEPTH, D_MODEL, LANES - 4 * HEADS), F32)
    w_tok = jnp.concatenate([w_in[:, :, :c_gate], w_in[:, :, c_gate:c_rt], gate_pad,
                             w_in[:, :, c_rt:c_hy]], axis=2).astype(BF16)
    w_gt = jnp.swapaxes(w_in[:, :, c_gate:c_rt], 1, 2).astype(BF16)
    w_hyt = jnp.swapaxes(w_in[:, :, c_hy:], 1, 2).astype(BF16)
    w_out_b = w_out.astype(BF16)
    gate_b = ml_gate_b.reshape(DEPTH, 1, 4 * HEADS)
    gate_b_row = jnp.concatenate([gate_b, jnp.zeros((DEPTH, 1, LANES - 4 * HEADS), F32)], axis=2)
    gate_b_col = ml_gate_b.reshape(DEPTH, 4 * HEADS, 1)

    cw = hy_conv_w.reshape(DEPTH, 3, 3, WIDTH)
    cbias = hy_conv_b.reshape(DEPTH, 1, 3, WIDTH)
    taps = jnp.concatenate([cw, cbias], axis=1)
    taps = jnp.transpose(taps, (0, 3, 2, 1)).reshape(DEPTH, WIDTH, 12)
    hy_prm = jnp.concatenate([taps, jnp.transpose(hy_skip, (0, 2, 1)),
                              jnp.zeros((DEPTH, WIDTH, 2), F32)], axis=2).reshape(DEPTH, WIDTH * 16)

    w1t = jnp.concatenate([jnp.swapaxes(hy_w1, 1, 2),
                           jnp.zeros((DEPTH, HY_HIDDEN, FEAT_ROWS - hy_w1.shape[1]), F32)], axis=2)
    w2t = jnp.swapaxes(hy_w2, 1, 2)
    w3t = jnp.swapaxes(hy_w3, 1, 2)
    b1 = hy_b1[:, :, None]
    b2 = hy_b2[:, :, None]
    freq = hy_freq[:, :, None]
    deltas = hy_deltas.reshape(DEPTH, HY_ORDER * 2 * WIDTH, 1)

    tables, spectra = [], []
    for (_, _, seq) in groups:
        tab = _dft_tables(seq)
        gb, gf, s = _filters(seq, w1t, b1, w2t, b2, freq, w3t, deltas)
        g = jnp.concatenate([gf, gb], axis=-1).reshape(DEPTH * HY_ORDER * WIDTH, tab["n1"], LANES)
        h = _spectrum(g, s.reshape(DEPTH * HY_ORDER * WIDTH, -1), tab)
        tables.append(tab)
        spectra.append(h.reshape(DEPTH, HY_ORDER, WIDTH, tab["khp"], 2 * LANES))
    rot = [_rotary_tables(seq) for (_, _, seq) in groups]
    dec, dvec = _retention_tables()

    for layer in range(DEPTH):
        u_ml, u_g, u_rt, u_gt, u_hyt = _in_proj(x, norm_g[layer][None, :], w_tok[layer],
                                                w_gt[layer], w_hyt[layer])
        gc, gr = _gates(u_g, u_gt, gate_b_row[layer], gate_b_col[layer])
        y_ml, y_rt, y_hy = [], [], []
        for gi, (r0, bsz, seq) in enumerate(groups):
            y_ml.append(_mlstm(u_ml, gc, gr, ml_conv_w[layer], ml_conv_b[layer][None, :],
                               ml_norm_g[layer][None, :], r0, bsz, seq))
            y_rt.append(_retention(u_rt, rot[gi][0], rot[gi][1], dec, dvec,
                                   rt_norm_g[layer][None, :], r0, bsz, seq))
            u_t = u_hyt[:, r0:r0 + bsz * seq].reshape(HY_ROWS, bsz, seq // LANES, LANES)
            y = _hyena_fft(hy_prm[layer], u_t, spectra[gi][layer], tables[gi], bsz, seq)
            y_hy.append(y.reshape(WIDTH, bsz * seq))
        x = _out_proj(x, jnp.concatenate(y_ml, axis=0), jnp.concatenate(y_rt, axis=0),
                      jnp.concatenate(y_hy, axis=1), w_out_b[layer], final_g[None, :],
                      layer == DEPTH - 1)

    n_prompt = x_prompt.shape[0] * x_prompt.shape[1]
    return x[:n_prompt].reshape(x_prompt.shape), x[n_prompt:].reshape(x_sample.shape)
```

```python
import functools
import math

import ml_dtypes
import numpy as np
import jax
import jax.numpy as jnp
from jax import lax
from jax.experimental import pallas as pl
from jax.experimental.pallas import tpu as pltpu

F32 = jnp.float32
BF16 = jnp.bfloat16

D_MODEL = 1024
DEPTH = 4
CHUNK = 128
HEADS = 4
HEAD_DIM = 128
WIDTH = 512
HY_ORDER = 2
HY_BANDS = 16
HY_HIDDEN = 64
FEAT_ROWS = 40
ROPE_BASE = 10000.0
RMS_EPS = 1e-6
HEAD_NORM_EPS = 1e-5
M_INIT = -1e30
RT_LOG_GAMMA_FWD = tuple(math.log(1.0 - 2.0 ** (-5.0 - h)) for h in range(HEADS))
RT_LOG_GAMMA_BWD = tuple(math.log(1.0 - 2.0 ** (-5.5 - h)) for h in range(HEADS))

LANES = 128
SUBLANES = 8
TM_IN = 256
TM_OUT = 512
TB_SEQ = 512
TG = 1024
CB_HY = 8
CB_SPEC = 8
KH_PAD = 16
LT_FILT = 2048
ML_COLS = 5 * WIDTH
RT_COLS = 4 * WIDTH
HY_ROWS = 4 * WIDTH
HIGHEST = lax.Precision.HIGHEST


def _dot(a, b):
    return jnp.dot(a, b, preferred_element_type=F32)


def _dot_nt(a, b):
    return lax.dot_general(a, b, (((1,), (1,)), ((), ())), preferred_element_type=F32)


def _dot_tn(a, b):
    return lax.dot_general(a, b, (((0,), (0,)), ((), ())), preferred_element_type=F32)


def _dot_f32(a, b):
    return jnp.dot(a, b, precision=HIGHEST, preferred_element_type=F32)


def _silu(y):
    return y * jax.nn.sigmoid(y)


def _log_sigmoid(x):
    return -(jnp.maximum(-x, 0.0) + jnp.log1p(jnp.exp(-jnp.abs(x))))


def _head_norm(h, gain):
    mu = jnp.mean(h, axis=-1, keepdims=True)
    hc = h - mu
    var = jnp.mean(hc * hc, axis=-1, keepdims=True)
    return hc * lax.rsqrt(var + HEAD_NORM_EPS) * gain


def _in_kernel(x_ref, g_ref, wtok_ref, wgt_ref, whyt_ref,
               oml_ref, og_ref, ort_ref, ogt_ref, ohyt_ref):
    x = x_ref[...]
    ms = jnp.mean(x * x, axis=-1, keepdims=True)
    h = (x * lax.rsqrt(ms + RMS_EPS) * g_ref[...]).astype(BF16)
    oml_ref[...] = _dot(h, wtok_ref[:, 0:ML_COLS])
    og_ref[...] = _dot(h, wtok_ref[:, ML_COLS:ML_COLS + LANES])
    ort_ref[...] = _dot(h, wtok_ref[:, ML_COLS + LANES:])
    ogt_ref[...] = _dot_nt(wgt_ref[...], h)
    ohyt_ref[...] = _dot_nt(whyt_ref[...], h)


def _in_proj(x, gain, w_tok, w_gt, w_hyt):
    t = x.shape[0]
    ncol = w_tok.shape[1]
    return pl.pallas_call(
        _in_kernel,
        grid=(t // TM_IN,),
        in_specs=[
            pl.BlockSpec((TM_IN, D_MODEL), lambda i: (i, 0)),
            pl.BlockSpec((1, D_MODEL), lambda i: (0, 0)),
            pl.BlockSpec((D_MODEL, ncol), lambda i: (0, 0)),
            pl.BlockSpec((4 * HEADS, D_MODEL), lambda i: (0, 0)),
            pl.BlockSpec((HY_ROWS, D_MODEL), lambda i: (0, 0)),
        ],
        out_specs=[
            pl.BlockSpec((TM_IN, ML_COLS), lambda i: (i, 0)),
            pl.BlockSpec((TM_IN, LANES), lambda i: (i, 0)),
            pl.BlockSpec((TM_IN, RT_COLS), lambda i: (i, 0)),
            pl.BlockSpec((4 * HEADS, TM_IN), lambda i: (0, i)),
            pl.BlockSpec((HY_ROWS, TM_IN), lambda i: (0, i)),
        ],
        out_shape=[
            jax.ShapeDtypeStruct((t, ML_COLS), F32),
            jax.ShapeDtypeStruct((t, LANES), F32),
            jax.ShapeDtypeStruct((t, RT_COLS), F32),
            jax.ShapeDtypeStruct((4 * HEADS, t), F32),
            jax.ShapeDtypeStruct((HY_ROWS, t), F32),
        ],
        name="in_proj",
    )(x, gain, w_tok, w_gt, w_hyt)


def _gates_kernel(g_ref, gt_ref, b_ref, bt_ref, pb_ref, cb_ref, gr_ref, gs_ref):
    row = lax.broadcasted_iota(jnp.int32, (CHUNK, CHUNK), 0)
    col = lax.broadcasted_iota(jnp.int32, (CHUNK, CHUNK), 1)
    lower = (row >= col).astype(F32)
    upper = (row <= col).astype(F32)
    gr_ref[...] = jnp.zeros_like(gr_ref)
    gs_ref[...] = jnp.zeros_like(gs_ref)
    for c in range(TG // CHUNK):
        sl = slice(c * CHUNK, (c + 1) * CHUNK)
        lf = _log_sigmoid(g_ref[sl, :] + b_ref[...])
        cum_c = (_dot_f32(lower, lf), _dot_f32(upper, lf))
        pre_t = gt_ref[:, sl] + bt_ref[...]
        lf_t = _log_sigmoid(pre_t)
        cum_r = (_dot_f32(lf_t, upper), _dot_f32(lf_t, lower))
        for d in range(2):
            for h in range(HEADS):
                kf = (2 * d + 1) * HEADS + h
                ki = 2 * d * HEADS + h
                cum = cum_r[d][kf:kf + 1, :]
                li = pre_t[ki:ki + 1, :]
                r = li - cum
                f_tot = cum[:, 0:1] if d else cum[:, CHUNK - 1:CHUNK]
                g = f_tot - cum + li
                allowed = (row <= col) if d else (row >= col)
                pmax = jnp.max(jnp.where(allowed, r, -jnp.inf), axis=1, keepdims=True)
                pb_ref[d, h, sl, :] = jnp.broadcast_to(pmax, (CHUNK, LANES))
                cb_ref[d, h, sl, :] = jnp.broadcast_to(cum_c[d][:, kf:kf + 1], (CHUNK, LANES))
                gr_ref[d, h, 0:1, sl] = r
                gr_ref[d, h, 1:2, sl] = g
                gs_ref[d, h, c, 0:1, :] = jnp.broadcast_to(f_tot, (1, LANES))
                gs_ref[d, h, c, 1:2, :] = jnp.broadcast_to(jnp.max(g, axis=1, keepdims=True), (1, LANES))


def _gates(g, gt, bias_row, bias_col):
    t = g.shape[0]
    nc = TG // CHUNK
    return pl.pallas_call(
        _gates_kernel,
        grid=(t // TG,),
        in_specs=[
            pl.BlockSpec((TG, LANES), lambda i: (i, 0)),
            pl.BlockSpec((4 * HEADS, TG), lambda i: (0, i)),
            pl.BlockSpec((1, LANES), lambda i: (0, 0)),
            pl.BlockSpec((4 * HEADS, 1), lambda i: (0, 0)),
        ],
        out_specs=[
            pl.BlockSpec((2, HEADS, TG, LANES), lambda i: (0, 0, i, 0)),
            pl.BlockSpec((2, HEADS, TG, LANES), lambda i: (0, 0, i, 0)),
            pl.BlockSpec((2, HEADS, SUBLANES, TG), lambda i: (0, 0, 0, i)),
            pl.BlockSpec((2, HEADS, nc, SUBLANES, LANES), lambda i: (0, 0, i, 0, 0)),
        ],
        out_shape=[
            jax.ShapeDtypeStruct((2, HEADS, t, LANES), F32),
            jax.ShapeDtypeStruct((2, HEADS, t, LANES), F32),
            jax.ShapeDtypeStruct((2, HEADS, SUBLANES, t), F32),
            jax.ShapeDtypeStruct((2, HEADS, t // CHUNK, SUBLANES, LANES), F32),
        ],
        name="gates",
    )(g, gt, bias_row, bias_col)


def _ml_kernel(rev, nb, *refs):
    if rev:
        (q_ref, k_ref, v_ref, qp_ref, qn_ref, kp_ref, kn_ref, pb_ref, cb_ref, gr_ref, gs_ref,
         cw_ref, cbias_ref, o_ref, z_ref, hf_ref, ng_ref,
         out_ref, st_ref, m_ref, stg_ref, qs_ref, ks_ref) = refs
    else:
        (q_ref, k_ref, v_ref, qp_ref, qn_ref, kp_ref, kn_ref, pb_ref, cb_ref, gr_ref, gs_ref,
         cw_ref, cbias_ref, out_ref, st_ref, m_ref, stg_ref, qs_ref, ks_ref) = refs
    tb = TB_SEQ
    nch = tb // CHUNK
    n = pl.program_id(1)
    sb = (nb - 1 - n) if rev else n

    @pl.when(n == 0)
    def _():
        st_ref[...] = jnp.zeros_like(st_ref)
        m_ref[...] = jnp.full_like(m_ref, M_INIT)

    def conv_silu(raw_ref, prev_ref, next_ref, c0):
        stg_ref[SUBLANES:SUBLANES + tb, :] = raw_ref[...]
        stg_ref[SUBLANES - 1:SUBLANES, :] = jnp.where(sb > 0, prev_ref[SUBLANES - 1:SUBLANES, :], 0.0)
        stg_ref[SUBLANES + tb:SUBLANES + tb + 1, :] = jnp.where(sb < nb - 1, next_ref[0:1, :], 0.0)
        w = cw_ref[:, c0:c0 + WIDTH]
        y = (stg_ref[SUBLANES - 1:SUBLANES - 1 + tb, :] * w[0:1, :]
             + stg_ref[SUBLANES:SUBLANES + tb, :] * w[1:2, :]
             + stg_ref[SUBLANES + 1:SUBLANES + 1 + tb, :] * w[2:3, :] + cbias_ref[:, c0:c0 + WIDTH])
        return _silu(y)

    qs_ref[...] = conv_silu(q_ref, qp_ref, qn_ref, 0).astype(BF16)
    ks_ref[...] = conv_silu(k_ref, kp_ref, kn_ref, WIDTH) * (HEAD_DIM ** -0.5)

    row = lax.broadcasted_iota(jnp.int32, (CHUNK, CHUNK), 0)
    col = lax.broadcasted_iota(jnp.int32, (CHUNK, CHUNK), 1)
    allowed = (row <= col) if rev else (row >= col)
    ones = jnp.ones((CHUNK, HEAD_DIM), BF16)

    def chunk(c, h):
        sl = pl.ds(pl.multiple_of(c * CHUNK, CHUNK), CHUNK)
        hs = slice(h * HEAD_DIM, (h + 1) * HEAD_DIM)
        q = qs_ref[sl, hs]
        kt = ks_ref[sl, hs].T
        v2 = jnp.concatenate([v_ref[sl, hs].astype(BF16), ones], axis=1)
        m_prev = m_ref[h, 0:1, :]
        mb = jnp.maximum(pb_ref[h, sl, :], m_prev)
        p = jnp.exp(jnp.where(allowed, gr_ref[h, 0:1, sl] - mb, -jnp.inf))
        w_inter = jnp.exp(m_prev - mb)
        floor = jnp.exp(-(cb_ref[h, sl, :] + mb))
        s = _dot(q, kt.astype(BF16)) * p
        inter = _dot(q, st_ref[h].astype(BF16))
        num2 = _dot(s.astype(BF16), v2) + jnp.concatenate([w_inter, w_inter], axis=1) * inter
        out = num2[:, 0:HEAD_DIM] / jnp.maximum(jnp.abs(num2[:, HEAD_DIM:]), floor)
        f_tot = gs_ref[h, c, 0:1, :]
        m_chunk = gs_ref[h, c, 1:2, :]
        kw = (kt * jnp.exp(gr_ref[h, 1:2, sl] - m_chunk)).astype(BF16)
        kv2 = _dot(kw, v2)
        m_new = jnp.maximum(f_tot + m_prev, m_chunk)
        a = jnp.exp(f_tot + m_prev - m_new)
        b = jnp.exp(m_chunk - m_new)
        st_ref[h] = (jnp.concatenate([a, a], axis=1) * st_ref[h]
                     + jnp.concatenate([b, b], axis=1) * kv2)
        m_ref[h] = jnp.broadcast_to(m_new, (SUBLANES, LANES))
        return out, sl, hs

    def body(i, carry):
        c = (nch - 1 - i) if rev else i
        for h in range(HEADS):
            out, sl, hs = chunk(c, h)
            if rev:
                hh = jax.nn.sigmoid(o_ref[sl, hs]) * (out + hf_ref[sl, hs])
                out_ref[sl, hs] = _head_norm(hh, ng_ref[:, hs]) * _silu(z_ref[sl, hs])
            else:
                out_ref[sl, hs] = out
        return carry

    lax.fori_loop(0, nch, body, 0, unroll=True)


def _mlstm_dir(rev, u_ml, pb, cb, gr, gs, conv_w, conv_b, norm_g, h_fwd, row0, bsz, seq):
    t = u_ml.shape[0]
    tb = TB_SEQ
    nb = seq // tb
    blk0 = row0 // tb
    t8 = tb // SUBLANES
    nch = tb // CHUNK
    d = 1 if rev else 0

    def rb(b, n):
        return blk0 + b * nb + ((nb - 1 - n) if rev else n)

    def col(g):
        return lambda b, n: (rb(b, n), g)

    def halo_prev(g):
        return lambda b, n: (jnp.maximum(rb(b, n) * t8 - 1, 0), g)

    def halo_next(g):
        return lambda b, n: (jnp.minimum((rb(b, n) + 1) * t8, t // SUBLANES - 1), g)

    blk = (tb, WIDTH)
    halo = (SUBLANES, WIDTH)
    local = lambda b, n: (rb(b, n) - blk0, 0)
    in_specs = [
        pl.BlockSpec(blk, col(0)), pl.BlockSpec(blk, col(1)), pl.BlockSpec(blk, col(2)),
        pl.BlockSpec(halo, halo_prev(0)), pl.BlockSpec(halo, halo_next(0)),
        pl.BlockSpec(halo, halo_prev(1)), pl.BlockSpec(halo, halo_next(1)),
        pl.BlockSpec((None, HEADS, tb, LANES), lambda b, n: (d, 0, rb(b, n), 0)),
        pl.BlockSpec((None, HEADS, tb, LANES), lambda b, n: (d, 0, rb(b, n), 0)),
        pl.BlockSpec((None, HEADS, SUBLANES, tb), lambda b, n: (d, 0, 0, rb(b, n))),
        pl.BlockSpec((None, HEADS, nch, SUBLANES, LANES), lambda b, n: (d, 0, rb(b, n), 0, 0)),
        pl.BlockSpec((3, 2 * WIDTH), lambda b, n: (0, 0)),
        pl.BlockSpec((1, 2 * WIDTH), lambda b, n: (0, 0)),
    ]
    args = [u_ml, u_ml, u_ml, u_ml, u_ml, u_ml, u_ml, pb, cb, gr, gs, conv_w, conv_b]
    if rev:
        in_specs += [pl.BlockSpec(blk, col(3)), pl.BlockSpec(blk, col(4)), pl.BlockSpec(blk, local),
                     pl.BlockSpec((1, WIDTH), lambda b, n: (0, 0))]
        args += [u_ml, u_ml, h_fwd, norm_g]
    return pl.pallas_call(
        functools.partial(_ml_kernel, rev, nb),
        grid=(bsz, nb),
        in_specs=in_specs,
        out_specs=pl.BlockSpec(blk, local),
        out_shape=jax.ShapeDtypeStruct((bsz * seq, WIDTH), F32),
        scratch_shapes=[
            pltpu.VMEM((HEADS, HEAD_DIM, 2 * HEAD_DIM), F32),
            pltpu.VMEM((HEADS, SUBLANES, LANES), F32),
            pltpu.VMEM((tb + 2 * SUBLANES, WIDTH), F32),
            pltpu.VMEM((tb, WIDTH), BF16),
            pltpu.VMEM((tb, WIDTH), F32),
        ],
        name="mlstm_bwd" if rev else "mlstm_fwd",
    )(*args)


def _mlstm(u_ml, pb, cb, gr, gs, conv_w, conv_b, norm_g, row0, bsz, seq):
    h_fwd = _mlstm_dir(False, u_ml, pb, cb, gr, gs, conv_w, conv_b, norm_g, None, row0, bsz, seq)
    return _mlstm_dir(True, u_ml, pb, cb, gr, gs, conv_w, conv_b, norm_g, h_fwd, row0, bsz, seq)


def _rt_kernel(rev, *refs):
    if rev:
        (q_ref, k_ref, v_ref, cc_ref, ss_ref, dec_ref, qd_ref, kd_ref, z_ref, hf_ref, ng_ref,
         out_ref, st_ref, qs_ref, ks_ref) = refs
    else:
        (q_ref, k_ref, v_ref, cc_ref, ss_ref, dec_ref, qd_ref, kd_ref,
         out_ref, st_ref, qs_ref, ks_ref) = refs
    nch = TB_SEQ // CHUNK
    n = pl.program_id(1)

    @pl.when(n == 0)
    def _():
        st_ref[...] = jnp.zeros_like(st_ref)

    for h in range(HEADS):
        hs = slice(h * HEAD_DIM, (h + 1) * HEAD_DIM)
        cc = cc_ref[...]
        ss = ss_ref[...]
        q = q_ref[:, hs]
        k = k_ref[:, hs]
        qs_ref[:, hs] = (q * cc + pltpu.roll(q, HEAD_DIM // 2, 1) * ss) * (HEAD_DIM ** -0.5)
        ks_ref[:, hs] = k * cc + pltpu.roll(k, HEAD_DIM // 2, 1) * ss

    def body(i, carry):
        c = (nch - 1 - i) if rev else i
        sl = pl.ds(pl.multiple_of(c * CHUNK, CHUNK), CHUNK)
        for h in range(HEADS):
            hs = slice(h * HEAD_DIM, (h + 1) * HEAD_DIM)
            q = qs_ref[sl, hs]
            kt = ks_ref[sl, hs].T
            vb = v_ref[sl, hs].astype(BF16)
            s = _dot(q.astype(BF16), kt.astype(BF16)) * dec_ref[h]
            out = _dot(s.astype(BF16), vb) + _dot((q * qd_ref[h]).astype(BF16), st_ref[h].astype(BF16))
            st_ref[h] = kd_ref[h, 1:2, :] * st_ref[h] + _dot((kt * kd_ref[h, 0:1, :]).astype(BF16), vb)
            if rev:
                out_ref[sl, hs] = _head_norm(out + hf_ref[sl, hs], ng_ref[:, hs]) * _silu(z_ref[sl, hs])
            else:
                out_ref[sl, hs] = out
        return carry

    lax.fori_loop(0, nch, body, 0, unroll=not rev)


def _retention_dir(rev, u_rt, rot_c, rot_s, dec, qdec, kdec, norm_g, h_fwd, row0, bsz, seq):
    tb = TB_SEQ
    nb = seq // tb
    blk0 = row0 // tb
    d = 1 if rev else 0

    def sb(n):
        return (nb - 1 - n) if rev else n

    def rb(b, n):
        return blk0 + b * nb + sb(n)

    blk = (tb, WIDTH)
    local = lambda b, n: (rb(b, n) - blk0, 0)
    tbl = lambda b, n: (d, 0, 0, 0)
    in_specs = [
        pl.BlockSpec(blk, lambda b, n: (rb(b, n), 0)),
        pl.BlockSpec(blk, lambda b, n: (rb(b, n), 1)),
        pl.BlockSpec(blk, lambda b, n: (rb(b, n), 2)),
        pl.BlockSpec((tb, HEAD_DIM), lambda b, n: (sb(n), 0)),
        pl.BlockSpec((tb, HEAD_DIM), lambda b, n: (sb(n), 0)),
        pl.BlockSpec((None, HEADS, CHUNK, CHUNK), tbl),
        pl.BlockSpec((None, HEADS, CHUNK, HEAD_DIM), tbl),
        pl.BlockSpec((None, HEADS, SUBLANES, CHUNK), tbl),
    ]
    args = [u_rt, u_rt, u_rt, rot_c, rot_s, dec, qdec, kdec]
    if rev:
        in_specs += [pl.BlockSpec(blk, lambda b, n: (rb(b, n), 3)), pl.BlockSpec(blk, local),
                     pl.BlockSpec((1, WIDTH), lambda b, n: (0, 0))]
        args += [u_rt, h_fwd, norm_g]
    return pl.pallas_call(
        functools.partial(_rt_kernel, rev),
        grid=(bsz, nb),
        in_specs=in_specs,
        out_specs=pl.BlockSpec(blk, local),
        out_shape=jax.ShapeDtypeStruct((bsz * seq, WIDTH), F32),
        scratch_shapes=[
            pltpu.VMEM((HEADS, HEAD_DIM, HEAD_DIM), F32),
            pltpu.VMEM((tb, WIDTH), F32),
            pltpu.VMEM((tb, WIDTH), F32),
        ],
        name="retention_bwd" if rev else "retention_fwd",
    )(*args)


def _retention(u_rt, rot_c, rot_s, tabs, norm_g, row0, bsz, seq):
    dec, qdec, kdec = tabs
    h_fwd = _retention_dir(False, u_rt, rot_c, rot_s, dec, qdec, kdec, norm_g, None, row0, bsz, seq)
    return _retention_dir(True, u_rt, rot_c, rot_s, dec, qdec, kdec, norm_g, h_fwd, row0, bsz, seq)


def _rotary_tables(seq):
    inv = ROPE_BASE ** (-jnp.arange(0, HEAD_DIM, 2, dtype=F32) / HEAD_DIM)
    ang = jnp.arange(seq, dtype=F32)[:, None] * inv[None, :]
    cos, sin = jnp.cos(ang), jnp.sin(ang)
    return jnp.concatenate([cos, cos], axis=1), jnp.concatenate([-sin, sin], axis=1)


def _retention_tables():
    pos = np.arange(CHUNK, dtype=np.float64)
    rel = pos[:, None] - pos[None, :]
    dec = np.zeros((2, HEADS, CHUNK, CHUNK))
    qdec = np.zeros((2, HEADS, CHUNK, HEAD_DIM))
    kdec = np.zeros((2, HEADS, SUBLANES, CHUNK))
    for h in range(HEADS):
        lf, lb = RT_LOG_GAMMA_FWD[h], RT_LOG_GAMMA_BWD[h]
        dec[0, h] = np.where(rel >= 0, np.exp(np.maximum(rel, 0.0) * lf), 0.0)
        dec[1, h] = np.where(rel <= 0, np.exp(np.maximum(-rel, 0.0) * lb), 0.0)
        qdec[0, h] = np.exp((pos + 1.0) * lf)[:, None]
        qdec[1, h] = np.exp((CHUNK - pos) * lb)[:, None]
        kdec[0, h, 0] = np.exp((CHUNK - 1.0 - pos) * lf)
        kdec[1, h, 0] = np.exp(pos * lb)
        kdec[0, h, 1] = np.exp(CHUNK * lf)
        kdec[1, h, 1] = np.exp(CHUNK * lb)
    return jnp.asarray(dec, F32), jnp.asarray(qdec, F32), jnp.asarray(kdec, F32)


def _filt_kernel(ft_ref, fr_ref, f0_ref, w1_ref, b1_ref, w2_ref, b2_ref, fq_ref,
                 w3f_ref, w3b_ref, dlf_ref, dlb_ref, gb_ref, gf_ref, s_ref, hf3_ref, hr3_ref):
    lt = pl.program_id(1)
    fq = fq_ref[...]

    def hidden(ft):
        h1 = jnp.sin(fq * (_dot_f32(w1_ref[...], ft) + b1_ref[...]))
        return jnp.sin(fq * (_dot_f32(w2_ref[...], h1) + b2_ref[...]))

    @pl.when((pl.program_id(2) == 0) & (pl.program_id(3) == 0))
    def _():
        hf3_ref[...] = _split3_rows(hidden(ft_ref[...]))
        hr3_ref[...] = _split3_rows(hidden(fr_ref[...]))

    def lhs3(w):
        hi, lo = _split_hi_lo(w)
        return jnp.concatenate([hi, hi, lo], axis=1)

    ft = ft_ref[...]
    fr = fr_ref[...]
    dl_f = jnp.abs(dlf_ref[...])
    dl_b = jnp.abs(dlb_ref[...])
    fwd = _dot(lhs3(w3f_ref[...]), hf3_ref[...]) * jnp.exp(-ft[0:1, :] * dl_f)
    bwd = _dot(lhs3(w3b_ref[...]), hr3_ref[...]) * jnp.exp(-fr[0:1, :] * dl_b)
    f0 = f0_ref[...]
    bwd0 = (_dot_f32(w3b_ref[...], hidden(f0)) * jnp.exp(-f0[0:1, :] * dl_b))[:, 0:1]
    lag = lax.broadcasted_iota(jnp.int32, fwd.shape, 1) + lt * LT_FILT
    fwd = jnp.where(lag == 0, fwd + bwd0, fwd)
    bwd = jnp.where(lag == 0, 0.0, bwd)
    gb_ref[...] = bwd
    gf_ref[...] = fwd
    part = jnp.sum(jnp.abs(fwd) + jnp.abs(bwd), axis=1, keepdims=True)
    lane = lax.broadcasted_iota(jnp.int32, (LANES, LANES), 1)
    s_ref[...] = jnp.where(lane == 0, part, 0.0)


def _filters(seq, w1t, b1, w2t, b2, freq, w3t, deltas):
    t = jnp.linspace(0.0, 1.0, seq, dtype=F32)[None, :]
    w = (2.0 * math.pi / seq) * jnp.arange(seq, dtype=F32)[None, :]
    bands = jnp.linspace(1e-4, HY_BANDS - 1, HY_BANDS, dtype=F32)[:, None]
    feats = jnp.concatenate([t, jnp.cos(bands * w), -jnp.sin(bands * w),
                             jnp.zeros((FEAT_ROWS - 1 - 2 * HY_BANDS, seq), F32)], axis=0)
    feats_rev = jnp.concatenate([feats[:, :1], feats[:, :0:-1]], axis=1)
    feats0 = jnp.broadcast_to(feats[:, :1], (FEAT_ROWS, LANES))
    nlt = seq // LT_FILT
    ct = WIDTH // LANES
    lyr = lambda l, i, o, c: (l, 0, 0)
    return pl.pallas_call(
        _filt_kernel,
        grid=(DEPTH, nlt, HY_ORDER, ct),
        in_specs=[
            pl.BlockSpec((FEAT_ROWS, LT_FILT), lambda l, i, o, c: (0, i)),
            pl.BlockSpec((FEAT_ROWS, LT_FILT), lambda l, i, o, c: (0, i)),
            pl.BlockSpec((FEAT_ROWS, LANES), lambda l, i, o, c: (0, 0)),
            pl.BlockSpec((None, HY_HIDDEN, FEAT_ROWS), lyr),
            pl.BlockSpec((None, HY_HIDDEN, 1), lyr),
            pl.BlockSpec((None, HY_HIDDEN, HY_HIDDEN), lyr),
            pl.BlockSpec((None, HY_HIDDEN, 1), lyr),
            pl.BlockSpec((None, HY_HIDDEN, 1), lyr),
            pl.BlockSpec((None, LANES, HY_HIDDEN), lambda l, i, o, c: (l, o * 2 * ct + c, 0)),
            pl.BlockSpec((None, LANES, HY_HIDDEN), lambda l, i, o, c: (l, o * 2 * ct + ct + c, 0)),
            pl.BlockSpec((None, LANES, 1), lambda l, i, o, c: (l, o * 2 * ct + c, 0)),
            pl.BlockSpec((None, LANES, 1), lambda l, i, o, c: (l, o * 2 * ct + ct + c, 0)),
        ],
        out_specs=[
            pl.BlockSpec((None, None, LANES, LT_FILT), lambda l, i, o, c: (l, o, c, i)),
            pl.BlockSpec((None, None, LANES, LT_FILT), lambda l, i, o, c: (l, o, c, i)),
            pl.BlockSpec((None, None, LANES, LANES), lambda l, i, o, c: (l, o, c, i)),
        ],
        out_shape=[
            jax.ShapeDtypeStruct((DEPTH, HY_ORDER, WIDTH, seq), F32),
            jax.ShapeDtypeStruct((DEPTH, HY_ORDER, WIDTH, seq), F32),
            jax.ShapeDtypeStruct((DEPTH, HY_ORDER, WIDTH, nlt * LANES), F32),
        ],
        scratch_shapes=[pltpu.VMEM((3 * HY_HIDDEN, LT_FILT), BF16),
                        pltpu.VMEM((3 * HY_HIDDEN, LT_FILT), BF16)],
        name="hyena_filters",
    )(feats, feats_rev, feats0, w1t, b1, w2t, b2, freq, w3t, w3t, deltas, deltas)


def _split_hi_lo(x):
    hi = x.astype(BF16)
    lo = (x - hi.astype(F32)).astype(BF16)
    return hi, lo


def _split3_rows(x):
    hi, lo = _split_hi_lo(x)
    return jnp.concatenate([hi, lo, hi], axis=0)


def _split3_cols(x):
    hi, lo = _split_hi_lo(x)
    return jnp.concatenate([hi, lo, hi], axis=1)


def _np_split(m):
    hi = m.astype(ml_dtypes.bfloat16)
    lo = (m - hi.astype(np.float64)).astype(ml_dtypes.bfloat16)
    return hi, lo


def _dft_tables(seq):
    nj = seq // LANES
    n1 = 2 * nj
    n = n1 * LANES
    khp = n1 // 2 + KH_PAD
    k1 = np.arange(khp, dtype=np.float64)[:, None]
    valid = (k1 <= n1 // 2).astype(np.float64)
    ang = 2.0 * np.pi * k1 * np.arange(n1, dtype=np.float64)[None, :] / n1
    fa_full = np.concatenate([valid * np.cos(ang), -valid * np.sin(ang)], axis=0)

    def lhs3(m):
        hi, lo = _np_split(m)
        return jnp.asarray(np.concatenate([hi, hi, lo], axis=1))

    def rhs3(m):
        hi, lo = _np_split(m)
        return jnp.asarray(np.concatenate([hi, hi, lo], axis=0))

    n2 = np.arange(LANES, dtype=np.float64)
    ang2 = 2.0 * np.pi * n2[:, None] * n2[None, :] / LANES
    c2, s2 = np.cos(ang2), np.sin(ang2)
    fb = np.block([[c2, -s2], [s2, c2]])
    fbi = np.block([[c2, s2], [-s2, c2]])
    w = np.where((k1 == 0) | (k1 == n1 // 2), 1.0, 2.0) * valid
    th = 2.0 * np.pi * np.arange(nj, dtype=np.float64)[:, None] * k1.T / n1
    fai = np.concatenate([(w.T / n) * np.cos(th), -(w.T / n) * np.sin(th)], axis=1)
    phi = 2.0 * np.pi * k1 * n2[None, :] / n
    return dict(
        nj=nj, n1=n1, khp=khp,
        fa=lhs3(fa_full[:, :nj]), fa_full=lhs3(fa_full), fb=rhs3(fb), fbi=rhs3(fbi), fai=lhs3(fai),
        twc=jnp.asarray(valid * np.cos(phi), F32), tws=jnp.asarray(valid * np.sin(phi), F32))


def _dft_rows(fa_ref, twc, tws, khp, cols):
    rhs = cols[0] if len(cols) == 1 else jnp.concatenate(cols, axis=1)
    y = _dot(fa_ref[...], _split3_rows(rhs))
    out = []
    for i in range(len(cols)):
        yre = y[0:khp, i * LANES:(i + 1) * LANES]
        yim = y[khp:2 * khp, i * LANES:(i + 1) * LANES]
        out.append((yre * twc + yim * tws, yim * twc - yre * tws))
    return out


def _spec_kernel(khp, g_ref, s_ref, fa_ref, fb_ref, twc_ref, tws_ref, h_ref, ybuf):
    twc = twc_ref[...]
    tws = tws_ref[...]
    for c in range(0, CB_SPEC, 2):
        res = _dft_rows(fa_ref, twc, tws, khp, [g_ref[c], g_ref[c + 1]])
        for i, (yre, yim) in enumerate(res):
            r0 = (c + i) * khp
            ybuf[r0:r0 + khp, 0:LANES] = yre
            ybuf[r0:r0 + khp, LANES:2 * LANES] = yim
    x = _dot(_split3_cols(ybuf[...]), fb_ref[...])
    for c in range(CB_SPEC):
        norm = jnp.sum(s_ref[c:c + 1, :], axis=1, keepdims=True)
        h_ref[c] = x[c * khp:(c + 1) * khp, :] / norm


def _spectrum(g, s, tab):
    nch = g.shape[0]
    n1, khp = tab["n1"], tab["khp"]
    const = lambda a: pl.BlockSpec(a.shape, lambda i: (0, 0))
    return pl.pallas_call(
        functools.partial(_spec_kernel, khp),
        grid=(nch // CB_SPEC,),
        in_specs=[
            pl.BlockSpec((CB_SPEC, n1, LANES), lambda i: (i, 0, 0)),
            pl.BlockSpec((CB_SPEC, s.shape[-1]), lambda i: (i, 0)),
            const(tab["fa_full"]), const(tab["fb"]), const(tab["twc"]), const(tab["tws"]),
        ],
        out_specs=pl.BlockSpec((CB_SPEC, khp, 2 * LANES), lambda i: (i, 0, 0)),
        out_shape=jax.ShapeDtypeStruct((nch, khp, 2 * LANES), F32),
        scratch_shapes=[pltpu.VMEM((CB_SPEC * khp, 2 * LANES), F32)],
        name="hyena_spectrum",
    )(g, s, tab["fa_full"], tab["fb"], tab["twc"], tab["tws"])


def _hyfft_kernel(bsz, nj, khp, prm_ref, v_ref, x1_ref, x2_ref, z_ref, h_ref,
                  fa_ref, fb_ref, fbi_ref, fai_ref, twc_ref, tws_ref, out_ref,
                  xs_ref, ybuf, pbuf):
    ct = pl.program_id(0)
    rows = bsz * nj
    pairs = [(b, b + 1) for b in range(0, bsz, 2)] if bsz > 1 else [(0,)]
    twc = twc_ref[...]
    tws = tws_ref[...]

    lane = lax.broadcasted_iota(jnp.int32, (rows, LANES), 1)
    blk = lax.broadcasted_iota(jnp.int32, (rows, LANES), 0) & (nj - 1)
    first = (lane == 0) & (blk == 0)
    last = (lane == LANES - 1) & (blk == nj - 1)

    def conv3(x, base):
        r = pltpu.roll(x, 1, 1)
        xm1 = jnp.where(lane == 0, pltpu.roll(r, 1, 0), r)
        xm1 = jnp.where(first, 0.0, xm1)
        l = pltpu.roll(x, LANES - 1, 1)
        xp1 = jnp.where(lane == LANES - 1, pltpu.roll(l, rows - 1, 0), l)
        xp1 = jnp.where(last, 0.0, xp1)
        return prm_ref[base] * xm1 + prm_ref[base + 1] * x + prm_ref[base + 2] * xp1 + prm_ref[base + 3]

    def row0(c, b):
        return (c * bsz + b) * khp

    def forward(c, x):
        for pr in pairs:
            res = _dft_rows(fa_ref, twc, tws, khp, [x[b * nj:(b + 1) * nj, :] for b in pr])
            for b, (yre, yim) in zip(pr, res):
                r0 = row0(c, b)
                ybuf[r0:r0 + khp, 0:LANES] = yre
                ybuf[r0:r0 + khp, LANES:2 * LANES] = yim

    def spectral(order):
        x = _dot(_split3_cols(ybuf[...]), fb_ref[...])
        for c in range(CB_HY):
            hre = h_ref[order, c, :, 0:LANES]
            him = h_ref[order, c, :, LANES:2 * LANES]
            for b in range(bsz):
                r0 = row0(c, b)
                xre = x[r0:r0 + khp, 0:LANES]
                xim = x[r0:r0 + khp, LANES:2 * LANES]
                pbuf[r0:r0 + khp, 0:LANES] = xre * hre - xim * him
                pbuf[r0:r0 + khp, LANES:2 * LANES] = xre * him + xim * hre
        r = _dot(_split3_cols(pbuf[...]), fbi_ref[...])
        for c in range(CB_HY):
            for b in range(bsz):
                r0 = row0(c, b)
                rre = r[r0:r0 + khp, 0:LANES]
                rim = r[r0:r0 + khp, LANES:2 * LANES]
                ybuf[r0:r0 + khp, 0:LANES] = rre * twc - rim * tws
                ybuf[r0:r0 + khp, LANES:2 * LANES] = rre * tws + rim * twc

    def inverse(c):
        outs = [None] * bsz
        for pr in pairs:
            rre = [ybuf[row0(c, b):row0(c, b) + khp, 0:LANES] for b in pr]
            rim = [ybuf[row0(c, b):row0(c, b) + khp, LANES:2 * LANES] for b in pr]
            if len(pr) > 1:
                rre, rim = [jnp.concatenate(rre, axis=1)], [jnp.concatenate(rim, axis=1)]
            y = _dot(fai_ref[...], _split3_rows(jnp.concatenate([rre[0], rim[0]], axis=0)))
            for i, b in enumerate(pr):
                outs[b] = y[:, i * LANES:(i + 1) * LANES]
        return outs[0] if bsz == 1 else jnp.concatenate(outs, axis=0)

    for c in range(CB_HY):
        base = (ct * CB_HY + c) * 16
        v = conv3(v_ref[c].reshape(rows, LANES), base)
        xs_ref[0, c] = v
        xs_ref[1, c] = conv3(x1_ref[c].reshape(rows, LANES), base + 4)
        xs_ref[2, c] = conv3(x2_ref[c].reshape(rows, LANES), base + 8)
        forward(c, v)
    spectral(0)
    for c in range(CB_HY):
        base = (ct * CB_HY + c) * 16
        v = xs_ref[0, c]
        z1 = xs_ref[1, c] * (inverse(c) + prm_ref[base + 12] * v)
        xs_ref[0, c] = z1
        forward(c, z1)
    spectral(1)
    for c in range(CB_HY):
        base = (ct * CB_HY + c) * 16
        z1 = xs_ref[0, c]
        y2 = inverse(c) + prm_ref[base + 13] * z1
        out = xs_ref[2, c] * y2 * _silu(z_ref[c].reshape(rows, LANES))
        out_ref[c] = out.reshape(bsz, nj, LANES)


def _hyena_fft(prm, u_t, h, tab, bsz, seq):
    nj, khp = tab["nj"], tab["khp"]
    nct = WIDTH // CB_HY
    blk = (CB_HY, bsz, nj, LANES)
    const = lambda a: pl.BlockSpec(a.shape, lambda i: (0, 0))
    return pl.pallas_call(
        functools.partial(_hyfft_kernel, bsz, nj, khp),
        grid=(nct,),
        in_specs=[
            pl.BlockSpec(memory_space=pltpu.SMEM),
            pl.BlockSpec(blk, lambda i: (i, 0, 0, 0)),
            pl.BlockSpec(blk, lambda i: (nct + i, 0, 0, 0)),
            pl.BlockSpec(blk, lambda i: (2 * nct + i, 0, 0, 0)),
            pl.BlockSpec(blk, lambda i: (3 * nct + i, 0, 0, 0)),
            pl.BlockSpec((HY_ORDER, CB_HY, khp, 2 * LANES), lambda i: (0, i, 0, 0)),
            const(tab["fa"]), const(tab["fb"]), const(tab["fbi"]), const(tab["fai"]),
            const(tab["twc"]), const(tab["tws"]),
        ],
        out_specs=pl.BlockSpec(blk, lambda i: (i, 0, 0, 0)),
        out_shape=jax.ShapeDtypeStruct((WIDTH, bsz, nj, LANES), F32),
        scratch_shapes=[
            pltpu.VMEM((3, CB_HY, bsz * nj, LANES), F32),
            pltpu.VMEM((CB_HY * bsz * khp, 2 * LANES), F32),
            pltpu.VMEM((CB_HY * bsz * khp, 2 * LANES), F32),
        ],
        name="hyena_fft",
    )(prm, u_t, u_t, u_t, u_t, h, tab["fa"], tab["fb"], tab["fbi"], tab["fai"],
      tab["twc"], tab["tws"])


def _out_kernel(final, x_ref, yml_ref, yrt_ref, yhyt_ref, w_ref, g_ref, o_ref):
    acc = _dot(yml_ref[...].astype(BF16), w_ref[0:WIDTH, :])
    acc = acc + _dot(yrt_ref[...].astype(BF16), w_ref[WIDTH:2 * WIDTH, :])
    acc = acc + _dot_tn(yhyt_ref[...].astype(BF16), w_ref[2 * WIDTH:, :])
    x = x_ref[...] + acc
    if final:
        ms = jnp.mean(x * x, axis=-1, keepdims=True)
        x = x * lax.rsqrt(ms + RMS_EPS) * g_ref[...]
    o_ref[...] = x


def _out_proj(x, y_ml, y_rt, y_hyt, w, gain, final):
    t = x.shape[0]
    return pl.pallas_call(
        functools.partial(_out_kernel, final),
        grid=(t // TM_OUT,),
        in_specs=[
            pl.BlockSpec((TM_OUT, D_MODEL), lambda i: (i, 0)),
            pl.BlockSpec((TM_OUT, WIDTH), lambda i: (i, 0)),
            pl.BlockSpec((TM_OUT, WIDTH), lambda i: (i, 0)),
            pl.BlockSpec((WIDTH, TM_OUT), lambda i: (0, i)),
            pl.BlockSpec((3 * WIDTH, D_MODEL), lambda i: (0, 0)),
            pl.BlockSpec((1, D_MODEL), lambda i: (0, 0)),
        ],
        out_specs=pl.BlockSpec((TM_OUT, D_MODEL), lambda i: (i, 0)),
        out_shape=jax.ShapeDtypeStruct((t, D_MODEL), F32),
        name="out_proj",
    )(x, y_ml, y_rt, y_hyt, w, gain)


def kernel(x_prompt, x_sample, norm_g, w_in, ml_conv_w, ml_conv_b, ml_gate_b, ml_norm_g, rt_norm_g, hy_conv_w, hy_conv_b, hy_w1, hy_b1, hy_w2, hy_b2, hy_w3, hy_freq, hy_deltas, hy_skip, w_out, final_g):
    groups = []
    row0 = 0
    for xg in (x_prompt, x_sample):
        bsz, seq, _ = xg.shape
        groups.append((row0, bsz, seq))
        row0 += bsz * seq
    x = jnp.concatenate([x_prompt.reshape(-1, D_MODEL), x_sample.reshape(-1, D_MODEL)], axis=0)

    c_gate = ML_COLS
    c_rt = c_gate + 4 * HEADS
    c_hy = c_rt + RT_COLS
    gate_pad = jnp.zeros((DEPTH, D_MODEL, LANES - 4 * HEADS), F32)
    w_tok = jnp.concatenate([w_in[:, :, :c_gate], w_in[:, :, c_gate:c_rt], gate_pad,
                             w_in[:, :, c_rt:c_hy]], axis=2).astype(BF16)
    w_gt = jnp.swapaxes(w_in[:, :, c_gate:c_rt], 1, 2).astype(BF16)
    w_hyt = jnp.swapaxes(w_in[:, :, c_hy:], 1, 2).astype(BF16)
    w_out_b = w_out.astype(BF16)
    gate_b = ml_gate_b.reshape(DEPTH, 1, 4 * HEADS)
    gate_b_row = jnp.concatenate([gate_b, jnp.zeros((DEPTH, 1, LANES - 4 * HEADS), F32)], axis=2)
    gate_b_col = ml_gate_b.reshape(DEPTH, 4 * HEADS, 1)

    cw = hy_conv_w.reshape(DEPTH, 3, 3, WIDTH)
    cbias = hy_conv_b.reshape(DEPTH, 1, 3, WIDTH)
    taps = jnp.concatenate([cw, cbias], axis=1)
    taps = jnp.transpose(taps, (0, 3, 2, 1)).reshape(DEPTH, WIDTH, 12)
    hy_prm = jnp.concatenate([taps, jnp.transpose(hy_skip, (0, 2, 1)),
                              jnp.zeros((DEPTH, WIDTH, 2), F32)], axis=2).reshape(DEPTH, WIDTH * 16)

    w1t = jnp.concatenate([jnp.swapaxes(hy_w1, 1, 2),
                           jnp.zeros((DEPTH, HY_HIDDEN, FEAT_ROWS - hy_w1.shape[1]), F32)], axis=2)
    w2t = jnp.swapaxes(hy_w2, 1, 2)
    w3t = jnp.swapaxes(hy_w3, 1, 2)
    b1 = hy_b1[:, :, None]
    b2 = hy_b2[:, :, None]
    freq = hy_freq[:, :, None]
    deltas = hy_deltas.reshape(DEPTH, HY_ORDER * 2 * WIDTH, 1)

    tables, spectra = [], []
    for (_, _, seq) in groups:
        tab = _dft_tables(seq)
        gb, gf, s = _filters(seq, w1t, b1, w2t, b2, freq, w3t, deltas)
        g = jnp.concatenate([gf, gb], axis=-1).reshape(DEPTH * HY_ORDER * WIDTH, tab["n1"], LANES)
        h = _spectrum(g, s.reshape(DEPTH * HY_ORDER * WIDTH, -1), tab)
        tables.append(tab)
        spectra.append(h.reshape(DEPTH, HY_ORDER, WIDTH, tab["khp"], 2 * LANES))
    rot = [_rotary_tables(seq) for (_, _, seq) in groups]
    rt_tabs = _retention_tables()

    for layer in range(DEPTH):
        u_ml, u_g, u_rt, u_gt, u_hyt = _in_proj(x, norm_g[layer][None, :], w_tok[layer],
                                                w_gt[layer], w_hyt[layer])
        pb, cb, gr, gs = _gates(u_g, u_gt, gate_b_row[layer], gate_b_col[layer])
        y_ml, y_rt, y_hy = [], [], []
        for gi, (r0, bsz, seq) in enumerate(groups):
            y_ml.append(_mlstm(u_ml, pb, cb, gr, gs, ml_conv_w[layer], ml_conv_b[layer][None, :],
                               ml_norm_g[layer][None, :], r0, bsz, seq))
            y_rt.append(_retention(u_rt, rot[gi][0], rot[gi][1], rt_tabs,
                                   rt_norm_g[layer][None, :], r0, bsz, seq))
            u_t = u_hyt[:, r0:r0 + bsz * seq].reshape(HY_ROWS, bsz, seq // LANES, LANES)
            y = _hyena_fft(hy_prm[layer], u_t, spectra[gi][layer], tables[gi], bsz, seq)
            y_hy.append(y.reshape(WIDTH, bsz * seq))
        x = _out_proj(x, jnp.concatenate(y_ml, axis=0), jnp.concatenate(y_rt, axis=0),
                      jnp.concatenate(y_hy, axis=1), w_out_b[layer], final_g[None, :],
                      layer == DEPTH - 1)

    n_prompt = x_prompt.shape[0] * x_prompt.shape[1]
    return x[:n_prompt].reshape(x_prompt.shape), x[n_prompt:].reshape(x_sample.shape)
```

```python
import functools
import math

import ml_dtypes
import numpy as np
import jax
import jax.numpy as jnp
from jax import lax
from jax.experimental import pallas as pl
from jax.experimental.pallas import tpu as pltpu

F32 = jnp.float32
BF16 = jnp.bfloat16

D_MODEL = 1024
DEPTH = 4
CHUNK = 128
HEADS = 4
HEAD_DIM = 128
WIDTH = 512
HY_ORDER = 2
HY_BANDS = 16
HY_HIDDEN = 64
FEAT_ROWS = 40
ROPE_BASE = 10000.0
RMS_EPS = 1e-6
HEAD_NORM_EPS = 1e-5
M_INIT = -1e30
RT_LOG_GAMMA_FWD = tuple(math.log(1.0 - 2.0 ** (-5.0 - h)) for h in range(HEADS))
RT_LOG_GAMMA_BWD = tuple(math.log(1.0 - 2.0 ** (-5.5 - h)) for h in range(HEADS))

LANES = 128
SUBLANES = 8
TM_IN = 256
TM_HY = 1024
TR_HY = 1024
TM_OUT = 1024
TB_SEQ = 512
TG = 1024
CB_HY = 8
CB_SPEC = 8
KH_PAD = 16
LT_FILT = 2048
ML_COLS = 5 * WIDTH
RT_COLS = 4 * WIDTH
HY_ROWS = 4 * WIDTH
HIGHEST = lax.Precision.HIGHEST


def _dot(a, b):
    return jnp.dot(a, b, preferred_element_type=F32)


def _dot_nt(a, b):
    return lax.dot_general(a, b, (((1,), (1,)), ((), ())), preferred_element_type=F32)


def _dot_tn(a, b):
    return lax.dot_general(a, b, (((0,), (0,)), ((), ())), preferred_element_type=F32)


def _dot_f32(a, b):
    return jnp.dot(a, b, precision=HIGHEST, preferred_element_type=F32)


def _silu(y):
    return y * jax.nn.sigmoid(y)


def _log_sigmoid(x):
    return -(jnp.maximum(-x, 0.0) + jnp.log1p(jnp.exp(-jnp.abs(x))))


def _head_norm(h, gain):
    mu = jnp.mean(h, axis=-1, keepdims=True)
    hc = h - mu
    var = jnp.mean(hc * hc, axis=-1, keepdims=True)
    return hc * lax.rsqrt(var + HEAD_NORM_EPS) * gain


def _rms_bf16(x_ref, g_ref):
    x = x_ref[...]
    ms = jnp.mean(x * x, axis=-1, keepdims=True)
    return (x * lax.rsqrt(ms + RMS_EPS) * g_ref[...]).astype(BF16)


def _in_kernel(x_ref, g_ref, wtok_ref, wgt_ref, oml_ref, og_ref, ort_ref, ogt_ref):
    h = _rms_bf16(x_ref, g_ref)
    oml_ref[...] = _dot(h, wtok_ref[:, 0:ML_COLS])
    og_ref[...] = _dot(h, wtok_ref[:, ML_COLS:ML_COLS + LANES])
    ort_ref[...] = _dot(h, wtok_ref[:, ML_COLS + LANES:])
    ogt_ref[...] = _dot_nt(wgt_ref[...], h)


def _in_proj(x, gain, w_tok, w_gt):
    t = x.shape[0]
    ncol = w_tok.shape[1]
    return pl.pallas_call(
        _in_kernel,
        grid=(t // TM_IN,),
        in_specs=[
            pl.BlockSpec((TM_IN, D_MODEL), lambda i: (i, 0)),
            pl.BlockSpec((1, D_MODEL), lambda i: (0, 0)),
            pl.BlockSpec((D_MODEL, ncol), lambda i: (0, 0)),
            pl.BlockSpec((4 * HEADS, D_MODEL), lambda i: (0, 0)),
        ],
        out_specs=[
            pl.BlockSpec((TM_IN, ML_COLS), lambda i: (i, 0)),
            pl.BlockSpec((TM_IN, LANES), lambda i: (i, 0)),
            pl.BlockSpec((TM_IN, RT_COLS), lambda i: (i, 0)),
            pl.BlockSpec((4 * HEADS, TM_IN), lambda i: (0, i)),
        ],
        out_shape=[
            jax.ShapeDtypeStruct((t, ML_COLS), F32),
            jax.ShapeDtypeStruct((t, LANES), F32),
            jax.ShapeDtypeStruct((t, RT_COLS), F32),
            jax.ShapeDtypeStruct((4 * HEADS, t), F32),
        ],
        name="in_proj",
    )(x, gain, w_tok, w_gt)


def _in_hy_kernel(x_ref, g_ref, w_ref, o_ref):
    y = _dot_nt(w_ref[...], _rms_bf16(x_ref, g_ref))
    o_ref[...] = y.reshape(o_ref.shape)


def _in_proj_hy(x, gain, w_hyt):
    t = x.shape[0]
    nr = HY_ROWS // TR_HY
    return pl.pallas_call(
        _in_hy_kernel,
        grid=(t // TM_HY, nr),
        in_specs=[
            pl.BlockSpec((TM_HY, D_MODEL), lambda i, r: (i, 0)),
            pl.BlockSpec((1, D_MODEL), lambda i, r: (0, 0)),
            pl.BlockSpec((TR_HY, D_MODEL), lambda i, r: (r, 0)),
        ],
        out_specs=pl.BlockSpec((TR_HY, TM_HY // LANES, LANES), lambda i, r: (r, i, 0)),
        out_shape=jax.ShapeDtypeStruct((HY_ROWS, t // LANES, LANES), F32),
        name="in_proj_hyena",
    )(x, gain, w_hyt)


def _gates_kernel(g_ref, gt_ref, b_ref, bt_ref, pb_ref, cb_ref, gr_ref, gs_ref):
    row = lax.broadcasted_iota(jnp.int32, (CHUNK, CHUNK), 0)
    col = lax.broadcasted_iota(jnp.int32, (CHUNK, CHUNK), 1)
    lower = (row >= col).astype(F32)
    upper = (row <= col).astype(F32)
    gr_ref[...] = jnp.zeros_like(gr_ref)
    gs_ref[...] = jnp.zeros_like(gs_ref)
    for c in range(TG // CHUNK):
        sl = slice(c * CHUNK, (c + 1) * CHUNK)
        lf = _log_sigmoid(g_ref[sl, :] + b_ref[...])
        cum_c = (_dot_f32(lower, lf), _dot_f32(upper, lf))
        pre_t = gt_ref[:, sl] + bt_ref[...]
        lf_t = _log_sigmoid(pre_t)
        cum_r = (_dot_f32(lf_t, upper), _dot_f32(lf_t, lower))
        for d in range(2):
            for h in range(HEADS):
                kf = (2 * d + 1) * HEADS + h
                ki = 2 * d * HEADS + h
                cum = cum_r[d][kf:kf + 1, :]
                li = pre_t[ki:ki + 1, :]
                r = li - cum
                f_tot = cum[:, 0:1] if d else cum[:, CHUNK - 1:CHUNK]
                g = f_tot - cum + li
                allowed = (row <= col) if d else (row >= col)
                pmax = jnp.max(jnp.where(allowed, r, -jnp.inf), axis=1, keepdims=True)
                pb_ref[d, h, sl, :] = jnp.broadcast_to(pmax, (CHUNK, LANES))
                cb_ref[d, h, sl, :] = jnp.broadcast_to(cum_c[d][:, kf:kf + 1], (CHUNK, LANES))
                gr_ref[d, h, 0:1, sl] = r
                gr_ref[d, h, 1:2, sl] = g
                gs_ref[d, h, c, 0:1, :] = jnp.broadcast_to(f_tot, (1, LANES))
                gs_ref[d, h, c, 1:2, :] = jnp.broadcast_to(jnp.max(g, axis=1, keepdims=True), (1, LANES))


def _gates(g, gt, bias_row, bias_col):
    t = g.shape[0]
    nc = TG // CHUNK
    return pl.pallas_call(
        _gates_kernel,
        grid=(t // TG,),
        in_specs=[
            pl.BlockSpec((TG, LANES), lambda i: (i, 0)),
            pl.BlockSpec((4 * HEADS, TG), lambda i: (0, i)),
            pl.BlockSpec((1, LANES), lambda i: (0, 0)),
            pl.BlockSpec((4 * HEADS, 1), lambda i: (0, 0)),
        ],
        out_specs=[
            pl.BlockSpec((2, HEADS, TG, LANES), lambda i: (0, 0, i, 0)),
            pl.BlockSpec((2, HEADS, TG, LANES), lambda i: (0, 0, i, 0)),
            pl.BlockSpec((2, HEADS, SUBLANES, TG), lambda i: (0, 0, 0, i)),
            pl.BlockSpec((2, HEADS, nc, SUBLANES, LANES), lambda i: (0, 0, i, 0, 0)),
        ],
        out_shape=[
            jax.ShapeDtypeStruct((2, HEADS, t, LANES), F32),
            jax.ShapeDtypeStruct((2, HEADS, t, LANES), F32),
            jax.ShapeDtypeStruct((2, HEADS, SUBLANES, t), F32),
            jax.ShapeDtypeStruct((2, HEADS, t // CHUNK, SUBLANES, LANES), F32),
        ],
        name="gates",
    )(g, gt, bias_row, bias_col)


def _seq_pos(g, groups):
    pos, nbs = None, None
    start = 0
    for (_, bsz, seq) in groups:
        nb = seq // TB_SEQ
        p = (g - start) % nb
        pos = p if pos is None else jnp.where(g >= start, p, pos)
        nbs = nb if nbs is None else jnp.where(g >= start, nb, nbs)
        start += bsz * nb
    return pos, nbs


def _num_blocks(groups):
    return sum(bsz * seq // TB_SEQ for (_, bsz, seq) in groups)


def _ml_kernel(rev, groups, *refs):
    if rev:
        (q_ref, k_ref, v_ref, qp_ref, qn_ref, kp_ref, kn_ref, pb_ref, cb_ref, gr_ref, gs_ref,
         cw_ref, cbias_ref, o_ref, z_ref, hf_ref, ng_ref,
         out_ref, st_ref, m_ref, stg_ref, qs_ref, ks_ref) = refs
    else:
        (q_ref, k_ref, v_ref, qp_ref, qn_ref, kp_ref, kn_ref, pb_ref, cb_ref, gr_ref, gs_ref,
         cw_ref, cbias_ref, out_ref, st_ref, m_ref, stg_ref, qs_ref, ks_ref) = refs
    tb = TB_SEQ
    nch = tb // CHUNK
    n = pl.program_id(0)
    sb, nb = _seq_pos((_num_blocks(groups) - 1 - n) if rev else n, groups)

    @pl.when(sb == (nb - 1 if rev else 0))
    def _():
        st_ref[...] = jnp.zeros_like(st_ref)
        m_ref[...] = jnp.full_like(m_ref, M_INIT)

    def conv_silu(raw_ref, prev_ref, next_ref, c0):
        stg_ref[SUBLANES:SUBLANES + tb, :] = raw_ref[...]
        stg_ref[SUBLANES - 1:SUBLANES, :] = jnp.where(sb > 0, prev_ref[SUBLANES - 1:SUBLANES, :], 0.0)
        stg_ref[SUBLANES + tb:SUBLANES + tb + 1, :] = jnp.where(sb < nb - 1, next_ref[0:1, :], 0.0)
        w = cw_ref[:, c0:c0 + WIDTH]
        y = (stg_ref[SUBLANES - 1:SUBLANES - 1 + tb, :] * w[0:1, :]
             + stg_ref[SUBLANES:SUBLANES + tb, :] * w[1:2, :]
             + stg_ref[SUBLANES + 1:SUBLANES + 1 + tb, :] * w[2:3, :] + cbias_ref[:, c0:c0 + WIDTH])
        return _silu(y)

    qs_ref[...] = conv_silu(q_ref, qp_ref, qn_ref, 0).astype(BF16)
    ks_ref[...] = conv_silu(k_ref, kp_ref, kn_ref, WIDTH) * (HEAD_DIM ** -0.5)

    row = lax.broadcasted_iota(jnp.int32, (CHUNK, CHUNK), 0)
    col = lax.broadcasted_iota(jnp.int32, (CHUNK, CHUNK), 1)
    allowed = (row <= col) if rev else (row >= col)
    ones = jnp.ones((CHUNK, HEAD_DIM), BF16)

    def chunk(c, h):
        sl = pl.ds(pl.multiple_of(c * CHUNK, CHUNK), CHUNK)
        hs = slice(h * HEAD_DIM, (h + 1) * HEAD_DIM)
        q = qs_ref[sl, hs]
        kt = ks_ref[sl, hs].T
        v2 = jnp.concatenate([v_ref[sl, hs].astype(BF16), ones], axis=1)
        m_prev = m_ref[h, 0:1, :]
        mb = jnp.maximum(pb_ref[h, sl, :], m_prev)
        p = jnp.exp(jnp.where(allowed, gr_ref[h, 0:1, sl] - mb, -jnp.inf))
        w_inter = jnp.exp(m_prev - mb)
        floor = jnp.exp(-(cb_ref[h, sl, :] + mb))
        s = _dot(q, kt.astype(BF16)) * p
        inter = _dot(q, st_ref[h].astype(BF16))
        num2 = _dot(s.astype(BF16), v2) + jnp.concatenate([w_inter, w_inter], axis=1) * inter
        out = num2[:, 0:HEAD_DIM] / jnp.maximum(jnp.abs(num2[:, HEAD_DIM:]), floor)
        f_tot = gs_ref[h, c, 0:1, :]
        m_chunk = gs_ref[h, c, 1:2, :]
        kw = (kt * jnp.exp(gr_ref[h, 1:2, sl] - m_chunk)).astype(BF16)
        kv2 = _dot(kw, v2)
        m_new = jnp.maximum(f_tot + m_prev, m_chunk)
        a = jnp.exp(f_tot + m_prev - m_new)
        b = jnp.exp(m_chunk - m_new)
        st_ref[h] = (jnp.concatenate([a, a], axis=1) * st_ref[h]
                     + jnp.concatenate([b, b], axis=1) * kv2)
        m_ref[h] = jnp.broadcast_to(m_new, (SUBLANES, LANES))
        return out, sl, hs

    def body(i, carry):
        c = (nch - 1 - i) if rev else i
        for h in range(HEADS):
            out, sl, hs = chunk(c, h)
            if rev:
                hh = jax.nn.sigmoid(o_ref[sl, hs]) * (out + hf_ref[sl, hs])
                out_ref[sl, hs] = _head_norm(hh, ng_ref[:, hs]) * _silu(z_ref[sl, hs])
            else:
                out_ref[sl, hs] = out
        return carry

    lax.fori_loop(0, nch, body, 0, unroll=True)


def _mlstm_dir(rev, u_ml, pb, cb, gr, gs, conv_w, conv_b, norm_g, h_fwd, groups):
    t = u_ml.shape[0]
    tb = TB_SEQ
    t8 = tb // SUBLANES
    nch = tb // CHUNK
    nblk = _num_blocks(groups)
    d = 1 if rev else 0

    def rb(n):
        return (nblk - 1 - n) if rev else n

    def col(g):
        return lambda n: (rb(n), g)

    def halo_prev(g):
        return lambda n: (jnp.maximum(rb(n) * t8 - 1, 0), g)

    def halo_next(g):
        return lambda n: (jnp.minimum((rb(n) + 1) * t8, t // SUBLANES - 1), g)

    blk = (tb, WIDTH)
    halo = (SUBLANES, WIDTH)
    in_specs = [
        pl.BlockSpec(blk, col(0)), pl.BlockSpec(blk, col(1)), pl.BlockSpec(blk, col(2)),
        pl.BlockSpec(halo, halo_prev(0)), pl.BlockSpec(halo, halo_next(0)),
        pl.BlockSpec(halo, halo_prev(1)), pl.BlockSpec(halo, halo_next(1)),
        pl.BlockSpec((None, HEADS, tb, LANES), lambda n: (d, 0, rb(n), 0)),
        pl.BlockSpec((None, HEADS, tb, LANES), lambda n: (d, 0, rb(n), 0)),
        pl.BlockSpec((None, HEADS, SUBLANES, tb), lambda n: (d, 0, 0, rb(n))),
        pl.BlockSpec((None, HEADS, nch, SUBLANES, LANES), lambda n: (d, 0, rb(n), 0, 0)),
        pl.BlockSpec((3, 2 * WIDTH), lambda n: (0, 0)),
        pl.BlockSpec((1, 2 * WIDTH), lambda n: (0, 0)),
    ]
    args = [u_ml, u_ml, u_ml, u_ml, u_ml, u_ml, u_ml, pb, cb, gr, gs, conv_w, conv_b]
    if rev:
        in_specs += [pl.BlockSpec(blk, col(3)), pl.BlockSpec(blk, col(4)), pl.BlockSpec(blk, col(0)),
                     pl.BlockSpec((1, WIDTH), lambda n: (0, 0))]
        args += [u_ml, u_ml, h_fwd, norm_g]
    return pl.pallas_call(
        functools.partial(_ml_kernel, rev, groups),
        grid=(nblk,),
        in_specs=in_specs,
        out_specs=pl.BlockSpec(blk, col(0)),
        out_shape=jax.ShapeDtypeStruct((t, WIDTH), F32),
        scratch_shapes=[
            pltpu.VMEM((HEADS, HEAD_DIM, 2 * HEAD_DIM), F32),
            pltpu.VMEM((HEADS, SUBLANES, LANES), F32),
            pltpu.VMEM((tb + 2 * SUBLANES, WIDTH), F32),
            pltpu.VMEM((tb, WIDTH), BF16),
            pltpu.VMEM((tb, WIDTH), F32),
        ],
        name="mlstm_bwd" if rev else "mlstm_fwd",
    )(*args)


def _mlstm(u_ml, pb, cb, gr, gs, conv_w, conv_b, norm_g, groups):
    h_fwd = _mlstm_dir(False, u_ml, pb, cb, gr, gs, conv_w, conv_b, norm_g, None, groups)
    return _mlstm_dir(True, u_ml, pb, cb, gr, gs, conv_w, conv_b, norm_g, h_fwd, groups)


def _rt_kernel(rev, groups, *refs):
    if rev:
        (q_ref, k_ref, v_ref, cc_ref, ss_ref, dec_ref, qd_ref, kd_ref, z_ref, hf_ref, ng_ref,
         out_ref, st_ref, qs_ref, ks_ref) = refs
    else:
        (q_ref, k_ref, v_ref, cc_ref, ss_ref, dec_ref, qd_ref, kd_ref,
         out_ref, st_ref, qs_ref, ks_ref) = refs
    nch = TB_SEQ // CHUNK
    n = pl.program_id(0)
    sb, nb = _seq_pos((_num_blocks(groups) - 1 - n) if rev else n, groups)

    @pl.when(sb == (nb - 1 if rev else 0))
    def _():
        st_ref[...] = jnp.zeros_like(st_ref)

    for h in range(HEADS):
        hs = slice(h * HEAD_DIM, (h + 1) * HEAD_DIM)
        cc = cc_ref[...]
        ss = ss_ref[...]
        q = q_ref[:, hs]
        k = k_ref[:, hs]
        qs_ref[:, hs] = (q * cc + pltpu.roll(q, HEAD_DIM // 2, 1) * ss) * (HEAD_DIM ** -0.5)
        ks_ref[:, hs] = k * cc + pltpu.roll(k, HEAD_DIM // 2, 1) * ss

    def body(i, carry):
        c = (nch - 1 - i) if rev else i
        sl = pl.ds(pl.multiple_of(c * CHUNK, CHUNK), CHUNK)
        for h in range(HEADS):
            hs = slice(h * HEAD_DIM, (h + 1) * HEAD_DIM)
            q = qs_ref[sl, hs]
            kt = ks_ref[sl, hs].T
            vb = v_ref[sl, hs].astype(BF16)
            s = _dot(q.astype(BF16), kt.astype(BF16)) * dec_ref[h]
            out = _dot(s.astype(BF16), vb) + _dot((q * qd_ref[h]).astype(BF16), st_ref[h].astype(BF16))
            st_ref[h] = kd_ref[h, 1:2, :] * st_ref[h] + _dot((kt * kd_ref[h, 0:1, :]).astype(BF16), vb)
            if rev:
                out_ref[sl, hs] = _head_norm(out + hf_ref[sl, hs], ng_ref[:, hs]) * _silu(z_ref[sl, hs])
            else:
                out_ref[sl, hs] = out
        return carry

    lax.fori_loop(0, nch, body, 0, unroll=not rev)


def _retention_dir(rev, u_rt, rot_c, rot_s, dec, qdec, kdec, norm_g, h_fwd, groups):
    tb = TB_SEQ
    nblk = _num_blocks(groups)
    d = 1 if rev else 0

    def rb(n):
        return (nblk - 1 - n) if rev else n

    def pos(n):
        return _seq_pos(rb(n), groups)[0]

    blk = (tb, WIDTH)
    tbl = lambda n: (d, 0, 0, 0)
    in_specs = [
        pl.BlockSpec(blk, lambda n: (rb(n), 0)),
        pl.BlockSpec(blk, lambda n: (rb(n), 1)),
        pl.BlockSpec(blk, lambda n: (rb(n), 2)),
        pl.BlockSpec((tb, HEAD_DIM), lambda n: (pos(n), 0)),
        pl.BlockSpec((tb, HEAD_DIM), lambda n: (pos(n), 0)),
        pl.BlockSpec((None, HEADS, CHUNK, CHUNK), tbl),
        pl.BlockSpec((None, HEADS, CHUNK, HEAD_DIM), tbl),
        pl.BlockSpec((None, HEADS, SUBLANES, CHUNK), tbl),
    ]
    args = [u_rt, u_rt, u_rt, rot_c, rot_s, dec, qdec, kdec]
    if rev:
        in_specs += [pl.BlockSpec(blk, lambda n: (rb(n), 3)), pl.BlockSpec(blk, lambda n: (rb(n), 0)),
                     pl.BlockSpec((1, WIDTH), lambda n: (0, 0))]
        args += [u_rt, h_fwd, norm_g]
    return pl.pallas_call(
        functools.partial(_rt_kernel, rev, groups),
        grid=(nblk,),
        in_specs=in_specs,
        out_specs=pl.BlockSpec(blk, lambda n: (rb(n), 0)),
        out_shape=jax.ShapeDtypeStruct((u_rt.shape[0], WIDTH), F32),
        scratch_shapes=[
            pltpu.VMEM((HEADS, HEAD_DIM, HEAD_DIM), F32),
            pltpu.VMEM((tb, WIDTH), F32),
            pltpu.VMEM((tb, WIDTH), F32),
        ],
        name="retention_bwd" if rev else "retention_fwd",
    )(*args)


def _retention(u_rt, rot_c, rot_s, tabs, norm_g, groups):
    dec, qdec, kdec = tabs
    h_fwd = _retention_dir(False, u_rt, rot_c, rot_s, dec, qdec, kdec, norm_g, None, groups)
    return _retention_dir(True, u_rt, rot_c, rot_s, dec, qdec, kdec, norm_g, h_fwd, groups)


def _rotary_tables(seq):
    inv = ROPE_BASE ** (-jnp.arange(0, HEAD_DIM, 2, dtype=F32) / HEAD_DIM)
    ang = jnp.arange(seq, dtype=F32)[:, None] * inv[None, :]
    cos, sin = jnp.cos(ang), jnp.sin(ang)
    return jnp.concatenate([cos, cos], axis=1), jnp.concatenate([-sin, sin], axis=1)


def _retention_tables():
    pos = np.arange(CHUNK, dtype=np.float64)
    rel = pos[:, None] - pos[None, :]
    dec = np.zeros((2, HEADS, CHUNK, CHUNK))
    qdec = np.zeros((2, HEADS, CHUNK, HEAD_DIM))
    kdec = np.zeros((2, HEADS, SUBLANES, CHUNK))
    for h in range(HEADS):
        lf, lb = RT_LOG_GAMMA_FWD[h], RT_LOG_GAMMA_BWD[h]
        dec[0, h] = np.where(rel >= 0, np.exp(np.maximum(rel, 0.0) * lf), 0.0)
        dec[1, h] = np.where(rel <= 0, np.exp(np.maximum(-rel, 0.0) * lb), 0.0)
        qdec[0, h] = np.exp((pos + 1.0) * lf)[:, None]
        qdec[1, h] = np.exp((CHUNK - pos) * lb)[:, None]
        kdec[0, h, 0] = np.exp((CHUNK - 1.0 - pos) * lf)
        kdec[1, h, 0] = np.exp(pos * lb)
        kdec[0, h, 1] = np.exp(CHUNK * lf)
        kdec[1, h, 1] = np.exp(CHUNK * lb)
    return jnp.asarray(dec, F32), jnp.asarray(qdec, F32), jnp.asarray(kdec, F32)


def _filt_kernel(ft_ref, fr_ref, f0_ref, w1_ref, b1_ref, w2_ref, b2_ref, fq_ref,
                 w3f_ref, w3b_ref, dlf_ref, dlb_ref, gb_ref, gf_ref, s_ref, hf3_ref, hr3_ref):
    lt = pl.program_id(1)
    fq = fq_ref[...]

    def hidden(ft):
        h1 = jnp.sin(fq * (_dot_f32(w1_ref[...], ft) + b1_ref[...]))
        return jnp.sin(fq * (_dot_f32(w2_ref[...], h1) + b2_ref[...]))

    @pl.when((pl.program_id(2) == 0) & (pl.program_id(3) == 0))
    def _():
        hf3_ref[...] = _split3_rows(hidden(ft_ref[...]))
        hr3_ref[...] = _split3_rows(hidden(fr_ref[...]))

    def lhs3(w):
        hi, lo = _split_hi_lo(w)
        return jnp.concatenate([hi, hi, lo], axis=1)

    ft = ft_ref[...]
    fr = fr_ref[...]
    dl_f = jnp.abs(dlf_ref[...])
    dl_b = jnp.abs(dlb_ref[...])
    fwd = _dot(lhs3(w3f_ref[...]), hf3_ref[...]) * jnp.exp(-ft[0:1, :] * dl_f)
    bwd = _dot(lhs3(w3b_ref[...]), hr3_ref[...]) * jnp.exp(-fr[0:1, :] * dl_b)
    f0 = f0_ref[...]
    bwd0 = (_dot_f32(w3b_ref[...], hidden(f0)) * jnp.exp(-f0[0:1, :] * dl_b))[:, 0:1]
    lag = lax.broadcasted_iota(jnp.int32, fwd.shape, 1) + lt * LT_FILT
    fwd = jnp.where(lag == 0, fwd + bwd0, fwd)
    bwd = jnp.where(lag == 0, 0.0, bwd)
    gb_ref[...] = bwd
    gf_ref[...] = fwd
    part = jnp.sum(jnp.abs(fwd) + jnp.abs(bwd), axis=1, keepdims=True)
    lane = lax.broadcasted_iota(jnp.int32, (LANES, LANES), 1)
    s_ref[...] = jnp.where(lane == 0, part, 0.0)


def _filters(seq, w1t, b1, w2t, b2, freq, w3t, deltas):
    t = jnp.linspace(0.0, 1.0, seq, dtype=F32)[None, :]
    w = (2.0 * math.pi / seq) * jnp.arange(seq, dtype=F32)[None, :]
    bands = jnp.linspace(1e-4, HY_BANDS - 1, HY_BANDS, dtype=F32)[:, None]
    feats = jnp.concatenate([t, jnp.cos(bands * w), -jnp.sin(bands * w),
                             jnp.zeros((FEAT_ROWS - 1 - 2 * HY_BANDS, seq), F32)], axis=0)
    feats_rev = jnp.concatenate([feats[:, :1], feats[:, :0:-1]], axis=1)
    feats0 = jnp.broadcast_to(feats[:, :1], (FEAT_ROWS, LANES))
    nlt = seq // LT_FILT
    ct = WIDTH // LANES
    lyr = lambda l, i, o, c: (l, 0, 0)
    return pl.pallas_call(
        _filt_kernel,
        grid=(DEPTH, nlt, HY_ORDER, ct),
        in_specs=[
            pl.BlockSpec((FEAT_ROWS, LT_FILT), lambda l, i, o, c: (0, i)),
            pl.BlockSpec((FEAT_ROWS, LT_FILT), lambda l, i, o, c: (0, i)),
            pl.BlockSpec((FEAT_ROWS, LANES), lambda l, i, o, c: (0, 0)),
            pl.BlockSpec((None, HY_HIDDEN, FEAT_ROWS), lyr),
            pl.BlockSpec((None, HY_HIDDEN, 1), lyr),
            pl.BlockSpec((None, HY_HIDDEN, HY_HIDDEN), lyr),
            pl.BlockSpec((None, HY_HIDDEN, 1), lyr),
            pl.BlockSpec((None, HY_HIDDEN, 1), lyr),
            pl.BlockSpec((None, LANES, HY_HIDDEN), lambda l, i, o, c: (l, o * 2 * ct + c, 0)),
            pl.BlockSpec((None, LANES, HY_HIDDEN), lambda l, i, o, c: (l, o * 2 * ct + ct + c, 0)),
            pl.BlockSpec((None, LANES, 1), lambda l, i, o, c: (l, o * 2 * ct + c, 0)),
            pl.BlockSpec((None, LANES, 1), lambda l, i, o, c: (l, o * 2 * ct + ct + c, 0)),
        ],
        out_specs=[
            pl.BlockSpec((None, None, LANES, LT_FILT), lambda l, i, o, c: (l, o, c, i)),
            pl.BlockSpec((None, None, LANES, LT_FILT), lambda l, i, o, c: (l, o, c, i)),
            pl.BlockSpec((None, None, LANES, LANES), lambda l, i, o, c: (l, o, c, i)),
        ],
        out_shape=[
            jax.ShapeDtypeStruct((DEPTH, HY_ORDER, WIDTH, seq), F32),
            jax.ShapeDtypeStruct((DEPTH, HY_ORDER, WIDTH, seq), F32),
            jax.ShapeDtypeStruct((DEPTH, HY_ORDER, WIDTH, nlt * LANES), F32),
        ],
        scratch_shapes=[pltpu.VMEM((3 * HY_HIDDEN, LT_FILT), BF16),
                        pltpu.VMEM((3 * HY_HIDDEN, LT_FILT), BF16)],
        name="hyena_filters",
    )(feats, feats_rev, feats0, w1t, b1, w2t, b2, freq, w3t, w3t, deltas, deltas)


def _split_hi_lo(x):
    hi = x.astype(BF16)
    lo = (x - hi.astype(F32)).astype(BF16)
    return hi, lo


def _split3_rows(x):
    hi, lo = _split_hi_lo(x)
    return jnp.concatenate([hi, lo, hi], axis=0)


def _split3_cols(x):
    hi, lo = _split_hi_lo(x)
    return jnp.concatenate([hi, lo, hi], axis=1)


def _np_split(m):
    hi = m.astype(ml_dtypes.bfloat16)
    lo = (m - hi.astype(np.float64)).astype(ml_dtypes.bfloat16)
    return hi, lo


def _dft_tables(seq):
    nj = seq // LANES
    n1 = 2 * nj
    n = n1 * LANES
    khp = n1 // 2 + KH_PAD
    k1 = np.arange(khp, dtype=np.float64)[:, None]
    valid = (k1 <= n1 // 2).astype(np.float64)
    ang = 2.0 * np.pi * k1 * np.arange(n1, dtype=np.float64)[None, :] / n1
    fa_full = np.concatenate([valid * np.cos(ang), -valid * np.sin(ang)], axis=0)

    def lhs3(m):
        hi, lo = _np_split(m)
        return jnp.asarray(np.concatenate([hi, hi, lo], axis=1))

    def rhs3(m):
        hi, lo = _np_split(m)
        return jnp.asarray(np.concatenate([hi, hi, lo], axis=0))

    n2 = np.arange(LANES, dtype=np.float64)
    ang2 = 2.0 * np.pi * n2[:, None] * n2[None, :] / LANES
    c2, s2 = np.cos(ang2), np.sin(ang2)
    fb = np.block([[c2, -s2], [s2, c2]])
    fbi = np.block([[c2, s2], [-s2, c2]])
    w = np.where((k1 == 0) | (k1 == n1 // 2), 1.0, 2.0) * valid
    th = 2.0 * np.pi * np.arange(nj, dtype=np.float64)[:, None] * k1.T / n1
    fai = np.concatenate([(w.T / n) * np.cos(th), -(w.T / n) * np.sin(th)], axis=1)
    phi = 2.0 * np.pi * k1 * n2[None, :] / n
    return dict(
        nj=nj, n1=n1, khp=khp,
        fa=lhs3(fa_full[:, :nj]), fa_full=lhs3(fa_full), fb=rhs3(fb), fbi=rhs3(fbi), fai=lhs3(fai),
        twc=jnp.asarray(valid * np.cos(phi), F32), tws=jnp.asarray(valid * np.sin(phi), F32))


def _dft_rows(fa_ref, twc, tws, khp, cols):
    rhs = cols[0] if len(cols) == 1 else jnp.concatenate(cols, axis=1)
    y = _dot(fa_ref[...], _split3_rows(rhs))
    out = []
    for i in range(len(cols)):
        yre = y[0:khp, i * LANES:(i + 1) * LANES]
        yim = y[khp:2 * khp, i * LANES:(i + 1) * LANES]
        out.append((yre * twc + yim * tws, yim * twc - yre * tws))
    return out


def _spec_kernel(khp, g_ref, s_ref, fa_ref, fb_ref, twc_ref, tws_ref, h_ref, ybuf):
    twc = twc_ref[...]
    tws = tws_ref[...]
    for c in range(0, CB_SPEC, 2):
        res = _dft_rows(fa_ref, twc, tws, khp, [g_ref[c], g_ref[c + 1]])
        for i, (yre, yim) in enumerate(res):
            r0 = (c + i) * khp
            ybuf[r0:r0 + khp, 0:LANES] = yre
            ybuf[r0:r0 + khp, LANES:2 * LANES] = yim
    x = _dot(_split3_cols(ybuf[...]), fb_ref[...])
    for c in range(CB_SPEC):
        norm = jnp.sum(s_ref[c:c + 1, :], axis=1, keepdims=True)
        h_ref[c] = x[c * khp:(c + 1) * khp, :] / norm


def _spectrum(g, s, tab):
    nch = g.shape[0]
    n1, khp = tab["n1"], tab["khp"]
    const = lambda a: pl.BlockSpec(a.shape, lambda i: (0, 0))
    return pl.pallas_call(
        functools.partial(_spec_kernel, khp),
        grid=(nch // CB_SPEC,),
        in_specs=[
            pl.BlockSpec((CB_SPEC, n1, LANES), lambda i: (i, 0, 0)),
            pl.BlockSpec((CB_SPEC, s.shape[-1]), lambda i: (i, 0)),
            const(tab["fa_full"]), const(tab["fb"]), const(tab["twc"]), const(tab["tws"]),
        ],
        out_specs=pl.BlockSpec((CB_SPEC, khp, 2 * LANES), lambda i: (i, 0, 0)),
        out_shape=jax.ShapeDtypeStruct((nch, khp, 2 * LANES), F32),
        scratch_shapes=[pltpu.VMEM((CB_SPEC * khp, 2 * LANES), F32)],
        name="hyena_spectrum",
    )(g, s, tab["fa_full"], tab["fb"], tab["twc"], tab["tws"])


def _hyfft_kernel(bsz, nj, khp, prm_ref, v_ref, x1_ref, x2_ref, z_ref, h_ref,
                  fa_ref, fb_ref, fbi_ref, fai_ref, twc_ref, tws_ref, out_ref,
                  xs_ref, ybuf, pbuf):
    ct = pl.program_id(0)
    rows = bsz * nj
    pairs = [(b, b + 1) for b in range(0, bsz, 2)] if bsz > 1 else [(0,)]
    twc = twc_ref[...]
    tws = tws_ref[...]

    lane = lax.broadcasted_iota(jnp.int32, (rows, LANES), 1)
    blk = lax.broadcasted_iota(jnp.int32, (rows, LANES), 0) & (nj - 1)
    first = (lane == 0) & (blk == 0)
    last = (lane == LANES - 1) & (blk == nj - 1)

    def conv3(x, base):
        r = pltpu.roll(x, 1, 1)
        xm1 = jnp.where(lane == 0, pltpu.roll(r, 1, 0), r)
        xm1 = jnp.where(first, 0.0, xm1)
        l = pltpu.roll(x, LANES - 1, 1)
        xp1 = jnp.where(lane == LANES - 1, pltpu.roll(l, rows - 1, 0), l)
        xp1 = jnp.where(last, 0.0, xp1)
        return prm_ref[base] * xm1 + prm_ref[base + 1] * x + prm_ref[base + 2] * xp1 + prm_ref[base + 3]

    def row0(c, b):
        return (c * bsz + b) * khp

    def forward(c, x):
        for pr in pairs:
            res = _dft_rows(fa_ref, twc, tws, khp, [x[b * nj:(b + 1) * nj, :] for b in pr])
            for b, (yre, yim) in zip(pr, res):
                r0 = row0(c, b)
                ybuf[r0:r0 + khp, 0:LANES] = yre
                ybuf[r0:r0 + khp, LANES:2 * LANES] = yim

    def spectral(order):
        x = _dot(_split3_cols(ybuf[...]), fb_ref[...])
        for c in range(CB_HY):
            hre = h_ref[order, c, :, 0:LANES]
            him = h_ref[order, c, :, LANES:2 * LANES]
            for b in range(bsz):
                r0 = row0(c, b)
                xre = x[r0:r0 + khp, 0:LANES]
                xim = x[r0:r0 + khp, LANES:2 * LANES]
                pbuf[r0:r0 + khp, 0:LANES] = xre * hre - xim * him
                pbuf[r0:r0 + khp, LANES:2 * LANES] = xre * him + xim * hre
        r = _dot(_split3_cols(pbuf[...]), fbi_ref[...])
        for c in range(CB_HY):
            for b in range(bsz):
                r0 = row0(c, b)
                rre = r[r0:r0 + khp, 0:LANES]
                rim = r[r0:r0 + khp, LANES:2 * LANES]
                ybuf[r0:r0 + khp, 0:LANES] = rre * twc - rim * tws
                ybuf[r0:r0 + khp, LANES:2 * LANES] = rre * tws + rim * twc

    def inverse(c):
        outs = [None] * bsz
        for pr in pairs:
            rre = [ybuf[row0(c, b):row0(c, b) + khp, 0:LANES] for b in pr]
            rim = [ybuf[row0(c, b):row0(c, b) + khp, LANES:2 * LANES] for b in pr]
            if len(pr) > 1:
                rre, rim = [jnp.concatenate(rre, axis=1)], [jnp.concatenate(rim, axis=1)]
            y = _dot(fai_ref[...], _split3_rows(jnp.concatenate([rre[0], rim[0]], axis=0)))
            for i, b in enumerate(pr):
                outs[b] = y[:, i * LANES:(i + 1) * LANES]
        return outs[0] if bsz == 1 else jnp.concatenate(outs, axis=0)

    for c in range(CB_HY):
        base = (ct * CB_HY + c) * 16
        v = conv3(v_ref[c], base)
        xs_ref[0, c] = v
        xs_ref[1, c] = conv3(x1_ref[c], base + 4)
        xs_ref[2, c] = conv3(x2_ref[c], base + 8)
        forward(c, v)
    spectral(0)
    for c in range(CB_HY):
        base = (ct * CB_HY + c) * 16
        v = xs_ref[0, c]
        z1 = xs_ref[1, c] * (inverse(c) + prm_ref[base + 12] * v)
        xs_ref[0, c] = z1
        forward(c, z1)
    spectral(1)
    for c in range(CB_HY):
        base = (ct * CB_HY + c) * 16
        z1 = xs_ref[0, c]
        y2 = inverse(c) + prm_ref[base + 13] * z1
        out_ref[c] = xs_ref[2, c] * y2 * _silu(z_ref[c])


def _hyena_fft(prm, u_t, h, tab, row0, bsz, seq):
    nj, khp = tab["nj"], tab["khp"]
    nct = WIDTH // CB_HY
    rows = bsz * nj
    blk = (CB_HY, rows, LANES)
    rblk = row0 // (rows * LANES)
    assert rblk * rows * LANES == row0
    const = lambda a: pl.BlockSpec(a.shape, lambda i: (0, 0))
    return pl.pallas_call(
        functools.partial(_hyfft_kernel, bsz, nj, khp),
        grid=(nct,),
        in_specs=[
            pl.BlockSpec(memory_space=pltpu.SMEM),
            pl.BlockSpec(blk, lambda i: (i, rblk, 0)),
            pl.BlockSpec(blk, lambda i: (nct + i, rblk, 0)),
            pl.BlockSpec(blk, lambda i: (2 * nct + i, rblk, 0)),
            pl.BlockSpec(blk, lambda i: (3 * nct + i, rblk, 0)),
            pl.BlockSpec((HY_ORDER, CB_HY, khp, 2 * LANES), lambda i: (0, i, 0, 0)),
            const(tab["fa"]), const(tab["fb"]), const(tab["fbi"]), const(tab["fai"]),
            const(tab["twc"]), const(tab["tws"]),
        ],
        out_specs=pl.BlockSpec(blk, lambda i: (i, 0, 0)),
        out_shape=jax.ShapeDtypeStruct((WIDTH, rows, LANES), F32),
        scratch_shapes=[
            pltpu.VMEM((3, CB_HY, bsz * nj, LANES), F32),
            pltpu.VMEM((CB_HY * bsz * khp, 2 * LANES), F32),
            pltpu.VMEM((CB_HY * bsz * khp, 2 * LANES), F32),
        ],
        name="hyena_fft",
    )(prm, u_t, u_t, u_t, u_t, h, tab["fa"], tab["fb"], tab["fbi"], tab["fai"],
      tab["twc"], tab["tws"])


def _out_kernel(final, starts, *refs):
    ng = len(starts)
    x_ref, yml_ref, yrt_ref = refs[0:3]
    yhy_refs = refs[3:3 + ng]
    w_ref, g_ref = refs[3 + ng:5 + ng]
    out_refs = refs[5 + ng:]
    i = pl.program_id(0)
    acc = _dot(yml_ref[...].astype(BF16), w_ref[0:WIDTH, :])
    acc = acc + _dot(yrt_ref[...].astype(BF16), w_ref[WIDTH:2 * WIDTH, :])
    w_hy = w_ref[2 * WIDTH:, :]
    parts = []
    for j in range(TM_OUT // LANES):
        y = yhy_refs[0][:, j, :]
        for gi in range(1, ng):
            y = jnp.where(i >= starts[gi], yhy_refs[gi][:, j, :], y)
        parts.append(_dot_tn(y.astype(BF16), w_hy))
    x = x_ref[...] + acc + jnp.concatenate(parts, axis=0)
    if not final:
        out_refs[0][...] = x
        return
    ms = jnp.mean(x * x, axis=-1, keepdims=True)
    x = x * lax.rsqrt(ms + RMS_EPS) * g_ref[...]
    for gi in range(ng):
        inside = i >= starts[gi]
        if gi + 1 < ng:
            inside = inside & (i < starts[gi + 1])

        @pl.when(inside)
        def _(gi=gi):
            out_refs[gi][...] = x


def _out_proj(x, y_ml, y_rt, y_hy, w, gain, final, groups):
    t = x.shape[0]
    jb = TM_OUT // LANES
    starts = [r0 // TM_OUT for (r0, _, _) in groups]
    counts = [bsz * seq // TM_OUT for (_, bsz, seq) in groups]

    def tile(gi):
        return lambda i: jnp.clip(i - starts[gi], 0, counts[gi] - 1)

    hy_specs = [pl.BlockSpec((WIDTH, jb, LANES), lambda i, f=tile(gi): (0, f(i), 0))
                for gi in range(len(groups))]
    if final:
        out_specs = [pl.BlockSpec((TM_OUT, D_MODEL), lambda i, f=tile(gi): (f(i), 0))
                     for gi in range(len(groups))]
        out_shape = [jax.ShapeDtypeStruct((bsz * seq, D_MODEL), F32) for (_, bsz, seq) in groups]
    else:
        out_specs = [pl.BlockSpec((TM_OUT, D_MODEL), lambda i: (i, 0))]
        out_shape = [jax.ShapeDtypeStruct((t, D_MODEL), F32)]
    return pl.pallas_call(
        functools.partial(_out_kernel, final, starts),
        grid=(t // TM_OUT,),
        in_specs=[
            pl.BlockSpec((TM_OUT, D_MODEL), lambda i: (i, 0)),
            pl.BlockSpec((TM_OUT, WIDTH), lambda i: (i, 0)),
            pl.BlockSpec((TM_OUT, WIDTH), lambda i: (i, 0)),
            *hy_specs,
            pl.BlockSpec((3 * WIDTH, D_MODEL), lambda i: (0, 0)),
            pl.BlockSpec((1, D_MODEL), lambda i: (0, 0)),
        ],
        out_specs=out_specs,
        out_shape=out_shape,
        name="out_proj",
    )(x, y_ml, y_rt, *y_hy, w, gain)


def kernel(x_prompt, x_sample, norm_g, w_in, ml_conv_w, ml_conv_b, ml_gate_b, ml_norm_g, rt_norm_g, hy_conv_w, hy_conv_b, hy_w1, hy_b1, hy_w2, hy_b2, hy_w3, hy_freq, hy_deltas, hy_skip, w_out, final_g):
    groups = []
    row0 = 0
    for xg in (x_prompt, x_sample):
        bsz, seq, _ = xg.shape
        groups.append((row0, bsz, seq))
        row0 += bsz * seq
    x = jnp.concatenate([x_prompt.reshape(-1, D_MODEL), x_sample.reshape(-1, D_MODEL)], axis=0)

    c_gate = ML_COLS
    c_rt = c_gate + 4 * HEADS
    c_hy = c_rt + RT_COLS
    gate_pad = jnp.zeros((DEPTH, D_MODEL, LANES - 4 * HEADS), F32)
    w_tok = jnp.concatenate([w_in[:, :, :c_gate], w_in[:, :, c_gate:c_rt], gate_pad,
                             w_in[:, :, c_rt:c_hy]], axis=2).astype(BF16)
    w_gt = jnp.swapaxes(w_in[:, :, c_gate:c_rt], 1, 2).astype(BF16)
    w_hyt = jnp.swapaxes(w_in[:, :, c_hy:], 1, 2).astype(BF16)
    w_out_b = w_out.astype(BF16)
    gate_b = ml_gate_b.reshape(DEPTH, 1, 4 * HEADS)
    gate_b_row = jnp.concatenate([gate_b, jnp.zeros((DEPTH, 1, LANES - 4 * HEADS), F32)], axis=2)
    gate_b_col = ml_gate_b.reshape(DEPTH, 4 * HEADS, 1)

    cw = hy_conv_w.reshape(DEPTH, 3, 3, WIDTH)
    cbias = hy_conv_b.reshape(DEPTH, 1, 3, WIDTH)
    taps = jnp.concatenate([cw, cbias], axis=1)
    taps = jnp.transpose(taps, (0, 3, 2, 1)).reshape(DEPTH, WIDTH, 12)
    hy_prm = jnp.concatenate([taps, jnp.transpose(hy_skip, (0, 2, 1)),
                              jnp.zeros((DEPTH, WIDTH, 2), F32)], axis=2).reshape(DEPTH, WIDTH * 16)

    w1t = jnp.concatenate([jnp.swapaxes(hy_w1, 1, 2),
                           jnp.zeros((DEPTH, HY_HIDDEN, FEAT_ROWS - hy_w1.shape[1]), F32)], axis=2)
    w2t = jnp.swapaxes(hy_w2, 1, 2)
    w3t = jnp.swapaxes(hy_w3, 1, 2)
    b1 = hy_b1[:, :, None]
    b2 = hy_b2[:, :, None]
    freq = hy_freq[:, :, None]
    deltas = hy_deltas.reshape(DEPTH, HY_ORDER * 2 * WIDTH, 1)

    tables, spectra = [], []
    for (_, _, seq) in groups:
        tab = _dft_tables(seq)
        gb, gf, s = _filters(seq, w1t, b1, w2t, b2, freq, w3t, deltas)
        g = jnp.concatenate([gf, gb], axis=-1).reshape(DEPTH * HY_ORDER * WIDTH, tab["n1"], LANES)
        h = _spectrum(g, s.reshape(DEPTH * HY_ORDER * WIDTH, -1), tab)
        tables.append(tab)
        spectra.append(h.reshape(DEPTH, HY_ORDER, WIDTH, tab["khp"], 2 * LANES))
    rot_c, rot_s = _rotary_tables(max(seq for (_, _, seq) in groups))
    rt_tabs = _retention_tables()

    for layer in range(DEPTH):
        gain = norm_g[layer][None, :]
        u_ml, u_g, u_rt, u_gt = _in_proj(x, gain, w_tok[layer], w_gt[layer])
        u_hyt = _in_proj_hy(x, gain, w_hyt[layer])
        pb, cb, gr, gs = _gates(u_g, u_gt, gate_b_row[layer], gate_b_col[layer])
        y_ml = _mlstm(u_ml, pb, cb, gr, gs, ml_conv_w[layer], ml_conv_b[layer][None, :],
                      ml_norm_g[layer][None, :], groups)
        y_rt = _retention(u_rt, rot_c, rot_s, rt_tabs, rt_norm_g[layer][None, :], groups)
        y_hy = [_hyena_fft(hy_prm[layer], u_hyt, spectra[gi][layer], tables[gi], r0, bsz, seq)
                for gi, (r0, bsz, seq) in enumerate(groups)]
        outs = _out_proj(x, y_ml, y_rt, y_hy, w_out_b[layer], final_g[None, :],
                         layer == DEPTH - 1, groups)
        x = outs[0]

    return outs[0].reshape(x_prompt.shape), outs[1].reshape(x_sample.shape)
```

```python
import functools
import math

import ml_dtypes
import numpy as np
import jax
import jax.numpy as jnp
from jax import lax
from jax.experimental import pallas as pl
from jax.experimental.pallas import tpu as pltpu

F32 = jnp.float32
BF16 = jnp.bfloat16

D_MODEL = 1024
DEPTH = 4
CHUNK = 128
HEADS = 4
HEAD_DIM = 128
WIDTH = 512
HY_ORDER = 2
HY_BANDS = 16
HY_HIDDEN = 64
FEAT_ROWS = 40
ROPE_BASE = 10000.0
RMS_EPS = 1e-6
HEAD_NORM_EPS = 1e-5
M_INIT = -1e30
RT_LOG_GAMMA_FWD = tuple(math.log(1.0 - 2.0 ** (-5.0 - h)) for h in range(HEADS))
RT_LOG_GAMMA_BWD = tuple(math.log(1.0 - 2.0 ** (-5.5 - h)) for h in range(HEADS))

LANES = 128
SUBLANES = 8
TM_IN = 256
TM_HY = 1024
TR_HY = 1024
TM_OUT = 1024
TB_SEQ = 512
TG = 1024
CB_HY = 8
CB_SPEC = 8
KH_PAD = 8
LT_FILT = 2048
ML_COLS = 5 * WIDTH
RT_COLS = 4 * WIDTH
HY_ROWS = 4 * WIDTH
HIGHEST = lax.Precision.HIGHEST


def _dot(a, b):
    return jnp.dot(a, b, preferred_element_type=F32)


def _dot_nt(a, b):
    return lax.dot_general(a, b, (((1,), (1,)), ((), ())), preferred_element_type=F32)


def _dot_tn(a, b):
    return lax.dot_general(a, b, (((0,), (0,)), ((), ())), preferred_element_type=F32)


def _dot_f32(a, b):
    return jnp.dot(a, b, precision=HIGHEST, preferred_element_type=F32)


def _silu(y):
    return y * jax.nn.sigmoid(y)


def _log_sigmoid(x):
    return -(jnp.maximum(-x, 0.0) + jnp.log1p(jnp.exp(-jnp.abs(x))))


def _head_norm(h, gain):
    mu = jnp.mean(h, axis=-1, keepdims=True)
    hc = h - mu
    var = jnp.mean(hc * hc, axis=-1, keepdims=True)
    return hc * lax.rsqrt(var + HEAD_NORM_EPS) * gain


def _rms_bf16(x_ref, g_ref):
    x = x_ref[...]
    ms = jnp.mean(x * x, axis=-1, keepdims=True)
    return (x * lax.rsqrt(ms + RMS_EPS) * g_ref[...]).astype(BF16)


def _in_kernel(x_ref, g_ref, wtok_ref, wgt_ref, oml_ref, og_ref, ort_ref, ogt_ref):
    h = _rms_bf16(x_ref, g_ref)
    oml_ref[...] = _dot(h, wtok_ref[:, 0:ML_COLS])
    og_ref[...] = _dot(h, wtok_ref[:, ML_COLS:ML_COLS + LANES])
    ort_ref[...] = _dot(h, wtok_ref[:, ML_COLS + LANES:])
    ogt_ref[...] = _dot_nt(wgt_ref[...], h)


def _in_proj(x, gain, w_tok, w_gt):
    t = x.shape[0]
    ncol = w_tok.shape[1]
    return pl.pallas_call(
        _in_kernel,
        grid=(t // TM_IN,),
        in_specs=[
            pl.BlockSpec((TM_IN, D_MODEL), lambda i: (i, 0)),
            pl.BlockSpec((1, D_MODEL), lambda i: (0, 0)),
            pl.BlockSpec((D_MODEL, ncol), lambda i: (0, 0)),
            pl.BlockSpec((4 * HEADS, D_MODEL), lambda i: (0, 0)),
        ],
        out_specs=[
            pl.BlockSpec((TM_IN, ML_COLS), lambda i: (i, 0)),
            pl.BlockSpec((TM_IN, LANES), lambda i: (i, 0)),
            pl.BlockSpec((TM_IN, RT_COLS), lambda i: (i, 0)),
            pl.BlockSpec((4 * HEADS, TM_IN), lambda i: (0, i)),
        ],
        out_shape=[
            jax.ShapeDtypeStruct((t, ML_COLS), F32),
            jax.ShapeDtypeStruct((t, LANES), F32),
            jax.ShapeDtypeStruct((t, RT_COLS), F32),
            jax.ShapeDtypeStruct((4 * HEADS, t), F32),
        ],
        name="in_proj",
    )(x, gain, w_tok, w_gt)


def _in_hy_kernel(x_ref, g_ref, w_ref, o_ref):
    y = _dot_nt(w_ref[...], _rms_bf16(x_ref, g_ref))
    o_ref[...] = y.reshape(o_ref.shape)


def _in_proj_hy(x, gain, w_hyt):
    t = x.shape[0]
    nr = HY_ROWS // TR_HY
    return pl.pallas_call(
        _in_hy_kernel,
        grid=(t // TM_HY, nr),
        in_specs=[
            pl.BlockSpec((TM_HY, D_MODEL), lambda i, r: (i, 0)),
            pl.BlockSpec((1, D_MODEL), lambda i, r: (0, 0)),
            pl.BlockSpec((TR_HY, D_MODEL), lambda i, r: (r, 0)),
        ],
        out_specs=pl.BlockSpec((TR_HY, TM_HY // LANES, LANES), lambda i, r: (r, i, 0)),
        out_shape=jax.ShapeDtypeStruct((HY_ROWS, t // LANES, LANES), F32),
        name="in_proj_hyena",
    )(x, gain, w_hyt)


def _gates_kernel(g_ref, gt_ref, b_ref, bt_ref, pb_ref, cb_ref, gr_ref, gs_ref):
    row = lax.broadcasted_iota(jnp.int32, (CHUNK, CHUNK), 0)
    col = lax.broadcasted_iota(jnp.int32, (CHUNK, CHUNK), 1)
    lower = (row >= col).astype(F32)
    upper = (row <= col).astype(F32)
    gr_ref[...] = jnp.zeros_like(gr_ref)
    gs_ref[...] = jnp.zeros_like(gs_ref)
    for c in range(TG // CHUNK):
        sl = slice(c * CHUNK, (c + 1) * CHUNK)
        lf = _log_sigmoid(g_ref[sl, :] + b_ref[...])
        cum_c = (_dot_f32(lower, lf), _dot_f32(upper, lf))
        pre_t = gt_ref[:, sl] + bt_ref[...]
        lf_t = _log_sigmoid(pre_t)
        cum_r = (_dot_f32(lf_t, upper), _dot_f32(lf_t, lower))
        for d in range(2):
            for h in range(HEADS):
                kf = (2 * d + 1) * HEADS + h
                ki = 2 * d * HEADS + h
                cum = cum_r[d][kf:kf + 1, :]
                li = pre_t[ki:ki + 1, :]
                r = li - cum
                f_tot = cum[:, 0:1] if d else cum[:, CHUNK - 1:CHUNK]
                g = f_tot - cum + li
                allowed = (row <= col) if d else (row >= col)
                pmax = jnp.max(jnp.where(allowed, r, -jnp.inf), axis=1, keepdims=True)
                pb_ref[d, h, sl, :] = jnp.broadcast_to(pmax, (CHUNK, LANES))
                cb_ref[d, h, sl, :] = jnp.broadcast_to(cum_c[d][:, kf:kf + 1], (CHUNK, LANES))
                gr_ref[d, h, 0:1, sl] = r
                gr_ref[d, h, 1:2, sl] = g
                gs_ref[d, h, c, 0:1, :] = jnp.broadcast_to(f_tot, (1, LANES))
                gs_ref[d, h, c, 1:2, :] = jnp.broadcast_to(jnp.max(g, axis=1, keepdims=True), (1, LANES))


def _gates(g, gt, bias_row, bias_col):
    t = g.shape[0]
    nc = TG // CHUNK
    return pl.pallas_call(
        _gates_kernel,
        grid=(t // TG,),
        in_specs=[
            pl.BlockSpec((TG, LANES), lambda i: (i, 0)),
            pl.BlockSpec((4 * HEADS, TG), lambda i: (0, i)),
            pl.BlockSpec((1, LANES), lambda i: (0, 0)),
            pl.BlockSpec((4 * HEADS, 1), lambda i: (0, 0)),
        ],
        out_specs=[
            pl.BlockSpec((2, HEADS, TG, LANES), lambda i: (0, 0, i, 0)),
            pl.BlockSpec((2, HEADS, TG, LANES), lambda i: (0, 0, i, 0)),
            pl.BlockSpec((2, HEADS, SUBLANES, TG), lambda i: (0, 0, 0, i)),
            pl.BlockSpec((2, HEADS, nc, SUBLANES, LANES), lambda i: (0, 0, i, 0, 0)),
        ],
        out_shape=[
            jax.ShapeDtypeStruct((2, HEADS, t, LANES), F32),
            jax.ShapeDtypeStruct((2, HEADS, t, LANES), F32),
            jax.ShapeDtypeStruct((2, HEADS, SUBLANES, t), F32),
            jax.ShapeDtypeStruct((2, HEADS, t // CHUNK, SUBLANES, LANES), F32),
        ],
        name="gates",
    )(g, gt, bias_row, bias_col)


def _seq_pos(g, groups):
    pos, nbs = None, None
    start = 0
    for (_, bsz, seq) in groups:
        nb = seq // TB_SEQ
        p = (g - start) % nb
        pos = p if pos is None else jnp.where(g >= start, p, pos)
        nbs = nb if nbs is None else jnp.where(g >= start, nb, nbs)
        start += bsz * nb
    return pos, nbs


def _num_blocks(groups):
    return sum(bsz * seq // TB_SEQ for (_, bsz, seq) in groups)


def _ml_kernel(rev, groups, *refs):
    if rev:
        (qs_ref, ks_ref, v_ref, pb_ref, cb_ref, gr_ref, gs_ref, o_ref, z_ref, hf_ref, ng_ref,
         out_ref, st_ref, m_ref) = refs
    else:
        (q_ref, k_ref, v_ref, qp_ref, qn_ref, kp_ref, kn_ref, pb_ref, cb_ref, gr_ref, gs_ref,
         cw_ref, cbias_ref, out_ref, qs_ref, ks_ref, st_ref, m_ref, stg_ref) = refs
    tb = TB_SEQ
    nch = tb // CHUNK
    n = pl.program_id(0)
    sb, nb = _seq_pos((_num_blocks(groups) - 1 - n) if rev else n, groups)

    @pl.when(sb == (nb - 1 if rev else 0))
    def _():
        st_ref[...] = jnp.zeros_like(st_ref)
        m_ref[...] = jnp.full_like(m_ref, M_INIT)

    def conv_silu(raw_ref, prev_ref, next_ref, c0):
        stg_ref[SUBLANES:SUBLANES + tb, :] = raw_ref[...]
        stg_ref[SUBLANES - 1:SUBLANES, :] = jnp.where(sb > 0, prev_ref[SUBLANES - 1:SUBLANES, :], 0.0)
        stg_ref[SUBLANES + tb:SUBLANES + tb + 1, :] = jnp.where(sb < nb - 1, next_ref[0:1, :], 0.0)
        w = cw_ref[:, c0:c0 + WIDTH]
        y = (stg_ref[SUBLANES - 1:SUBLANES - 1 + tb, :] * w[0:1, :]
             + stg_ref[SUBLANES:SUBLANES + tb, :] * w[1:2, :]
             + stg_ref[SUBLANES + 1:SUBLANES + 1 + tb, :] * w[2:3, :] + cbias_ref[:, c0:c0 + WIDTH])
        return _silu(y)

    if not rev:
        qs_ref[...] = conv_silu(q_ref, qp_ref, qn_ref, 0).astype(BF16)
        ks_ref[...] = conv_silu(k_ref, kp_ref, kn_ref, WIDTH) * (HEAD_DIM ** -0.5)

    row = lax.broadcasted_iota(jnp.int32, (CHUNK, CHUNK), 0)
    col = lax.broadcasted_iota(jnp.int32, (CHUNK, CHUNK), 1)
    allowed = (row <= col) if rev else (row >= col)
    ones = jnp.ones((CHUNK, HEAD_DIM), BF16)

    def chunk(c, h):
        sl = pl.ds(pl.multiple_of(c * CHUNK, CHUNK), CHUNK)
        hs = slice(h * HEAD_DIM, (h + 1) * HEAD_DIM)
        q = qs_ref[sl, hs]
        kt = ks_ref[sl, hs].T
        v2 = jnp.concatenate([v_ref[sl, hs].astype(BF16), ones], axis=1)
        m_prev = m_ref[h, 0:1, :]
        mb = jnp.maximum(pb_ref[h, sl, :], m_prev)
        p = jnp.exp(jnp.where(allowed, gr_ref[h, 0:1, sl] - mb, -jnp.inf))
        w_inter = jnp.exp(m_prev - mb)
        floor = jnp.exp(-(cb_ref[h, sl, :] + mb))
        s = _dot(q, kt.astype(BF16)) * p
        inter = _dot(q, st_ref[h].astype(BF16))
        num2 = _dot(s.astype(BF16), v2) + jnp.concatenate([w_inter, w_inter], axis=1) * inter
        out = num2[:, 0:HEAD_DIM] / jnp.maximum(jnp.abs(num2[:, HEAD_DIM:]), floor)
        f_tot = gs_ref[h, c, 0:1, :]
        m_chunk = gs_ref[h, c, 1:2, :]
        m_new = jnp.maximum(f_tot + m_prev, m_chunk)
        kw = (kt * jnp.exp(gr_ref[h, 1:2, sl] - m_new)).astype(BF16)
        kv2 = _dot(kw, v2)
        a = jnp.exp(f_tot + m_prev - m_new)
        st_ref[h] = jnp.concatenate([a, a], axis=1) * st_ref[h] + kv2
        m_ref[h] = jnp.broadcast_to(m_new, (SUBLANES, LANES))
        return out, sl, hs

    def body(i, carry):
        c = (nch - 1 - i) if rev else i
        for h in range(HEADS):
            out, sl, hs = chunk(c, h)
            if rev:
                hh = jax.nn.sigmoid(o_ref[sl, hs]) * (out + hf_ref[sl, hs])
                out_ref[sl, hs] = _head_norm(hh, ng_ref[:, hs]) * _silu(z_ref[sl, hs])
            else:
                out_ref[sl, hs] = out
        return carry

    lax.fori_loop(0, nch, body, 0, unroll=True)


def _mlstm_dir(rev, u_ml, pb, cb, gr, gs, conv_w, conv_b, norm_g, fwd, groups):
    t = u_ml.shape[0]
    tb = TB_SEQ
    t8 = tb // SUBLANES
    nch = tb // CHUNK
    nblk = _num_blocks(groups)
    d = 1 if rev else 0

    def rb(n):
        return (nblk - 1 - n) if rev else n

    def col(g):
        return lambda n: (rb(n), g)

    def halo_prev(g):
        return lambda n: (jnp.maximum(rb(n) * t8 - 1, 0), g)

    def halo_next(g):
        return lambda n: (jnp.minimum((rb(n) + 1) * t8, t // SUBLANES - 1), g)

    blk = (tb, WIDTH)
    halo = (SUBLANES, WIDTH)
    gate_specs = [
        pl.BlockSpec((None, HEADS, tb, LANES), lambda n: (d, 0, rb(n), 0)),
        pl.BlockSpec((None, HEADS, tb, LANES), lambda n: (d, 0, rb(n), 0)),
        pl.BlockSpec((None, HEADS, SUBLANES, tb), lambda n: (d, 0, 0, rb(n))),
        pl.BlockSpec((None, HEADS, nch, SUBLANES, LANES), lambda n: (d, 0, rb(n), 0, 0)),
    ]
    scratch = [
        pltpu.VMEM((HEADS, HEAD_DIM, 2 * HEAD_DIM), F32),
        pltpu.VMEM((HEADS, SUBLANES, LANES), F32),
    ]
    full = jax.ShapeDtypeStruct((t, WIDTH), F32)
    if rev:
        qs, ks, h_fwd = fwd
        in_specs = [pl.BlockSpec(blk, col(0)), pl.BlockSpec(blk, col(0)), pl.BlockSpec(blk, col(2)),
                    *gate_specs,
                    pl.BlockSpec(blk, col(3)), pl.BlockSpec(blk, col(4)), pl.BlockSpec(blk, col(0)),
                    pl.BlockSpec((1, WIDTH), lambda n: (0, 0))]
        args = [qs, ks, u_ml, pb, cb, gr, gs, u_ml, u_ml, h_fwd, norm_g]
        out_specs, out_shape = pl.BlockSpec(blk, col(0)), full
    else:
        in_specs = [pl.BlockSpec(blk, col(0)), pl.BlockSpec(blk, col(1)), pl.BlockSpec(blk, col(2)),
                    pl.BlockSpec(halo, halo_prev(0)), pl.BlockSpec(halo, halo_next(0)),
                    pl.BlockSpec(halo, halo_prev(1)), pl.BlockSpec(halo, halo_next(1)),
                    *gate_specs,
                    pl.BlockSpec((3, 2 * WIDTH), lambda n: (0, 0)),
                    pl.BlockSpec((1, 2 * WIDTH), lambda n: (0, 0))]
        args = [u_ml, u_ml, u_ml, u_ml, u_ml, u_ml, u_ml, pb, cb, gr, gs, conv_w, conv_b]
        out_specs = [pl.BlockSpec(blk, col(0))] * 3
        out_shape = [full, jax.ShapeDtypeStruct((t, WIDTH), BF16), full]
        scratch.append(pltpu.VMEM((tb + 2 * SUBLANES, WIDTH), F32))
    return pl.pallas_call(
        functools.partial(_ml_kernel, rev, groups),
        grid=(nblk,),
        in_specs=in_specs,
        out_specs=out_specs,
        out_shape=out_shape,
        scratch_shapes=scratch,
        name="mlstm_bwd" if rev else "mlstm_fwd",
    )(*args)


def _mlstm(u_ml, pb, cb, gr, gs, conv_w, conv_b, norm_g, groups):
    h_fwd, qs, ks = _mlstm_dir(False, u_ml, pb, cb, gr, gs, conv_w, conv_b, norm_g, None, groups)
    return _mlstm_dir(True, u_ml, pb, cb, gr, gs, conv_w, conv_b, norm_g, (qs, ks, h_fwd), groups)


def _rt_kernel(rev, groups, *refs):
    if rev:
        (qs_ref, ks_ref, v_ref, dec_ref, qd_ref, kd_ref, z_ref, hf_ref, ng_ref,
         out_ref, st_ref) = refs
    else:
        (q_ref, k_ref, v_ref, cc_ref, ss_ref, dec_ref, qd_ref, kd_ref,
         out_ref, qs_ref, ks_ref, st_ref) = refs
    nch = TB_SEQ // CHUNK
    n = pl.program_id(0)
    sb, nb = _seq_pos((_num_blocks(groups) - 1 - n) if rev else n, groups)

    @pl.when(sb == (nb - 1 if rev else 0))
    def _():
        st_ref[...] = jnp.zeros_like(st_ref)

    if not rev:
        for h in range(HEADS):
            hs = slice(h * HEAD_DIM, (h + 1) * HEAD_DIM)
            cc = cc_ref[...]
            ss = ss_ref[...]
            q = q_ref[:, hs]
            k = k_ref[:, hs]
            qs_ref[:, hs] = (q * cc + pltpu.roll(q, HEAD_DIM // 2, 1) * ss) * (HEAD_DIM ** -0.5)
            ks_ref[:, hs] = k * cc + pltpu.roll(k, HEAD_DIM // 2, 1) * ss

    def body(i, carry):
        c = (nch - 1 - i) if rev else i
        sl = pl.ds(pl.multiple_of(c * CHUNK, CHUNK), CHUNK)
        for h in range(HEADS):
            hs = slice(h * HEAD_DIM, (h + 1) * HEAD_DIM)
            q = qs_ref[sl, hs]
            kt = ks_ref[sl, hs].T
            vb = v_ref[sl, hs].astype(BF16)
            s = _dot(q.astype(BF16), kt.astype(BF16)) * dec_ref[h]
            out = _dot(s.astype(BF16), vb) + _dot((q * qd_ref[h]).astype(BF16), st_ref[h].astype(BF16))
            st_ref[h] = kd_ref[h, 1:2, :] * st_ref[h] + _dot((kt * kd_ref[h, 0:1, :]).astype(BF16), vb)
            if rev:
                out_ref[sl, hs] = _head_norm(out + hf_ref[sl, hs], ng_ref[:, hs]) * _silu(z_ref[sl, hs])
            else:
                out_ref[sl, hs] = out
        return carry

    lax.fori_loop(0, nch, body, 0, unroll=not rev)


def _retention_dir(rev, u_rt, rot_c, rot_s, dec, qdec, kdec, norm_g, fwd, groups):
    tb = TB_SEQ
    nblk = _num_blocks(groups)
    d = 1 if rev else 0

    def rb(n):
        return (nblk - 1 - n) if rev else n

    def pos(n):
        return _seq_pos(rb(n), groups)[0]

    def col(g):
        return lambda n: (rb(n), g)

    blk = (tb, WIDTH)
    tbl = lambda n: (d, 0, 0, 0)
    tab_specs = [
        pl.BlockSpec((None, HEADS, CHUNK, CHUNK), tbl),
        pl.BlockSpec((None, HEADS, CHUNK, HEAD_DIM), tbl),
        pl.BlockSpec((None, HEADS, SUBLANES, CHUNK), tbl),
    ]
    full = jax.ShapeDtypeStruct((u_rt.shape[0], WIDTH), F32)
    if rev:
        qs, ks, h_fwd = fwd
        in_specs = [pl.BlockSpec(blk, col(0)), pl.BlockSpec(blk, col(0)), pl.BlockSpec(blk, col(2)),
                    *tab_specs,
                    pl.BlockSpec(blk, col(3)), pl.BlockSpec(blk, col(0)),
                    pl.BlockSpec((1, WIDTH), lambda n: (0, 0))]
        args = [qs, ks, u_rt, dec, qdec, kdec, u_rt, h_fwd, norm_g]
        out_specs, out_shape = pl.BlockSpec(blk, col(0)), full
    else:
        in_specs = [pl.BlockSpec(blk, col(0)), pl.BlockSpec(blk, col(1)), pl.BlockSpec(blk, col(2)),
                    pl.BlockSpec((tb, HEAD_DIM), lambda n: (pos(n), 0)),
                    pl.BlockSpec((tb, HEAD_DIM), lambda n: (pos(n), 0)),
                    *tab_specs]
        args = [u_rt, u_rt, u_rt, rot_c, rot_s, dec, qdec, kdec]
        out_specs, out_shape = [pl.BlockSpec(blk, col(0))] * 3, [full] * 3
    return pl.pallas_call(
        functools.partial(_rt_kernel, rev, groups),
        grid=(nblk,),
        in_specs=in_specs,
        out_specs=out_specs,
        out_shape=out_shape,
        scratch_shapes=[pltpu.VMEM((HEADS, HEAD_DIM, HEAD_DIM), F32)],
        name="retention_bwd" if rev else "retention_fwd",
    )(*args)


def _retention(u_rt, rot_c, rot_s, tabs, norm_g, groups):
    dec, qdec, kdec = tabs
    h_fwd, qs, ks = _retention_dir(False, u_rt, rot_c, rot_s, dec, qdec, kdec, norm_g, None, groups)
    return _retention_dir(True, u_rt, rot_c, rot_s, dec, qdec, kdec, norm_g, (qs, ks, h_fwd), groups)


def _rotary_tables(seq):
    inv = ROPE_BASE ** (-jnp.arange(0, HEAD_DIM, 2, dtype=F32) / HEAD_DIM)
    ang = jnp.arange(seq, dtype=F32)[:, None] * inv[None, :]
    cos, sin = jnp.cos(ang), jnp.sin(ang)
    return jnp.concatenate([cos, cos], axis=1), jnp.concatenate([-sin, sin], axis=1)


def _retention_tables():
    pos = np.arange(CHUNK, dtype=np.float64)
    rel = pos[:, None] - pos[None, :]
    dec = np.zeros((2, HEADS, CHUNK, CHUNK))
    qdec = np.zeros((2, HEADS, CHUNK, HEAD_DIM))
    kdec = np.zeros((2, HEADS, SUBLANES, CHUNK))
    for h in range(HEADS):
        lf, lb = RT_LOG_GAMMA_FWD[h], RT_LOG_GAMMA_BWD[h]
        dec[0, h] = np.where(rel >= 0, np.exp(np.maximum(rel, 0.0) * lf), 0.0)
        dec[1, h] = np.where(rel <= 0, np.exp(np.maximum(-rel, 0.0) * lb), 0.0)
        qdec[0, h] = np.exp((pos + 1.0) * lf)[:, None]
        qdec[1, h] = np.exp((CHUNK - pos) * lb)[:, None]
        kdec[0, h, 0] = np.exp((CHUNK - 1.0 - pos) * lf)
        kdec[1, h, 0] = np.exp(pos * lb)
        kdec[0, h, 1] = np.exp(CHUNK * lf)
        kdec[1, h, 1] = np.exp(CHUNK * lb)
    return jnp.asarray(dec, F32), jnp.asarray(qdec, F32), jnp.asarray(kdec, F32)


def _filt_kernel(ft_ref, fr_ref, f0_ref, w1_ref, b1_ref, w2_ref, b2_ref, fq_ref,
                 w3f_ref, w3b_ref, dlf_ref, dlb_ref, gb_ref, gf_ref, s_ref, hf3_ref, hr3_ref):
    lt = pl.program_id(1)
    fq = fq_ref[...]

    def hidden(ft):
        h1 = jnp.sin(fq * (_dot_f32(w1_ref[...], ft) + b1_ref[...]))
        return jnp.sin(fq * (_dot_f32(w2_ref[...], h1) + b2_ref[...]))

    @pl.when((pl.program_id(2) == 0) & (pl.program_id(3) == 0))
    def _():
        hf3_ref[...] = _split3_rows(hidden(ft_ref[...]))
        hr3_ref[...] = _split3_rows(hidden(fr_ref[...]))

    def lhs3(w):
        hi, lo = _split_hi_lo(w)
        return jnp.concatenate([hi, hi, lo], axis=1)

    ft = ft_ref[...]
    fr = fr_ref[...]
    dl_f = jnp.abs(dlf_ref[...])
    dl_b = jnp.abs(dlb_ref[...])
    fwd = _dot(lhs3(w3f_ref[...]), hf3_ref[...]) * jnp.exp(-ft[0:1, :] * dl_f)
    bwd = _dot(lhs3(w3b_ref[...]), hr3_ref[...]) * jnp.exp(-fr[0:1, :] * dl_b)
    f0 = f0_ref[...]
    bwd0 = (_dot_f32(w3b_ref[...], hidden(f0)) * jnp.exp(-f0[0:1, :] * dl_b))[:, 0:1]
    lag = lax.broadcasted_iota(jnp.int32, fwd.shape, 1) + lt * LT_FILT
    fwd = jnp.where(lag == 0, fwd + bwd0, fwd)
    bwd = jnp.where(lag == 0, 0.0, bwd)
    gb_ref[...] = bwd
    gf_ref[...] = fwd
    part = jnp.sum(jnp.abs(fwd) + jnp.abs(bwd), axis=1, keepdims=True)
    lane = lax.broadcasted_iota(jnp.int32, (LANES, LANES), 1)
    s_ref[...] = jnp.where(lane == 0, part, 0.0)


def _filters(seq, w1t, b1, w2t, b2, freq, w3t, deltas):
    t = jnp.linspace(0.0, 1.0, seq, dtype=F32)[None, :]
    w = (2.0 * math.pi / seq) * jnp.arange(seq, dtype=F32)[None, :]
    bands = jnp.linspace(1e-4, HY_BANDS - 1, HY_BANDS, dtype=F32)[:, None]
    feats = jnp.concatenate([t, jnp.cos(bands * w), -jnp.sin(bands * w),
                             jnp.zeros((FEAT_ROWS - 1 - 2 * HY_BANDS, seq), F32)], axis=0)
    feats_rev = jnp.concatenate([feats[:, :1], feats[:, :0:-1]], axis=1)
    feats0 = jnp.broadcast_to(feats[:, :1], (FEAT_ROWS, LANES))
    nlt = seq // LT_FILT
    ct = WIDTH // LANES
    lyr = lambda l, i, o, c: (l, 0, 0)
    return pl.pallas_call(
        _filt_kernel,
        grid=(DEPTH, nlt, HY_ORDER, ct),
        in_specs=[
            pl.BlockSpec((FEAT_ROWS, LT_FILT), lambda l, i, o, c: (0, i)),
            pl.BlockSpec((FEAT_ROWS, LT_FILT), lambda l, i, o, c: (0, i)),
            pl.BlockSpec((FEAT_ROWS, LANES), lambda l, i, o, c: (0, 0)),
            pl.BlockSpec((None, HY_HIDDEN, FEAT_ROWS), lyr),
            pl.BlockSpec((None, HY_HIDDEN, 1), lyr),
            pl.BlockSpec((None, HY_HIDDEN, HY_HIDDEN), lyr),
            pl.BlockSpec((None, HY_HIDDEN, 1), lyr),
            pl.BlockSpec((None, HY_HIDDEN, 1), lyr),
            pl.BlockSpec((None, LANES, HY_HIDDEN), lambda l, i, o, c: (l, o * 2 * ct + c, 0)),
            pl.BlockSpec((None, LANES, HY_HIDDEN), lambda l, i, o, c: (l, o * 2 * ct + ct + c, 0)),
            pl.BlockSpec((None, LANES, 1), lambda l, i, o, c: (l, o * 2 * ct + c, 0)),
            pl.BlockSpec((None, LANES, 1), lambda l, i, o, c: (l, o * 2 * ct + ct + c, 0)),
        ],
        out_specs=[
            pl.BlockSpec((None, None, LANES, LT_FILT), lambda l, i, o, c: (l, o, c, i)),
            pl.BlockSpec((None, None, LANES, LT_FILT), lambda l, i, o, c: (l, o, c, i)),
            pl.BlockSpec((None, None, LANES, LANES), lambda l, i, o, c: (l, o, c, i)),
        ],
        out_shape=[
            jax.ShapeDtypeStruct((DEPTH, HY_ORDER, WIDTH, seq), F32),
            jax.ShapeDtypeStruct((DEPTH, HY_ORDER, WIDTH, seq), F32),
            jax.ShapeDtypeStruct((DEPTH, HY_ORDER, WIDTH, nlt * LANES), F32),
        ],
        scratch_shapes=[pltpu.VMEM((3 * HY_HIDDEN, LT_FILT), BF16),
                        pltpu.VMEM((3 * HY_HIDDEN, LT_FILT), BF16)],
        name="hyena_filters",
    )(feats, feats_rev, feats0, w1t, b1, w2t, b2, freq, w3t, w3t, deltas, deltas)


def _split_hi_lo(x):
    hi = x.astype(BF16)
    lo = (x - hi.astype(F32)).astype(BF16)
    return hi, lo


def _split3_rows(x):
    hi, lo = _split_hi_lo(x)
    return jnp.concatenate([hi, lo, hi], axis=0)


def _split3_cols(x):
    hi, lo = _split_hi_lo(x)
    return jnp.concatenate([hi, lo, hi], axis=1)


def _np_split(m):
    hi = m.astype(ml_dtypes.bfloat16)
    lo = (m - hi.astype(np.float64)).astype(ml_dtypes.bfloat16)
    return hi, lo


def _dft_tables(seq):
    nj = seq // LANES
    n1 = 2 * nj
    n = n1 * LANES
    khp = n1 // 2 + KH_PAD
    k1 = np.arange(khp, dtype=np.float64)[:, None]
    valid = (k1 <= n1 // 2).astype(np.float64)
    ang = 2.0 * np.pi * k1 * np.arange(n1, dtype=np.float64)[None, :] / n1
    fa_full = np.concatenate([valid * np.cos(ang), -valid * np.sin(ang)], axis=0)

    def lhs3(m):
        hi, lo = _np_split(m)
        return jnp.asarray(np.concatenate([hi, hi, lo], axis=1))

    def rhs3(m):
        hi, lo = _np_split(m)
        return jnp.asarray(np.concatenate([hi, hi, lo], axis=0))

    n2 = np.arange(LANES, dtype=np.float64)
    ang2 = 2.0 * np.pi * n2[:, None] * n2[None, :] / LANES
    c2, s2 = np.cos(ang2), np.sin(ang2)
    fb = np.block([[c2, -s2], [s2, c2]])
    fbi = np.block([[c2, s2], [-s2, c2]])
    w = np.where((k1 == 0) | (k1 == n1 // 2), 1.0, 2.0) * valid
    th = 2.0 * np.pi * np.arange(nj, dtype=np.float64)[:, None] * k1.T / n1
    fai = np.concatenate([(w.T / n) * np.cos(th), -(w.T / n) * np.sin(th)], axis=1)
    phi = 2.0 * np.pi * k1 * n2[None, :] / n
    return dict(
        nj=nj, n1=n1, khp=khp,
        fa=lhs3(fa_full[:, :nj]), fa_full=lhs3(fa_full), fb=rhs3(fb), fbi=rhs3(fbi), fai=lhs3(fai),
        twc=jnp.asarray(valid * np.cos(phi), F32), tws=jnp.asarray(valid * np.sin(phi), F32))


def _dft_rows(fa_ref, twc, tws, khp, cols):
    rhs = cols[0] if len(cols) == 1 else jnp.concatenate(cols, axis=1)
    y = _dot(fa_ref[...], _split3_rows(rhs))
    out = []
    for i in range(len(cols)):
        yre = y[0:khp, i * LANES:(i + 1) * LANES]
        yim = y[khp:2 * khp, i * LANES:(i + 1) * LANES]
        out.append((yre * twc + yim * tws, yim * twc - yre * tws))
    return out


def _spec_kernel(khp, g_ref, s_ref, fa_ref, fb_ref, twc_ref, tws_ref, h_ref, ybuf):
    twc = twc_ref[...]
    tws = tws_ref[...]
    for c in range(0, CB_SPEC, 2):
        res = _dft_rows(fa_ref, twc, tws, khp, [g_ref[c], g_ref[c + 1]])
        for i, (yre, yim) in enumerate(res):
            r0 = (c + i) * khp
            ybuf[r0:r0 + khp, 0:LANES] = yre
            ybuf[r0:r0 + khp, LANES:2 * LANES] = yim
    x = _dot(_split3_cols(ybuf[...]), fb_ref[...])
    for c in range(CB_SPEC):
        norm = jnp.sum(s_ref[c:c + 1, :], axis=1, keepdims=True)
        h_ref[c] = x[c * khp:(c + 1) * khp, :] / norm


def _spectrum(g, s, tab):
    nch = g.shape[0]
    n1, khp = tab["n1"], tab["khp"]
    const = lambda a: pl.BlockSpec(a.shape, lambda i: (0, 0))
    return pl.pallas_call(
        functools.partial(_spec_kernel, khp),
        grid=(nch // CB_SPEC,),
        in_specs=[
            pl.BlockSpec((CB_SPEC, n1, LANES), lambda i: (i, 0, 0)),
            pl.BlockSpec((CB_SPEC, s.shape[-1]), lambda i: (i, 0)),
            const(tab["fa_full"]), const(tab["fb"]), const(tab["twc"]), const(tab["tws"]),
        ],
        out_specs=pl.BlockSpec((CB_SPEC, khp, 2 * LANES), lambda i: (i, 0, 0)),
        out_shape=jax.ShapeDtypeStruct((nch, khp, 2 * LANES), F32),
        scratch_shapes=[pltpu.VMEM((CB_SPEC * khp, 2 * LANES), F32)],
        name="hyena_spectrum",
    )(g, s, tab["fa_full"], tab["fb"], tab["twc"], tab["tws"])


def _hyfft_kernel(bsz, nj, khp, prm_ref, v_ref, x1_ref, x2_ref, z_ref, h_ref,
                  fa_ref, fb_ref, fbi_ref, fai_ref, twc_ref, tws_ref, out_ref,
                  xs_ref, ybuf, pbuf):
    ct = pl.program_id(0)
    rows = bsz * nj
    pairs = [(b, b + 1) for b in range(0, bsz, 2)] if bsz > 1 else [(0,)]
    twc = twc_ref[...]
    tws = tws_ref[...]

    lane = lax.broadcasted_iota(jnp.int32, (rows, LANES), 1)
    blk = lax.broadcasted_iota(jnp.int32, (rows, LANES), 0) & (nj - 1)
    first = (lane == 0) & (blk == 0)
    last = (lane == LANES - 1) & (blk == nj - 1)

    def conv3(x, base):
        r = pltpu.roll(x, 1, 1)
        xm1 = jnp.where(lane == 0, pltpu.roll(r, 1, 0), r)
        xm1 = jnp.where(first, 0.0, xm1)
        l = pltpu.roll(x, LANES - 1, 1)
        xp1 = jnp.where(lane == LANES - 1, pltpu.roll(l, rows - 1, 0), l)
        xp1 = jnp.where(last, 0.0, xp1)
        return prm_ref[base] * xm1 + prm_ref[base + 1] * x + prm_ref[base + 2] * xp1 + prm_ref[base + 3]

    def row0(c, b):
        return (c * bsz + b) * khp

    def forward(c, x):
        for pr in pairs:
            res = _dft_rows(fa_ref, twc, tws, khp, [x[b * nj:(b + 1) * nj, :] for b in pr])
            for b, (yre, yim) in zip(pr, res):
                r0 = row0(c, b)
                ybuf[r0:r0 + khp, 0:LANES] = yre
                ybuf[r0:r0 + khp, LANES:2 * LANES] = yim

    def spectral(order):
        x = _dot(_split3_cols(ybuf[...]), fb_ref[...])
        for c in range(CB_HY):
            hre = h_ref[order, c, :, 0:LANES]
            him = h_ref[order, c, :, LANES:2 * LANES]
            for b in range(bsz):
                r0 = row0(c, b)
                xre = x[r0:r0 + khp, 0:LANES]
                xim = x[r0:r0 + khp, LANES:2 * LANES]
                pbuf[r0:r0 + khp, 0:LANES] = xre * hre - xim * him
                pbuf[r0:r0 + khp, LANES:2 * LANES] = xre * him + xim * hre
        r = _dot(_split3_cols(pbuf[...]), fbi_ref[...])
        for c in range(CB_HY):
            for b in range(bsz):
                r0 = row0(c, b)
                rre = r[r0:r0 + khp, 0:LANES]
                rim = r[r0:r0 + khp, LANES:2 * LANES]
                ybuf[r0:r0 + khp, 0:LANES] = rre * twc - rim * tws
                ybuf[r0:r0 + khp, LANES:2 * LANES] = rre * tws + rim * twc

    def inverse(c):
        outs = [None] * bsz
        for pr in pairs:
            rre = [ybuf[row0(c, b):row0(c, b) + khp, 0:LANES] for b in pr]
            rim = [ybuf[row0(c, b):row0(c, b) + khp, LANES:2 * LANES] for b in pr]
            if len(pr) > 1:
                rre, rim = [jnp.concatenate(rre, axis=1)], [jnp.concatenate(rim, axis=1)]
            y = _dot(fai_ref[...], _split3_rows(jnp.concatenate([rre[0], rim[0]], axis=0)))
            for i, b in enumerate(pr):
                outs[b] = y[:, i * LANES:(i + 1) * LANES]
        return outs[0] if bsz == 1 else jnp.concatenate(outs, axis=0)

    for c in range(CB_HY):
        base = (ct * CB_HY + c) * 16
        v = conv3(v_ref[c], base)
        xs_ref[0, c] = v
        xs_ref[1, c] = conv3(x1_ref[c], base + 4)
        xs_ref[2, c] = conv3(x2_ref[c], base + 8)
        forward(c, v)
    spectral(0)
    for c in range(CB_HY):
        base = (ct * CB_HY + c) * 16
        v = xs_ref[0, c]
        z1 = xs_ref[1, c] * (inverse(c) + prm_ref[base + 12] * v)
        xs_ref[0, c] = z1
        forward(c, z1)
    spectral(1)
    for c in range(CB_HY):
        base = (ct * CB_HY + c) * 16
        z1 = xs_ref[0, c]
        y2 = inverse(c) + prm_ref[base + 13] * z1
        out_ref[c] = xs_ref[2, c] * y2 * _silu(z_ref[c])


def _hyena_fft(prm, u_t, h, layer, tab, row0, bsz, seq):
    nj, khp = tab["nj"], tab["khp"]
    nct = WIDTH // CB_HY
    rows = bsz * nj
    blk = (CB_HY, rows, LANES)
    rblk = row0 // (rows * LANES)
    assert rblk * rows * LANES == row0
    const = lambda a: pl.BlockSpec(a.shape, lambda i: (0, 0))
    return pl.pallas_call(
        functools.partial(_hyfft_kernel, bsz, nj, khp),
        grid=(nct,),
        in_specs=[
            pl.BlockSpec(memory_space=pltpu.SMEM),
            pl.BlockSpec(blk, lambda i: (i, rblk, 0)),
            pl.BlockSpec(blk, lambda i: (nct + i, rblk, 0)),
            pl.BlockSpec(blk, lambda i: (2 * nct + i, rblk, 0)),
            pl.BlockSpec(blk, lambda i: (3 * nct + i, rblk, 0)),
            pl.BlockSpec((None, HY_ORDER, CB_HY, khp, 2 * LANES), lambda i: (layer, 0, i, 0, 0)),
            const(tab["fa"]), const(tab["fb"]), const(tab["fbi"]), const(tab["fai"]),
            const(tab["twc"]), const(tab["tws"]),
        ],
        out_specs=pl.BlockSpec(blk, lambda i: (i, 0, 0)),
        out_shape=jax.ShapeDtypeStruct((WIDTH, rows, LANES), F32),
        scratch_shapes=[
            pltpu.VMEM((3, CB_HY, bsz * nj, LANES), F32),
            pltpu.VMEM((CB_HY * bsz * khp, 2 * LANES), F32),
            pltpu.VMEM((CB_HY * bsz * khp, 2 * LANES), F32),
        ],
        name="hyena_fft",
    )(prm, u_t, u_t, u_t, u_t, h, tab["fa"], tab["fb"], tab["fbi"], tab["fai"],
      tab["twc"], tab["tws"])


def _out_kernel(final, starts, *refs):
    ng = len(starts)
    x_ref, yml_ref, yrt_ref = refs[0:3]
    yhy_refs = refs[3:3 + ng]
    w_ref, g_ref = refs[3 + ng:5 + ng]
    out_refs = refs[5 + ng:]
    i = pl.program_id(0)
    acc = _dot(yml_ref[...].astype(BF16), w_ref[0:WIDTH, :])
    acc = acc + _dot(yrt_ref[...].astype(BF16), w_ref[WIDTH:2 * WIDTH, :])
    w_hy = w_ref[2 * WIDTH:, :]
    parts = []
    for j in range(TM_OUT // LANES):
        y = yhy_refs[0][:, j, :]
        for gi in range(1, ng):
            y = jnp.where(i >= starts[gi], yhy_refs[gi][:, j, :], y)
        parts.append(_dot_tn(y.astype(BF16), w_hy))
    x = x_ref[...] + acc + jnp.concatenate(parts, axis=0)
    if not final:
        out_refs[0][...] = x
        return
    ms = jnp.mean(x * x, axis=-1, keepdims=True)
    x = x * lax.rsqrt(ms + RMS_EPS) * g_ref[...]
    for gi in range(ng):
        inside = i >= starts[gi]
        if gi + 1 < ng:
            inside = inside & (i < starts[gi + 1])

        @pl.when(inside)
        def _(gi=gi):
            out_refs[gi][...] = x


def _out_proj(x, y_ml, y_rt, y_hy, w, gain, final, groups):
    t = x.shape[0]
    jb = TM_OUT // LANES
    starts = [r0 // TM_OUT for (r0, _, _) in groups]
    counts = [bsz * seq // TM_OUT for (_, bsz, seq) in groups]

    def tile(gi):
        return lambda i: jnp.clip(i - starts[gi], 0, counts[gi] - 1)

    hy_specs = [pl.BlockSpec((WIDTH, jb, LANES), lambda i, f=tile(gi): (0, f(i), 0))
                for gi in range(len(groups))]
    if final:
        out_specs = [pl.BlockSpec((TM_OUT, D_MODEL), lambda i, f=tile(gi): (f(i), 0))
                     for gi in range(len(groups))]
        out_shape = [jax.ShapeDtypeStruct((bsz * seq, D_MODEL), F32) for (_, bsz, seq) in groups]
    else:
        out_specs = [pl.BlockSpec((TM_OUT, D_MODEL), lambda i: (i, 0))]
        out_shape = [jax.ShapeDtypeStruct((t, D_MODEL), F32)]
    return pl.pallas_call(
        functools.partial(_out_kernel, final, starts),
        grid=(t // TM_OUT,),
        in_specs=[
            pl.BlockSpec((TM_OUT, D_MODEL), lambda i: (i, 0)),
            pl.BlockSpec((TM_OUT, WIDTH), lambda i: (i, 0)),
            pl.BlockSpec((TM_OUT, WIDTH), lambda i: (i, 0)),
            *hy_specs,
            pl.BlockSpec((3 * WIDTH, D_MODEL), lambda i: (0, 0)),
            pl.BlockSpec((1, D_MODEL), lambda i: (0, 0)),
        ],
        out_specs=out_specs,
        out_shape=out_shape,
        name="out_proj",
    )(x, y_ml, y_rt, *y_hy, w, gain)


def kernel(x_prompt, x_sample, norm_g, w_in, ml_conv_w, ml_conv_b, ml_gate_b, ml_norm_g, rt_norm_g, hy_conv_w, hy_conv_b, hy_w1, hy_b1, hy_w2, hy_b2, hy_w3, hy_freq, hy_deltas, hy_skip, w_out, final_g):
    groups = []
    row0 = 0
    for xg in (x_prompt, x_sample):
        bsz, seq, _ = xg.shape
        groups.append((row0, bsz, seq))
        row0 += bsz * seq
    x = jnp.concatenate([x_prompt.reshape(-1, D_MODEL), x_sample.reshape(-1, D_MODEL)], axis=0)

    c_gate = ML_COLS
    c_rt = c_gate + 4 * HEADS
    c_hy = c_rt + RT_COLS
    gate_pad = jnp.zeros((DEPTH, D_MODEL, LANES - 4 * HEADS), F32)
    w_tok = jnp.concatenate([w_in[:, :, :c_gate], w_in[:, :, c_gate:c_rt], gate_pad,
                             w_in[:, :, c_rt:c_hy]], axis=2).astype(BF16)
    w_gt = jnp.swapaxes(w_in[:, :, c_gate:c_rt], 1, 2).astype(BF16)
    w_hyt = jnp.swapaxes(w_in[:, :, c_hy:], 1, 2).astype(BF16)
    w_out_b = w_out.astype(BF16)
    gate_b = ml_gate_b.reshape(DEPTH, 1, 4 * HEADS)
    gate_b_row = jnp.concatenate([gate_b, jnp.zeros((DEPTH, 1, LANES - 4 * HEADS), F32)], axis=2)
    gate_b_col = ml_gate_b.reshape(DEPTH, 4 * HEADS, 1)

    cw = hy_conv_w.reshape(DEPTH, 3, 3, WIDTH)
    cbias = hy_conv_b.reshape(DEPTH, 1, 3, WIDTH)
    taps = jnp.concatenate([cw, cbias], axis=1)
    taps = jnp.transpose(taps, (0, 3, 2, 1)).reshape(DEPTH, WIDTH, 12)
    hy_prm = jnp.concatenate([taps, jnp.transpose(hy_skip, (0, 2, 1)),
                              jnp.zeros((DEPTH, WIDTH, 2), F32)], axis=2).reshape(DEPTH, WIDTH * 16)

    w1t = jnp.concatenate([jnp.swapaxes(hy_w1, 1, 2),
                           jnp.zeros((DEPTH, HY_HIDDEN, FEAT_ROWS - hy_w1.shape[1]), F32)], axis=2)
    w2t = jnp.swapaxes(hy_w2, 1, 2)
    w3t = jnp.swapaxes(hy_w3, 1, 2)
    b1 = hy_b1[:, :, None]
    b2 = hy_b2[:, :, None]
    freq = hy_freq[:, :, None]
    deltas = hy_deltas.reshape(DEPTH, HY_ORDER * 2 * WIDTH, 1)

    tables, spectra = [], []
    for (_, _, seq) in groups:
        tab = _dft_tables(seq)
        gb, gf, s = _filters(seq, w1t, b1, w2t, b2, freq, w3t, deltas)
        g = jnp.concatenate([gf, gb], axis=-1).reshape(DEPTH * HY_ORDER * WIDTH, tab["n1"], LANES)
        h = _spectrum(g, s.reshape(DEPTH * HY_ORDER * WIDTH, -1), tab)
        tables.append(tab)
        spectra.append(h.reshape(DEPTH, HY_ORDER, WIDTH, tab["khp"], 2 * LANES))
    rot_c, rot_s = _rotary_tables(max(seq for (_, _, seq) in groups))
    rt_tabs = _retention_tables()

    for layer in range(DEPTH):
        gain = norm_g[layer][None, :]
        u_ml, u_g, u_rt, u_gt = _in_proj(x, gain, w_tok[layer], w_gt[layer])
        u_hyt = _in_proj_hy(x, gain, w_hyt[layer])
        pb, cb, gr, gs = _gates(u_g, u_gt, gate_b_row[layer], gate_b_col[layer])
        y_ml = _mlstm(u_ml, pb, cb, gr, gs, ml_conv_w[layer], ml_conv_b[layer][None, :],
                      ml_norm_g[layer][None, :], groups)
        y_rt = _retention(u_rt, rot_c, rot_s, rt_tabs, rt_norm_g[layer][None, :], groups)
        y_hy = [_hyena_fft(hy_prm[layer], u_hyt, spectra[gi], layer, tables[gi], r0, bsz, seq)
                for gi, (r0, bsz, seq) in enumerate(groups)]
        outs = _out_proj(x, y_ml, y_rt, y_hy, w_out_b[layer], final_g[None, :],
                         layer == DEPTH - 1, groups)
        x = outs[0]

    return outs[0].reshape(x_prompt.shape), outs[1].reshape(x_sample.shape)
```

```python
import functools
import math

import ml_dtypes
import numpy as np
import jax
import jax.numpy as jnp
from jax import lax
from jax.experimental import pallas as pl
from jax.experimental.pallas import tpu as pltpu

F32 = jnp.float32
BF16 = jnp.bfloat16

D_MODEL = 1024
DEPTH = 4
CHUNK = 128
HEADS = 4
HEAD_DIM = 128
WIDTH = 512
HY_ORDER = 2
HY_BANDS = 16
HY_HIDDEN = 64
FEAT_ROWS = 40
ROPE_BASE = 10000.0
RMS_EPS = 1e-6
HEAD_NORM_EPS = 1e-5
M_INIT = -1e30
RT_LOG_GAMMA_FWD = tuple(math.log(1.0 - 2.0 ** (-5.0 - h)) for h in range(HEADS))
RT_LOG_GAMMA_BWD = tuple(math.log(1.0 - 2.0 ** (-5.5 - h)) for h in range(HEADS))

LANES = 128
SUBLANES = 8
TM_IN = 256
TM_HY = 1024
TR_HY = 1024
TM_OUT = 1024
TB_SEQ = 512
TG = 1024
CB_HY = 8
CB_SPEC = 8
KH_PAD = 8
LT_FILT = 2048
ML_COLS = 5 * WIDTH
RT_COLS = 4 * WIDTH
HY_ROWS = 4 * WIDTH
HIGHEST = lax.Precision.HIGHEST


def _dot(a, b):
    return jnp.dot(a, b, preferred_element_type=F32)


def _dot_nt(a, b):
    return lax.dot_general(a, b, (((1,), (1,)), ((), ())), preferred_element_type=F32)


def _dot_tn(a, b):
    return lax.dot_general(a, b, (((0,), (0,)), ((), ())), preferred_element_type=F32)


def _dot_f32(a, b):
    return jnp.dot(a, b, precision=HIGHEST, preferred_element_type=F32)


def _silu(y):
    return y * jax.nn.sigmoid(y)


def _log_sigmoid(x):
    return -(jnp.maximum(-x, 0.0) + jnp.log1p(jnp.exp(-jnp.abs(x))))


def _head_norm(h, gain):
    mu = jnp.mean(h, axis=-1, keepdims=True)
    hc = h - mu
    var = jnp.mean(hc * hc, axis=-1, keepdims=True)
    return hc * lax.rsqrt(var + HEAD_NORM_EPS) * gain


def _tile_starts(xs, tm):
    starts, s = [], 0
    for a in xs:
        starts.append(s)
        s += a.shape[0] // tm
    return starts, s


def _x_specs(xs, tm, ngrid):
    starts, _ = _tile_starts(xs, tm)
    specs = []
    for a, s in zip(xs, starts):
        cnt = a.shape[0] // tm
        if ngrid == 1:
            specs.append(pl.BlockSpec((tm, D_MODEL), lambda i, s=s, cnt=cnt: (jnp.clip(i - s, 0, cnt - 1), 0)))
        else:
            specs.append(pl.BlockSpec((tm, D_MODEL), lambda i, r, s=s, cnt=cnt: (jnp.clip(i - s, 0, cnt - 1), 0)))
    return specs


def _x_tile(x_refs, starts):
    x = x_refs[0][...]
    for ref, s in zip(x_refs[1:], starts[1:]):
        x = jnp.where(pl.program_id(0) >= s, ref[...], x)
    return x


def _rms_bf16(x, g_ref):
    ms = jnp.mean(x * x, axis=-1, keepdims=True)
    return (x * lax.rsqrt(ms + RMS_EPS) * g_ref[...]).astype(BF16)


def _in_kernel(starts, *refs):
    nx = len(starts)
    g_ref, wtok_ref, wgt_ref, oml_ref, og_ref, ort_ref, ogt_ref = refs[nx:]
    h = _rms_bf16(_x_tile(refs[:nx], starts), g_ref)
    y = _dot(h, wtok_ref[...])
    oml_ref[...] = y[:, 0:ML_COLS]
    og_ref[...] = y[:, ML_COLS:ML_COLS + LANES]
    ort_ref[...] = y[:, ML_COLS + LANES:]
    ogt_ref[...] = _dot_nt(wgt_ref[...], h)


def _in_proj(xs, gain, w_tok, w_gt):
    starts, nt = _tile_starts(xs, TM_IN)
    t = nt * TM_IN
    ncol = w_tok.shape[1]
    return pl.pallas_call(
        functools.partial(_in_kernel, starts),
        grid=(nt,),
        in_specs=[
            *_x_specs(xs, TM_IN, 1),
            pl.BlockSpec((1, D_MODEL), lambda i: (0, 0)),
            pl.BlockSpec((D_MODEL, ncol), lambda i: (0, 0)),
            pl.BlockSpec((4 * HEADS, D_MODEL), lambda i: (0, 0)),
        ],
        out_specs=[
            pl.BlockSpec((TM_IN, ML_COLS), lambda i: (i, 0)),
            pl.BlockSpec((TM_IN, LANES), lambda i: (i, 0)),
            pl.BlockSpec((TM_IN, RT_COLS), lambda i: (i, 0)),
            pl.BlockSpec((4 * HEADS, TM_IN), lambda i: (0, i)),
        ],
        out_shape=[
            jax.ShapeDtypeStruct((t, ML_COLS), F32),
            jax.ShapeDtypeStruct((t, LANES), F32),
            jax.ShapeDtypeStruct((t, RT_COLS), F32),
            jax.ShapeDtypeStruct((4 * HEADS, t), F32),
        ],
        name="in_proj",
    )(*xs, gain, w_tok, w_gt)


def _in_hy_kernel(starts, *refs):
    nx = len(starts)
    g_ref, w_ref, o_ref = refs[nx:]
    y = _dot_nt(w_ref[...], _rms_bf16(_x_tile(refs[:nx], starts), g_ref))
    o_ref[...] = y.reshape(o_ref.shape)


def _in_proj_hy(xs, gain, w_hyt):
    starts, nt = _tile_starts(xs, TM_HY)
    t = nt * TM_HY
    nr = HY_ROWS // TR_HY
    return pl.pallas_call(
        functools.partial(_in_hy_kernel, starts),
        grid=(nt, nr),
        in_specs=[
            *_x_specs(xs, TM_HY, 2),
            pl.BlockSpec((1, D_MODEL), lambda i, r: (0, 0)),
            pl.BlockSpec((TR_HY, D_MODEL), lambda i, r: (r, 0)),
        ],
        out_specs=pl.BlockSpec((TR_HY, TM_HY // LANES, LANES), lambda i, r: (r, i, 0)),
        out_shape=jax.ShapeDtypeStruct((HY_ROWS, t // LANES, LANES), F32),
        name="in_proj_hyena",
    )(*xs, gain, w_hyt)


def _gates_kernel(g_ref, gt_ref, b_ref, bt_ref, pb_ref, cb_ref, gr_ref, gs_ref):
    row = lax.broadcasted_iota(jnp.int32, (CHUNK, CHUNK), 0)
    col = lax.broadcasted_iota(jnp.int32, (CHUNK, CHUNK), 1)
    lower = (row >= col).astype(F32)
    upper = (row <= col).astype(F32)
    gr_ref[...] = jnp.zeros_like(gr_ref)
    gs_ref[...] = jnp.zeros_like(gs_ref)
    for c in range(TG // CHUNK):
        sl = slice(c * CHUNK, (c + 1) * CHUNK)
        lf = _log_sigmoid(g_ref[sl, :] + b_ref[...])
        cum_c = (_dot_f32(lower, lf), _dot_f32(upper, lf))
        pre_t = gt_ref[:, sl] + bt_ref[...]
        lf_t = _log_sigmoid(pre_t)
        cum_r = (_dot_f32(lf_t, upper), _dot_f32(lf_t, lower))
        for d in range(2):
            for h in range(HEADS):
                kf = (2 * d + 1) * HEADS + h
                ki = 2 * d * HEADS + h
                cum = cum_r[d][kf:kf + 1, :]
                li = pre_t[ki:ki + 1, :]
                r = li - cum
                f_tot = cum[:, 0:1] if d else cum[:, CHUNK - 1:CHUNK]
                g = f_tot - cum + li
                allowed = (row <= col) if d else (row >= col)
                pmax = jnp.max(jnp.where(allowed, r, -jnp.inf), axis=1, keepdims=True)
                pb_ref[d, h, sl, :] = jnp.broadcast_to(pmax, (CHUNK, LANES))
                cb_ref[d, h, sl, :] = jnp.broadcast_to(cum_c[d][:, kf:kf + 1], (CHUNK, LANES))
                gr_ref[d, h, 0:1, sl] = r
                gr_ref[d, h, 1:2, sl] = g
                gs_ref[d, h, c, 0:1, :] = jnp.broadcast_to(f_tot, (1, LANES))
                gs_ref[d, h, c, 1:2, :] = jnp.broadcast_to(jnp.max(g, axis=1, keepdims=True), (1, LANES))


def _gates(g, gt, bias_row, bias_col):
    t = g.shape[0]
    nc = TG // CHUNK
    return pl.pallas_call(
        _gates_kernel,
        grid=(t // TG,),
        in_specs=[
            pl.BlockSpec((TG, LANES), lambda i: (i, 0)),
            pl.BlockSpec((4 * HEADS, TG), lambda i: (0, i)),
            pl.BlockSpec((1, LANES), lambda i: (0, 0)),
            pl.BlockSpec((4 * HEADS, 1), lambda i: (0, 0)),
        ],
        out_specs=[
            pl.BlockSpec((2, HEADS, TG, LANES), lambda i: (0, 0, i, 0)),
            pl.BlockSpec((2, HEADS, TG, LANES), lambda i: (0, 0, i, 0)),
            pl.BlockSpec((2, HEADS, SUBLANES, TG), lambda i: (0, 0, 0, i)),
            pl.BlockSpec((2, HEADS, nc, SUBLANES, LANES), lambda i: (0, 0, i, 0, 0)),
        ],
        out_shape=[
            jax.ShapeDtypeStruct((2, HEADS, t, LANES), F32),
            jax.ShapeDtypeStruct((2, HEADS, t, LANES), F32),
            jax.ShapeDtypeStruct((2, HEADS, SUBLANES, t), F32),
            jax.ShapeDtypeStruct((2, HEADS, t // CHUNK, SUBLANES, LANES), F32),
        ],
        name="gates",
    )(g, gt, bias_row, bias_col)


def _seq_pos(g, groups):
    pos, nbs = None, None
    start = 0
    for (_, bsz, seq) in groups:
        nb = seq // TB_SEQ
        p = (g - start) % nb
        pos = p if pos is None else jnp.where(g >= start, p, pos)
        nbs = nb if nbs is None else jnp.where(g >= start, nb, nbs)
        start += bsz * nb
    return pos, nbs


def _num_blocks(groups):
    return sum(bsz * seq // TB_SEQ for (_, bsz, seq) in groups)


def _ml_kernel(rev, groups, *refs):
    if rev:
        (qs_ref, ks_ref, v_ref, pb_ref, cb_ref, gr_ref, gs_ref, o_ref, z_ref, hf_ref, ng_ref,
         out_ref, st_ref, m_ref) = refs
    else:
        (q_ref, k_ref, v_ref, qp_ref, qn_ref, kp_ref, kn_ref, pb_ref, cb_ref, gr_ref, gs_ref,
         cw_ref, cbias_ref, out_ref, qs_ref, ks_ref, st_ref, m_ref, stg_ref) = refs
    tb = TB_SEQ
    nch = tb // CHUNK
    n = pl.program_id(0)
    sb, nb = _seq_pos((_num_blocks(groups) - 1 - n) if rev else n, groups)

    @pl.when(sb == (nb - 1 if rev else 0))
    def _():
        st_ref[...] = jnp.zeros_like(st_ref)
        m_ref[...] = jnp.full_like(m_ref, M_INIT)

    def conv_silu(raw_ref, prev_ref, next_ref, c0):
        stg_ref[SUBLANES:SUBLANES + tb, :] = raw_ref[...]
        stg_ref[SUBLANES - 1:SUBLANES, :] = jnp.where(sb > 0, prev_ref[SUBLANES - 1:SUBLANES, :], 0.0)
        stg_ref[SUBLANES + tb:SUBLANES + tb + 1, :] = jnp.where(sb < nb - 1, next_ref[0:1, :], 0.0)
        w = cw_ref[:, c0:c0 + WIDTH]
        y = (stg_ref[SUBLANES - 1:SUBLANES - 1 + tb, :] * w[0:1, :]
             + stg_ref[SUBLANES:SUBLANES + tb, :] * w[1:2, :]
             + stg_ref[SUBLANES + 1:SUBLANES + 1 + tb, :] * w[2:3, :] + cbias_ref[:, c0:c0 + WIDTH])
        return _silu(y)

    if not rev:
        qs_ref[...] = conv_silu(q_ref, qp_ref, qn_ref, 0).astype(BF16)
        ks_ref[...] = conv_silu(k_ref, kp_ref, kn_ref, WIDTH) * (HEAD_DIM ** -0.5)

    row = lax.broadcasted_iota(jnp.int32, (CHUNK, CHUNK), 0)
    col = lax.broadcasted_iota(jnp.int32, (CHUNK, CHUNK), 1)
    allowed = (row <= col) if rev else (row >= col)
    ones = jnp.ones((CHUNK, HEAD_DIM), BF16)

    def chunk(c, h):
        sl = pl.ds(pl.multiple_of(c * CHUNK, CHUNK), CHUNK)
        hs = slice(h * HEAD_DIM, (h + 1) * HEAD_DIM)
        q = qs_ref[sl, hs]
        kt = ks_ref[sl, hs].T
        v2 = jnp.concatenate([v_ref[sl, hs].astype(BF16), ones], axis=1)
        m_prev = m_ref[h, 0:1, :]
        mb = jnp.maximum(pb_ref[h, sl, :], m_prev)
        p = jnp.exp(jnp.where(allowed, gr_ref[h, 0:1, sl] - mb, -jnp.inf))
        w_inter = jnp.exp(m_prev - mb)
        floor = jnp.exp(-(cb_ref[h, sl, :] + mb))
        s = _dot(q, kt.astype(BF16)) * p
        inter = _dot(q, st_ref[h].astype(BF16))
        num2 = _dot(s.astype(BF16), v2) + jnp.concatenate([w_inter, w_inter], axis=1) * inter
        out = num2[:, 0:HEAD_DIM] / jnp.maximum(jnp.abs(num2[:, HEAD_DIM:]), floor)
        f_tot = gs_ref[h, c, 0:1, :]
        m_chunk = gs_ref[h, c, 1:2, :]
        m_new = jnp.maximum(f_tot + m_prev, m_chunk)
        kw = (kt * jnp.exp(gr_ref[h, 1:2, sl] - m_new)).astype(BF16)
        kv2 = _dot(kw, v2)
        a = jnp.exp(f_tot + m_prev - m_new)
        st_ref[h] = jnp.concatenate([a, a], axis=1) * st_ref[h] + kv2
        m_ref[h] = jnp.broadcast_to(m_new, (SUBLANES, LANES))
        return out, sl, hs

    def body(i, carry):
        c = (nch - 1 - i) if rev else i
        for h in range(HEADS):
            out, sl, hs = chunk(c, h)
            if rev:
                hh = jax.nn.sigmoid(o_ref[sl, hs]) * (out + hf_ref[sl, hs])
                out_ref[sl, hs] = _head_norm(hh, ng_ref[:, hs]) * _silu(z_ref[sl, hs])
            else:
                out_ref[sl, hs] = out
        return carry

    lax.fori_loop(0, nch, body, 0, unroll=True)


def _mlstm_dir(rev, u_ml, pb, cb, gr, gs, conv_w, conv_b, norm_g, fwd, groups):
    t = u_ml.shape[0]
    tb = TB_SEQ
    t8 = tb // SUBLANES
    nch = tb // CHUNK
    nblk = _num_blocks(groups)
    d = 1 if rev else 0

    def rb(n):
        return (nblk - 1 - n) if rev else n

    def col(g):
        return lambda n: (rb(n), g)

    def halo_prev(g):
        return lambda n: (jnp.maximum(rb(n) * t8 - 1, 0), g)

    def halo_next(g):
        return lambda n: (jnp.minimum((rb(n) + 1) * t8, t // SUBLANES - 1), g)

    blk = (tb, WIDTH)
    halo = (SUBLANES, WIDTH)
    gate_specs = [
        pl.BlockSpec((None, HEADS, tb, LANES), lambda n: (d, 0, rb(n), 0)),
        pl.BlockSpec((None, HEADS, tb, LANES), lambda n: (d, 0, rb(n), 0)),
        pl.BlockSpec((None, HEADS, SUBLANES, tb), lambda n: (d, 0, 0, rb(n))),
        pl.BlockSpec((None, HEADS, nch, SUBLANES, LANES), lambda n: (d, 0, rb(n), 0, 0)),
    ]
    scratch = [
        pltpu.VMEM((HEADS, HEAD_DIM, 2 * HEAD_DIM), F32),
        pltpu.VMEM((HEADS, SUBLANES, LANES), F32),
    ]
    full = jax.ShapeDtypeStruct((t, WIDTH), F32)
    if rev:
        qs, ks, h_fwd = fwd
        in_specs = [pl.BlockSpec(blk, col(0)), pl.BlockSpec(blk, col(0)), pl.BlockSpec(blk, col(2)),
                    *gate_specs,
                    pl.BlockSpec(blk, col(3)), pl.BlockSpec(blk, col(4)), pl.BlockSpec(blk, col(0)),
                    pl.BlockSpec((1, WIDTH), lambda n: (0, 0))]
        args = [qs, ks, u_ml, pb, cb, gr, gs, u_ml, u_ml, h_fwd, norm_g]
        out_specs, out_shape = pl.BlockSpec(blk, col(0)), full
    else:
        in_specs = [pl.BlockSpec(blk, col(0)), pl.BlockSpec(blk, col(1)), pl.BlockSpec(blk, col(2)),
                    pl.BlockSpec(halo, halo_prev(0)), pl.BlockSpec(halo, halo_next(0)),
                    pl.BlockSpec(halo, halo_prev(1)), pl.BlockSpec(halo, halo_next(1)),
                    *gate_specs,
                    pl.BlockSpec((3, 2 * WIDTH), lambda n: (0, 0)),
                    pl.BlockSpec((1, 2 * WIDTH), lambda n: (0, 0))]
        args = [u_ml, u_ml, u_ml, u_ml, u_ml, u_ml, u_ml, pb, cb, gr, gs, conv_w, conv_b]
        out_specs = [pl.BlockSpec(blk, col(0))] * 3
        out_shape = [full, jax.ShapeDtypeStruct((t, WIDTH), BF16), full]
        scratch.append(pltpu.VMEM((tb + 2 * SUBLANES, WIDTH), F32))
    return pl.pallas_call(
        functools.partial(_ml_kernel, rev, groups),
        grid=(nblk,),
        in_specs=in_specs,
        out_specs=out_specs,
        out_shape=out_shape,
        scratch_shapes=scratch,
        name="mlstm_bwd" if rev else "mlstm_fwd",
    )(*args)


def _mlstm(u_ml, pb, cb, gr, gs, conv_w, conv_b, norm_g, groups):
    h_fwd, qs, ks = _mlstm_dir(False, u_ml, pb, cb, gr, gs, conv_w, conv_b, norm_g, None, groups)
    return _mlstm_dir(True, u_ml, pb, cb, gr, gs, conv_w, conv_b, norm_g, (qs, ks, h_fwd), groups)


def _rt_kernel(rev, groups, *refs):
    if rev:
        (qs_ref, ks_ref, v_ref, dec_ref, qd_ref, kd_ref, z_ref, hf_ref, ng_ref,
         out_ref, st_ref) = refs
    else:
        (q_ref, k_ref, v_ref, cc_ref, ss_ref, dec_ref, qd_ref, kd_ref,
         out_ref, qs_ref, ks_ref, st_ref) = refs
    nch = TB_SEQ // CHUNK
    n = pl.program_id(0)
    sb, nb = _seq_pos((_num_blocks(groups) - 1 - n) if rev else n, groups)

    @pl.when(sb == (nb - 1 if rev else 0))
    def _():
        st_ref[...] = jnp.zeros_like(st_ref)

    if not rev:
        for h in range(HEADS):
            hs = slice(h * HEAD_DIM, (h + 1) * HEAD_DIM)
            cc = cc_ref[...]
            ss = ss_ref[...]
            q = q_ref[:, hs]
            k = k_ref[:, hs]
            qs_ref[:, hs] = (q * cc + pltpu.roll(q, HEAD_DIM // 2, 1) * ss) * (HEAD_DIM ** -0.5)
            ks_ref[:, hs] = k * cc + pltpu.roll(k, HEAD_DIM // 2, 1) * ss

    def body(i, carry):
        c = (nch - 1 - i) if rev else i
        sl = pl.ds(pl.multiple_of(c * CHUNK, CHUNK), CHUNK)
        for h in range(HEADS):
            hs = slice(h * HEAD_DIM, (h + 1) * HEAD_DIM)
            q = qs_ref[sl, hs]
            kt = ks_ref[sl, hs].T
            vb = v_ref[sl, hs].astype(BF16)
            s = _dot(q.astype(BF16), kt.astype(BF16)) * dec_ref[h]
            out = _dot(s.astype(BF16), vb) + _dot((q * qd_ref[h]).astype(BF16), st_ref[h].astype(BF16))
            st_ref[h] = kd_ref[h, 1:2, :] * st_ref[h] + _dot((kt * kd_ref[h, 0:1, :]).astype(BF16), vb)
            if rev:
                out_ref[sl, hs] = _head_norm(out + hf_ref[sl, hs], ng_ref[:, hs]) * _silu(z_ref[sl, hs])
            else:
                out_ref[sl, hs] = out
        return carry

    lax.fori_loop(0, nch, body, 0, unroll=not rev)


def _retention_dir(rev, u_rt, rot_c, rot_s, dec, qdec, kdec, norm_g, fwd, groups):
    tb = TB_SEQ
    nblk = _num_blocks(groups)
    d = 1 if rev else 0

    def rb(n):
        return (nblk - 1 - n) if rev else n

    def pos(n):
        return _seq_pos(rb(n), groups)[0]

    def col(g):
        return lambda n: (rb(n), g)

    blk = (tb, WIDTH)
    tbl = lambda n: (d, 0, 0, 0)
    tab_specs = [
        pl.BlockSpec((None, HEADS, CHUNK, CHUNK), tbl),
        pl.BlockSpec((None, HEADS, CHUNK, HEAD_DIM), tbl),
        pl.BlockSpec((None, HEADS, SUBLANES, CHUNK), tbl),
    ]
    full = jax.ShapeDtypeStruct((u_rt.shape[0], WIDTH), F32)
    if rev:
        qs, ks, h_fwd = fwd
        in_specs = [pl.BlockSpec(blk, col(0)), pl.BlockSpec(blk, col(0)), pl.BlockSpec(blk, col(2)),
                    *tab_specs,
                    pl.BlockSpec(blk, col(3)), pl.BlockSpec(blk, col(0)),
                    pl.BlockSpec((1, WIDTH), lambda n: (0, 0))]
        args = [qs, ks, u_rt, dec, qdec, kdec, u_rt, h_fwd, norm_g]
        out_specs, out_shape = pl.BlockSpec(blk, col(0)), full
    else:
        in_specs = [pl.BlockSpec(blk, col(0)), pl.BlockSpec(blk, col(1)), pl.BlockSpec(blk, col(2)),
                    pl.BlockSpec((tb, HEAD_DIM), lambda n: (pos(n), 0)),
                    pl.BlockSpec((tb, HEAD_DIM), lambda n: (pos(n), 0)),
                    *tab_specs]
        args = [u_rt, u_rt, u_rt, rot_c, rot_s, dec, qdec, kdec]
        out_specs, out_shape = [pl.BlockSpec(blk, col(0))] * 3, [full] * 3
    return pl.pallas_call(
        functools.partial(_rt_kernel, rev, groups),
        grid=(nblk,),
        in_specs=in_specs,
        out_specs=out_specs,
        out_shape=out_shape,
        scratch_shapes=[pltpu.VMEM((HEADS, HEAD_DIM, HEAD_DIM), F32)],
        name="retention_bwd" if rev else "retention_fwd",
    )(*args)


def _retention(u_rt, rot_c, rot_s, tabs, norm_g, groups):
    dec, qdec, kdec = tabs
    h_fwd, qs, ks = _retention_dir(False, u_rt, rot_c, rot_s, dec, qdec, kdec, norm_g, None, groups)
    return _retention_dir(True, u_rt, rot_c, rot_s, dec, qdec, kdec, norm_g, (qs, ks, h_fwd), groups)


def _rotary_tables(seq):
    inv = ROPE_BASE ** (-jnp.arange(0, HEAD_DIM, 2, dtype=F32) / HEAD_DIM)
    ang = jnp.arange(seq, dtype=F32)[:, None] * inv[None, :]
    cos, sin = jnp.cos(ang), jnp.sin(ang)
    return jnp.concatenate([cos, cos], axis=1), jnp.concatenate([-sin, sin], axis=1)


def _retention_tables():
    pos = np.arange(CHUNK, dtype=np.float64)
    rel = pos[:, None] - pos[None, :]
    dec = np.zeros((2, HEADS, CHUNK, CHUNK))
    qdec = np.zeros((2, HEADS, CHUNK, HEAD_DIM))
    kdec = np.zeros((2, HEADS, SUBLANES, CHUNK))
    for h in range(HEADS):
        lf, lb = RT_LOG_GAMMA_FWD[h], RT_LOG_GAMMA_BWD[h]
        dec[0, h] = np.where(rel >= 0, np.exp(np.maximum(rel, 0.0) * lf), 0.0)
        dec[1, h] = np.where(rel <= 0, np.exp(np.maximum(-rel, 0.0) * lb), 0.0)
        qdec[0, h] = np.exp((pos + 1.0) * lf)[:, None]
        qdec[1, h] = np.exp((CHUNK - pos) * lb)[:, None]
        kdec[0, h, 0] = np.exp((CHUNK - 1.0 - pos) * lf)
        kdec[1, h, 0] = np.exp(pos * lb)
        kdec[0, h, 1] = np.exp(CHUNK * lf)
        kdec[1, h, 1] = np.exp(CHUNK * lb)
    return jnp.asarray(dec, F32), jnp.asarray(qdec, F32), jnp.asarray(kdec, F32)


def _filt_kernel(ft_ref, fr_ref, f0_ref, w1_ref, b1_ref, w2_ref, b2_ref, fq_ref,
                 w3f_ref, w3b_ref, dlf_ref, dlb_ref, gb_ref, gf_ref, s_ref, hf3_ref, hr3_ref):
    lt = pl.program_id(1)
    fq = fq_ref[...]

    def hidden(ft):
        h1 = jnp.sin(fq * (_dot_f32(w1_ref[...], ft) + b1_ref[...]))
        return jnp.sin(fq * (_dot_f32(w2_ref[...], h1) + b2_ref[...]))

    @pl.when((pl.program_id(2) == 0) & (pl.program_id(3) == 0))
    def _():
        hf3_ref[...] = _split3_rows(hidden(ft_ref[...]))
        hr3_ref[...] = _split3_rows(hidden(fr_ref[...]))

    def lhs3(w):
        hi, lo = _split_hi_lo(w)
        return jnp.concatenate([hi, hi, lo], axis=1)

    ft = ft_ref[...]
    fr = fr_ref[...]
    dl_f = jnp.abs(dlf_ref[...])
    dl_b = jnp.abs(dlb_ref[...])
    fwd = _dot(lhs3(w3f_ref[...]), hf3_ref[...]) * jnp.exp(-ft[0:1, :] * dl_f)
    bwd = _dot(lhs3(w3b_ref[...]), hr3_ref[...]) * jnp.exp(-fr[0:1, :] * dl_b)
    f0 = f0_ref[...]
    bwd0 = (_dot_f32(w3b_ref[...], hidden(f0)) * jnp.exp(-f0[0:1, :] * dl_b))[:, 0:1]
    lag = lax.broadcasted_iota(jnp.int32, fwd.shape, 1) + lt * LT_FILT
    fwd = jnp.where(lag == 0, fwd + bwd0, fwd)
    bwd = jnp.where(lag == 0, 0.0, bwd)
    gb_ref[...] = bwd
    gf_ref[...] = fwd
    part = jnp.sum(jnp.abs(fwd) + jnp.abs(bwd), axis=1, keepdims=True)
    lane = lax.broadcasted_iota(jnp.int32, (LANES, LANES), 1)
    s_ref[...] = jnp.where(lane == 0, part, 0.0)


def _filters(seq, w1t, b1, w2t, b2, freq, w3t, deltas):
    t = jnp.linspace(0.0, 1.0, seq, dtype=F32)[None, :]
    w = (2.0 * math.pi / seq) * jnp.arange(seq, dtype=F32)[None, :]
    bands = jnp.linspace(1e-4, HY_BANDS - 1, HY_BANDS, dtype=F32)[:, None]
    feats = jnp.concatenate([t, jnp.cos(bands * w), -jnp.sin(bands * w),
                             jnp.zeros((FEAT_ROWS - 1 - 2 * HY_BANDS, seq), F32)], axis=0)
    feats_rev = jnp.concatenate([feats[:, :1], feats[:, :0:-1]], axis=1)
    feats0 = jnp.broadcast_to(feats[:, :1], (FEAT_ROWS, LANES))
    nlt = seq // LT_FILT
    ct = WIDTH // LANES
    lyr = lambda l, i, o, c: (l, 0, 0)
    return pl.pallas_call(
        _filt_kernel,
        grid=(DEPTH, nlt, HY_ORDER, ct),
        in_specs=[
            pl.BlockSpec((FEAT_ROWS, LT_FILT), lambda l, i, o, c: (0, i)),
            pl.BlockSpec((FEAT_ROWS, LT_FILT), lambda l, i, o, c: (0, i)),
            pl.BlockSpec((FEAT_ROWS, LANES), lambda l, i, o, c: (0, 0)),
            pl.BlockSpec((None, HY_HIDDEN, FEAT_ROWS), lyr),
            pl.BlockSpec((None, HY_HIDDEN, 1), lyr),
            pl.BlockSpec((None, HY_HIDDEN, HY_HIDDEN), lyr),
            pl.BlockSpec((None, HY_HIDDEN, 1), lyr),
            pl.BlockSpec((None, HY_HIDDEN, 1), lyr),
            pl.BlockSpec((None, LANES, HY_HIDDEN), lambda l, i, o, c: (l, o * 2 * ct + c, 0)),
            pl.BlockSpec((None, LANES, HY_HIDDEN), lambda l, i, o, c: (l, o * 2 * ct + ct + c, 0)),
            pl.BlockSpec((None, LANES, 1), lambda l, i, o, c: (l, o * 2 * ct + c, 0)),
            pl.BlockSpec((None, LANES, 1), lambda l, i, o, c: (l, o * 2 * ct + ct + c, 0)),
        ],
        out_specs=[
            pl.BlockSpec((None, None, LANES, LT_FILT), lambda l, i, o, c: (l, o, c, i)),
            pl.BlockSpec((None, None, LANES, LT_FILT), lambda l, i, o, c: (l, o, c, i)),
            pl.BlockSpec((None, None, LANES, LANES), lambda l, i, o, c: (l, o, c, i)),
        ],
        out_shape=[
            jax.ShapeDtypeStruct((DEPTH, HY_ORDER, WIDTH, seq), F32),
            jax.ShapeDtypeStruct((DEPTH, HY_ORDER, WIDTH, seq), F32),
            jax.ShapeDtypeStruct((DEPTH, HY_ORDER, WIDTH, nlt * LANES), F32),
        ],
        scratch_shapes=[pltpu.VMEM((3 * HY_HIDDEN, LT_FILT), BF16),
                        pltpu.VMEM((3 * HY_HIDDEN, LT_FILT), BF16)],
        name="hyena_filters",
    )(feats, feats_rev, feats0, w1t, b1, w2t, b2, freq, w3t, w3t, deltas, deltas)


def _split_hi_lo(x):
    hi = x.astype(BF16)
    lo = (x - hi.astype(F32)).astype(BF16)
    return hi, lo


def _split3_rows(x):
    hi, lo = _split_hi_lo(x)
    return jnp.concatenate([hi, lo, hi], axis=0)


def _split3_cols(x):
    hi, lo = _split_hi_lo(x)
    return jnp.concatenate([hi, lo, hi], axis=1)


def _np_split(m):
    hi = m.astype(ml_dtypes.bfloat16)
    lo = (m - hi.astype(np.float64)).astype(ml_dtypes.bfloat16)
    return hi, lo


def _dft_tables(seq):
    nj = seq // LANES
    n1 = 2 * nj
    n = n1 * LANES
    khp = n1 // 2 + KH_PAD
    k1 = np.arange(khp, dtype=np.float64)[:, None]
    valid = (k1 <= n1 // 2).astype(np.float64)
    ang = 2.0 * np.pi * k1 * np.arange(n1, dtype=np.float64)[None, :] / n1
    fa_full = np.concatenate([valid * np.cos(ang), -valid * np.sin(ang)], axis=0)

    def lhs3(m):
        hi, lo = _np_split(m)
        return jnp.asarray(np.concatenate([hi, hi, lo], axis=1))

    def rhs3(m):
        hi, lo = _np_split(m)
        return jnp.asarray(np.concatenate([hi, hi, lo], axis=0))

    n2 = np.arange(LANES, dtype=np.float64)
    ang2 = 2.0 * np.pi * n2[:, None] * n2[None, :] / LANES
    c2, s2 = np.cos(ang2), np.sin(ang2)
    fb = np.block([[c2, -s2], [s2, c2]])
    fbi = np.block([[c2, s2], [-s2, c2]])
    w = np.where((k1 == 0) | (k1 == n1 // 2), 1.0, 2.0) * valid
    th = 2.0 * np.pi * np.arange(nj, dtype=np.float64)[:, None] * k1.T / n1
    fai = np.concatenate([(w.T / n) * np.cos(th), -(w.T / n) * np.sin(th)], axis=1)
    phi = 2.0 * np.pi * k1 * n2[None, :] / n
    return dict(
        nj=nj, n1=n1, khp=khp,
        fa=lhs3(fa_full[:, :nj]), fa_full=lhs3(fa_full), fb=rhs3(fb), fbi=rhs3(fbi), fai=lhs3(fai),
        twc=jnp.asarray(valid * np.cos(phi), F32), tws=jnp.asarray(valid * np.sin(phi), F32))


def _dft_rows(fa_ref, twc, tws, khp, cols):
    rhs = cols[0] if len(cols) == 1 else jnp.concatenate(cols, axis=1)
    y = _dot(fa_ref[...], _split3_rows(rhs))
    out = []
    for i in range(len(cols)):
        yre = y[0:khp, i * LANES:(i + 1) * LANES]
        yim = y[khp:2 * khp, i * LANES:(i + 1) * LANES]
        out.append((yre * twc + yim * tws, yim * twc - yre * tws))
    return out


def _spec_kernel(khp, g_ref, s_ref, fa_ref, fb_ref, twc_ref, tws_ref, h_ref, ybuf):
    twc = twc_ref[...]
    tws = tws_ref[...]
    for c in range(0, CB_SPEC, 2):
        res = _dft_rows(fa_ref, twc, tws, khp, [g_ref[c], g_ref[c + 1]])
        for i, (yre, yim) in enumerate(res):
            r0 = (c + i) * khp
            ybuf[r0:r0 + khp, 0:LANES] = yre
            ybuf[r0:r0 + khp, LANES:2 * LANES] = yim
    x = _dot(_split3_cols(ybuf[...]), fb_ref[...])
    for c in range(CB_SPEC):
        norm = jnp.sum(s_ref[c:c + 1, :], axis=1, keepdims=True)
        h_ref[c] = x[c * khp:(c + 1) * khp, :] / norm


def _spectrum(g, s, tab):
    nch = g.shape[0]
    n1, khp = tab["n1"], tab["khp"]
    const = lambda a: pl.BlockSpec(a.shape, lambda i: (0, 0))
    return pl.pallas_call(
        functools.partial(_spec_kernel, khp),
        grid=(nch // CB_SPEC,),
        in_specs=[
            pl.BlockSpec((CB_SPEC, n1, LANES), lambda i: (i, 0, 0)),
            pl.BlockSpec((CB_SPEC, s.shape[-1]), lambda i: (i, 0)),
            const(tab["fa_full"]), const(tab["fb"]), const(tab["twc"]), const(tab["tws"]),
        ],
        out_specs=pl.BlockSpec((CB_SPEC, khp, 2 * LANES), lambda i: (i, 0, 0)),
        out_shape=jax.ShapeDtypeStruct((nch, khp, 2 * LANES), F32),
        scratch_shapes=[pltpu.VMEM((CB_SPEC * khp, 2 * LANES), F32)],
        name="hyena_spectrum",
    )(g, s, tab["fa_full"], tab["fb"], tab["twc"], tab["tws"])


def _hyfft_kernel(bsz, nj, khp, prm_ref, v_ref, x1_ref, x2_ref, z_ref, h_ref,
                  fa_ref, fb_ref, fbi_ref, fai_ref, twc_ref, tws_ref, out_ref,
                  xs_ref, ybuf, pbuf):
    ct = pl.program_id(0)
    rows = bsz * nj
    pairs = [(b, b + 1) for b in range(0, bsz, 2)] if bsz > 1 else [(0,)]
    twc = twc_ref[...]
    tws = tws_ref[...]

    lane = lax.broadcasted_iota(jnp.int32, (rows, LANES), 1)
    blk = lax.broadcasted_iota(jnp.int32, (rows, LANES), 0) & (nj - 1)
    first = (lane == 0) & (blk == 0)
    last = (lane == LANES - 1) & (blk == nj - 1)

    def conv3(x, base):
        r = pltpu.roll(x, 1, 1)
        xm1 = jnp.where(lane == 0, pltpu.roll(r, 1, 0), r)
        xm1 = jnp.where(first, 0.0, xm1)
        l = pltpu.roll(x, LANES - 1, 1)
        xp1 = jnp.where(lane == LANES - 1, pltpu.roll(l, rows - 1, 0), l)
        xp1 = jnp.where(last, 0.0, xp1)
        return prm_ref[base] * xm1 + prm_ref[base + 1] * x + prm_ref[base + 2] * xp1 + prm_ref[base + 3]

    def row0(c, b):
        return (c * bsz + b) * khp

    def forward(c, x):
        for pr in pairs:
            res = _dft_rows(fa_ref, twc, tws, khp, [x[b * nj:(b + 1) * nj, :] for b in pr])
            for b, (yre, yim) in zip(pr, res):
                r0 = row0(c, b)
                ybuf[r0:r0 + khp, 0:LANES] = yre
                ybuf[r0:r0 + khp, LANES:2 * LANES] = yim

    def spectral(order):
        x = _dot(_split3_cols(ybuf[...]), fb_ref[...])
        for c in range(CB_HY):
            hre = h_ref[order, c, :, 0:LANES]
            him = h_ref[order, c, :, LANES:2 * LANES]
            for b in range(bsz):
                r0 = row0(c, b)
                xre = x[r0:r0 + khp, 0:LANES]
                xim = x[r0:r0 + khp, LANES:2 * LANES]
                pbuf[r0:r0 + khp, 0:LANES] = xre * hre - xim * him
                pbuf[r0:r0 + khp, LANES:2 * LANES] = xre * him + xim * hre
        r = _dot(_split3_cols(pbuf[...]), fbi_ref[...])
        for c in range(CB_HY):
            for b in range(bsz):
                r0 = row0(c, b)
                rre = r[r0:r0 + khp, 0:LANES]
                rim = r[r0:r0 + khp, LANES:2 * LANES]
                ybuf[r0:r0 + khp, 0:LANES] = rre * twc - rim * tws
                ybuf[r0:r0 + khp, LANES:2 * LANES] = rre * tws + rim * twc

    def inverse(c):
        outs = [None] * bsz
        for pr in pairs:
            rre = [ybuf[row0(c, b):row0(c, b) + khp, 0:LANES] for b in pr]
            rim = [ybuf[row0(c, b):row0(c, b) + khp, LANES:2 * LANES] for b in pr]
            if len(pr) > 1:
                rre, rim = [jnp.concatenate(rre, axis=1)], [jnp.concatenate(rim, axis=1)]
            y = _dot(fai_ref[...], _split3_rows(jnp.concatenate([rre[0], rim[0]], axis=0)))
            for i, b in enumerate(pr):
                outs[b] = y[:, i * LANES:(i + 1) * LANES]
        return outs[0] if bsz == 1 else jnp.concatenate(outs, axis=0)

    for c in range(CB_HY):
        base = (ct * CB_HY + c) * 16
        v = conv3(v_ref[c], base)
        xs_ref[0, c] = v
        xs_ref[1, c] = conv3(x1_ref[c], base + 4)
        xs_ref[2, c] = conv3(x2_ref[c], base + 8)
        forward(c, v)
    spectral(0)
    for c in range(CB_HY):
        base = (ct * CB_HY + c) * 16
        v = xs_ref[0, c]
        z1 = xs_ref[1, c] * (inverse(c) + prm_ref[base + 12] * v)
        xs_ref[0, c] = z1
        forward(c, z1)
    spectral(1)
    for c in range(CB_HY):
        base = (ct * CB_HY + c) * 16
        z1 = xs_ref[0, c]
        y2 = inverse(c) + prm_ref[base + 13] * z1
        out_ref[:, c, :] = xs_ref[2, c] * y2 * _silu(z_ref[c])


def _hyena_fft(prm, u_t, h, layer, tab, row0, bsz, seq):
    nj, khp = tab["nj"], tab["khp"]
    nct = WIDTH // CB_HY
    rows = bsz * nj
    blk = (CB_HY, rows, LANES)
    rblk = row0 // (rows * LANES)
    assert rblk * rows * LANES == row0
    const = lambda a: pl.BlockSpec(a.shape, lambda i: (0, 0))
    return pl.pallas_call(
        functools.partial(_hyfft_kernel, bsz, nj, khp),
        grid=(nct,),
        in_specs=[
            pl.BlockSpec(memory_space=pltpu.SMEM),
            pl.BlockSpec(blk, lambda i: (i, rblk, 0)),
            pl.BlockSpec(blk, lambda i: (nct + i, rblk, 0)),
            pl.BlockSpec(blk, lambda i: (2 * nct + i, rblk, 0)),
            pl.BlockSpec(blk, lambda i: (3 * nct + i, rblk, 0)),
            pl.BlockSpec((None, HY_ORDER, CB_HY, khp, 2 * LANES), lambda i: (layer, 0, i, 0, 0)),
            const(tab["fa"]), const(tab["fb"]), const(tab["fbi"]), const(tab["fai"]),
            const(tab["twc"]), const(tab["tws"]),
        ],
        out_specs=pl.BlockSpec((rows, CB_HY, LANES), lambda i: (0, i, 0)),
        out_shape=jax.ShapeDtypeStruct((rows, WIDTH, LANES), F32),
        scratch_shapes=[
            pltpu.VMEM((3, CB_HY, bsz * nj, LANES), F32),
            pltpu.VMEM((CB_HY * bsz * khp, 2 * LANES), F32),
            pltpu.VMEM((CB_HY * bsz * khp, 2 * LANES), F32),
        ],
        name="hyena_fft",
    )(prm, u_t, u_t, u_t, u_t, h, tab["fa"], tab["fb"], tab["fbi"], tab["fai"],
      tab["twc"], tab["tws"])


def _out_kernel(final, starts, nx, *refs):
    ng = len(starts)
    x_refs = refs[0:nx]
    yml_ref, yrt_ref = refs[nx:nx + 2]
    yhy_refs = refs[nx + 2:nx + 2 + ng]
    w_ref, g_ref = refs[nx + 2 + ng:nx + 4 + ng]
    out_refs = refs[nx + 4 + ng:-1]
    lhs_ref = refs[-1]
    i = pl.program_id(0)
    lhs_ref[:, 0:WIDTH] = yml_ref[...].astype(BF16)
    lhs_ref[:, WIDTH:2 * WIDTH] = yrt_ref[...].astype(BF16)
    jb = TM_OUT // LANES

    def fill_hyena(ref):
        for j in range(jb):
            lhs_ref[j * LANES:(j + 1) * LANES, 2 * WIDTH:] = ref[j].T.astype(BF16)

    for gi in range(ng):
        inside = i >= starts[gi]
        if gi + 1 < ng:
            inside = inside & (i < starts[gi + 1])
        pl.when(inside)(functools.partial(fill_hyena, yhy_refs[gi]))
    x = _x_tile(x_refs, starts if nx > 1 else [0]) + _dot(lhs_ref[...], w_ref[...])
    if not final:
        out_refs[0][...] = x
        return
    ms = jnp.mean(x * x, axis=-1, keepdims=True)
    x = x * lax.rsqrt(ms + RMS_EPS) * g_ref[...]
    for gi in range(ng):
        inside = i >= starts[gi]
        if gi + 1 < ng:
            inside = inside & (i < starts[gi + 1])

        @pl.when(inside)
        def _(gi=gi):
            out_refs[gi][...] = x


def _out_proj(xs, y_ml, y_rt, y_hy, w, gain, final, groups):
    t = y_ml.shape[0]
    jb = TM_OUT // LANES
    starts = [r0 // TM_OUT for (r0, _, _) in groups]
    counts = [bsz * seq // TM_OUT for (_, bsz, seq) in groups]

    def tile(gi):
        return lambda i: jnp.clip(i - starts[gi], 0, counts[gi] - 1)

    hy_specs = [pl.BlockSpec((jb, WIDTH, LANES), lambda i, f=tile(gi): (f(i), 0, 0))
                for gi in range(len(groups))]
    if final:
        out_specs = [pl.BlockSpec((TM_OUT, D_MODEL), lambda i, f=tile(gi): (f(i), 0))
                     for gi in range(len(groups))]
        out_shape = [jax.ShapeDtypeStruct((bsz * seq, D_MODEL), F32) for (_, bsz, seq) in groups]
    else:
        out_specs = [pl.BlockSpec((TM_OUT, D_MODEL), lambda i: (i, 0))]
        out_shape = [jax.ShapeDtypeStruct((t, D_MODEL), F32)]
    return pl.pallas_call(
        functools.partial(_out_kernel, final, starts, len(xs)),
        grid=(t // TM_OUT,),
        in_specs=[
            *_x_specs(xs, TM_OUT, 1),
            pl.BlockSpec((TM_OUT, WIDTH), lambda i: (i, 0)),
            pl.BlockSpec((TM_OUT, WIDTH), lambda i: (i, 0)),
            *hy_specs,
            pl.BlockSpec((3 * WIDTH, D_MODEL), lambda i: (0, 0)),
            pl.BlockSpec((1, D_MODEL), lambda i: (0, 0)),
        ],
        out_specs=out_specs,
        out_shape=out_shape,
        scratch_shapes=[pltpu.VMEM((TM_OUT, 3 * WIDTH), BF16)],
        name="out_proj",
    )(*xs, y_ml, y_rt, *y_hy, w, gain)


def kernel(x_prompt, x_sample, norm_g, w_in, ml_conv_w, ml_conv_b, ml_gate_b, ml_norm_g, rt_norm_g, hy_conv_w, hy_conv_b, hy_w1, hy_b1, hy_w2, hy_b2, hy_w3, hy_freq, hy_deltas, hy_skip, w_out, final_g):
    groups = []
    row0 = 0
    for xg in (x_prompt, x_sample):
        bsz, seq, _ = xg.shape
        groups.append((row0, bsz, seq))
        row0 += bsz * seq
    xs = [x_prompt.reshape(-1, D_MODEL), x_sample.reshape(-1, D_MODEL)]

    c_gate = ML_COLS
    c_rt = c_gate + 4 * HEADS
    c_hy = c_rt + RT_COLS
    gate_pad = jnp.zeros((DEPTH, D_MODEL, LANES - 4 * HEADS), F32)
    w_tok = jnp.concatenate([w_in[:, :, :c_gate], w_in[:, :, c_gate:c_rt], gate_pad,
                             w_in[:, :, c_rt:c_hy]], axis=2).astype(BF16)
    w_gt = jnp.swapaxes(w_in[:, :, c_gate:c_rt], 1, 2).astype(BF16)
    w_hyt = jnp.swapaxes(w_in[:, :, c_hy:], 1, 2).astype(BF16)
    w_out_b = w_out.astype(BF16)
    gate_b = ml_gate_b.reshape(DEPTH, 1, 4 * HEADS)
    gate_b_row = jnp.concatenate([gate_b, jnp.zeros((DEPTH, 1, LANES - 4 * HEADS), F32)], axis=2)
    gate_b_col = ml_gate_b.reshape(DEPTH, 4 * HEADS, 1)

    cw = hy_conv_w.reshape(DEPTH, 3, 3, WIDTH)
    cbias = hy_conv_b.reshape(DEPTH, 1, 3, WIDTH)
    taps = jnp.concatenate([cw, cbias], axis=1)
    taps = jnp.transpose(taps, (0, 3, 2, 1)).reshape(DEPTH, WIDTH, 12)
    hy_prm = jnp.concatenate([taps, jnp.transpose(hy_skip, (0, 2, 1)),
                              jnp.zeros((DEPTH, WIDTH, 2), F32)], axis=2).reshape(DEPTH, WIDTH * 16)

    w1t = jnp.concatenate([jnp.swapaxes(hy_w1, 1, 2),
                           jnp.zeros((DEPTH, HY_HIDDEN, FEAT_ROWS - hy_w1.shape[1]), F32)], axis=2)
    w2t = jnp.swapaxes(hy_w2, 1, 2)
    w3t = jnp.swapaxes(hy_w3, 1, 2)
    b1 = hy_b1[:, :, None]
    b2 = hy_b2[:, :, None]
    freq = hy_freq[:, :, None]
    deltas = hy_deltas.reshape(DEPTH, HY_ORDER * 2 * WIDTH, 1)

    tables, spectra = [], []
    for (_, _, seq) in groups:
        tab = _dft_tables(seq)
        gb, gf, s = _filters(seq, w1t, b1, w2t, b2, freq, w3t, deltas)
        g = jnp.concatenate([gf, gb], axis=-1).reshape(DEPTH * HY_ORDER * WIDTH, tab["n1"], LANES)
        h = _spectrum(g, s.reshape(DEPTH * HY_ORDER * WIDTH, -1), tab)
        tables.append(tab)
        spectra.append(h.reshape(DEPTH, HY_ORDER, WIDTH, tab["khp"], 2 * LANES))
    rot_c, rot_s = _rotary_tables(max(seq for (_, _, seq) in groups))
    rt_tabs = _retention_tables()

    for layer in range(DEPTH):
        gain = norm_g[layer][None, :]
        u_ml, u_g, u_rt, u_gt = _in_proj(xs, gain, w_tok[layer], w_gt[layer])
        u_hyt = _in_proj_hy(xs, gain, w_hyt[layer])
        pb, cb, gr, gs = _gates(u_g, u_gt, gate_b_row[layer], gate_b_col[layer])
        y_ml = _mlstm(u_ml, pb, cb, gr, gs, ml_conv_w[layer], ml_conv_b[layer][None, :],
                      ml_norm_g[layer][None, :], groups)
        y_rt = _retention(u_rt, rot_c, rot_s, rt_tabs, rt_norm_g[layer][None, :], groups)
        y_hy = [_hyena_fft(hy_prm[layer], u_hyt, spectra[gi], layer, tables[gi], r0, bsz, seq)
                for gi, (r0, bsz, seq) in enumerate(groups)]
        outs = _out_proj(xs, y_ml, y_rt, y_hy, w_out_b[layer], final_g[None, :],
                         layer == DEPTH - 1, groups)
        xs = [outs[0]]

    return outs[0].reshape(x_prompt.shape), outs[1].reshape(x_sample.shape)
```

```python
import functools
import math

import ml_dtypes
import numpy as np
import jax
import jax.numpy as jnp
from jax import lax
from jax.experimental import pallas as pl
from jax.experimental.pallas import tpu as pltpu

F32 = jnp.float32
BF16 = jnp.bfloat16

D_MODEL = 1024
DEPTH = 4
CHUNK = 128
HEADS = 4
HEAD_DIM = 128
WIDTH = 512
HY_ORDER = 2
HY_BANDS = 16
HY_HIDDEN = 64
FEAT_ROWS = 40
ROPE_BASE = 10000.0
RMS_EPS = 1e-6
HEAD_NORM_EPS = 1e-5
M_INIT = -1e30
RT_LOG_GAMMA_FWD = tuple(math.log(1.0 - 2.0 ** (-5.0 - h)) for h in range(HEADS))
RT_LOG_GAMMA_BWD = tuple(math.log(1.0 - 2.0 ** (-5.5 - h)) for h in range(HEADS))

LANES = 128
SUBLANES = 8
TM_IN = 256
TM_HY = 1024
TR_HY = 1024
TM_OUT = 1024
TB_SEQ = 512
TG = 1024
CB_HY = 8
CG_HY = 2
CB_SPEC = 8
KH_PAD = 8
LT_FILT = 2048
ML_COLS = 5 * WIDTH
RT_COLS = 4 * WIDTH
HY_ROWS = 4 * WIDTH
HIGHEST = lax.Precision.HIGHEST


def _dot(a, b):
    return jnp.dot(a, b, preferred_element_type=F32)


def _dot_nt(a, b):
    return lax.dot_general(a, b, (((1,), (1,)), ((), ())), preferred_element_type=F32)


def _dot_tn(a, b):
    return lax.dot_general(a, b, (((0,), (0,)), ((), ())), preferred_element_type=F32)


def _dot_f32(a, b):
    return jnp.dot(a, b, precision=HIGHEST, preferred_element_type=F32)


def _silu(y):
    return y * jax.nn.sigmoid(y)


def _log_sigmoid(x):
    return -(jnp.maximum(-x, 0.0) + jnp.log1p(jnp.exp(-jnp.abs(x))))


def _head_norm(h, gain):
    mu = jnp.mean(h, axis=-1, keepdims=True)
    hc = h - mu
    var = jnp.mean(hc * hc, axis=-1, keepdims=True)
    return hc * lax.rsqrt(var + HEAD_NORM_EPS) * gain


def _tile_starts(xs, tm):
    starts, s = [], 0
    for a in xs:
        starts.append(s)
        s += a.shape[0] // tm
    return starts, s


def _x_specs(xs, tm, ngrid):
    starts, _ = _tile_starts(xs, tm)
    specs = []
    for a, s in zip(xs, starts):
        cnt = a.shape[0] // tm
        if ngrid == 1:
            specs.append(pl.BlockSpec((tm, D_MODEL), lambda i, s=s, cnt=cnt: (jnp.clip(i - s, 0, cnt - 1), 0)))
        else:
            specs.append(pl.BlockSpec((tm, D_MODEL), lambda i, r, s=s, cnt=cnt: (jnp.clip(i - s, 0, cnt - 1), 0)))
    return specs


def _x_tile(x_refs, starts):
    x = x_refs[0][...]
    for ref, s in zip(x_refs[1:], starts[1:]):
        x = jnp.where(pl.program_id(0) >= s, ref[...], x)
    return x


def _rms_bf16(x, g_ref):
    ms = jnp.mean(x * x, axis=-1, keepdims=True)
    return (x * lax.rsqrt(ms + RMS_EPS) * g_ref[...]).astype(BF16)


def _in_kernel(starts, *refs):
    nx = len(starts)
    g_ref, wtok_ref, wgt_ref, oml_ref, og_ref, ort_ref, ogt_ref = refs[nx:]
    h = _rms_bf16(_x_tile(refs[:nx], starts), g_ref)
    y = _dot(h, wtok_ref[...])
    oml_ref[...] = y[:, 0:ML_COLS]
    og_ref[...] = y[:, ML_COLS:ML_COLS + LANES]
    ort_ref[...] = y[:, ML_COLS + LANES:]
    ogt_ref[...] = _dot_nt(wgt_ref[...], h)


def _in_proj(xs, gain, w_tok, w_gt):
    starts, nt = _tile_starts(xs, TM_IN)
    t = nt * TM_IN
    ncol = w_tok.shape[1]
    return pl.pallas_call(
        functools.partial(_in_kernel, starts),
        grid=(nt,),
        in_specs=[
            *_x_specs(xs, TM_IN, 1),
            pl.BlockSpec((1, D_MODEL), lambda i: (0, 0)),
            pl.BlockSpec((D_MODEL, ncol), lambda i: (0, 0)),
            pl.BlockSpec((4 * HEADS, D_MODEL), lambda i: (0, 0)),
        ],
        out_specs=[
            pl.BlockSpec((TM_IN, ML_COLS), lambda i: (i, 0)),
            pl.BlockSpec((TM_IN, LANES), lambda i: (i, 0)),
            pl.BlockSpec((TM_IN, RT_COLS), lambda i: (i, 0)),
            pl.BlockSpec((4 * HEADS, TM_IN), lambda i: (0, i)),
        ],
        out_shape=[
            jax.ShapeDtypeStruct((t, ML_COLS), F32),
            jax.ShapeDtypeStruct((t, LANES), F32),
            jax.ShapeDtypeStruct((t, RT_COLS), F32),
            jax.ShapeDtypeStruct((4 * HEADS, t), F32),
        ],
        name="in_proj",
    )(*xs, gain, w_tok, w_gt)


def _in_hy_kernel(starts, *refs):
    nx = len(starts)
    g_ref, w_ref, o_ref = refs[nx:]
    y = _dot_nt(w_ref[...], _rms_bf16(_x_tile(refs[:nx], starts), g_ref))
    o_ref[...] = y.reshape(o_ref.shape)


def _in_proj_hy(xs, gain, w_hyt):
    starts, nt = _tile_starts(xs, TM_HY)
    t = nt * TM_HY
    nr = HY_ROWS // TR_HY
    return pl.pallas_call(
        functools.partial(_in_hy_kernel, starts),
        grid=(nt, nr),
        in_specs=[
            *_x_specs(xs, TM_HY, 2),
            pl.BlockSpec((1, D_MODEL), lambda i, r: (0, 0)),
            pl.BlockSpec((TR_HY, D_MODEL), lambda i, r: (r, 0)),
        ],
        out_specs=pl.BlockSpec((TR_HY, TM_HY // LANES, LANES), lambda i, r: (r, i, 0)),
        out_shape=jax.ShapeDtypeStruct((HY_ROWS, t // LANES, LANES), F32),
        name="in_proj_hyena",
    )(*xs, gain, w_hyt)


def _gates_kernel(g_ref, gt_ref, b_ref, bt_ref, pb_ref, cb_ref, gr_ref, gs_ref):
    row = lax.broadcasted_iota(jnp.int32, (CHUNK, CHUNK), 0)
    col = lax.broadcasted_iota(jnp.int32, (CHUNK, CHUNK), 1)
    lower = (row >= col).astype(F32)
    upper = (row <= col).astype(F32)
    gr_ref[...] = jnp.zeros_like(gr_ref)
    gs_ref[...] = jnp.zeros_like(gs_ref)
    for c in range(TG // CHUNK):
        sl = slice(c * CHUNK, (c + 1) * CHUNK)
        lf = _log_sigmoid(g_ref[sl, :] + b_ref[...])
        cum_c = (_dot_f32(lower, lf), _dot_f32(upper, lf))
        pre_t = gt_ref[:, sl] + bt_ref[...]
        lf_t = _log_sigmoid(pre_t)
        cum_r = (_dot_f32(lf_t, upper), _dot_f32(lf_t, lower))
        for d in range(2):
            for h in range(HEADS):
                kf = (2 * d + 1) * HEADS + h
                ki = 2 * d * HEADS + h
                cum = cum_r[d][kf:kf + 1, :]
                li = pre_t[ki:ki + 1, :]
                r = li - cum
                f_tot = cum[:, 0:1] if d else cum[:, CHUNK - 1:CHUNK]
                g = f_tot - cum + li
                allowed = (row <= col) if d else (row >= col)
                pmax = jnp.max(jnp.where(allowed, r, -jnp.inf), axis=1, keepdims=True)
                pb_ref[d, h, sl, :] = jnp.broadcast_to(pmax, (CHUNK, LANES))
                cb_ref[d, h, sl, :] = jnp.broadcast_to(cum_c[d][:, kf:kf + 1], (CHUNK, LANES))
                gr_ref[d, h, 0:1, sl] = r
                gr_ref[d, h, 1:2, sl] = g
                gs_ref[d, h, c, 0:1, :] = jnp.broadcast_to(f_tot, (1, LANES))
                gs_ref[d, h, c, 1:2, :] = jnp.broadcast_to(jnp.max(g, axis=1, keepdims=True), (1, LANES))


def _gates(g, gt, bias_row, bias_col):
    t = g.shape[0]
    nc = TG // CHUNK
    return pl.pallas_call(
        _gates_kernel,
        grid=(t // TG,),
        in_specs=[
            pl.BlockSpec((TG, LANES), lambda i: (i, 0)),
            pl.BlockSpec((4 * HEADS, TG), lambda i: (0, i)),
            pl.BlockSpec((1, LANES), lambda i: (0, 0)),
            pl.BlockSpec((4 * HEADS, 1), lambda i: (0, 0)),
        ],
        out_specs=[
            pl.BlockSpec((2, HEADS, TG, LANES), lambda i: (0, 0, i, 0)),
            pl.BlockSpec((2, HEADS, TG, LANES), lambda i: (0, 0, i, 0)),
            pl.BlockSpec((2, HEADS, SUBLANES, TG), lambda i: (0, 0, 0, i)),
            pl.BlockSpec((2, HEADS, nc, SUBLANES, LANES), lambda i: (0, 0, i, 0, 0)),
        ],
        out_shape=[
            jax.ShapeDtypeStruct((2, HEADS, t, LANES), F32),
            jax.ShapeDtypeStruct((2, HEADS, t, LANES), F32),
            jax.ShapeDtypeStruct((2, HEADS, SUBLANES, t), F32),
            jax.ShapeDtypeStruct((2, HEADS, t // CHUNK, SUBLANES, LANES), F32),
        ],
        name="gates",
    )(g, gt, bias_row, bias_col)


def _seq_pos(g, groups):
    pos, nbs = None, None
    start = 0
    for (_, bsz, seq) in groups:
        nb = seq // TB_SEQ
        p = (g - start) % nb
        pos = p if pos is None else jnp.where(g >= start, p, pos)
        nbs = nb if nbs is None else jnp.where(g >= start, nb, nbs)
        start += bsz * nb
    return pos, nbs


def _num_blocks(groups):
    return sum(bsz * seq // TB_SEQ for (_, bsz, seq) in groups)


def _ml_kernel(rev, groups, *refs):
    if rev:
        (qs_ref, ks_ref, v_ref, pb_ref, cb_ref, gr_ref, gs_ref, o_ref, z_ref, hf_ref, ng_ref,
         out_ref, st_ref, m_ref) = refs
    else:
        (q_ref, k_ref, v_ref, qp_ref, qn_ref, kp_ref, kn_ref, pb_ref, cb_ref, gr_ref, gs_ref,
         cw_ref, cbias_ref, out_ref, qs_ref, ks_ref, st_ref, m_ref, stg_ref) = refs
    tb = TB_SEQ
    nch = tb // CHUNK
    n = pl.program_id(0)
    sb, nb = _seq_pos((_num_blocks(groups) - 1 - n) if rev else n, groups)

    @pl.when(sb == (nb - 1 if rev else 0))
    def _():
        st_ref[...] = jnp.zeros_like(st_ref)
        m_ref[...] = jnp.full_like(m_ref, M_INIT)

    def conv_silu(raw_ref, prev_ref, next_ref, c0):
        stg_ref[SUBLANES:SUBLANES + tb, :] = raw_ref[...]
        stg_ref[SUBLANES - 1:SUBLANES, :] = jnp.where(sb > 0, prev_ref[SUBLANES - 1:SUBLANES, :], 0.0)
        stg_ref[SUBLANES + tb:SUBLANES + tb + 1, :] = jnp.where(sb < nb - 1, next_ref[0:1, :], 0.0)
        w = cw_ref[:, c0:c0 + WIDTH]
        y = (stg_ref[SUBLANES - 1:SUBLANES - 1 + tb, :] * w[0:1, :]
             + stg_ref[SUBLANES:SUBLANES + tb, :] * w[1:2, :]
             + stg_ref[SUBLANES + 1:SUBLANES + 1 + tb, :] * w[2:3, :] + cbias_ref[:, c0:c0 + WIDTH])
        return _silu(y)

    if not rev:
        qs_ref[...] = conv_silu(q_ref, qp_ref, qn_ref, 0).astype(BF16)
        ks_ref[...] = conv_silu(k_ref, kp_ref, kn_ref, WIDTH) * (HEAD_DIM ** -0.5)

    row = lax.broadcasted_iota(jnp.int32, (CHUNK, CHUNK), 0)
    col = lax.broadcasted_iota(jnp.int32, (CHUNK, CHUNK), 1)
    allowed = (row <= col) if rev else (row >= col)
    ones = jnp.ones((CHUNK, HEAD_DIM), BF16)

    def chunk(c, h):
        sl = pl.ds(pl.multiple_of(c * CHUNK, CHUNK), CHUNK)
        hs = slice(h * HEAD_DIM, (h + 1) * HEAD_DIM)
        q = qs_ref[sl, hs]
        kt = ks_ref[sl, hs].T
        v2 = jnp.concatenate([v_ref[sl, hs].astype(BF16), ones], axis=1)
        m_prev = m_ref[h, 0:1, :]
        mb = jnp.maximum(pb_ref[h, sl, :], m_prev)
        p = jnp.exp(jnp.where(allowed, gr_ref[h, 0:1, sl] - mb, -jnp.inf))
        w_inter = jnp.exp(m_prev - mb)
        floor = jnp.exp(-(cb_ref[h, sl, :] + mb))
        s = _dot(q, kt.astype(BF16)) * p
        inter = _dot(q, st_ref[h].astype(BF16))
        num2 = _dot(s.astype(BF16), v2) + jnp.concatenate([w_inter, w_inter], axis=1) * inter
        out = num2[:, 0:HEAD_DIM] / jnp.maximum(jnp.abs(num2[:, HEAD_DIM:]), floor)
        f_tot = gs_ref[h, c, 0:1, :]
        m_chunk = gs_ref[h, c, 1:2, :]
        m_new = jnp.maximum(f_tot + m_prev, m_chunk)
        kw = (kt * jnp.exp(gr_ref[h, 1:2, sl] - m_new)).astype(BF16)
        kv2 = _dot(kw, v2)
        a = jnp.exp(f_tot + m_prev - m_new)
        st_ref[h] = jnp.concatenate([a, a], axis=1) * st_ref[h] + kv2
        m_ref[h] = jnp.broadcast_to(m_new, (SUBLANES, LANES))
        return out, sl, hs

    def body(i, carry):
        c = (nch - 1 - i) if rev else i
        for h in range(HEADS):
            out, sl, hs = chunk(c, h)
            if rev:
                hh = jax.nn.sigmoid(o_ref[sl, hs]) * (out + hf_ref[sl, hs])
                y = _head_norm(hh, ng_ref[:, hs]) * _silu(z_ref[sl, hs])
                out_ref[sl, hs] = y.astype(out_ref.dtype)
            else:
                out_ref[sl, hs] = out
        return carry

    lax.fori_loop(0, nch, body, 0, unroll=True)


def _mlstm_dir(rev, u_ml, pb, cb, gr, gs, conv_w, conv_b, norm_g, fwd, groups):
    t = u_ml.shape[0]
    tb = TB_SEQ
    t8 = tb // SUBLANES
    nch = tb // CHUNK
    nblk = _num_blocks(groups)
    d = 1 if rev else 0

    def rb(n):
        return (nblk - 1 - n) if rev else n

    def col(g):
        return lambda n: (rb(n), g)

    def halo_prev(g):
        return lambda n: (jnp.maximum(rb(n) * t8 - 1, 0), g)

    def halo_next(g):
        return lambda n: (jnp.minimum((rb(n) + 1) * t8, t // SUBLANES - 1), g)

    blk = (tb, WIDTH)
    halo = (SUBLANES, WIDTH)
    gate_specs = [
        pl.BlockSpec((None, HEADS, tb, LANES), lambda n: (d, 0, rb(n), 0)),
        pl.BlockSpec((None, HEADS, tb, LANES), lambda n: (d, 0, rb(n), 0)),
        pl.BlockSpec((None, HEADS, SUBLANES, tb), lambda n: (d, 0, 0, rb(n))),
        pl.BlockSpec((None, HEADS, nch, SUBLANES, LANES), lambda n: (d, 0, rb(n), 0, 0)),
    ]
    scratch = [
        pltpu.VMEM((HEADS, HEAD_DIM, 2 * HEAD_DIM), F32),
        pltpu.VMEM((HEADS, SUBLANES, LANES), F32),
    ]
    full = jax.ShapeDtypeStruct((t, WIDTH), F32)
    if rev:
        qs, ks, h_fwd = fwd
        in_specs = [pl.BlockSpec(blk, col(0)), pl.BlockSpec(blk, col(0)), pl.BlockSpec(blk, col(2)),
                    *gate_specs,
                    pl.BlockSpec(blk, col(3)), pl.BlockSpec(blk, col(4)), pl.BlockSpec(blk, col(0)),
                    pl.BlockSpec((1, WIDTH), lambda n: (0, 0))]
        args = [qs, ks, u_ml, pb, cb, gr, gs, u_ml, u_ml, h_fwd, norm_g]
        out_specs = pl.BlockSpec(blk, col(0))
        out_shape = jax.ShapeDtypeStruct(full.shape, BF16)
    else:
        in_specs = [pl.BlockSpec(blk, col(0)), pl.BlockSpec(blk, col(1)), pl.BlockSpec(blk, col(2)),
                    pl.BlockSpec(halo, halo_prev(0)), pl.BlockSpec(halo, halo_next(0)),
                    pl.BlockSpec(halo, halo_prev(1)), pl.BlockSpec(halo, halo_next(1)),
                    *gate_specs,
                    pl.BlockSpec((3, 2 * WIDTH), lambda n: (0, 0)),
                    pl.BlockSpec((1, 2 * WIDTH), lambda n: (0, 0))]
        args = [u_ml, u_ml, u_ml, u_ml, u_ml, u_ml, u_ml, pb, cb, gr, gs, conv_w, conv_b]
        out_specs = [pl.BlockSpec(blk, col(0))] * 3
        out_shape = [full, jax.ShapeDtypeStruct((t, WIDTH), BF16), full]
        scratch.append(pltpu.VMEM((tb + 2 * SUBLANES, WIDTH), F32))
    return pl.pallas_call(
        functools.partial(_ml_kernel, rev, groups),
        grid=(nblk,),
        in_specs=in_specs,
        out_specs=out_specs,
        out_shape=out_shape,
        scratch_shapes=scratch,
        name="mlstm_bwd" if rev else "mlstm_fwd",
    )(*args)


def _mlstm(u_ml, pb, cb, gr, gs, conv_w, conv_b, norm_g, groups):
    h_fwd, qs, ks = _mlstm_dir(False, u_ml, pb, cb, gr, gs, conv_w, conv_b, norm_g, None, groups)
    return _mlstm_dir(True, u_ml, pb, cb, gr, gs, conv_w, conv_b, norm_g, (qs, ks, h_fwd), groups)


def _rt_kernel(rev, groups, *refs):
    if rev:
        (qs_ref, ks_ref, v_ref, dec_ref, qd_ref, kd_ref, z_ref, hf_ref, ng_ref,
         out_ref, st_ref) = refs
    else:
        (q_ref, k_ref, v_ref, cc_ref, ss_ref, dec_ref, qd_ref, kd_ref,
         out_ref, qs_ref, ks_ref, st_ref) = refs
    nch = TB_SEQ // CHUNK
    n = pl.program_id(0)
    sb, nb = _seq_pos((_num_blocks(groups) - 1 - n) if rev else n, groups)

    @pl.when(sb == (nb - 1 if rev else 0))
    def _():
        st_ref[...] = jnp.zeros_like(st_ref)

    if not rev:
        for h in range(HEADS):
            hs = slice(h * HEAD_DIM, (h + 1) * HEAD_DIM)
            cc = cc_ref[...]
            ss = ss_ref[...]
            q = q_ref[:, hs]
            k = k_ref[:, hs]
            qs_ref[:, hs] = (q * cc + pltpu.roll(q, HEAD_DIM // 2, 1) * ss) * (HEAD_DIM ** -0.5)
            ks_ref[:, hs] = k * cc + pltpu.roll(k, HEAD_DIM // 2, 1) * ss

    def body(i, carry):
        c = (nch - 1 - i) if rev else i
        sl = pl.ds(pl.multiple_of(c * CHUNK, CHUNK), CHUNK)
        for h in range(HEADS):
            hs = slice(h * HEAD_DIM, (h + 1) * HEAD_DIM)
            q = qs_ref[sl, hs]
            kt = ks_ref[sl, hs].T
            vb = v_ref[sl, hs].astype(BF16)
            s = _dot(q.astype(BF16), kt.astype(BF16)) * dec_ref[h]
            out = _dot(s.astype(BF16), vb) + _dot((q * qd_ref[h]).astype(BF16), st_ref[h].astype(BF16))
            st_ref[h] = kd_ref[h, 1:2, :] * st_ref[h] + _dot((kt * kd_ref[h, 0:1, :]).astype(BF16), vb)
            if rev:
                y = _head_norm(out + hf_ref[sl, hs], ng_ref[:, hs]) * _silu(z_ref[sl, hs])
                out_ref[sl, hs] = y.astype(out_ref.dtype)
            else:
                out_ref[sl, hs] = out
        return carry

    lax.fori_loop(0, nch, body, 0, unroll=not rev)


def _retention_dir(rev, u_rt, rot_c, rot_s, dec, qdec, kdec, norm_g, fwd, groups):
    tb = TB_SEQ
    nblk = _num_blocks(groups)
    d = 1 if rev else 0

    def rb(n):
        return (nblk - 1 - n) if rev else n

    def pos(n):
        return _seq_pos(rb(n), groups)[0]

    def col(g):
        return lambda n: (rb(n), g)

    blk = (tb, WIDTH)
    tbl = lambda n: (d, 0, 0, 0)
    tab_specs = [
        pl.BlockSpec((None, HEADS, CHUNK, CHUNK), tbl),
        pl.BlockSpec((None, HEADS, CHUNK, HEAD_DIM), tbl),
        pl.BlockSpec((None, HEADS, SUBLANES, CHUNK), tbl),
    ]
    full = jax.ShapeDtypeStruct((u_rt.shape[0], WIDTH), F32)
    if rev:
        qs, ks, h_fwd = fwd
        in_specs = [pl.BlockSpec(blk, col(0)), pl.BlockSpec(blk, col(0)), pl.BlockSpec(blk, col(2)),
                    *tab_specs,
                    pl.BlockSpec(blk, col(3)), pl.BlockSpec(blk, col(0)),
                    pl.BlockSpec((1, WIDTH), lambda n: (0, 0))]
        args = [qs, ks, u_rt, dec, qdec, kdec, u_rt, h_fwd, norm_g]
        out_specs = pl.BlockSpec(blk, col(0))
        out_shape = jax.ShapeDtypeStruct(full.shape, BF16)
    else:
        in_specs = [pl.BlockSpec(blk, col(0)), pl.BlockSpec(blk, col(1)), pl.BlockSpec(blk, col(2)),
                    pl.BlockSpec((tb, HEAD_DIM), lambda n: (pos(n), 0)),
                    pl.BlockSpec((tb, HEAD_DIM), lambda n: (pos(n), 0)),
                    *tab_specs]
        args = [u_rt, u_rt, u_rt, rot_c, rot_s, dec, qdec, kdec]
        out_specs, out_shape = [pl.BlockSpec(blk, col(0))] * 3, [full] * 3
    return pl.pallas_call(
        functools.partial(_rt_kernel, rev, groups),
        grid=(nblk,),
        in_specs=in_specs,
        out_specs=out_specs,
        out_shape=out_shape,
        scratch_shapes=[pltpu.VMEM((HEADS, HEAD_DIM, HEAD_DIM), F32)],
        name="retention_bwd" if rev else "retention_fwd",
    )(*args)


def _retention(u_rt, rot_c, rot_s, tabs, norm_g, groups):
    dec, qdec, kdec = tabs
    h_fwd, qs, ks = _retention_dir(False, u_rt, rot_c, rot_s, dec, qdec, kdec, norm_g, None, groups)
    return _retention_dir(True, u_rt, rot_c, rot_s, dec, qdec, kdec, norm_g, (qs, ks, h_fwd), groups)


def _rotary_tables(seq):
    inv = ROPE_BASE ** (-jnp.arange(0, HEAD_DIM, 2, dtype=F32) / HEAD_DIM)
    ang = jnp.arange(seq, dtype=F32)[:, None] * inv[None, :]
    cos, sin = jnp.cos(ang), jnp.sin(ang)
    return jnp.concatenate([cos, cos], axis=1), jnp.concatenate([-sin, sin], axis=1)


def _retention_tables():
    pos = np.arange(CHUNK, dtype=np.float64)
    rel = pos[:, None] - pos[None, :]
    dec = np.zeros((2, HEADS, CHUNK, CHUNK))
    qdec = np.zeros((2, HEADS, CHUNK, HEAD_DIM))
    kdec = np.zeros((2, HEADS, SUBLANES, CHUNK))
    for h in range(HEADS):
        lf, lb = RT_LOG_GAMMA_FWD[h], RT_LOG_GAMMA_BWD[h]
        dec[0, h] = np.where(rel >= 0, np.exp(np.maximum(rel, 0.0) * lf), 0.0)
        dec[1, h] = np.where(rel <= 0, np.exp(np.maximum(-rel, 0.0) * lb), 0.0)
        qdec[0, h] = np.exp((pos + 1.0) * lf)[:, None]
        qdec[1, h] = np.exp((CHUNK - pos) * lb)[:, None]
        kdec[0, h, 0] = np.exp((CHUNK - 1.0 - pos) * lf)
        kdec[1, h, 0] = np.exp(pos * lb)
        kdec[0, h, 1] = np.exp(CHUNK * lf)
        kdec[1, h, 1] = np.exp(CHUNK * lb)
    return jnp.asarray(dec, F32), jnp.asarray(qdec, F32), jnp.asarray(kdec, F32)


def _filt_kernel(ft_ref, fr_ref, f0_ref, w1_ref, b1_ref, w2_ref, b2_ref, fq_ref,
                 w3f_ref, w3b_ref, dlf_ref, dlb_ref, gb_ref, gf_ref, s_ref, hf3_ref, hr3_ref):
    lt = pl.program_id(1)
    fq = fq_ref[...]

    def hidden(ft):
        h1 = jnp.sin(fq * (_dot_f32(w1_ref[...], ft) + b1_ref[...]))
        return jnp.sin(fq * (_dot_f32(w2_ref[...], h1) + b2_ref[...]))

    @pl.when((pl.program_id(2) == 0) & (pl.program_id(3) == 0))
    def _():
        hf3_ref[...] = _split3_rows(hidden(ft_ref[...]))
        hr3_ref[...] = _split3_rows(hidden(fr_ref[...]))

    def lhs3(w):
        hi, lo = _split_hi_lo(w)
        return jnp.concatenate([hi, hi, lo], axis=1)

    ft = ft_ref[...]
    fr = fr_ref[...]
    dl_f = jnp.abs(dlf_ref[...])
    dl_b = jnp.abs(dlb_ref[...])
    fwd = _dot(lhs3(w3f_ref[...]), hf3_ref[...]) * jnp.exp(-ft[0:1, :] * dl_f)
    bwd = _dot(lhs3(w3b_ref[...]), hr3_ref[...]) * jnp.exp(-fr[0:1, :] * dl_b)
    f0 = f0_ref[...]
    bwd0 = (_dot_f32(w3b_ref[...], hidden(f0)) * jnp.exp(-f0[0:1, :] * dl_b))[:, 0:1]
    lag = lax.broadcasted_iota(jnp.int32, fwd.shape, 1) + lt * LT_FILT
    fwd = jnp.where(lag == 0, fwd + bwd0, fwd)
    bwd = jnp.where(lag == 0, 0.0, bwd)
    gb_ref[...] = bwd
    gf_ref[...] = fwd
    part = jnp.sum(jnp.abs(fwd) + jnp.abs(bwd), axis=1, keepdims=True)
    lane = lax.broadcasted_iota(jnp.int32, (LANES, LANES), 1)
    s_ref[...] = jnp.where(lane == 0, part, 0.0)


def _filters(seq, w1t, b1, w2t, b2, freq, w3t, deltas):
    t = jnp.linspace(0.0, 1.0, seq, dtype=F32)[None, :]
    w = (2.0 * math.pi / seq) * jnp.arange(seq, dtype=F32)[None, :]
    bands = jnp.linspace(1e-4, HY_BANDS - 1, HY_BANDS, dtype=F32)[:, None]
    feats = jnp.concatenate([t, jnp.cos(bands * w), -jnp.sin(bands * w),
                             jnp.zeros((FEAT_ROWS - 1 - 2 * HY_BANDS, seq), F32)], axis=0)
    feats_rev = jnp.concatenate([feats[:, :1], feats[:, :0:-1]], axis=1)
    feats0 = jnp.broadcast_to(feats[:, :1], (FEAT_ROWS, LANES))
    nlt = seq // LT_FILT
    ct = WIDTH // LANES
    lyr = lambda l, i, o, c: (l, 0, 0)
    return pl.pallas_call(
        _filt_kernel,
        grid=(DEPTH, nlt, HY_ORDER, ct),
        in_specs=[
            pl.BlockSpec((FEAT_ROWS, LT_FILT), lambda l, i, o, c: (0, i)),
            pl.BlockSpec((FEAT_ROWS, LT_FILT), lambda l, i, o, c: (0, i)),
            pl.BlockSpec((FEAT_ROWS, LANES), lambda l, i, o, c: (0, 0)),
            pl.BlockSpec((None, HY_HIDDEN, FEAT_ROWS), lyr),
            pl.BlockSpec((None, HY_HIDDEN, 1), lyr),
            pl.BlockSpec((None, HY_HIDDEN, HY_HIDDEN), lyr),
            pl.BlockSpec((None, HY_HIDDEN, 1), lyr),
            pl.BlockSpec((None, HY_HIDDEN, 1), lyr),
            pl.BlockSpec((None, LANES, HY_HIDDEN), lambda l, i, o, c: (l, o * 2 * ct + c, 0)),
            pl.BlockSpec((None, LANES, HY_HIDDEN), lambda l, i, o, c: (l, o * 2 * ct + ct + c, 0)),
            pl.BlockSpec((None, LANES, 1), lambda l, i, o, c: (l, o * 2 * ct + c, 0)),
            pl.BlockSpec((None, LANES, 1), lambda l, i, o, c: (l, o * 2 * ct + ct + c, 0)),
        ],
        out_specs=[
            pl.BlockSpec((None, None, LANES, LT_FILT), lambda l, i, o, c: (l, o, c, i)),
            pl.BlockSpec((None, None, LANES, LT_FILT), lambda l, i, o, c: (l, o, c, i)),
            pl.BlockSpec((None, None, LANES, LANES), lambda l, i, o, c: (l, o, c, i)),
        ],
        out_shape=[
            jax.ShapeDtypeStruct((DEPTH, HY_ORDER, WIDTH, seq), F32),
            jax.ShapeDtypeStruct((DEPTH, HY_ORDER, WIDTH, seq), F32),
            jax.ShapeDtypeStruct((DEPTH, HY_ORDER, WIDTH, nlt * LANES), F32),
        ],
        scratch_shapes=[pltpu.VMEM((3 * HY_HIDDEN, LT_FILT), BF16),
                        pltpu.VMEM((3 * HY_HIDDEN, LT_FILT), BF16)],
        name="hyena_filters",
    )(feats, feats_rev, feats0, w1t, b1, w2t, b2, freq, w3t, w3t, deltas, deltas)


def _split_hi_lo(x):
    hi = x.astype(BF16)
    lo = (x - hi.astype(F32)).astype(BF16)
    return hi, lo


def _split3_rows(x):
    hi, lo = _split_hi_lo(x)
    return jnp.concatenate([hi, lo, hi], axis=0)


def _split3_cols(x):
    hi, lo = _split_hi_lo(x)
    return jnp.concatenate([hi, lo, hi], axis=1)


def _np_split(m):
    hi = m.astype(ml_dtypes.bfloat16)
    lo = (m - hi.astype(np.float64)).astype(ml_dtypes.bfloat16)
    return hi, lo


def _dft_tables(seq):
    nj = seq // LANES
    n1 = 2 * nj
    n = n1 * LANES
    khp = n1 // 2 + KH_PAD
    k1 = np.arange(khp, dtype=np.float64)[:, None]
    valid = (k1 <= n1 // 2).astype(np.float64)
    ang = 2.0 * np.pi * k1 * np.arange(n1, dtype=np.float64)[None, :] / n1
    fa_full = np.concatenate([valid * np.cos(ang), -valid * np.sin(ang)], axis=0)

    def lhs3(m):
        hi, lo = _np_split(m)
        return jnp.asarray(np.concatenate([hi, hi, lo], axis=1))

    def rhs3(m):
        hi, lo = _np_split(m)
        return jnp.asarray(np.concatenate([hi, hi, lo], axis=0))

    n2 = np.arange(LANES, dtype=np.float64)
    ang2 = 2.0 * np.pi * n2[:, None] * n2[None, :] / LANES
    c2, s2 = np.cos(ang2), np.sin(ang2)
    fb = np.block([[c2, -s2], [s2, c2]])
    fbi = np.block([[c2, s2], [-s2, c2]])
    w = np.where((k1 == 0) | (k1 == n1 // 2), 1.0, 2.0) * valid
    th = 2.0 * np.pi * np.arange(nj, dtype=np.float64)[:, None] * k1.T / n1
    fai = np.concatenate([(w.T / n) * np.cos(th), -(w.T / n) * np.sin(th)], axis=1)
    phi = 2.0 * np.pi * k1 * n2[None, :] / n
    return dict(
        nj=nj, n1=n1, khp=khp,
        fa=lhs3(fa_full[:, :nj]), fa_full=lhs3(fa_full), fb=rhs3(fb), fbi=rhs3(fbi), fai=lhs3(fai),
        twc=jnp.asarray(valid * np.cos(phi), F32), tws=jnp.asarray(valid * np.sin(phi), F32))


def _dft_rows(fa_ref, twc, tws, khp, cols):
    rhs = cols[0] if len(cols) == 1 else jnp.concatenate(cols, axis=1)
    y = _dot(fa_ref[...], _split3_rows(rhs))
    out = []
    for i in range(len(cols)):
        yre = y[0:khp, i * LANES:(i + 1) * LANES]
        yim = y[khp:2 * khp, i * LANES:(i + 1) * LANES]
        out.append((yre * twc + yim * tws, yim * twc - yre * tws))
    return out


def _spec_kernel(khp, g_ref, s_ref, fa_ref, fb_ref, twc_ref, tws_ref, h_ref, ybuf):
    twc = twc_ref[...]
    tws = tws_ref[...]
    for c in range(0, CB_SPEC, 2):
        res = _dft_rows(fa_ref, twc, tws, khp, [g_ref[c], g_ref[c + 1]])
        for i, (yre, yim) in enumerate(res):
            r0 = (c + i) * khp
            ybuf[r0:r0 + khp, 0:LANES] = yre
            ybuf[r0:r0 + khp, LANES:2 * LANES] = yim
    x = _dot(_split3_cols(ybuf[...]), fb_ref[...])
    for c in range(CB_SPEC):
        norm = jnp.sum(s_ref[c:c + 1, :], axis=1, keepdims=True)
        h_ref[c] = x[c * khp:(c + 1) * khp, :] / norm


def _spectrum(g, s, tab):
    nch = g.shape[0]
    n1, khp = tab["n1"], tab["khp"]
    const = lambda a: pl.BlockSpec(a.shape, lambda i: (0, 0))
    return pl.pallas_call(
        functools.partial(_spec_kernel, khp),
        grid=(nch // CB_SPEC,),
        in_specs=[
            pl.BlockSpec((CB_SPEC, n1, LANES), lambda i: (i, 0, 0)),
            pl.BlockSpec((CB_SPEC, s.shape[-1]), lambda i: (i, 0)),
            const(tab["fa_full"]), const(tab["fb"]), const(tab["twc"]), const(tab["tws"]),
        ],
        out_specs=pl.BlockSpec((CB_SPEC, khp, 2 * LANES), lambda i: (i, 0, 0)),
        out_shape=jax.ShapeDtypeStruct((nch, khp, 2 * LANES), F32),
        scratch_shapes=[pltpu.VMEM((CB_SPEC * khp, 2 * LANES), F32)],
        name="hyena_spectrum",
    )(g, s, tab["fa_full"], tab["fb"], tab["twc"], tab["tws"])


def _hyfft_kernel(bsz, nj, khp, prm_ref, v_ref, x1_ref, x2_ref, z_ref, h_ref,
                  fa_ref, fb_ref, fbi_ref, fai_ref, twc_ref, tws_ref, out_ref,
                  xs_ref, ybuf, pbuf):
    ct = pl.program_id(0)
    rows = bsz * nj
    pairs = [(b, b + 1) for b in range(0, bsz, 2)] if bsz > 1 else [(0,)]
    twc = twc_ref[...]
    tws = tws_ref[...]

    lane = lax.broadcasted_iota(jnp.int32, (rows, LANES), 1)
    blk = lax.broadcasted_iota(jnp.int32, (rows, LANES), 0) & (nj - 1)
    first = (lane == 0) & (blk == 0)
    last = (lane == LANES - 1) & (blk == nj - 1)

    def conv3(x, base):
        r = pltpu.roll(x, 1, 1)
        xm1 = jnp.where(lane == 0, pltpu.roll(r, 1, 0), r)
        xm1 = jnp.where(first, 0.0, xm1)
        l = pltpu.roll(x, LANES - 1, 1)
        xp1 = jnp.where(lane == LANES - 1, pltpu.roll(l, rows - 1, 0), l)
        xp1 = jnp.where(last, 0.0, xp1)
        return prm_ref[base] * xm1 + prm_ref[base + 1] * x + prm_ref[base + 2] * xp1 + prm_ref[base + 3]

    def row0(c, b):
        return (c * bsz + b) * khp

    def forward(c, x):
        for pr in pairs:
            res = _dft_rows(fa_ref, twc, tws, khp, [x[b * nj:(b + 1) * nj, :] for b in pr])
            for b, (yre, yim) in zip(pr, res):
                r0 = row0(c, b)
                ybuf[r0:r0 + khp, 0:LANES] = yre
                ybuf[r0:r0 + khp, LANES:2 * LANES] = yim

    def spectral(order):
        for c0 in range(0, CB_HY, CG_HY):
            g0, g1 = row0(c0, 0), row0(c0 + CG_HY, 0)
            x = _dot(_split3_cols(ybuf[g0:g1, :]), fb_ref[...])
            for c in range(c0, c0 + CG_HY):
                hre = h_ref[order, c, :, 0:LANES]
                him = h_ref[order, c, :, LANES:2 * LANES]
                for b in range(bsz):
                    r0 = row0(c, b)
                    xre = x[r0 - g0:r0 - g0 + khp, 0:LANES]
                    xim = x[r0 - g0:r0 - g0 + khp, LANES:2 * LANES]
                    pbuf[r0:r0 + khp, 0:LANES] = xre * hre - xim * him
                    pbuf[r0:r0 + khp, LANES:2 * LANES] = xre * him + xim * hre
            r = _dot(_split3_cols(pbuf[g0:g1, :]), fbi_ref[...])
            for c in range(c0, c0 + CG_HY):
                for b in range(bsz):
                    r0 = row0(c, b)
                    rre = r[r0 - g0:r0 - g0 + khp, 0:LANES]
                    rim = r[r0 - g0:r0 - g0 + khp, LANES:2 * LANES]
                    ybuf[r0:r0 + khp, 0:LANES] = rre * twc - rim * tws
                    ybuf[r0:r0 + khp, LANES:2 * LANES] = rre * tws + rim * twc

    def inverse(c):
        outs = [None] * bsz
        for pr in pairs:
            rre = [ybuf[row0(c, b):row0(c, b) + khp, 0:LANES] for b in pr]
            rim = [ybuf[row0(c, b):row0(c, b) + khp, LANES:2 * LANES] for b in pr]
            if len(pr) > 1:
                rre, rim = [jnp.concatenate(rre, axis=1)], [jnp.concatenate(rim, axis=1)]
            y = _dot(fai_ref[...], _split3_rows(jnp.concatenate([rre[0], rim[0]], axis=0)))
            for i, b in enumerate(pr):
                outs[b] = y[:, i * LANES:(i + 1) * LANES]
        return outs[0] if bsz == 1 else jnp.concatenate(outs, axis=0)

    for c in range(CB_HY):
        base = (ct * CB_HY + c) * 16
        v = conv3(v_ref[c], base)
        xs_ref[0, c] = v
        xs_ref[1, c] = conv3(x1_ref[c], base + 4)
        xs_ref[2, c] = conv3(x2_ref[c], base + 8)
        forward(c, v)
    spectral(0)
    for c in range(CB_HY):
        base = (ct * CB_HY + c) * 16
        v = xs_ref[0, c]
        z1 = xs_ref[1, c] * (inverse(c) + prm_ref[base + 12] * v)
        xs_ref[0, c] = z1
        forward(c, z1)
    spectral(1)
    for c in range(CB_HY):
        base = (ct * CB_HY + c) * 16
        z1 = xs_ref[0, c]
        y2 = inverse(c) + prm_ref[base + 13] * z1
        out_ref[:, c, :] = xs_ref[2, c] * y2 * _silu(z_ref[c])


def _hyena_fft(prm, u_t, h, layer, tab, row0, bsz, seq):
    nj, khp = tab["nj"], tab["khp"]
    nct = WIDTH // CB_HY
    rows = bsz * nj
    blk = (CB_HY, rows, LANES)
    rblk = row0 // (rows * LANES)
    assert rblk * rows * LANES == row0
    const = lambda a: pl.BlockSpec(a.shape, lambda i: (0, 0))
    return pl.pallas_call(
        functools.partial(_hyfft_kernel, bsz, nj, khp),
        grid=(nct,),
        in_specs=[
            pl.BlockSpec(memory_space=pltpu.SMEM),
            pl.BlockSpec(blk, lambda i: (i, rblk, 0)),
            pl.BlockSpec(blk, lambda i: (nct + i, rblk, 0)),
            pl.BlockSpec(blk, lambda i: (2 * nct + i, rblk, 0)),
            pl.BlockSpec(blk, lambda i: (3 * nct + i, rblk, 0)),
            pl.BlockSpec((None, HY_ORDER, CB_HY, khp, 2 * LANES), lambda i: (layer, 0, i, 0, 0)),
            const(tab["fa"]), const(tab["fb"]), const(tab["fbi"]), const(tab["fai"]),
            const(tab["twc"]), const(tab["tws"]),
        ],
        out_specs=pl.BlockSpec((rows, CB_HY, LANES), lambda i: (0, i, 0)),
        out_shape=jax.ShapeDtypeStruct((rows, WIDTH, LANES), F32),
        scratch_shapes=[
            pltpu.VMEM((3, CB_HY, bsz * nj, LANES), F32),
            pltpu.VMEM((CB_HY * bsz * khp, 2 * LANES), F32),
            pltpu.VMEM((CB_HY * bsz * khp, 2 * LANES), F32),
        ],
        name="hyena_fft",
    )(prm, u_t, u_t, u_t, u_t, h, tab["fa"], tab["fb"], tab["fbi"], tab["fai"],
      tab["twc"], tab["tws"])


def _out_kernel(final, starts, nx, *refs):
    ng = len(starts)
    x_refs = refs[0:nx]
    yml_ref, yrt_ref = refs[nx:nx + 2]
    yhy_refs = refs[nx + 2:nx + 2 + ng]
    w_ref, g_ref = refs[nx + 2 + ng:nx + 4 + ng]
    out_refs = refs[nx + 4 + ng:-1]
    lhs_ref = refs[-1]
    i = pl.program_id(0)
    lhs_ref[:, 0:WIDTH] = yml_ref[...]
    lhs_ref[:, WIDTH:2 * WIDTH] = yrt_ref[...]
    jb = TM_OUT // LANES

    def fill_hyena(ref):
        for j in range(jb):
            lhs_ref[j * LANES:(j + 1) * LANES, 2 * WIDTH:] = ref[j].T.astype(BF16)

    for gi in range(ng):
        inside = i >= starts[gi]
        if gi + 1 < ng:
            inside = inside & (i < starts[gi + 1])
        pl.when(inside)(functools.partial(fill_hyena, yhy_refs[gi]))
    x = _x_tile(x_refs, starts if nx > 1 else [0]) + _dot(lhs_ref[...], w_ref[...])
    if not final:
        out_refs[0][...] = x
        return
    ms = jnp.mean(x * x, axis=-1, keepdims=True)
    x = x * lax.rsqrt(ms + RMS_EPS) * g_ref[...]
    for gi in range(ng):
        inside = i >= starts[gi]
        if gi + 1 < ng:
            inside = inside & (i < starts[gi + 1])

        @pl.when(inside)
        def _(gi=gi):
            out_refs[gi][...] = x


def _out_proj(xs, y_ml, y_rt, y_hy, w, gain, final, groups):
    t = y_ml.shape[0]
    jb = TM_OUT // LANES
    starts = [r0 // TM_OUT for (r0, _, _) in groups]
    counts = [bsz * seq // TM_OUT for (_, bsz, seq) in groups]

    def tile(gi):
        return lambda i: jnp.clip(i - starts[gi], 0, counts[gi] - 1)

    hy_specs = [pl.BlockSpec((jb, WIDTH, LANES), lambda i, f=tile(gi): (f(i), 0, 0))
                for gi in range(len(groups))]
    if final:
        out_specs = [pl.BlockSpec((TM_OUT, D_MODEL), lambda i, f=tile(gi): (f(i), 0))
                     for gi in range(len(groups))]
        out_shape = [jax.ShapeDtypeStruct((bsz * seq, D_MODEL), F32) for (_, bsz, seq) in groups]
    else:
        out_specs = [pl.BlockSpec((TM_OUT, D_MODEL), lambda i: (i, 0))]
        out_shape = [jax.ShapeDtypeStruct((t, D_MODEL), F32)]
    return pl.pallas_call(
        functools.partial(_out_kernel, final, starts, len(xs)),
        grid=(t // TM_OUT,),
        in_specs=[
            *_x_specs(xs, TM_OUT, 1),
            pl.BlockSpec((TM_OUT, WIDTH), lambda i: (i, 0)),
            pl.BlockSpec((TM_OUT, WIDTH), lambda i: (i, 0)),
            *hy_specs,
            pl.BlockSpec((3 * WIDTH, D_MODEL), lambda i: (0, 0)),
            pl.BlockSpec((1, D_MODEL), lambda i: (0, 0)),
        ],
        out_specs=out_specs,
        out_shape=out_shape,
        scratch_shapes=[pltpu.VMEM((TM_OUT, 3 * WIDTH), BF16)],
        name="out_proj",
    )(*xs, y_ml, y_rt, *y_hy, w, gain)


def kernel(x_prompt, x_sample, norm_g, w_in, ml_conv_w, ml_conv_b, ml_gate_b, ml_norm_g, rt_norm_g, hy_conv_w, hy_conv_b, hy_w1, hy_b1, hy_w2, hy_b2, hy_w3, hy_freq, hy_deltas, hy_skip, w_out, final_g):
    groups = []
    row0 = 0
    for xg in (x_prompt, x_sample):
        bsz, seq, _ = xg.shape
        groups.append((row0, bsz, seq))
        row0 += bsz * seq
    xs = [x_prompt.reshape(-1, D_MODEL), x_sample.reshape(-1, D_MODEL)]

    c_gate = ML_COLS
    c_rt = c_gate + 4 * HEADS
    c_hy = c_rt + RT_COLS
    gate_pad = jnp.zeros((DEPTH, D_MODEL, LANES - 4 * HEADS), F32)
    w_tok = jnp.concatenate([w_in[:, :, :c_gate], w_in[:, :, c_gate:c_rt], gate_pad,
                             w_in[:, :, c_rt:c_hy]], axis=2).astype(BF16)
    w_gt = jnp.swapaxes(w_in[:, :, c_gate:c_rt], 1, 2).astype(BF16)
    w_hyt = jnp.swapaxes(w_in[:, :, c_hy:], 1, 2).astype(BF16)
    w_out_b = w_out.astype(BF16)
    gate_b = ml_gate_b.reshape(DEPTH, 1, 4 * HEADS)
    gate_b_row = jnp.concatenate([gate_b, jnp.zeros((DEPTH, 1, LANES - 4 * HEADS), F32)], axis=2)
    gate_b_col = ml_gate_b.reshape(DEPTH, 4 * HEADS, 1)

    cw = hy_conv_w.reshape(DEPTH, 3, 3, WIDTH)
    cbias = hy_conv_b.reshape(DEPTH, 1, 3, WIDTH)
    taps = jnp.concatenate([cw, cbias], axis=1)
    taps = jnp.transpose(taps, (0, 3, 2, 1)).reshape(DEPTH, WIDTH, 12)
    hy_prm = jnp.concatenate([taps, jnp.transpose(hy_skip, (0, 2, 1)),
                              jnp.zeros((DEPTH, WIDTH, 2), F32)], axis=2).reshape(DEPTH, WIDTH * 16)

    w1t = jnp.concatenate([jnp.swapaxes(hy_w1, 1, 2),
                           jnp.zeros((DEPTH, HY_HIDDEN, FEAT_ROWS - hy_w1.shape[1]), F32)], axis=2)
    w2t = jnp.swapaxes(hy_w2, 1, 2)
    w3t = jnp.swapaxes(hy_w3, 1, 2)
    b1 = hy_b1[:, :, None]
    b2 = hy_b2[:, :, None]
    freq = hy_freq[:, :, None]
    deltas = hy_deltas.reshape(DEPTH, HY_ORDER * 2 * WIDTH, 1)

    tables, spectra = [], []
    for (_, _, seq) in groups:
        tab = _dft_tables(seq)
        gb, gf, s = _filters(seq, w1t, b1, w2t, b2, freq, w3t, deltas)
        g = jnp.concatenate([gf, gb], axis=-1).reshape(DEPTH * HY_ORDER * WIDTH, tab["n1"], LANES)
        h = _spectrum(g, s.reshape(DEPTH * HY_ORDER * WIDTH, -1), tab)
        tables.append(tab)
        spectra.append(h.reshape(DEPTH, HY_ORDER, WIDTH, tab["khp"], 2 * LANES))
    rot_c, rot_s = _rotary_tables(max(seq for (_, _, seq) in groups))
    rt_tabs = _retention_tables()

    for layer in range(DEPTH):
        gain = norm_g[layer][None, :]
        u_ml, u_g, u_rt, u_gt = _in_proj(xs, gain, w_tok[layer], w_gt[layer])
        u_hyt = _in_proj_hy(xs, gain, w_hyt[layer])
        pb, cb, gr, gs = _gates(u_g, u_gt, gate_b_row[layer], gate_b_col[layer])
        y_ml = _mlstm(u_ml, pb, cb, gr, gs, ml_conv_w[layer], ml_conv_b[layer][None, :],
                      ml_norm_g[layer][None, :], groups)
        y_rt = _retention(u_rt, rot_c, rot_s, rt_tabs, rt_norm_g[layer][None, :], groups)
        y_hy = [_hyena_fft(hy_prm[layer], u_hyt, spectra[gi], layer, tables[gi], r0, bsz, seq)
                for gi, (r0, bsz, seq) in enumerate(groups)]
        outs = _out_proj(xs, y_ml, y_rt, y_hy, w_out_b[layer], final_g[None, :],
                         layer == DEPTH - 1, groups)
        xs = [outs[0]]

    return outs[0].reshape(x_prompt.shape), outs[1].reshape(x_sample.shape)
```

```python
import functools
import math

import ml_dtypes
import numpy as np
import jax
import jax.numpy as jnp
from jax import lax
from jax.experimental import pallas as pl
from jax.experimental.pallas import tpu as pltpu

F32 = jnp.float32
BF16 = jnp.bfloat16

D_MODEL = 1024
DEPTH = 4
CHUNK = 128
HEADS = 4
HEAD_DIM = 128
WIDTH = 512
HY_ORDER = 2
HY_BANDS = 16
HY_HIDDEN = 64
FEAT_ROWS = 40
ROPE_BASE = 10000.0
RMS_EPS = 1e-6
HEAD_NORM_EPS = 1e-5
M_INIT = -1e30
RT_LOG_GAMMA_FWD = tuple(math.log(1.0 - 2.0 ** (-5.0 - h)) for h in range(HEADS))
RT_LOG_GAMMA_BWD = tuple(math.log(1.0 - 2.0 ** (-5.5 - h)) for h in range(HEADS))

LANES = 128
SUBLANES = 8
TM_IN = 256
TM_HY = 1024
TR_HY = 1024
TM_OUT = 1024
TB_SEQ = 1024
TG = 1024
CB_HY = 8
CG_HY = 2
CB_SPEC = 8
KH_PAD = 8
LT_FILT = 2048
ML_COLS = 5 * WIDTH
RT_COLS = 4 * WIDTH
HY_ROWS = 4 * WIDTH
HIGHEST = lax.Precision.HIGHEST


def _dot(a, b):
    return jnp.dot(a, b, preferred_element_type=F32)


def _dot_nt(a, b):
    return lax.dot_general(a, b, (((1,), (1,)), ((), ())), preferred_element_type=F32)


def _dot_tn(a, b):
    return lax.dot_general(a, b, (((0,), (0,)), ((), ())), preferred_element_type=F32)


def _dot_f32(a, b):
    return jnp.dot(a, b, precision=HIGHEST, preferred_element_type=F32)


def _silu(y):
    return y * jax.nn.sigmoid(y)


def _log_sigmoid(x):
    return -(jnp.maximum(-x, 0.0) + jnp.log1p(jnp.exp(-jnp.abs(x))))


def _head_norm(h, gain):
    mu = jnp.mean(h, axis=-1, keepdims=True)
    hc = h - mu
    var = jnp.mean(hc * hc, axis=-1, keepdims=True)
    return hc * lax.rsqrt(var + HEAD_NORM_EPS) * gain


def _tile_starts(xs, tm):
    starts, s = [], 0
    for a in xs:
        starts.append(s)
        s += a.shape[0] // tm
    return starts, s


def _x_specs(xs, tm, ngrid):
    starts, _ = _tile_starts(xs, tm)
    specs = []
    for a, s in zip(xs, starts):
        cnt = a.shape[0] // tm
        if ngrid == 1:
            specs.append(pl.BlockSpec((tm, D_MODEL), lambda i, s=s, cnt=cnt: (jnp.clip(i - s, 0, cnt - 1), 0)))
        else:
            specs.append(pl.BlockSpec((tm, D_MODEL), lambda i, r, s=s, cnt=cnt: (jnp.clip(i - s, 0, cnt - 1), 0)))
    return specs


def _x_tile(x_refs, starts):
    x = x_refs[0][...]
    for ref, s in zip(x_refs[1:], starts[1:]):
        x = jnp.where(pl.program_id(0) >= s, ref[...], x)
    return x


def _rms_bf16(x, g_ref):
    ms = jnp.mean(x * x, axis=-1, keepdims=True)
    return (x * lax.rsqrt(ms + RMS_EPS) * g_ref[...]).astype(BF16)


def _in_kernel(starts, *refs):
    nx = len(starts)
    g_ref, wtok_ref, wgt_ref, oml_ref, og_ref, ort_ref, ogt_ref = refs[nx:]
    h = _rms_bf16(_x_tile(refs[:nx], starts), g_ref)
    y = _dot(h, wtok_ref[...])
    oml_ref[...] = y[:, 0:ML_COLS]
    og_ref[...] = y[:, ML_COLS:ML_COLS + LANES]
    ort_ref[...] = y[:, ML_COLS + LANES:]
    ogt_ref[...] = _dot_nt(wgt_ref[...], h)


def _in_proj(xs, gain, w_tok, w_gt):
    starts, nt = _tile_starts(xs, TM_IN)
    t = nt * TM_IN
    ncol = w_tok.shape[1]
    return pl.pallas_call(
        functools.partial(_in_kernel, starts),
        grid=(nt,),
        in_specs=[
            *_x_specs(xs, TM_IN, 1),
            pl.BlockSpec((1, D_MODEL), lambda i: (0, 0)),
            pl.BlockSpec((D_MODEL, ncol), lambda i: (0, 0)),
            pl.BlockSpec((4 * HEADS, D_MODEL), lambda i: (0, 0)),
        ],
        out_specs=[
            pl.BlockSpec((TM_IN, ML_COLS), lambda i: (i, 0)),
            pl.BlockSpec((TM_IN, LANES), lambda i: (i, 0)),
            pl.BlockSpec((TM_IN, RT_COLS), lambda i: (i, 0)),
            pl.BlockSpec((4 * HEADS, TM_IN), lambda i: (0, i)),
        ],
        out_shape=[
            jax.ShapeDtypeStruct((t, ML_COLS), F32),
            jax.ShapeDtypeStruct((t, LANES), F32),
            jax.ShapeDtypeStruct((t, RT_COLS), F32),
            jax.ShapeDtypeStruct((4 * HEADS, t), F32),
        ],
        name="in_proj",
    )(*xs, gain, w_tok, w_gt)


def _in_hy_kernel(starts, *refs):
    nx = len(starts)
    g_ref, w_ref, o_ref = refs[nx:]
    y = _dot_nt(w_ref[...], _rms_bf16(_x_tile(refs[:nx], starts), g_ref))
    o_ref[...] = y.reshape(o_ref.shape)


def _in_proj_hy(xs, gain, w_hyt):
    starts, nt = _tile_starts(xs, TM_HY)
    t = nt * TM_HY
    nr = HY_ROWS // TR_HY
    return pl.pallas_call(
        functools.partial(_in_hy_kernel, starts),
        grid=(nt, nr),
        in_specs=[
            *_x_specs(xs, TM_HY, 2),
            pl.BlockSpec((1, D_MODEL), lambda i, r: (0, 0)),
            pl.BlockSpec((TR_HY, D_MODEL), lambda i, r: (r, 0)),
        ],
        out_specs=pl.BlockSpec((TR_HY, TM_HY // LANES, LANES), lambda i, r: (r, i, 0)),
        out_shape=jax.ShapeDtypeStruct((HY_ROWS, t // LANES, LANES), F32),
        name="in_proj_hyena",
    )(*xs, gain, w_hyt)


def _gates_kernel(g_ref, gt_ref, b_ref, bt_ref, pb_ref, cb_ref, gr_ref, gs_ref):
    row = lax.broadcasted_iota(jnp.int32, (CHUNK, CHUNK), 0)
    col = lax.broadcasted_iota(jnp.int32, (CHUNK, CHUNK), 1)
    lower = (row >= col).astype(F32)
    upper = (row <= col).astype(F32)
    gr_ref[...] = jnp.zeros_like(gr_ref)
    gs_ref[...] = jnp.zeros_like(gs_ref)
    for c in range(TG // CHUNK):
        sl = slice(c * CHUNK, (c + 1) * CHUNK)
        lf = _log_sigmoid(g_ref[sl, :] + b_ref[...])
        cum_c = (_dot_f32(lower, lf), _dot_f32(upper, lf))
        pre_t = gt_ref[:, sl] + bt_ref[...]
        lf_t = _log_sigmoid(pre_t)
        cum_r = (_dot_f32(lf_t, upper), _dot_f32(lf_t, lower))
        for d in range(2):
            for h in range(HEADS):
                kf = (2 * d + 1) * HEADS + h
                ki = 2 * d * HEADS + h
                cum = cum_r[d][kf:kf + 1, :]
                li = pre_t[ki:ki + 1, :]
                r = li - cum
                f_tot = cum[:, 0:1] if d else cum[:, CHUNK - 1:CHUNK]
                g = f_tot - cum + li
                allowed = (row <= col) if d else (row >= col)
                pmax = jnp.max(jnp.where(allowed, r, -jnp.inf), axis=1, keepdims=True)
                pb_ref[d, h, sl, :] = jnp.broadcast_to(pmax, (CHUNK, LANES))
                cb_ref[d, h, sl, :] = jnp.broadcast_to(cum_c[d][:, kf:kf + 1], (CHUNK, LANES))
                gr_ref[d, h, 0:1, sl] = r
                gr_ref[d, h, 1:2, sl] = g
                gs_ref[d, h, c, 0:1, :] = jnp.broadcast_to(f_tot, (1, LANES))
                gs_ref[d, h, c, 1:2, :] = jnp.broadcast_to(jnp.max(g, axis=1, keepdims=True), (1, LANES))


def _gates(g, gt, bias_row, bias_col):
    t = g.shape[0]
    nc = TG // CHUNK
    return pl.pallas_call(
        _gates_kernel,
        grid=(t // TG,),
        in_specs=[
            pl.BlockSpec((TG, LANES), lambda i: (i, 0)),
            pl.BlockSpec((4 * HEADS, TG), lambda i: (0, i)),
            pl.BlockSpec((1, LANES), lambda i: (0, 0)),
            pl.BlockSpec((4 * HEADS, 1), lambda i: (0, 0)),
        ],
        out_specs=[
            pl.BlockSpec((2, HEADS, TG, LANES), lambda i: (0, 0, i, 0)),
            pl.BlockSpec((2, HEADS, TG, LANES), lambda i: (0, 0, i, 0)),
            pl.BlockSpec((2, HEADS, SUBLANES, TG), lambda i: (0, 0, 0, i)),
            pl.BlockSpec((2, HEADS, nc, SUBLANES, LANES), lambda i: (0, 0, i, 0, 0)),
        ],
        out_shape=[
            jax.ShapeDtypeStruct((2, HEADS, t, LANES), F32),
            jax.ShapeDtypeStruct((2, HEADS, t, LANES), F32),
            jax.ShapeDtypeStruct((2, HEADS, SUBLANES, t), F32),
            jax.ShapeDtypeStruct((2, HEADS, t // CHUNK, SUBLANES, LANES), F32),
        ],
        name="gates",
    )(g, gt, bias_row, bias_col)


def _seq_pos(g, groups):
    pos, nbs = None, None
    start = 0
    for (_, bsz, seq) in groups:
        nb = seq // TB_SEQ
        p = (g - start) % nb
        pos = p if pos is None else jnp.where(g >= start, p, pos)
        nbs = nb if nbs is None else jnp.where(g >= start, nb, nbs)
        start += bsz * nb
    return pos, nbs


def _num_blocks(groups):
    return sum(bsz * seq // TB_SEQ for (_, bsz, seq) in groups)


def _ml_kernel(rev, groups, *refs):
    if rev:
        (qs_ref, ks_ref, v_ref, pb_ref, cb_ref, gr_ref, gs_ref, o_ref, z_ref, hf_ref, ng_ref,
         out_ref, st_ref, m_ref) = refs
    else:
        (q_ref, k_ref, v_ref, qp_ref, qn_ref, kp_ref, kn_ref, pb_ref, cb_ref, gr_ref, gs_ref,
         cw_ref, cbias_ref, out_ref, qs_ref, ks_ref, st_ref, m_ref, stg_ref) = refs
    tb = TB_SEQ
    nch = tb // CHUNK
    n = pl.program_id(0)
    sb, nb = _seq_pos((_num_blocks(groups) - 1 - n) if rev else n, groups)

    @pl.when(sb == (nb - 1 if rev else 0))
    def _():
        st_ref[...] = jnp.zeros_like(st_ref)
        m_ref[...] = jnp.full_like(m_ref, M_INIT)

    def conv_silu(raw_ref, prev_ref, next_ref, c0):
        stg_ref[SUBLANES:SUBLANES + tb, :] = raw_ref[...]
        stg_ref[SUBLANES - 1:SUBLANES, :] = jnp.where(sb > 0, prev_ref[SUBLANES - 1:SUBLANES, :], 0.0)
        stg_ref[SUBLANES + tb:SUBLANES + tb + 1, :] = jnp.where(sb < nb - 1, next_ref[0:1, :], 0.0)
        w = cw_ref[:, c0:c0 + WIDTH]
        y = (stg_ref[SUBLANES - 1:SUBLANES - 1 + tb, :] * w[0:1, :]
             + stg_ref[SUBLANES:SUBLANES + tb, :] * w[1:2, :]
             + stg_ref[SUBLANES + 1:SUBLANES + 1 + tb, :] * w[2:3, :] + cbias_ref[:, c0:c0 + WIDTH])
        return _silu(y)

    if not rev:
        qs_ref[...] = conv_silu(q_ref, qp_ref, qn_ref, 0).astype(BF16)
        ks_ref[...] = (conv_silu(k_ref, kp_ref, kn_ref, WIDTH) * (HEAD_DIM ** -0.5)).astype(BF16)

    row = lax.broadcasted_iota(jnp.int32, (CHUNK, CHUNK), 0)
    col = lax.broadcasted_iota(jnp.int32, (CHUNK, CHUNK), 1)
    allowed = (row <= col) if rev else (row >= col)
    ones = jnp.ones((CHUNK, HEAD_DIM), BF16)

    def chunk(c, h):
        sl = pl.ds(pl.multiple_of(c * CHUNK, CHUNK), CHUNK)
        hs = slice(h * HEAD_DIM, (h + 1) * HEAD_DIM)
        q = qs_ref[sl, hs]
        kt = ks_ref[sl, hs].astype(F32).T
        v2 = jnp.concatenate([v_ref[sl, hs].astype(BF16), ones], axis=1)
        m_prev = m_ref[h, 0:1, :]
        mb = jnp.maximum(pb_ref[h, sl, :], m_prev)
        p = jnp.exp(jnp.where(allowed, gr_ref[h, 0:1, sl] - mb, -jnp.inf))
        w_inter = jnp.exp(m_prev - mb)
        floor = jnp.exp(-(cb_ref[h, sl, :] + mb))
        s = _dot(q, kt.astype(BF16)) * p
        inter = _dot(q, st_ref[h].astype(BF16))
        num2 = _dot(s.astype(BF16), v2) + jnp.concatenate([w_inter, w_inter], axis=1) * inter
        out = num2[:, 0:HEAD_DIM] / jnp.maximum(jnp.abs(num2[:, HEAD_DIM:]), floor)
        f_tot = gs_ref[h, c, 0:1, :]
        m_chunk = gs_ref[h, c, 1:2, :]
        m_new = jnp.maximum(f_tot + m_prev, m_chunk)
        kw = (kt * jnp.exp(gr_ref[h, 1:2, sl] - m_new)).astype(BF16)
        kv2 = _dot(kw, v2)
        a = jnp.exp(f_tot + m_prev - m_new)
        st_ref[h] = jnp.concatenate([a, a], axis=1) * st_ref[h] + kv2
        m_ref[h] = jnp.broadcast_to(m_new, (SUBLANES, LANES))
        return out, sl, hs

    def body(i, carry):
        c = (nch - 1 - i) if rev else i
        for h in range(HEADS):
            out, sl, hs = chunk(c, h)
            if rev:
                hh = jax.nn.sigmoid(o_ref[sl, hs]) * (out + hf_ref[sl, hs])
                y = _head_norm(hh, ng_ref[:, hs]) * _silu(z_ref[sl, hs])
                out_ref[sl, hs] = y.astype(out_ref.dtype)
            else:
                out_ref[sl, hs] = out
        return carry

    lax.fori_loop(0, nch, body, 0, unroll=True)


def _mlstm_dir(rev, u_ml, pb, cb, gr, gs, conv_w, conv_b, norm_g, fwd, groups):
    t = u_ml.shape[0]
    tb = TB_SEQ
    t8 = tb // SUBLANES
    nch = tb // CHUNK
    nblk = _num_blocks(groups)
    d = 1 if rev else 0

    def rb(n):
        return (nblk - 1 - n) if rev else n

    def col(g):
        return lambda n: (rb(n), g)

    def halo_prev(g):
        return lambda n: (jnp.maximum(rb(n) * t8 - 1, 0), g)

    def halo_next(g):
        return lambda n: (jnp.minimum((rb(n) + 1) * t8, t // SUBLANES - 1), g)

    blk = (tb, WIDTH)
    halo = (SUBLANES, WIDTH)
    gate_specs = [
        pl.BlockSpec((None, HEADS, tb, LANES), lambda n: (d, 0, rb(n), 0)),
        pl.BlockSpec((None, HEADS, tb, LANES), lambda n: (d, 0, rb(n), 0)),
        pl.BlockSpec((None, HEADS, SUBLANES, tb), lambda n: (d, 0, 0, rb(n))),
        pl.BlockSpec((None, HEADS, nch, SUBLANES, LANES), lambda n: (d, 0, rb(n), 0, 0)),
    ]
    scratch = [
        pltpu.VMEM((HEADS, HEAD_DIM, 2 * HEAD_DIM), F32),
        pltpu.VMEM((HEADS, SUBLANES, LANES), F32),
    ]
    full = jax.ShapeDtypeStruct((t, WIDTH), F32)
    if rev:
        qs, ks, h_fwd = fwd
        in_specs = [pl.BlockSpec(blk, col(0)), pl.BlockSpec(blk, col(0)), pl.BlockSpec(blk, col(2)),
                    *gate_specs,
                    pl.BlockSpec(blk, col(3)), pl.BlockSpec(blk, col(4)), pl.BlockSpec(blk, col(0)),
                    pl.BlockSpec((1, WIDTH), lambda n: (0, 0))]
        args = [qs, ks, u_ml, pb, cb, gr, gs, u_ml, u_ml, h_fwd, norm_g]
        out_specs = pl.BlockSpec(blk, col(0))
        out_shape = jax.ShapeDtypeStruct(full.shape, BF16)
    else:
        in_specs = [pl.BlockSpec(blk, col(0)), pl.BlockSpec(blk, col(1)), pl.BlockSpec(blk, col(2)),
                    pl.BlockSpec(halo, halo_prev(0)), pl.BlockSpec(halo, halo_next(0)),
                    pl.BlockSpec(halo, halo_prev(1)), pl.BlockSpec(halo, halo_next(1)),
                    *gate_specs,
                    pl.BlockSpec((3, 2 * WIDTH), lambda n: (0, 0)),
                    pl.BlockSpec((1, 2 * WIDTH), lambda n: (0, 0))]
        args = [u_ml, u_ml, u_ml, u_ml, u_ml, u_ml, u_ml, pb, cb, gr, gs, conv_w, conv_b]
        out_specs = [pl.BlockSpec(blk, col(0))] * 3
        out_shape = [full, jax.ShapeDtypeStruct((t, WIDTH), BF16), jax.ShapeDtypeStruct((t, WIDTH), BF16)]
        scratch.append(pltpu.VMEM((tb + 2 * SUBLANES, WIDTH), F32))
    return pl.pallas_call(
        functools.partial(_ml_kernel, rev, groups),
        grid=(nblk,),
        in_specs=in_specs,
        out_specs=out_specs,
        out_shape=out_shape,
        scratch_shapes=scratch,
        name="mlstm_bwd" if rev else "mlstm_fwd",
    )(*args)


def _mlstm(u_ml, pb, cb, gr, gs, conv_w, conv_b, norm_g, groups):
    h_fwd, qs, ks = _mlstm_dir(False, u_ml, pb, cb, gr, gs, conv_w, conv_b, norm_g, None, groups)
    return _mlstm_dir(True, u_ml, pb, cb, gr, gs, conv_w, conv_b, norm_g, (qs, ks, h_fwd), groups)


def _rt_kernel(rev, groups, *refs):
    if rev:
        (qs_ref, ks_ref, v_ref, dec_ref, qd_ref, kd_ref, z_ref, hf_ref, ng_ref,
         out_ref, st_ref) = refs
    else:
        (q_ref, k_ref, v_ref, cc_ref, ss_ref, dec_ref, qd_ref, kd_ref,
         out_ref, qs_ref, ks_ref, st_ref) = refs
    nch = TB_SEQ // CHUNK
    n = pl.program_id(0)
    sb, nb = _seq_pos((_num_blocks(groups) - 1 - n) if rev else n, groups)

    @pl.when(sb == (nb - 1 if rev else 0))
    def _():
        st_ref[...] = jnp.zeros_like(st_ref)

    if not rev:
        for h in range(HEADS):
            hs = slice(h * HEAD_DIM, (h + 1) * HEAD_DIM)
            cc = cc_ref[...]
            ss = ss_ref[...]
            q = q_ref[:, hs]
            k = k_ref[:, hs]
            qs_ref[:, hs] = ((q * cc + pltpu.roll(q, HEAD_DIM // 2, 1) * ss) * (HEAD_DIM ** -0.5)).astype(BF16)
            ks_ref[:, hs] = (k * cc + pltpu.roll(k, HEAD_DIM // 2, 1) * ss).astype(BF16)

    def body(i, carry):
        c = (nch - 1 - i) if rev else i
        sl = pl.ds(pl.multiple_of(c * CHUNK, CHUNK), CHUNK)
        for h in range(HEADS):
            hs = slice(h * HEAD_DIM, (h + 1) * HEAD_DIM)
            q = qs_ref[sl, hs]
            kt = ks_ref[sl, hs].astype(F32).T
            vb = v_ref[sl, hs].astype(BF16)
            s = _dot(q, kt.astype(BF16)) * dec_ref[h]
            out = _dot(s.astype(BF16), vb) + _dot((q.astype(F32) * qd_ref[h]).astype(BF16),
                                                  st_ref[h].astype(BF16))
            st_ref[h] = kd_ref[h, 1:2, :] * st_ref[h] + _dot((kt * kd_ref[h, 0:1, :]).astype(BF16), vb)
            if rev:
                y = _head_norm(out + hf_ref[sl, hs], ng_ref[:, hs]) * _silu(z_ref[sl, hs])
                out_ref[sl, hs] = y.astype(out_ref.dtype)
            else:
                out_ref[sl, hs] = out
        return carry

    lax.fori_loop(0, nch, body, 0, unroll=not rev)


def _retention_dir(rev, u_rt, rot_c, rot_s, dec, qdec, kdec, norm_g, fwd, groups):
    tb = TB_SEQ
    nblk = _num_blocks(groups)
    d = 1 if rev else 0

    def rb(n):
        return (nblk - 1 - n) if rev else n

    def pos(n):
        return _seq_pos(rb(n), groups)[0]

    def col(g):
        return lambda n: (rb(n), g)

    blk = (tb, WIDTH)
    tbl = lambda n: (d, 0, 0, 0)
    tab_specs = [
        pl.BlockSpec((None, HEADS, CHUNK, CHUNK), tbl),
        pl.BlockSpec((None, HEADS, CHUNK, HEAD_DIM), tbl),
        pl.BlockSpec((None, HEADS, SUBLANES, CHUNK), tbl),
    ]
    full = jax.ShapeDtypeStruct((u_rt.shape[0], WIDTH), F32)
    if rev:
        qs, ks, h_fwd = fwd
        in_specs = [pl.BlockSpec(blk, col(0)), pl.BlockSpec(blk, col(0)), pl.BlockSpec(blk, col(2)),
                    *tab_specs,
                    pl.BlockSpec(blk, col(3)), pl.BlockSpec(blk, col(0)),
                    pl.BlockSpec((1, WIDTH), lambda n: (0, 0))]
        args = [qs, ks, u_rt, dec, qdec, kdec, u_rt, h_fwd, norm_g]
        out_specs = pl.BlockSpec(blk, col(0))
        out_shape = jax.ShapeDtypeStruct(full.shape, BF16)
    else:
        in_specs = [pl.BlockSpec(blk, col(0)), pl.BlockSpec(blk, col(1)), pl.BlockSpec(blk, col(2)),
                    pl.BlockSpec((tb, HEAD_DIM), lambda n: (pos(n), 0)),
                    pl.BlockSpec((tb, HEAD_DIM), lambda n: (pos(n), 0)),
                    *tab_specs]
        args = [u_rt, u_rt, u_rt, rot_c, rot_s, dec, qdec, kdec]
        half = jax.ShapeDtypeStruct(full.shape, BF16)
        out_specs, out_shape = [pl.BlockSpec(blk, col(0))] * 3, [full, half, half]
    return pl.pallas_call(
        functools.partial(_rt_kernel, rev, groups),
        grid=(nblk,),
        in_specs=in_specs,
        out_specs=out_specs,
        out_shape=out_shape,
        scratch_shapes=[pltpu.VMEM((HEADS, HEAD_DIM, HEAD_DIM), F32)],
        name="retention_bwd" if rev else "retention_fwd",
    )(*args)


def _retention(u_rt, rot_c, rot_s, tabs, norm_g, groups):
    dec, qdec, kdec = tabs
    h_fwd, qs, ks = _retention_dir(False, u_rt, rot_c, rot_s, dec, qdec, kdec, norm_g, None, groups)
    return _retention_dir(True, u_rt, rot_c, rot_s, dec, qdec, kdec, norm_g, (qs, ks, h_fwd), groups)


def _rotary_tables(seq):
    inv = ROPE_BASE ** (-jnp.arange(0, HEAD_DIM, 2, dtype=F32) / HEAD_DIM)
    ang = jnp.arange(seq, dtype=F32)[:, None] * inv[None, :]
    cos, sin = jnp.cos(ang), jnp.sin(ang)
    return jnp.concatenate([cos, cos], axis=1), jnp.concatenate([-sin, sin], axis=1)


def _retention_tables():
    pos = np.arange(CHUNK, dtype=np.float64)
    rel = pos[:, None] - pos[None, :]
    dec = np.zeros((2, HEADS, CHUNK, CHUNK))
    qdec = np.zeros((2, HEADS, CHUNK, HEAD_DIM))
    kdec = np.zeros((2, HEADS, SUBLANES, CHUNK))
    for h in range(HEADS):
        lf, lb = RT_LOG_GAMMA_FWD[h], RT_LOG_GAMMA_BWD[h]
        dec[0, h] = np.where(rel >= 0, np.exp(np.maximum(rel, 0.0) * lf), 0.0)
        dec[1, h] = np.where(rel <= 0, np.exp(np.maximum(-rel, 0.0) * lb), 0.0)
        qdec[0, h] = np.exp((pos + 1.0) * lf)[:, None]
        qdec[1, h] = np.exp((CHUNK - pos) * lb)[:, None]
        kdec[0, h, 0] = np.exp((CHUNK - 1.0 - pos) * lf)
        kdec[1, h, 0] = np.exp(pos * lb)
        kdec[0, h, 1] = np.exp(CHUNK * lf)
        kdec[1, h, 1] = np.exp(CHUNK * lb)
    return jnp.asarray(dec, F32), jnp.asarray(qdec, F32), jnp.asarray(kdec, F32)


def _filt_kernel(ft_ref, fr_ref, f0_ref, w1_ref, b1_ref, w2_ref, b2_ref, fq_ref,
                 w3f_ref, w3b_ref, dlf_ref, dlb_ref, gb_ref, gf_ref, s_ref, hf3_ref, hr3_ref):
    lt = pl.program_id(1)
    fq = fq_ref[...]

    def hidden(ft):
        h1 = jnp.sin(fq * (_dot_f32(w1_ref[...], ft) + b1_ref[...]))
        return jnp.sin(fq * (_dot_f32(w2_ref[...], h1) + b2_ref[...]))

    @pl.when((pl.program_id(2) == 0) & (pl.program_id(3) == 0))
    def _():
        hf3_ref[...] = _split3_rows(hidden(ft_ref[...]))
        hr3_ref[...] = _split3_rows(hidden(fr_ref[...]))

    def lhs3(w):
        hi, lo = _split_hi_lo(w)
        return jnp.concatenate([hi, hi, lo], axis=1)

    ft = ft_ref[...]
    fr = fr_ref[...]
    dl_f = jnp.abs(dlf_ref[...])
    dl_b = jnp.abs(dlb_ref[...])
    fwd = _dot(lhs3(w3f_ref[...]), hf3_ref[...]) * jnp.exp(-ft[0:1, :] * dl_f)
    bwd = _dot(lhs3(w3b_ref[...]), hr3_ref[...]) * jnp.exp(-fr[0:1, :] * dl_b)
    f0 = f0_ref[...]
    bwd0 = (_dot_f32(w3b_ref[...], hidden(f0)) * jnp.exp(-f0[0:1, :] * dl_b))[:, 0:1]
    lag = lax.broadcasted_iota(jnp.int32, fwd.shape, 1) + lt * LT_FILT
    fwd = jnp.where(lag == 0, fwd + bwd0, fwd)
    bwd = jnp.where(lag == 0, 0.0, bwd)
    gb_ref[...] = bwd
    gf_ref[...] = fwd
    part = jnp.sum(jnp.abs(fwd) + jnp.abs(bwd), axis=1, keepdims=True)
    lane = lax.broadcasted_iota(jnp.int32, (LANES, LANES), 1)
    s_ref[...] = jnp.where(lane == 0, part, 0.0)


def _filters(seq, w1t, b1, w2t, b2, freq, w3t, deltas):
    t = jnp.linspace(0.0, 1.0, seq, dtype=F32)[None, :]
    w = (2.0 * math.pi / seq) * jnp.arange(seq, dtype=F32)[None, :]
    bands = jnp.linspace(1e-4, HY_BANDS - 1, HY_BANDS, dtype=F32)[:, None]
    feats = jnp.concatenate([t, jnp.cos(bands * w), -jnp.sin(bands * w),
                             jnp.zeros((FEAT_ROWS - 1 - 2 * HY_BANDS, seq), F32)], axis=0)
    feats_rev = jnp.concatenate([feats[:, :1], feats[:, :0:-1]], axis=1)
    feats0 = jnp.broadcast_to(feats[:, :1], (FEAT_ROWS, LANES))
    nlt = seq // LT_FILT
    ct = WIDTH // LANES
    lyr = lambda l, i, o, c: (l, 0, 0)
    return pl.pallas_call(
        _filt_kernel,
        grid=(DEPTH, nlt, HY_ORDER, ct),
        in_specs=[
            pl.BlockSpec((FEAT_ROWS, LT_FILT), lambda l, i, o, c: (0, i)),
            pl.BlockSpec((FEAT_ROWS, LT_FILT), lambda l, i, o, c: (0, i)),
            pl.BlockSpec((FEAT_ROWS, LANES), lambda l, i, o, c: (0, 0)),
            pl.BlockSpec((None, HY_HIDDEN, FEAT_ROWS), lyr),
            pl.BlockSpec((None, HY_HIDDEN, 1), lyr),
            pl.BlockSpec((None, HY_HIDDEN, HY_HIDDEN), lyr),
            pl.BlockSpec((None, HY_HIDDEN, 1), lyr),
            pl.BlockSpec((None, HY_HIDDEN, 1), lyr),
            pl.BlockSpec((None, LANES, HY_HIDDEN), lambda l, i, o, c: (l, o * 2 * ct + c, 0)),
            pl.BlockSpec((None, LANES, HY_HIDDEN), lambda l, i, o, c: (l, o * 2 * ct + ct + c, 0)),
            pl.BlockSpec((None, LANES, 1), lambda l, i, o, c: (l, o * 2 * ct + c, 0)),
            pl.BlockSpec((None, LANES, 1), lambda l, i, o, c: (l, o * 2 * ct + ct + c, 0)),
        ],
        out_specs=[
            pl.BlockSpec((None, None, LANES, LT_FILT), lambda l, i, o, c: (l, o, c, i)),
            pl.BlockSpec((None, None, LANES, LT_FILT), lambda l, i, o, c: (l, o, c, i)),
            pl.BlockSpec((None, None, LANES, LANES), lambda l, i, o, c: (l, o, c, i)),
        ],
        out_shape=[
            jax.ShapeDtypeStruct((DEPTH, HY_ORDER, WIDTH, seq), F32),
            jax.ShapeDtypeStruct((DEPTH, HY_ORDER, WIDTH, seq), F32),
            jax.ShapeDtypeStruct((DEPTH, HY_ORDER, WIDTH, nlt * LANES), F32),
        ],
        scratch_shapes=[pltpu.VMEM((3 * HY_HIDDEN, LT_FILT), BF16),
                        pltpu.VMEM((3 * HY_HIDDEN, LT_FILT), BF16)],
        name="hyena_filters",
    )(feats, feats_rev, feats0, w1t, b1, w2t, b2, freq, w3t, w3t, deltas, deltas)


def _split_hi_lo(x):
    hi = x.astype(BF16)
    lo = (x - hi.astype(F32)).astype(BF16)
    return hi, lo


def _split3_rows(x):
    hi, lo = _split_hi_lo(x)
    return jnp.concatenate([hi, lo, hi], axis=0)


def _split3_cols(x):
    hi, lo = _split_hi_lo(x)
    return jnp.concatenate([hi, lo, hi], axis=1)


def _np_split(m):
    hi = m.astype(ml_dtypes.bfloat16)
    lo = (m - hi.astype(np.float64)).astype(ml_dtypes.bfloat16)
    return hi, lo


def _dft_tables(seq):
    nj = seq // LANES
    n1 = 2 * nj
    n = n1 * LANES
    khp = n1 // 2 + KH_PAD
    k1 = np.arange(khp, dtype=np.float64)[:, None]
    valid = (k1 <= n1 // 2).astype(np.float64)
    ang = 2.0 * np.pi * k1 * np.arange(n1, dtype=np.float64)[None, :] / n1
    fa_full = np.concatenate([valid * np.cos(ang), -valid * np.sin(ang)], axis=0)

    def lhs3(m):
        hi, lo = _np_split(m)
        return jnp.asarray(np.concatenate([hi, hi, lo], axis=1))

    def rhs3(m):
        hi, lo = _np_split(m)
        return jnp.asarray(np.concatenate([hi, hi, lo], axis=0))

    n2 = np.arange(LANES, dtype=np.float64)
    ang2 = 2.0 * np.pi * n2[:, None] * n2[None, :] / LANES
    c2, s2 = np.cos(ang2), np.sin(ang2)
    fb = np.block([[c2, -s2], [s2, c2]])
    fbi = np.block([[c2, s2], [-s2, c2]])
    w = np.where((k1 == 0) | (k1 == n1 // 2), 1.0, 2.0) * valid
    th = 2.0 * np.pi * np.arange(nj, dtype=np.float64)[:, None] * k1.T / n1
    fai = np.concatenate([(w.T / n) * np.cos(th), -(w.T / n) * np.sin(th)], axis=1)
    phi = 2.0 * np.pi * k1 * n2[None, :] / n
    return dict(
        nj=nj, n1=n1, khp=khp,
        fa=lhs3(fa_full[:, :nj]), fa_full=lhs3(fa_full), fb=rhs3(fb), fbi=rhs3(fbi), fai=lhs3(fai),
        twc=jnp.asarray(valid * np.cos(phi), F32), tws=jnp.asarray(valid * np.sin(phi), F32))


def _dft_rows(fa_ref, twc, tws, khp, cols):
    rhs = cols[0] if len(cols) == 1 else jnp.concatenate(cols, axis=1)
    y = _dot(fa_ref[...], _split3_rows(rhs))
    out = []
    for i in range(len(cols)):
        yre = y[0:khp, i * LANES:(i + 1) * LANES]
        yim = y[khp:2 * khp, i * LANES:(i + 1) * LANES]
        out.append((yre * twc + yim * tws, yim * twc - yre * tws))
    return out


def _spec_kernel(khp, g_ref, s_ref, fa_ref, fb_ref, twc_ref, tws_ref, h_ref, ybuf):
    twc = twc_ref[...]
    tws = tws_ref[...]
    for c in range(0, CB_SPEC, 2):
        res = _dft_rows(fa_ref, twc, tws, khp, [g_ref[c], g_ref[c + 1]])
        for i, (yre, yim) in enumerate(res):
            r0 = (c + i) * khp
            ybuf[r0:r0 + khp, 0:LANES] = yre
            ybuf[r0:r0 + khp, LANES:2 * LANES] = yim
    x = _dot(_split3_cols(ybuf[...]), fb_ref[...])
    for c in range(CB_SPEC):
        norm = jnp.sum(s_ref[c:c + 1, :], axis=1, keepdims=True)
        h_ref[c] = x[c * khp:(c + 1) * khp, :] / norm


def _spectrum(g, s, tab):
    nch = g.shape[0]
    n1, khp = tab["n1"], tab["khp"]
    const = lambda a: pl.BlockSpec(a.shape, lambda i: (0, 0))
    return pl.pallas_call(
        functools.partial(_spec_kernel, khp),
        grid=(nch // CB_SPEC,),
        in_specs=[
            pl.BlockSpec((CB_SPEC, n1, LANES), lambda i: (i, 0, 0)),
            pl.BlockSpec((CB_SPEC, s.shape[-1]), lambda i: (i, 0)),
            const(tab["fa_full"]), const(tab["fb"]), const(tab["twc"]), const(tab["tws"]),
        ],
        out_specs=pl.BlockSpec((CB_SPEC, khp, 2 * LANES), lambda i: (i, 0, 0)),
        out_shape=jax.ShapeDtypeStruct((nch, khp, 2 * LANES), F32),
        scratch_shapes=[pltpu.VMEM((CB_SPEC * khp, 2 * LANES), F32)],
        name="hyena_spectrum",
    )(g, s, tab["fa_full"], tab["fb"], tab["twc"], tab["tws"])


def _hyfft_kernel(bsz, nj, khp, prm_ref, v_ref, x1_ref, x2_ref, z_ref, h_ref,
                  fa_ref, fb_ref, fbi_ref, fai_ref, twc_ref, tws_ref, out_ref,
                  xs_ref, ybuf, pbuf):
    ct = pl.program_id(0)
    rows = bsz * nj
    pairs = [(b, b + 1) for b in range(0, bsz, 2)] if bsz > 1 else [(0,)]
    twc = twc_ref[...]
    tws = tws_ref[...]

    lane = lax.broadcasted_iota(jnp.int32, (rows, LANES), 1)
    blk = lax.broadcasted_iota(jnp.int32, (rows, LANES), 0) & (nj - 1)
    first = (lane == 0) & (blk == 0)
    last = (lane == LANES - 1) & (blk == nj - 1)

    def conv3(x, base):
        r = pltpu.roll(x, 1, 1)
        xm1 = jnp.where(lane == 0, pltpu.roll(r, 1, 0), r)
        xm1 = jnp.where(first, 0.0, xm1)
        l = pltpu.roll(x, LANES - 1, 1)
        xp1 = jnp.where(lane == LANES - 1, pltpu.roll(l, rows - 1, 0), l)
        xp1 = jnp.where(last, 0.0, xp1)
        return prm_ref[base] * xm1 + prm_ref[base + 1] * x + prm_ref[base + 2] * xp1 + prm_ref[base + 3]

    def row0(c, b):
        return (c * bsz + b) * khp

    def forward(c, x):
        for pr in pairs:
            res = _dft_rows(fa_ref, twc, tws, khp, [x[b * nj:(b + 1) * nj, :] for b in pr])
            for b, (yre, yim) in zip(pr, res):
                r0 = row0(c, b)
                ybuf[r0:r0 + khp, 0:LANES] = yre
                ybuf[r0:r0 + khp, LANES:2 * LANES] = yim

    def spectral(order):
        for c0 in range(0, CB_HY, CG_HY):
            g0, g1 = row0(c0, 0), row0(c0 + CG_HY, 0)
            x = _dot(_split3_cols(ybuf[g0:g1, :]), fb_ref[...])
            for c in range(c0, c0 + CG_HY):
                hre = h_ref[order, c, :, 0:LANES]
                him = h_ref[order, c, :, LANES:2 * LANES]
                for b in range(bsz):
                    r0 = row0(c, b)
                    xre = x[r0 - g0:r0 - g0 + khp, 0:LANES]
                    xim = x[r0 - g0:r0 - g0 + khp, LANES:2 * LANES]
                    pbuf[r0:r0 + khp, 0:LANES] = xre * hre - xim * him
                    pbuf[r0:r0 + khp, LANES:2 * LANES] = xre * him + xim * hre
            r = _dot(_split3_cols(pbuf[g0:g1, :]), fbi_ref[...])
            for c in range(c0, c0 + CG_HY):
                for b in range(bsz):
                    r0 = row0(c, b)
                    rre = r[r0 - g0:r0 - g0 + khp, 0:LANES]
                    rim = r[r0 - g0:r0 - g0 + khp, LANES:2 * LANES]
                    ybuf[r0:r0 + khp, 0:LANES] = rre * twc - rim * tws
                    ybuf[r0:r0 + khp, LANES:2 * LANES] = rre * tws + rim * twc

    def inverse(c):
        outs = [None] * bsz
        for pr in pairs:
            rre = [ybuf[row0(c, b):row0(c, b) + khp, 0:LANES] for b in pr]
            rim = [ybuf[row0(c, b):row0(c, b) + khp, LANES:2 * LANES] for b in pr]
            if len(pr) > 1:
                rre, rim = [jnp.concatenate(rre, axis=1)], [jnp.concatenate(rim, axis=1)]
            y = _dot(fai_ref[...], _split3_rows(jnp.concatenate([rre[0], rim[0]], axis=0)))
            for i, b in enumerate(pr):
                outs[b] = y[:, i * LANES:(i + 1) * LANES]
        return outs[0] if bsz == 1 else jnp.concatenate(outs, axis=0)

    for c in range(CB_HY):
        base = (ct * CB_HY + c) * 16
        v = conv3(v_ref[c], base)
        xs_ref[0, c] = v
        xs_ref[1, c] = conv3(x1_ref[c], base + 4)
        xs_ref[2, c] = conv3(x2_ref[c], base + 8)
        forward(c, v)
    spectral(0)
    for c in range(CB_HY):
        base = (ct * CB_HY + c) * 16
        v = xs_ref[0, c]
        z1 = xs_ref[1, c] * (inverse(c) + prm_ref[base + 12] * v)
        xs_ref[0, c] = z1
        forward(c, z1)
    spectral(1)
    for c in range(CB_HY):
        base = (ct * CB_HY + c) * 16
        z1 = xs_ref[0, c]
        y2 = inverse(c) + prm_ref[base + 13] * z1
        out_ref[:, c, :] = xs_ref[2, c] * y2 * _silu(z_ref[c])


def _hyena_fft(prm, u_t, h, layer, tab, row0, bsz, seq):
    nj, khp = tab["nj"], tab["khp"]
    nct = WIDTH // CB_HY
    rows = bsz * nj
    blk = (CB_HY, rows, LANES)
    rblk = row0 // (rows * LANES)
    assert rblk * rows * LANES == row0
    const = lambda a: pl.BlockSpec(a.shape, lambda i: (0, 0))
    return pl.pallas_call(
        functools.partial(_hyfft_kernel, bsz, nj, khp),
        grid=(nct,),
        in_specs=[
            pl.BlockSpec(memory_space=pltpu.SMEM),
            pl.BlockSpec(blk, lambda i: (i, rblk, 0)),
            pl.BlockSpec(blk, lambda i: (nct + i, rblk, 0)),
            pl.BlockSpec(blk, lambda i: (2 * nct + i, rblk, 0)),
            pl.BlockSpec(blk, lambda i: (3 * nct + i, rblk, 0)),
            pl.BlockSpec((None, HY_ORDER, CB_HY, khp, 2 * LANES), lambda i: (layer, 0, i, 0, 0)),
            const(tab["fa"]), const(tab["fb"]), const(tab["fbi"]), const(tab["fai"]),
            const(tab["twc"]), const(tab["tws"]),
        ],
        out_specs=pl.BlockSpec((rows, CB_HY, LANES), lambda i: (0, i, 0)),
        out_shape=jax.ShapeDtypeStruct((rows, WIDTH, LANES), F32),
        scratch_shapes=[
            pltpu.VMEM((3, CB_HY, bsz * nj, LANES), F32),
            pltpu.VMEM((CB_HY * bsz * khp, 2 * LANES), F32),
            pltpu.VMEM((CB_HY * bsz * khp, 2 * LANES), F32),
        ],
        name="hyena_fft",
    )(prm, u_t, u_t, u_t, u_t, h, tab["fa"], tab["fb"], tab["fbi"], tab["fai"],
      tab["twc"], tab["tws"])


def _out_kernel(final, starts, nx, *refs):
    ng = len(starts)
    x_refs = refs[0:nx]
    yml_ref, yrt_ref = refs[nx:nx + 2]
    yhy_refs = refs[nx + 2:nx + 2 + ng]
    w_ref, g_ref = refs[nx + 2 + ng:nx + 4 + ng]
    out_refs = refs[nx + 4 + ng:-1]
    lhs_ref = refs[-1]
    i = pl.program_id(0)
    lhs_ref[:, 0:WIDTH] = yml_ref[...]
    lhs_ref[:, WIDTH:2 * WIDTH] = yrt_ref[...]
    jb = TM_OUT // LANES

    def fill_hyena(ref):
        for j in range(jb):
            lhs_ref[j * LANES:(j + 1) * LANES, 2 * WIDTH:] = ref[j].T.astype(BF16)

    for gi in range(ng):
        inside = i >= starts[gi]
        if gi + 1 < ng:
            inside = inside & (i < starts[gi + 1])
        pl.when(inside)(functools.partial(fill_hyena, yhy_refs[gi]))
    x = _x_tile(x_refs, starts if nx > 1 else [0]) + _dot(lhs_ref[...], w_ref[...])
    if not final:
        out_refs[0][...] = x
        return
    ms = jnp.mean(x * x, axis=-1, keepdims=True)
    x = x * lax.rsqrt(ms + RMS_EPS) * g_ref[...]
    for gi in range(ng):
        inside = i >= starts[gi]
        if gi + 1 < ng:
            inside = inside & (i < starts[gi + 1])

        @pl.when(inside)
        def _(gi=gi):
            out_refs[gi][...] = x


def _out_proj(xs, y_ml, y_rt, y_hy, w, gain, final, groups):
    t = y_ml.shape[0]
    jb = TM_OUT // LANES
    starts = [r0 // TM_OUT for (r0, _, _) in groups]
    counts = [bsz * seq // TM_OUT for (_, bsz, seq) in groups]

    def tile(gi):
        return lambda i: jnp.clip(i - starts[gi], 0, counts[gi] - 1)

    hy_specs = [pl.BlockSpec((jb, WIDTH, LANES), lambda i, f=tile(gi): (f(i), 0, 0))
                for gi in range(len(groups))]
    if final:
        out_specs = [pl.BlockSpec((TM_OUT, D_MODEL), lambda i, f=tile(gi): (f(i), 0))
                     for gi in range(len(groups))]
        out_shape = [jax.ShapeDtypeStruct((bsz * seq, D_MODEL), F32) for (_, bsz, seq) in groups]
    else:
        out_specs = [pl.BlockSpec((TM_OUT, D_MODEL), lambda i: (i, 0))]
        out_shape = [jax.ShapeDtypeStruct((t, D_MODEL), F32)]
    return pl.pallas_call(
        functools.partial(_out_kernel, final, starts, len(xs)),
        grid=(t // TM_OUT,),
        in_specs=[
            *_x_specs(xs, TM_OUT, 1),
            pl.BlockSpec((TM_OUT, WIDTH), lambda i: (i, 0)),
            pl.BlockSpec((TM_OUT, WIDTH), lambda i: (i, 0)),
            *hy_specs,
            pl.BlockSpec((3 * WIDTH, D_MODEL), lambda i: (0, 0)),
            pl.BlockSpec((1, D_MODEL), lambda i: (0, 0)),
        ],
        out_specs=out_specs,
        out_shape=out_shape,
        scratch_shapes=[pltpu.VMEM((TM_OUT, 3 * WIDTH), BF16)],
        name="out_proj",
    )(*xs, y_ml, y_rt, *y_hy, w, gain)


def kernel(x_prompt, x_sample, norm_g, w_in, ml_conv_w, ml_conv_b, ml_gate_b, ml_norm_g, rt_norm_g, hy_conv_w, hy_conv_b, hy_w1, hy_b1, hy_w2, hy_b2, hy_w3, hy_freq, hy_deltas, hy_skip, w_out, final_g):
    groups = []
    row0 = 0
    for xg in (x_prompt, x_sample):
        bsz, seq, _ = xg.shape
        groups.append((row0, bsz, seq))
        row0 += bsz * seq
    xs = [x_prompt.reshape(-1, D_MODEL), x_sample.reshape(-1, D_MODEL)]

    c_gate = ML_COLS
    c_rt = c_gate + 4 * HEADS
    c_hy = c_rt + RT_COLS
    gate_pad = jnp.zeros((DEPTH, D_MODEL, LANES - 4 * HEADS), F32)
    w_tok = jnp.concatenate([w_in[:, :, :c_gate], w_in[:, :, c_gate:c_rt], gate_pad,
                             w_in[:, :, c_rt:c_hy]], axis=2).astype(BF16)
    w_gt = jnp.swapaxes(w_in[:, :, c_gate:c_rt], 1, 2).astype(BF16)
    w_hyt = jnp.swapaxes(w_in[:, :, c_hy:], 1, 2).astype(BF16)
    w_out_b = w_out.astype(BF16)
    gate_b = ml_gate_b.reshape(DEPTH, 1, 4 * HEADS)
    gate_b_row = jnp.concatenate([gate_b, jnp.zeros((DEPTH, 1, LANES - 4 * HEADS), F32)], axis=2)
    gate_b_col = ml_gate_b.reshape(DEPTH, 4 * HEADS, 1)

    cw = hy_conv_w.reshape(DEPTH, 3, 3, WIDTH)
    cbias = hy_conv_b.reshape(DEPTH, 1, 3, WIDTH)
    taps = jnp.concatenate([cw, cbias], axis=1)
    taps = jnp.transpose(taps, (0, 3, 2, 1)).reshape(DEPTH, WIDTH, 12)
    hy_prm = jnp.concatenate([taps, jnp.transpose(hy_skip, (0, 2, 1)),
                              jnp.zeros((DEPTH, WIDTH, 2), F32)], axis=2).reshape(DEPTH, WIDTH * 16)

    w1t = jnp.concatenate([jnp.swapaxes(hy_w1, 1, 2),
                           jnp.zeros((DEPTH, HY_HIDDEN, FEAT_ROWS - hy_w1.shape[1]), F32)], axis=2)
    w2t = jnp.swapaxes(hy_w2, 1, 2)
    w3t = jnp.swapaxes(hy_w3, 1, 2)
    b1 = hy_b1[:, :, None]
    b2 = hy_b2[:, :, None]
    freq = hy_freq[:, :, None]
    deltas = hy_deltas.reshape(DEPTH, HY_ORDER * 2 * WIDTH, 1)

    tables, spectra = [], []
    for (_, _, seq) in groups:
        tab = _dft_tables(seq)
        gb, gf, s = _filters(seq, w1t, b1, w2t, b2, freq, w3t, deltas)
        g = jnp.concatenate([gf, gb], axis=-1).reshape(DEPTH * HY_ORDER * WIDTH, tab["n1"], LANES)
        h = _spectrum(g, s.reshape(DEPTH * HY_ORDER * WIDTH, -1), tab)
        tables.append(tab)
        spectra.append(h.reshape(DEPTH, HY_ORDER, WIDTH, tab["khp"], 2 * LANES))
    rot_c, rot_s = _rotary_tables(max(seq for (_, _, seq) in groups))
    rt_tabs = _retention_tables()

    for layer in range(DEPTH):
        gain = norm_g[layer][None, :]
        u_ml, u_g, u_rt, u_gt = _in_proj(xs, gain, w_tok[layer], w_gt[layer])
        u_hyt = _in_proj_hy(xs, gain, w_hyt[layer])
        pb, cb, gr, gs = _gates(u_g, u_gt, gate_b_row[layer], gate_b_col[layer])
        y_ml = _mlstm(u_ml, pb, cb, gr, gs, ml_conv_w[layer], ml_conv_b[layer][None, :],
                      ml_norm_g[layer][None, :], groups)
        y_rt = _retention(u_rt, rot_c, rot_s, rt_tabs, rt_norm_g[layer][None, :], groups)
        y_hy = [_hyena_fft(hy_prm[layer], u_hyt, spectra[gi], layer, tables[gi], r0, bsz, seq)
                for gi, (r0, bsz, seq) in enumerate(groups)]
        outs = _out_proj(xs, y_ml, y_rt, y_hy, w_out_b[layer], final_g[None, :],
                         layer == DEPTH - 1, groups)
        xs = [outs[0]]

    return outs[0].reshape(x_prompt.shape), outs[1].reshape(x_sample.shape)
```

```python
import functools
import math

import ml_dtypes
import numpy as np
import jax
import jax.numpy as jnp
from jax import lax
from jax.experimental import pallas as pl
from jax.experimental.pallas import tpu as pltpu

F32 = jnp.float32
BF16 = jnp.bfloat16

D_MODEL = 1024
DEPTH = 4
CHUNK = 128
HEADS = 4
HEAD_DIM = 128
WIDTH = 512
HY_ORDER = 2
HY_BANDS = 16
HY_HIDDEN = 64
FEAT_ROWS = 40
HY_PRM = 16
ROPE_BASE = 10000.0
RMS_EPS = 1e-6
HEAD_NORM_EPS = 1e-5
M_INIT = -1e30
RT_LOG_GAMMA_FWD = tuple(math.log(1.0 - 2.0 ** (-5.0 - h)) for h in range(HEADS))
RT_LOG_GAMMA_BWD = tuple(math.log(1.0 - 2.0 ** (-5.5 - h)) for h in range(HEADS))

LANES = 128
SUBLANES = 8
TM_IN = 256
TM_HY = 1024
TR_HY = 1024
TM_OUT = 1024
TB_SEQ = 1024
TG = 1024
CB_HY = 8
CG_HY = 2
CB_SPEC = 8
KH_PAD = 8
LT_FILT = 2048
ML_COLS = 5 * WIDTH
RT_COLS = 4 * WIDTH
HY_ROWS = 4 * WIDTH
HIGHEST = lax.Precision.HIGHEST


def _dot(a, b):
    return jnp.dot(a, b, preferred_element_type=F32)


def _dot_nt(a, b):
    return lax.dot_general(a, b, (((1,), (1,)), ((), ())), preferred_element_type=F32)


def _dot_tn(a, b):
    return lax.dot_general(a, b, (((0,), (0,)), ((), ())), preferred_element_type=F32)


def _dot_f32(a, b):
    return jnp.dot(a, b, precision=HIGHEST, preferred_element_type=F32)


def _silu(y):
    return y * jax.nn.sigmoid(y)


def _log_sigmoid(x):
    return -(jnp.maximum(-x, 0.0) + jnp.log1p(jnp.exp(-jnp.abs(x))))


def _head_norm(h, gain):
    mu = jnp.mean(h, axis=-1, keepdims=True)
    hc = h - mu
    var = jnp.mean(hc * hc, axis=-1, keepdims=True)
    return hc * lax.rsqrt(var + HEAD_NORM_EPS) * gain


def _tile_starts(xs, tm):
    starts, s = [], 0
    for a in xs:
        starts.append(s)
        s += a.shape[0] // tm
    return starts, s


def _x_specs(xs, tm, ngrid):
    starts, _ = _tile_starts(xs, tm)
    specs = []
    for a, s in zip(xs, starts):
        cnt = a.shape[0] // tm
        if ngrid == 1:
            specs.append(pl.BlockSpec((tm, D_MODEL), lambda i, s=s, cnt=cnt: (jnp.clip(i - s, 0, cnt - 1), 0)))
        else:
            specs.append(pl.BlockSpec((tm, D_MODEL), lambda i, r, s=s, cnt=cnt: (jnp.clip(i - s, 0, cnt - 1), 0)))
    return specs


def _x_tile(x_refs, starts):
    x = x_refs[0][...]
    for ref, s in zip(x_refs[1:], starts[1:]):
        x = jnp.where(pl.program_id(0) >= s, ref[...], x)
    return x


def _rms_bf16(x, g_ref):
    ms = jnp.mean(x * x, axis=-1, keepdims=True)
    return (x * lax.rsqrt(ms + RMS_EPS) * g_ref[...]).astype(BF16)


def _in_kernel(starts, *refs):
    nx = len(starts)
    (g_ref, wtok_ref, wgt_ref, cc_ref, ss_ref,
     oml_ref, og_ref, oqk_ref, ovz_ref, ogt_ref) = refs[nx:]
    h = _rms_bf16(_x_tile(refs[:nx], starts), g_ref)
    y = _dot(h, wtok_ref[...])
    oml_ref[...] = y[:, 0:ML_COLS]
    og_ref[...] = y[:, ML_COLS:ML_COLS + LANES]
    c0 = ML_COLS + LANES
    cc = cc_ref[...]
    ss = ss_ref[...]
    for j in range(2 * HEADS):
        u = y[:, c0 + j * HEAD_DIM:c0 + (j + 1) * HEAD_DIM]
        r = u * cc + pltpu.roll(u, HEAD_DIM // 2, 1) * ss
        if j < HEADS:
            r = r * (HEAD_DIM ** -0.5)
        oqk_ref[:, j * HEAD_DIM:(j + 1) * HEAD_DIM] = r.astype(BF16)
    ovz_ref[...] = y[:, c0 + 2 * WIDTH:]
    ogt_ref[...] = _dot_nt(wgt_ref[...], h)


def _in_proj(xs, gain, w_tok, w_gt, rot_c, rot_s, groups):
    starts, nt = _tile_starts(xs, TM_IN)
    t = nt * TM_IN
    ncol = w_tok.shape[1]
    pos = lambda i: (_seq_pos(i, groups, TM_IN)[0], 0)
    return pl.pallas_call(
        functools.partial(_in_kernel, starts),
        grid=(nt,),
        in_specs=[
            *_x_specs(xs, TM_IN, 1),
            pl.BlockSpec((1, D_MODEL), lambda i: (0, 0)),
            pl.BlockSpec((D_MODEL, ncol), lambda i: (0, 0)),
            pl.BlockSpec((4 * HEADS, D_MODEL), lambda i: (0, 0)),
            pl.BlockSpec((TM_IN, HEAD_DIM), pos),
            pl.BlockSpec((TM_IN, HEAD_DIM), pos),
        ],
        out_specs=[
            pl.BlockSpec((TM_IN, ML_COLS), lambda i: (i, 0)),
            pl.BlockSpec((TM_IN, LANES), lambda i: (i, 0)),
            pl.BlockSpec((TM_IN, 2 * WIDTH), lambda i: (i, 0)),
            pl.BlockSpec((TM_IN, 2 * WIDTH), lambda i: (i, 0)),
            pl.BlockSpec((4 * HEADS, TM_IN), lambda i: (0, i)),
        ],
        out_shape=[
            jax.ShapeDtypeStruct((t, ML_COLS), F32),
            jax.ShapeDtypeStruct((t, LANES), F32),
            jax.ShapeDtypeStruct((t, 2 * WIDTH), BF16),
            jax.ShapeDtypeStruct((t, 2 * WIDTH), F32),
            jax.ShapeDtypeStruct((4 * HEADS, t), F32),
        ],
        name="in_proj",
    )(*xs, gain, w_tok, w_gt, rot_c, rot_s)


def _in_hy_kernel(starts, *refs):
    nx = len(starts)
    g_ref, w_ref, o_ref = refs[nx:]
    y = _dot_nt(w_ref[...], _rms_bf16(_x_tile(refs[:nx], starts), g_ref))
    o_ref[...] = y.reshape(o_ref.shape)


def _in_proj_hy(xs, gain, w_hyt):
    starts, nt = _tile_starts(xs, TM_HY)
    t = nt * TM_HY
    nr = HY_ROWS // TR_HY
    return pl.pallas_call(
        functools.partial(_in_hy_kernel, starts),
        grid=(nt, nr),
        in_specs=[
            *_x_specs(xs, TM_HY, 2),
            pl.BlockSpec((1, D_MODEL), lambda i, r: (0, 0)),
            pl.BlockSpec((TR_HY, D_MODEL), lambda i, r: (r, 0)),
        ],
        out_specs=pl.BlockSpec((TR_HY, TM_HY // LANES, LANES), lambda i, r: (r, i, 0)),
        out_shape=jax.ShapeDtypeStruct((HY_ROWS, t // LANES, LANES), F32),
        name="in_proj_hyena",
    )(*xs, gain, w_hyt)


def _gates_kernel(g_ref, gt_ref, b_ref, bt_ref, pb_ref, cb_ref, gr_ref, gs_ref):
    row = lax.broadcasted_iota(jnp.int32, (CHUNK, CHUNK), 0)
    col = lax.broadcasted_iota(jnp.int32, (CHUNK, CHUNK), 1)
    lower = (row >= col).astype(F32)
    upper = (row <= col).astype(F32)
    gr_ref[...] = jnp.zeros_like(gr_ref)
    gs_ref[...] = jnp.zeros_like(gs_ref)
    for c in range(TG // CHUNK):
        sl = slice(c * CHUNK, (c + 1) * CHUNK)
        lf = _log_sigmoid(g_ref[sl, :] + b_ref[...])
        cum_c = (_dot_f32(lower, lf), _dot_f32(upper, lf))
        pre_t = gt_ref[:, sl] + bt_ref[...]
        lf_t = _log_sigmoid(pre_t)
        cum_r = (_dot_f32(lf_t, upper), _dot_f32(lf_t, lower))
        for d in range(2):
            for h in range(HEADS):
                kf = (2 * d + 1) * HEADS + h
                ki = 2 * d * HEADS + h
                cum = cum_r[d][kf:kf + 1, :]
                li = pre_t[ki:ki + 1, :]
                r = li - cum
                f_tot = cum[:, 0:1] if d else cum[:, CHUNK - 1:CHUNK]
                g = f_tot - cum + li
                allowed = (row <= col) if d else (row >= col)
                pmax = jnp.max(jnp.where(allowed, r, -jnp.inf), axis=1, keepdims=True)
                pb_ref[d, h, sl, :] = jnp.broadcast_to(pmax, (CHUNK, LANES))
                cb_ref[d, h, sl, :] = jnp.broadcast_to(cum_c[d][:, kf:kf + 1], (CHUNK, LANES))
                gr_ref[d, h, 0:1, sl] = r
                gr_ref[d, h, 1:2, sl] = g
                gs_ref[d, h, c, 0:1, :] = jnp.broadcast_to(f_tot, (1, LANES))
                gs_ref[d, h, c, 1:2, :] = jnp.broadcast_to(jnp.max(g, axis=1, keepdims=True), (1, LANES))


def _gates(g, gt, bias_row, bias_col):
    t = g.shape[0]
    nc = TG // CHUNK
    return pl.pallas_call(
        _gates_kernel,
        grid=(t // TG,),
        in_specs=[
            pl.BlockSpec((TG, LANES), lambda i: (i, 0)),
            pl.BlockSpec((4 * HEADS, TG), lambda i: (0, i)),
            pl.BlockSpec((1, LANES), lambda i: (0, 0)),
            pl.BlockSpec((4 * HEADS, 1), lambda i: (0, 0)),
        ],
        out_specs=[
            pl.BlockSpec((2, HEADS, TG, LANES), lambda i: (0, 0, i, 0)),
            pl.BlockSpec((2, HEADS, TG, LANES), lambda i: (0, 0, i, 0)),
            pl.BlockSpec((2, HEADS, SUBLANES, TG), lambda i: (0, 0, 0, i)),
            pl.BlockSpec((2, HEADS, nc, SUBLANES, LANES), lambda i: (0, 0, i, 0, 0)),
        ],
        out_shape=[
            jax.ShapeDtypeStruct((2, HEADS, t, LANES), F32),
            jax.ShapeDtypeStruct((2, HEADS, t, LANES), F32),
            jax.ShapeDtypeStruct((2, HEADS, SUBLANES, t), F32),
            jax.ShapeDtypeStruct((2, HEADS, t // CHUNK, SUBLANES, LANES), F32),
        ],
        name="gates",
    )(g, gt, bias_row, bias_col)


def _seq_pos(g, groups, tb=TB_SEQ):
    pos, nbs = None, None
    start = 0
    for (_, bsz, seq) in groups:
        nb = seq // tb
        p = (g - start) % nb
        pos = p if pos is None else jnp.where(g >= start, p, pos)
        nbs = nb if nbs is None else jnp.where(g >= start, nb, nbs)
        start += bsz * nb
    return pos, nbs


def _num_blocks(groups):
    return sum(bsz * seq // TB_SEQ for (_, bsz, seq) in groups)


def _ml_kernel(rev, groups, *refs):
    if rev:
        (qs_ref, ks_ref, v_ref, pb_ref, cb_ref, gr_ref, gs_ref, o_ref, z_ref, hf_ref, ng_ref,
         out_ref, st_ref, m_ref) = refs
    else:
        (q_ref, k_ref, v_ref, qp_ref, qn_ref, kp_ref, kn_ref, pb_ref, cb_ref, gr_ref, gs_ref,
         cw_ref, cbias_ref, out_ref, qs_ref, ks_ref, st_ref, m_ref, stg_ref) = refs
    tb = TB_SEQ
    nch = tb // CHUNK
    n = pl.program_id(0)
    sb, nb = _seq_pos((_num_blocks(groups) - 1 - n) if rev else n, groups)

    @pl.when(sb == (nb - 1 if rev else 0))
    def _():
        st_ref[...] = jnp.zeros_like(st_ref)
        m_ref[...] = jnp.full_like(m_ref, M_INIT)

    def conv_silu(raw_ref, prev_ref, next_ref, c0):
        stg_ref[SUBLANES:SUBLANES + tb, :] = raw_ref[...]
        stg_ref[SUBLANES - 1:SUBLANES, :] = jnp.where(sb > 0, prev_ref[SUBLANES - 1:SUBLANES, :], 0.0)
        stg_ref[SUBLANES + tb:SUBLANES + tb + 1, :] = jnp.where(sb < nb - 1, next_ref[0:1, :], 0.0)
        w = cw_ref[:, c0:c0 + WIDTH]
        y = (stg_ref[SUBLANES - 1:SUBLANES - 1 + tb, :] * w[0:1, :]
             + stg_ref[SUBLANES:SUBLANES + tb, :] * w[1:2, :]
             + stg_ref[SUBLANES + 1:SUBLANES + 1 + tb, :] * w[2:3, :] + cbias_ref[:, c0:c0 + WIDTH])
        return _silu(y)

    if not rev:
        qs_ref[...] = conv_silu(q_ref, qp_ref, qn_ref, 0).astype(BF16)
        ks_ref[...] = (conv_silu(k_ref, kp_ref, kn_ref, WIDTH) * (HEAD_DIM ** -0.5)).astype(BF16)

    row = lax.broadcasted_iota(jnp.int32, (CHUNK, CHUNK), 0)
    col = lax.broadcasted_iota(jnp.int32, (CHUNK, CHUNK), 1)
    allowed = (row <= col) if rev else (row >= col)
    ones = jnp.ones((CHUNK, HEAD_DIM), BF16)

    def chunk(c, h):
        sl = pl.ds(pl.multiple_of(c * CHUNK, CHUNK), CHUNK)
        hs = slice(h * HEAD_DIM, (h + 1) * HEAD_DIM)
        q = qs_ref[sl, hs]
        kt = ks_ref[sl, hs].astype(F32).T
        v2 = jnp.concatenate([v_ref[sl, hs].astype(BF16), ones], axis=1)
        m_prev = m_ref[h, 0:1, :]
        mb = jnp.maximum(pb_ref[h, sl, :], m_prev)
        p = jnp.exp(jnp.where(allowed, gr_ref[h, 0:1, sl] - mb, -jnp.inf))
        w_inter = jnp.exp(m_prev - mb)
        floor = jnp.exp(-(cb_ref[h, sl, :] + mb))
        s = _dot(q, kt.astype(BF16)) * p
        inter = _dot(q, st_ref[h].astype(BF16))
        num2 = _dot(s.astype(BF16), v2) + jnp.concatenate([w_inter, w_inter], axis=1) * inter
        out = num2[:, 0:HEAD_DIM] / jnp.maximum(jnp.abs(num2[:, HEAD_DIM:]), floor)
        f_tot = gs_ref[h, c, 0:1, :]
        m_chunk = gs_ref[h, c, 1:2, :]
        m_new = jnp.maximum(f_tot + m_prev, m_chunk)
        kw = (kt * jnp.exp(gr_ref[h, 1:2, sl] - m_new)).astype(BF16)
        kv2 = _dot(kw, v2)
        a = jnp.exp(f_tot + m_prev - m_new)
        st_ref[h] = jnp.concatenate([a, a], axis=1) * st_ref[h] + kv2
        m_ref[h] = jnp.broadcast_to(m_new, (SUBLANES, LANES))
        return out, sl, hs

    def body(i, carry):
        c = (nch - 1 - i) if rev else i
        for h in range(HEADS):
            out, sl, hs = chunk(c, h)
            if rev:
                hh = jax.nn.sigmoid(o_ref[sl, hs]) * (out + hf_ref[sl, hs])
                y = _head_norm(hh, ng_ref[:, hs]) * _silu(z_ref[sl, hs])
                out_ref[sl, hs] = y.astype(out_ref.dtype)
            else:
                out_ref[sl, hs] = out
        return carry

    lax.fori_loop(0, nch, body, 0, unroll=True)


def _mlstm_dir(rev, u_ml, pb, cb, gr, gs, conv_w, conv_b, norm_g, fwd, groups):
    t = u_ml.shape[0]
    tb = TB_SEQ
    t8 = tb // SUBLANES
    nch = tb // CHUNK
    nblk = _num_blocks(groups)
    d = 1 if rev else 0

    def rb(n):
        return (nblk - 1 - n) if rev else n

    def col(g):
        return lambda n: (rb(n), g)

    def halo_prev(g):
        return lambda n: (jnp.maximum(rb(n) * t8 - 1, 0), g)

    def halo_next(g):
        return lambda n: (jnp.minimum((rb(n) + 1) * t8, t // SUBLANES - 1), g)

    blk = (tb, WIDTH)
    halo = (SUBLANES, WIDTH)
    gate_specs = [
        pl.BlockSpec((None, HEADS, tb, LANES), lambda n: (d, 0, rb(n), 0)),
        pl.BlockSpec((None, HEADS, tb, LANES), lambda n: (d, 0, rb(n), 0)),
        pl.BlockSpec((None, HEADS, SUBLANES, tb), lambda n: (d, 0, 0, rb(n))),
        pl.BlockSpec((None, HEADS, nch, SUBLANES, LANES), lambda n: (d, 0, rb(n), 0, 0)),
    ]
    scratch = [
        pltpu.VMEM((HEADS, HEAD_DIM, 2 * HEAD_DIM), F32),
        pltpu.VMEM((HEADS, SUBLANES, LANES), F32),
    ]
    full = jax.ShapeDtypeStruct((t, WIDTH), F32)
    if rev:
        qs, ks, h_fwd = fwd
        in_specs = [pl.BlockSpec(blk, col(0)), pl.BlockSpec(blk, col(0)), pl.BlockSpec(blk, col(2)),
                    *gate_specs,
                    pl.BlockSpec(blk, col(3)), pl.BlockSpec(blk, col(4)), pl.BlockSpec(blk, col(0)),
                    pl.BlockSpec((1, WIDTH), lambda n: (0, 0))]
        args = [qs, ks, u_ml, pb, cb, gr, gs, u_ml, u_ml, h_fwd, norm_g]
        out_specs = pl.BlockSpec(blk, col(0))
        out_shape = jax.ShapeDtypeStruct(full.shape, BF16)
    else:
        in_specs = [pl.BlockSpec(blk, col(0)), pl.BlockSpec(blk, col(1)), pl.BlockSpec(blk, col(2)),
                    pl.BlockSpec(halo, halo_prev(0)), pl.BlockSpec(halo, halo_next(0)),
                    pl.BlockSpec(halo, halo_prev(1)), pl.BlockSpec(halo, halo_next(1)),
                    *gate_specs,
                    pl.BlockSpec((3, 2 * WIDTH), lambda n: (0, 0)),
                    pl.BlockSpec((1, 2 * WIDTH), lambda n: (0, 0))]
        args = [u_ml, u_ml, u_ml, u_ml, u_ml, u_ml, u_ml, pb, cb, gr, gs, conv_w, conv_b]
        out_specs = [pl.BlockSpec(blk, col(0))] * 3
        out_shape = [full, jax.ShapeDtypeStruct((t, WIDTH), BF16), jax.ShapeDtypeStruct((t, WIDTH), BF16)]
        scratch.append(pltpu.VMEM((tb + 2 * SUBLANES, WIDTH), F32))
    return pl.pallas_call(
        functools.partial(_ml_kernel, rev, groups),
        grid=(nblk,),
        in_specs=in_specs,
        out_specs=out_specs,
        out_shape=out_shape,
        scratch_shapes=scratch,
        name="mlstm_bwd" if rev else "mlstm_fwd",
    )(*args)


def _mlstm(u_ml, pb, cb, gr, gs, conv_w, conv_b, norm_g, groups):
    h_fwd, qs, ks = _mlstm_dir(False, u_ml, pb, cb, gr, gs, conv_w, conv_b, norm_g, None, groups)
    return _mlstm_dir(True, u_ml, pb, cb, gr, gs, conv_w, conv_b, norm_g, (qs, ks, h_fwd), groups)


def _rt_kernel(rev, groups, *refs):
    if rev:
        (qs_ref, ks_ref, v_ref, dec_ref, qd_ref, kd_ref, z_ref, hf_ref, ng_ref,
         out_ref, st_ref) = refs
    else:
        (qs_ref, ks_ref, v_ref, dec_ref, qd_ref, kd_ref, out_ref, st_ref) = refs
    nch = TB_SEQ // CHUNK
    n = pl.program_id(0)
    sb, nb = _seq_pos((_num_blocks(groups) - 1 - n) if rev else n, groups)

    @pl.when(sb == (nb - 1 if rev else 0))
    def _():
        st_ref[...] = jnp.zeros_like(st_ref)

    def body(i, carry):
        c = (nch - 1 - i) if rev else i
        sl = pl.ds(pl.multiple_of(c * CHUNK, CHUNK), CHUNK)
        for h in range(HEADS):
            hs = slice(h * HEAD_DIM, (h + 1) * HEAD_DIM)
            q = qs_ref[sl, hs]
            kt = ks_ref[sl, hs].astype(F32).T
            vb = v_ref[sl, hs].astype(BF16)
            s = _dot(q, kt.astype(BF16)) * dec_ref[h]
            out = _dot(s.astype(BF16), vb) + _dot((q.astype(F32) * qd_ref[h]).astype(BF16),
                                                  st_ref[h].astype(BF16))
            st_ref[h] = kd_ref[h, 1:2, :] * st_ref[h] + _dot((kt * kd_ref[h, 0:1, :]).astype(BF16), vb)
            if rev:
                y = _head_norm(out + hf_ref[sl, hs], ng_ref[:, hs]) * _silu(z_ref[sl, hs])
                out_ref[sl, hs] = y.astype(out_ref.dtype)
            else:
                out_ref[sl, hs] = out
        return carry

    lax.fori_loop(0, nch, body, 0, unroll=not rev)


def _retention_dir(rev, u_qk, u_vz, dec, qdec, kdec, norm_g, h_fwd, groups):
    tb = TB_SEQ
    nblk = _num_blocks(groups)
    d = 1 if rev else 0

    def rb(n):
        return (nblk - 1 - n) if rev else n

    def col(g):
        return lambda n: (rb(n), g)

    blk = (tb, WIDTH)
    tbl = lambda n: (d, 0, 0, 0)
    in_specs = [
        pl.BlockSpec(blk, col(0)), pl.BlockSpec(blk, col(1)), pl.BlockSpec(blk, col(0)),
        pl.BlockSpec((None, HEADS, CHUNK, CHUNK), tbl),
        pl.BlockSpec((None, HEADS, CHUNK, HEAD_DIM), tbl),
        pl.BlockSpec((None, HEADS, SUBLANES, CHUNK), tbl),
    ]
    args = [u_qk, u_qk, u_vz, dec, qdec, kdec]
    if rev:
        in_specs += [pl.BlockSpec(blk, col(1)), pl.BlockSpec(blk, col(0)),
                     pl.BlockSpec((1, WIDTH), lambda n: (0, 0))]
        args += [u_vz, h_fwd, norm_g]
    out_shape = jax.ShapeDtypeStruct((u_qk.shape[0], WIDTH), BF16 if rev else F32)
    out_specs = pl.BlockSpec(blk, col(0))
    return pl.pallas_call(
        functools.partial(_rt_kernel, rev, groups),
        grid=(nblk,),
        in_specs=in_specs,
        out_specs=out_specs,
        out_shape=out_shape,
        scratch_shapes=[pltpu.VMEM((HEADS, HEAD_DIM, HEAD_DIM), F32)],
        name="retention_bwd" if rev else "retention_fwd",
    )(*args)


def _retention(u_qk, u_vz, tabs, norm_g, groups):
    dec, qdec, kdec = tabs
    h_fwd = _retention_dir(False, u_qk, u_vz, dec, qdec, kdec, norm_g, None, groups)
    return _retention_dir(True, u_qk, u_vz, dec, qdec, kdec, norm_g, h_fwd, groups)


def _rotary_tables(seq):
    inv = ROPE_BASE ** (-jnp.arange(0, HEAD_DIM, 2, dtype=F32) / HEAD_DIM)
    ang = jnp.arange(seq, dtype=F32)[:, None] * inv[None, :]
    cos, sin = jnp.cos(ang), jnp.sin(ang)
    return jnp.concatenate([cos, cos], axis=1), jnp.concatenate([-sin, sin], axis=1)


def _retention_tables():
    pos = np.arange(CHUNK, dtype=np.float64)
    rel = pos[:, None] - pos[None, :]
    dec = np.zeros((2, HEADS, CHUNK, CHUNK))
    qdec = np.zeros((2, HEADS, CHUNK, HEAD_DIM))
    kdec = np.zeros((2, HEADS, SUBLANES, CHUNK))
    for h in range(HEADS):
        lf, lb = RT_LOG_GAMMA_FWD[h], RT_LOG_GAMMA_BWD[h]
        dec[0, h] = np.where(rel >= 0, np.exp(np.maximum(rel, 0.0) * lf), 0.0)
        dec[1, h] = np.where(rel <= 0, np.exp(np.maximum(-rel, 0.0) * lb), 0.0)
        qdec[0, h] = np.exp((pos + 1.0) * lf)[:, None]
        qdec[1, h] = np.exp((CHUNK - pos) * lb)[:, None]
        kdec[0, h, 0] = np.exp((CHUNK - 1.0 - pos) * lf)
        kdec[1, h, 0] = np.exp(pos * lb)
        kdec[0, h, 1] = np.exp(CHUNK * lf)
        kdec[1, h, 1] = np.exp(CHUNK * lb)
    return jnp.asarray(dec, F32), jnp.asarray(qdec, F32), jnp.asarray(kdec, F32)


def _filt_kernel(ft_ref, fr_ref, f0_ref, w1_ref, b1_ref, w2_ref, b2_ref, fq_ref,
                 w3f_ref, w3b_ref, dlf_ref, dlb_ref, gb_ref, gf_ref, s_ref, hf3_ref, hr3_ref):
    lt = pl.program_id(1)
    fq = fq_ref[...]

    def hidden(ft):
        h1 = jnp.sin(fq * (_dot_f32(w1_ref[...], ft) + b1_ref[...]))
        return jnp.sin(fq * (_dot_f32(w2_ref[...], h1) + b2_ref[...]))

    @pl.when((pl.program_id(2) == 0) & (pl.program_id(3) == 0))
    def _():
        hf3_ref[...] = _split3_rows(hidden(ft_ref[...]))
        hr3_ref[...] = _split3_rows(hidden(fr_ref[...]))

    def lhs3(w):
        hi, lo = _split_hi_lo(w)
        return jnp.concatenate([hi, hi, lo], axis=1)

    ft = ft_ref[...]
    fr = fr_ref[...]
    dl_f = jnp.abs(dlf_ref[...])
    dl_b = jnp.abs(dlb_ref[...])
    fwd = _dot(lhs3(w3f_ref[...]), hf3_ref[...]) * jnp.exp(-ft[0:1, :] * dl_f)
    bwd = _dot(lhs3(w3b_ref[...]), hr3_ref[...]) * jnp.exp(-fr[0:1, :] * dl_b)
    f0 = f0_ref[...]
    bwd0 = (_dot_f32(w3b_ref[...], hidden(f0)) * jnp.exp(-f0[0:1, :] * dl_b))[:, 0:1]
    lag = lax.broadcasted_iota(jnp.int32, fwd.shape, 1) + lt * LT_FILT
    fwd = jnp.where(lag == 0, fwd + bwd0, fwd)
    bwd = jnp.where(lag == 0, 0.0, bwd)
    gb_ref[...] = bwd
    gf_ref[...] = fwd
    part = jnp.sum(jnp.abs(fwd) + jnp.abs(bwd), axis=1, keepdims=True)
    lane = lax.broadcasted_iota(jnp.int32, (LANES, LANES), 1)
    s_ref[...] = jnp.where(lane == 0, part, 0.0)


def _filters(seq, w1t, b1, w2t, b2, freq, w3t, deltas):
    t = jnp.linspace(0.0, 1.0, seq, dtype=F32)[None, :]
    w = (2.0 * math.pi / seq) * jnp.arange(seq, dtype=F32)[None, :]
    bands = jnp.linspace(1e-4, HY_BANDS - 1, HY_BANDS, dtype=F32)[:, None]
    feats = jnp.concatenate([t, jnp.cos(bands * w), -jnp.sin(bands * w),
                             jnp.zeros((FEAT_ROWS - 1 - 2 * HY_BANDS, seq), F32)], axis=0)
    feats_rev = jnp.concatenate([feats[:, :1], feats[:, :0:-1]], axis=1)
    feats0 = jnp.broadcast_to(feats[:, :1], (FEAT_ROWS, LANES))
    nlt = seq // LT_FILT
    ct = WIDTH // LANES
    lyr = lambda l, i, o, c: (l, 0, 0)
    return pl.pallas_call(
        _filt_kernel,
        grid=(DEPTH, nlt, HY_ORDER, ct),
        in_specs=[
            pl.BlockSpec((FEAT_ROWS, LT_FILT), lambda l, i, o, c: (0, i)),
            pl.BlockSpec((FEAT_ROWS, LT_FILT), lambda l, i, o, c: (0, i)),
            pl.BlockSpec((FEAT_ROWS, LANES), lambda l, i, o, c: (0, 0)),
            pl.BlockSpec((None, HY_HIDDEN, FEAT_ROWS), lyr),
            pl.BlockSpec((None, HY_HIDDEN, 1), lyr),
            pl.BlockSpec((None, HY_HIDDEN, HY_HIDDEN), lyr),
            pl.BlockSpec((None, HY_HIDDEN, 1), lyr),
            pl.BlockSpec((None, HY_HIDDEN, 1), lyr),
            pl.BlockSpec((None, LANES, HY_HIDDEN), lambda l, i, o, c: (l, o * 2 * ct + c, 0)),
            pl.BlockSpec((None, LANES, HY_HIDDEN), lambda l, i, o, c: (l, o * 2 * ct + ct + c, 0)),
            pl.BlockSpec((None, LANES, 1), lambda l, i, o, c: (l, o * 2 * ct + c, 0)),
            pl.BlockSpec((None, LANES, 1), lambda l, i, o, c: (l, o * 2 * ct + ct + c, 0)),
        ],
        out_specs=[
            pl.BlockSpec((None, None, LANES, LT_FILT), lambda l, i, o, c: (l, o, c, i)),
            pl.BlockSpec((None, None, LANES, LT_FILT), lambda l, i, o, c: (l, o, c, i)),
            pl.BlockSpec((None, None, LANES, LANES), lambda l, i, o, c: (l, o, c, i)),
        ],
        out_shape=[
            jax.ShapeDtypeStruct((DEPTH, HY_ORDER, WIDTH, seq), F32),
            jax.ShapeDtypeStruct((DEPTH, HY_ORDER, WIDTH, seq), F32),
            jax.ShapeDtypeStruct((DEPTH, HY_ORDER, WIDTH, nlt * LANES), F32),
        ],
        scratch_shapes=[pltpu.VMEM((3 * HY_HIDDEN, LT_FILT), BF16),
                        pltpu.VMEM((3 * HY_HIDDEN, LT_FILT), BF16)],
        name="hyena_filters",
    )(feats, feats_rev, feats0, w1t, b1, w2t, b2, freq, w3t, w3t, deltas, deltas)


def _split_hi_lo(x):
    hi = x.astype(BF16)
    lo = (x - hi.astype(F32)).astype(BF16)
    return hi, lo


def _split3_rows(x):
    hi, lo = _split_hi_lo(x)
    return jnp.concatenate([hi, lo, hi], axis=0)


def _split3_cols(x):
    hi, lo = _split_hi_lo(x)
    return jnp.concatenate([hi, lo, hi], axis=1)


def _np_split(m):
    hi = m.astype(ml_dtypes.bfloat16)
    lo = (m - hi.astype(np.float64)).astype(ml_dtypes.bfloat16)
    return hi, lo


def _dft_tables(seq):
    nj = seq // LANES
    n1 = 2 * nj
    n = n1 * LANES
    khp = n1 // 2 + KH_PAD
    k1 = np.arange(khp, dtype=np.float64)[:, None]
    valid = (k1 <= n1 // 2).astype(np.float64)
    ang = 2.0 * np.pi * k1 * np.arange(n1, dtype=np.float64)[None, :] / n1
    fa_full = np.concatenate([valid * np.cos(ang), -valid * np.sin(ang)], axis=0)

    def lhs3(m):
        hi, lo = _np_split(m)
        return jnp.asarray(np.concatenate([hi, hi, lo], axis=1))

    def rhs3(m):
        hi, lo = _np_split(m)
        return jnp.asarray(np.concatenate([hi, hi, lo], axis=0))

    n2 = np.arange(LANES, dtype=np.float64)
    ang2 = 2.0 * np.pi * n2[:, None] * n2[None, :] / LANES
    c2, s2 = np.cos(ang2), np.sin(ang2)
    fb = np.block([[c2, -s2], [s2, c2]])
    fbi = np.block([[c2, s2], [-s2, c2]])
    w = np.where((k1 == 0) | (k1 == n1 // 2), 1.0, 2.0) * valid
    th = 2.0 * np.pi * np.arange(nj, dtype=np.float64)[:, None] * k1.T / n1
    fai = np.concatenate([(w.T / n) * np.cos(th), -(w.T / n) * np.sin(th)], axis=1)
    phi = 2.0 * np.pi * k1 * n2[None, :] / n
    return dict(
        nj=nj, n1=n1, khp=khp,
        fa=lhs3(fa_full[:, :nj]), fa_full=lhs3(fa_full), fb=rhs3(fb), fbi=rhs3(fbi), fai=lhs3(fai),
        twc=jnp.asarray(valid * np.cos(phi), F32), tws=jnp.asarray(valid * np.sin(phi), F32))


def _dft_rows(fa_ref, twc, tws, khp, cols):
    rhs = cols[0] if len(cols) == 1 else jnp.concatenate(cols, axis=1)
    y = _dot(fa_ref[...], _split3_rows(rhs))
    out = []
    for i in range(len(cols)):
        yre = y[0:khp, i * LANES:(i + 1) * LANES]
        yim = y[khp:2 * khp, i * LANES:(i + 1) * LANES]
        out.append((yre * twc + yim * tws, yim * twc - yre * tws))
    return out


def _spec_kernel(khp, g_ref, s_ref, fa_ref, fb_ref, twc_ref, tws_ref, h_ref, ybuf):
    twc = twc_ref[...]
    tws = tws_ref[...]
    for c in range(0, CB_SPEC, 2):
        res = _dft_rows(fa_ref, twc, tws, khp, [g_ref[c], g_ref[c + 1]])
        for i, (yre, yim) in enumerate(res):
            r0 = (c + i) * khp
            ybuf[r0:r0 + khp, 0:LANES] = yre
            ybuf[r0:r0 + khp, LANES:2 * LANES] = yim
    x = _dot(_split3_cols(ybuf[...]), fb_ref[...])
    for c in range(CB_SPEC):
        norm = jnp.sum(s_ref[c:c + 1, :], axis=1, keepdims=True)
        h_ref[c] = x[c * khp:(c + 1) * khp, :] / norm


def _spectrum(g, s, tab):
    nch = g.shape[0]
    n1, khp = tab["n1"], tab["khp"]
    const = lambda a: pl.BlockSpec(a.shape, lambda i: (0, 0))
    return pl.pallas_call(
        functools.partial(_spec_kernel, khp),
        grid=(nch // CB_SPEC,),
        in_specs=[
            pl.BlockSpec((CB_SPEC, n1, LANES), lambda i: (i, 0, 0)),
            pl.BlockSpec((CB_SPEC, s.shape[-1]), lambda i: (i, 0)),
            const(tab["fa_full"]), const(tab["fb"]), const(tab["twc"]), const(tab["tws"]),
        ],
        out_specs=pl.BlockSpec((CB_SPEC, khp, 2 * LANES), lambda i: (i, 0, 0)),
        out_shape=jax.ShapeDtypeStruct((nch, khp, 2 * LANES), F32),
        scratch_shapes=[pltpu.VMEM((CB_SPEC * khp, 2 * LANES), F32)],
        name="hyena_spectrum",
    )(g, s, tab["fa_full"], tab["fb"], tab["twc"], tab["tws"])


def _hyfft_kernel(bsz, nj, khp, prm_ref, v_ref, x1_ref, x2_ref, z_ref, h_ref,
                  fa_ref, fb_ref, fbi_ref, fai_ref, twc_ref, tws_ref, out_ref,
                  xs_ref, ybuf, pbuf):
    ct = pl.program_id(0)
    rows = bsz * nj
    pairs = [(b, b + 1) for b in range(0, bsz, 2)] if bsz > 1 else [(0,)]
    twc = twc_ref[...]
    tws = tws_ref[...]

    lane = lax.broadcasted_iota(jnp.int32, (rows, LANES), 1)
    blk = lax.broadcasted_iota(jnp.int32, (rows, LANES), 0) & (nj - 1)
    first = (lane == 0) & (blk == 0)
    last = (lane == LANES - 1) & (blk == nj - 1)

    def conv3(x, base):
        r = pltpu.roll(x, 1, 1)
        xm1 = jnp.where(lane == 0, pltpu.roll(r, 1, 0), r)
        xm1 = jnp.where(first, 0.0, xm1)
        l = pltpu.roll(x, LANES - 1, 1)
        xp1 = jnp.where(lane == LANES - 1, pltpu.roll(l, rows - 1, 0), l)
        xp1 = jnp.where(last, 0.0, xp1)
        return prm_ref[base] * xm1 + prm_ref[base + 1] * x + prm_ref[base + 2] * xp1 + prm_ref[base + 3]

    def row0(c, b):
        return (c * bsz + b) * khp

    def forward(c, x):
        for pr in pairs:
            res = _dft_rows(fa_ref, twc, tws, khp, [x[b * nj:(b + 1) * nj, :] for b in pr])
            for b, (yre, yim) in zip(pr, res):
                r0 = row0(c, b)
                ybuf[r0:r0 + khp, 0:LANES] = yre
                ybuf[r0:r0 + khp, LANES:2 * LANES] = yim

    def spectral(order):
        for c0 in range(0, CB_HY, CG_HY):
            g0, g1 = row0(c0, 0), row0(c0 + CG_HY, 0)
            x = _dot(_split3_cols(ybuf[g0:g1, :]), fb_ref[...])
            for c in range(c0, c0 + CG_HY):
                hre = h_ref[order, c, :, 0:LANES]
                him = h_ref[order, c, :, LANES:2 * LANES]
                for b in range(bsz):
                    r0 = row0(c, b)
                    xre = x[r0 - g0:r0 - g0 + khp, 0:LANES]
                    xim = x[r0 - g0:r0 - g0 + khp, LANES:2 * LANES]
                    pbuf[r0:r0 + khp, 0:LANES] = xre * hre - xim * him
                    pbuf[r0:r0 + khp, LANES:2 * LANES] = xre * him + xim * hre
            r = _dot(_split3_cols(pbuf[g0:g1, :]), fbi_ref[...])
            for c in range(c0, c0 + CG_HY):
                for b in range(bsz):
                    r0 = row0(c, b)
                    rre = r[r0 - g0:r0 - g0 + khp, 0:LANES]
                    rim = r[r0 - g0:r0 - g0 + khp, LANES:2 * LANES]
                    ybuf[r0:r0 + khp, 0:LANES] = rre * twc - rim * tws
                    ybuf[r0:r0 + khp, LANES:2 * LANES] = rre * tws + rim * twc

    def inverse(c):
        outs = [None] * bsz
        for pr in pairs:
            rre = [ybuf[row0(c, b):row0(c, b) + khp, 0:LANES] for b in pr]
            rim = [ybuf[row0(c, b):row0(c, b) + khp, LANES:2 * LANES] for b in pr]
            if len(pr) > 1:
                rre, rim = [jnp.concatenate(rre, axis=1)], [jnp.concatenate(rim, axis=1)]
            y = _dot(fai_ref[...], _split3_rows(jnp.concatenate([rre[0], rim[0]], axis=0)))
            for i, b in enumerate(pr):
                outs[b] = y[:, i * LANES:(i + 1) * LANES]
        return outs[0] if bsz == 1 else jnp.concatenate(outs, axis=0)

    for c in range(CB_HY):
        base = (ct * CB_HY + c) * HY_PRM
        v = conv3(v_ref[c], base)
        xs_ref[0, c] = v
        xs_ref[1, c] = conv3(x1_ref[c], base + 4)
        xs_ref[2, c] = conv3(x2_ref[c], base + 8)
        forward(c, v)
    spectral(0)
    for c in range(CB_HY):
        base = (ct * CB_HY + c) * HY_PRM
        v = xs_ref[0, c]
        z1 = xs_ref[1, c] * (inverse(c) + prm_ref[base + 12] * v)
        xs_ref[0, c] = z1
        forward(c, z1)
    spectral(1)
    for c in range(CB_HY):
        base = (ct * CB_HY + c) * HY_PRM
        z1 = xs_ref[0, c]
        y2 = inverse(c) + prm_ref[base + 13] * z1
        out_ref[:, c, :] = xs_ref[2, c] * y2 * _silu(z_ref[c])


def _hyena_fft(prm, u_t, h, layer, tab, row0, bsz, seq):
    nj, khp = tab["nj"], tab["khp"]
    nct = WIDTH // CB_HY
    rows = bsz * nj
    blk = (CB_HY, rows, LANES)
    rblk = row0 // (rows * LANES)
    assert rblk * rows * LANES == row0
    const = lambda a: pl.BlockSpec(a.shape, lambda i: (0, 0))
    return pl.pallas_call(
        functools.partial(_hyfft_kernel, bsz, nj, khp),
        grid=(nct,),
        in_specs=[
            pl.BlockSpec(memory_space=pltpu.SMEM),
            pl.BlockSpec(blk, lambda i: (i, rblk, 0)),
            pl.BlockSpec(blk, lambda i: (nct + i, rblk, 0)),
            pl.BlockSpec(blk, lambda i: (2 * nct + i, rblk, 0)),
            pl.BlockSpec(blk, lambda i: (3 * nct + i, rblk, 0)),
            pl.BlockSpec((None, HY_ORDER, CB_HY, khp, 2 * LANES), lambda i: (layer, 0, i, 0, 0)),
            const(tab["fa"]), const(tab["fb"]), const(tab["fbi"]), const(tab["fai"]),
            const(tab["twc"]), const(tab["tws"]),
        ],
        out_specs=pl.BlockSpec((rows, CB_HY, LANES), lambda i: (0, i, 0)),
        out_shape=jax.ShapeDtypeStruct((rows, WIDTH, LANES), F32),
        scratch_shapes=[
            pltpu.VMEM((3, CB_HY, bsz * nj, LANES), F32),
            pltpu.VMEM((CB_HY * bsz * khp, 2 * LANES), F32),
            pltpu.VMEM((CB_HY * bsz * khp, 2 * LANES), F32),
        ],
        name="hyena_fft",
    )(prm, u_t, u_t, u_t, u_t, h, tab["fa"], tab["fb"], tab["fbi"], tab["fai"],
      tab["twc"], tab["tws"])


def _out_kernel(final, starts, nx, *refs):
    ng = len(starts)
    x_refs = refs[0:nx]
    yml_ref, yrt_ref = refs[nx:nx + 2]
    yhy_refs = refs[nx + 2:nx + 2 + ng]
    w_ref, g_ref = refs[nx + 2 + ng:nx + 4 + ng]
    out_refs = refs[nx + 4 + ng:-1]
    lhs_ref = refs[-1]
    i = pl.program_id(0)
    lhs_ref[:, 0:WIDTH] = yml_ref[...]
    lhs_ref[:, WIDTH:2 * WIDTH] = yrt_ref[...]
    jb = TM_OUT // LANES

    def fill_hyena(ref):
        for j in range(jb):
            lhs_ref[j * LANES:(j + 1) * LANES, 2 * WIDTH:] = ref[j].T.astype(BF16)

    for gi in range(ng):
        inside = i >= starts[gi]
        if gi + 1 < ng:
            inside = inside & (i < starts[gi + 1])
        pl.when(inside)(functools.partial(fill_hyena, yhy_refs[gi]))
    x = _x_tile(x_refs, starts if nx > 1 else [0]) + _dot(lhs_ref[...], w_ref[...])
    if not final:
        out_refs[0][...] = x
        return
    ms = jnp.mean(x * x, axis=-1, keepdims=True)
    x = x * lax.rsqrt(ms + RMS_EPS) * g_ref[...]
    for gi in range(ng):
        inside = i >= starts[gi]
        if gi + 1 < ng:
            inside = inside & (i < starts[gi + 1])

        @pl.when(inside)
        def _(gi=gi):
            out_refs[gi][...] = x


def _out_proj(xs, y_ml, y_rt, y_hy, w, gain, final, groups):
    t = y_ml.shape[0]
    jb = TM_OUT // LANES
    starts = [r0 // TM_OUT for (r0, _, _) in groups]
    counts = [bsz * seq // TM_OUT for (_, bsz, seq) in groups]

    def tile(gi):
        return lambda i: jnp.clip(i - starts[gi], 0, counts[gi] - 1)

    hy_specs = [pl.BlockSpec((jb, WIDTH, LANES), lambda i, f=tile(gi): (f(i), 0, 0))
                for gi in range(len(groups))]
    if final:
        out_specs = [pl.BlockSpec((TM_OUT, D_MODEL), lambda i, f=tile(gi): (f(i), 0))
                     for gi in range(len(groups))]
        out_shape = [jax.ShapeDtypeStruct((bsz * seq, D_MODEL), F32) for (_, bsz, seq) in groups]
    else:
        out_specs = [pl.BlockSpec((TM_OUT, D_MODEL), lambda i: (i, 0))]
        out_shape = [jax.ShapeDtypeStruct((t, D_MODEL), F32)]
    return pl.pallas_call(
        functools.partial(_out_kernel, final, starts, len(xs)),
        grid=(t // TM_OUT,),
        in_specs=[
            *_x_specs(xs, TM_OUT, 1),
            pl.BlockSpec((TM_OUT, WIDTH), lambda i: (i, 0)),
            pl.BlockSpec((TM_OUT, WIDTH), lambda i: (i, 0)),
            *hy_specs,
            pl.BlockSpec((3 * WIDTH, D_MODEL), lambda i: (0, 0)),
            pl.BlockSpec((1, D_MODEL), lambda i: (0, 0)),
        ],
        out_specs=out_specs,
        out_shape=out_shape,
        scratch_shapes=[pltpu.VMEM((TM_OUT, 3 * WIDTH), BF16)],
        name="out_proj",
    )(*xs, y_ml, y_rt, *y_hy, w, gain)


def kernel(x_prompt, x_sample, norm_g, w_in, ml_conv_w, ml_conv_b, ml_gate_b, ml_norm_g, rt_norm_g, hy_conv_w, hy_conv_b, hy_w1, hy_b1, hy_w2, hy_b2, hy_w3, hy_freq, hy_deltas, hy_skip, w_out, final_g):
    groups = []
    row0 = 0
    for xg in (x_prompt, x_sample):
        bsz, seq, _ = xg.shape
        groups.append((row0, bsz, seq))
        row0 += bsz * seq
    xs = [x_prompt.reshape(-1, D_MODEL), x_sample.reshape(-1, D_MODEL)]

    c_gate = ML_COLS
    c_rt = c_gate + 4 * HEADS
    c_hy = c_rt + RT_COLS
    gate_pad = jnp.zeros((DEPTH, D_MODEL, LANES - 4 * HEADS), F32)
    w_tok = jnp.concatenate([w_in[:, :, :c_gate], w_in[:, :, c_gate:c_rt], gate_pad,
                             w_in[:, :, c_rt:c_hy]], axis=2).astype(BF16)
    w_gt = jnp.swapaxes(w_in[:, :, c_gate:c_rt], 1, 2).astype(BF16)
    w_hyt = jnp.swapaxes(w_in[:, :, c_hy:], 1, 2).astype(BF16)
    w_out_b = w_out.astype(BF16)
    gate_b = ml_gate_b.reshape(DEPTH, 1, 4 * HEADS)
    gate_b_row = jnp.concatenate([gate_b, jnp.zeros((DEPTH, 1, LANES - 4 * HEADS), F32)], axis=2)
    gate_b_col = ml_gate_b.reshape(DEPTH, 4 * HEADS, 1)

    cw = hy_conv_w.reshape(DEPTH, 3, 3, WIDTH)
    cbias = hy_conv_b.reshape(DEPTH, 1, 3, WIDTH)
    taps = jnp.concatenate([cw, cbias], axis=1)
    taps = jnp.transpose(taps, (0, 3, 2, 1)).reshape(DEPTH, WIDTH, 12)
    hy_prm = jnp.concatenate([taps, jnp.transpose(hy_skip, (0, 2, 1)),
                              jnp.zeros((DEPTH, WIDTH, HY_PRM - 14), F32)],
                             axis=2).reshape(DEPTH, WIDTH * HY_PRM)

    w1t = jnp.concatenate([jnp.swapaxes(hy_w1, 1, 2),
                           jnp.zeros((DEPTH, HY_HIDDEN, FEAT_ROWS - hy_w1.shape[1]), F32)], axis=2)
    w2t = jnp.swapaxes(hy_w2, 1, 2)
    w3t = jnp.swapaxes(hy_w3, 1, 2)
    b1 = hy_b1[:, :, None]
    b2 = hy_b2[:, :, None]
    freq = hy_freq[:, :, None]
    deltas = hy_deltas.reshape(DEPTH, HY_ORDER * 2 * WIDTH, 1)

    tables, spectra = [], []
    for (_, _, seq) in groups:
        tab = _dft_tables(seq)
        gb, gf, s = _filters(seq, w1t, b1, w2t, b2, freq, w3t, deltas)
        g = jnp.concatenate([gf, gb], axis=-1).reshape(DEPTH * HY_ORDER * WIDTH, tab["n1"], LANES)
        h = _spectrum(g, s.reshape(DEPTH * HY_ORDER * WIDTH, -1), tab)
        tables.append(tab)
        spectra.append(h.reshape(DEPTH, HY_ORDER, WIDTH, tab["khp"], 2 * LANES))
    rot_c, rot_s = _rotary_tables(max(seq for (_, _, seq) in groups))
    rt_tabs = _retention_tables()

    for layer in range(DEPTH):
        gain = norm_g[layer][None, :]
        u_ml, u_g, u_qk, u_vz, u_gt = _in_proj(xs, gain, w_tok[layer], w_gt[layer], rot_c, rot_s, groups)
        u_hyt = _in_proj_hy(xs, gain, w_hyt[layer])
        pb, cb, gr, gs = _gates(u_g, u_gt, gate_b_row[layer], gate_b_col[layer])
        y_ml = _mlstm(u_ml, pb, cb, gr, gs, ml_conv_w[layer], ml_conv_b[layer][None, :],
                      ml_norm_g[layer][None, :], groups)
        y_rt = _retention(u_qk, u_vz, rt_tabs, rt_norm_g[layer][None, :], groups)
        y_hy = [_hyena_fft(hy_prm[layer], u_hyt, spectra[gi], layer, tables[gi], r0, bsz, seq)
                for gi, (r0, bsz, seq) in enumerate(groups)]
        outs = _out_proj(xs, y_ml, y_rt, y_hy, w_out_b[layer], final_g[None, :],
                         layer == DEPTH - 1, groups)
        xs = [outs[0]]

    return outs[0].reshape(x_prompt.shape), outs[1].reshape(x_sample.shape)
```

```python
import functools
import math

import ml_dtypes
import numpy as np
import jax
import jax.numpy as jnp
from jax import lax
from jax.experimental import pallas as pl
from jax.experimental.pallas import tpu as pltpu

F32 = jnp.float32
BF16 = jnp.bfloat16

D_MODEL = 1024
DEPTH = 4
CHUNK = 128
HEADS = 4
HEAD_DIM = 128
WIDTH = 512
HY_ORDER = 2
HY_BANDS = 16
HY_HIDDEN = 64
FEAT_ROWS = 40
HY_PRM = 16
ROPE_BASE = 10000.0
RMS_EPS = 1e-6
HEAD_NORM_EPS = 1e-5
M_INIT = -1e30
RT_LOG_GAMMA_FWD = tuple(math.log(1.0 - 2.0 ** (-5.0 - h)) for h in range(HEADS))
RT_LOG_GAMMA_BWD = tuple(math.log(1.0 - 2.0 ** (-5.5 - h)) for h in range(HEADS))

LANES = 128
SUBLANES = 8
TM_IN = 256
TM_HY = 1024
TR_HY = 1024
TM_OUT = 1024
TB_SEQ = 1024
TG = 1024
CB_HY = 8
CG_HY = 2
CB_SPEC = 8
KH_PAD = 8
LT_FILT = 2048
ML_COLS = 5 * WIDTH
RT_COLS = 4 * WIDTH
HY_ROWS = 4 * WIDTH
HIGHEST = lax.Precision.HIGHEST


def _dot(a, b):
    return jnp.dot(a, b, preferred_element_type=F32)


def _dot_nt(a, b):
    return lax.dot_general(a, b, (((1,), (1,)), ((), ())), preferred_element_type=F32)


def _dot_tn(a, b):
    return lax.dot_general(a, b, (((0,), (0,)), ((), ())), preferred_element_type=F32)


def _dot_f32(a, b):
    return jnp.dot(a, b, precision=HIGHEST, preferred_element_type=F32)


def _silu(y):
    return y * jax.nn.sigmoid(y)


def _log_sigmoid(x):
    return -(jnp.maximum(-x, 0.0) + jnp.log1p(jnp.exp(-jnp.abs(x))))


def _head_norm(h, gain):
    mu = jnp.mean(h, axis=-1, keepdims=True)
    hc = h - mu
    var = jnp.mean(hc * hc, axis=-1, keepdims=True)
    return hc * lax.rsqrt(var + HEAD_NORM_EPS) * gain


def _tile_starts(xs, tm):
    starts, s = [], 0
    for a in xs:
        starts.append(s)
        s += a.shape[0] // tm
    return starts, s


def _x_specs(xs, tm, ngrid):
    starts, _ = _tile_starts(xs, tm)
    specs = []
    for a, s in zip(xs, starts):
        cnt = a.shape[0] // tm
        if ngrid == 1:
            specs.append(pl.BlockSpec((tm, D_MODEL), lambda i, s=s, cnt=cnt: (jnp.clip(i - s, 0, cnt - 1), 0)))
        else:
            specs.append(pl.BlockSpec((tm, D_MODEL), lambda i, r, s=s, cnt=cnt: (jnp.clip(i - s, 0, cnt - 1), 0)))
    return specs


def _x_tile(x_refs, starts):
    x = x_refs[0][...]
    for ref, s in zip(x_refs[1:], starts[1:]):
        x = jnp.where(pl.program_id(0) >= s, ref[...], x)
    return x


def _rms_bf16(x, g_ref):
    ms = jnp.mean(x * x, axis=-1, keepdims=True)
    return (x * lax.rsqrt(ms + RMS_EPS) * g_ref[...]).astype(BF16)


def _in_kernel(starts, *refs):
    nx = len(starts)
    (g_ref, wtok_ref, wgt_ref, cc_ref, ss_ref,
     oml_ref, og_ref, oqk_ref, ovz_ref, ogt_ref) = refs[nx:]
    h = _rms_bf16(_x_tile(refs[:nx], starts), g_ref)
    y = _dot(h, wtok_ref[...])
    oml_ref[...] = y[:, 0:ML_COLS]
    og_ref[...] = y[:, ML_COLS:ML_COLS + LANES]
    c0 = ML_COLS + LANES
    cc = cc_ref[...]
    ss = ss_ref[...]
    for j in range(2 * HEADS):
        u = y[:, c0 + j * HEAD_DIM:c0 + (j + 1) * HEAD_DIM]
        r = u * cc + pltpu.roll(u, HEAD_DIM // 2, 1) * ss
        if j < HEADS:
            r = r * (HEAD_DIM ** -0.5)
        oqk_ref[:, j * HEAD_DIM:(j + 1) * HEAD_DIM] = r.astype(BF16)
    ovz_ref[...] = y[:, c0 + 2 * WIDTH:]
    ogt_ref[...] = _dot_nt(wgt_ref[...], h)


def _in_proj(xs, gain, w_tok, w_gt, rot_c, rot_s, groups):
    starts, nt = _tile_starts(xs, TM_IN)
    t = nt * TM_IN
    ncol = w_tok.shape[1]
    pos = lambda i: (_seq_pos(i, groups, TM_IN)[0], 0)
    return pl.pallas_call(
        functools.partial(_in_kernel, starts),
        grid=(nt,),
        in_specs=[
            *_x_specs(xs, TM_IN, 1),
            pl.BlockSpec((1, D_MODEL), lambda i: (0, 0)),
            pl.BlockSpec((D_MODEL, ncol), lambda i: (0, 0)),
            pl.BlockSpec((4 * HEADS, D_MODEL), lambda i: (0, 0)),
            pl.BlockSpec((TM_IN, HEAD_DIM), pos),
            pl.BlockSpec((TM_IN, HEAD_DIM), pos),
        ],
        out_specs=[
            pl.BlockSpec((TM_IN, ML_COLS), lambda i: (i, 0)),
            pl.BlockSpec((TM_IN, LANES), lambda i: (i, 0)),
            pl.BlockSpec((TM_IN, 2 * WIDTH), lambda i: (i, 0)),
            pl.BlockSpec((TM_IN, 2 * WIDTH), lambda i: (i, 0)),
            pl.BlockSpec((4 * HEADS, TM_IN), lambda i: (0, i)),
        ],
        out_shape=[
            jax.ShapeDtypeStruct((t, ML_COLS), F32),
            jax.ShapeDtypeStruct((t, LANES), F32),
            jax.ShapeDtypeStruct((t, 2 * WIDTH), BF16),
            jax.ShapeDtypeStruct((t, 2 * WIDTH), F32),
            jax.ShapeDtypeStruct((4 * HEADS, t), F32),
        ],
        name="in_proj",
    )(*xs, gain, w_tok, w_gt, rot_c, rot_s)


def _in_hy_kernel(starts, *refs):
    nx = len(starts)
    g_ref, w_ref, o_ref = refs[nx:]
    y = _dot_nt(w_ref[...], _rms_bf16(_x_tile(refs[:nx], starts), g_ref))
    o_ref[...] = y.reshape(o_ref.shape)


def _in_proj_hy(xs, gain, w_hyt):
    starts, nt = _tile_starts(xs, TM_HY)
    t = nt * TM_HY
    nr = HY_ROWS // TR_HY
    return pl.pallas_call(
        functools.partial(_in_hy_kernel, starts),
        grid=(nt, nr),
        in_specs=[
            *_x_specs(xs, TM_HY, 2),
            pl.BlockSpec((1, D_MODEL), lambda i, r: (0, 0)),
            pl.BlockSpec((TR_HY, D_MODEL), lambda i, r: (r, 0)),
        ],
        out_specs=pl.BlockSpec((TR_HY, TM_HY // LANES, LANES), lambda i, r: (r, i, 0)),
        out_shape=jax.ShapeDtypeStruct((HY_ROWS, t // LANES, LANES), F32),
        name="in_proj_hyena",
    )(*xs, gain, w_hyt)


def _gates_kernel(g_ref, gt_ref, b_ref, bt_ref, pb_ref, cb_ref, gr_ref, gs_ref):
    row = lax.broadcasted_iota(jnp.int32, (CHUNK, CHUNK), 0)
    col = lax.broadcasted_iota(jnp.int32, (CHUNK, CHUNK), 1)
    lower = (row >= col).astype(F32)
    upper = (row <= col).astype(F32)
    gr_ref[...] = jnp.zeros_like(gr_ref)
    gs_ref[...] = jnp.zeros_like(gs_ref)
    for c in range(TG // CHUNK):
        sl = slice(c * CHUNK, (c + 1) * CHUNK)
        lf = _log_sigmoid(g_ref[sl, :] + b_ref[...])
        cum_c = (_dot_f32(lower, lf), _dot_f32(upper, lf))
        pre_t = gt_ref[:, sl] + bt_ref[...]
        lf_t = _log_sigmoid(pre_t)
        cum_r = (_dot_f32(lf_t, upper), _dot_f32(lf_t, lower))
        for d in range(2):
            for h in range(HEADS):
                kf = (2 * d + 1) * HEADS + h
                ki = 2 * d * HEADS + h
                cum = cum_r[d][kf:kf + 1, :]
                li = pre_t[ki:ki + 1, :]
                r = li - cum
                f_tot = cum[:, 0:1] if d else cum[:, CHUNK - 1:CHUNK]
                g = f_tot - cum + li
                allowed = (row <= col) if d else (row >= col)
                pmax = jnp.max(jnp.where(allowed, r, -jnp.inf), axis=1, keepdims=True)
                pb_ref[d, h, sl, :] = jnp.broadcast_to(pmax, (CHUNK, LANES))
                cb_ref[d, h, sl, :] = jnp.broadcast_to(cum_c[d][:, kf:kf + 1], (CHUNK, LANES))
                gr_ref[d, h, 0:1, sl] = r
                gr_ref[d, h, 1:2, sl] = g
                gs_ref[d, h, c, 0:1, :] = jnp.broadcast_to(f_tot, (1, LANES))
                gs_ref[d, h, c, 1:2, :] = jnp.broadcast_to(jnp.max(g, axis=1, keepdims=True), (1, LANES))


def _gates(g, gt, bias_row, bias_col):
    t = g.shape[0]
    nc = TG // CHUNK
    return pl.pallas_call(
        _gates_kernel,
        grid=(t // TG,),
        in_specs=[
            pl.BlockSpec((TG, LANES), lambda i: (i, 0)),
            pl.BlockSpec((4 * HEADS, TG), lambda i: (0, i)),
            pl.BlockSpec((1, LANES), lambda i: (0, 0)),
            pl.BlockSpec((4 * HEADS, 1), lambda i: (0, 0)),
        ],
        out_specs=[
            pl.BlockSpec((2, HEADS, TG, LANES), lambda i: (0, 0, i, 0)),
            pl.BlockSpec((2, HEADS, TG, LANES), lambda i: (0, 0, i, 0)),
            pl.BlockSpec((2, HEADS, SUBLANES, TG), lambda i: (0, 0, 0, i)),
            pl.BlockSpec((2, HEADS, nc, SUBLANES, LANES), lambda i: (0, 0, i, 0, 0)),
        ],
        out_shape=[
            jax.ShapeDtypeStruct((2, HEADS, t, LANES), F32),
            jax.ShapeDtypeStruct((2, HEADS, t, LANES), F32),
            jax.ShapeDtypeStruct((2, HEADS, SUBLANES, t), F32),
            jax.ShapeDtypeStruct((2, HEADS, t // CHUNK, SUBLANES, LANES), F32),
        ],
        name="gates",
    )(g, gt, bias_row, bias_col)


def _seq_pos(g, groups, tb=TB_SEQ):
    pos, nbs = None, None
    start = 0
    for (_, bsz, seq) in groups:
        nb = seq // tb
        p = (g - start) % nb
        pos = p if pos is None else jnp.where(g >= start, p, pos)
        nbs = nb if nbs is None else jnp.where(g >= start, nb, nbs)
        start += bsz * nb
    return pos, nbs


def _num_blocks(groups):
    return sum(bsz * seq // TB_SEQ for (_, bsz, seq) in groups)


def _ml_kernel(rev, groups, *refs):
    if rev:
        (qs_ref, ks_ref, v_ref, pb_ref, cb_ref, gr_ref, gs_ref, o_ref, z_ref, hf_ref, ng_ref,
         out_ref, st_ref, m_ref) = refs
    else:
        (q_ref, k_ref, v_ref, qp_ref, qn_ref, kp_ref, kn_ref, pb_ref, cb_ref, gr_ref, gs_ref,
         cw_ref, cbias_ref, out_ref, qs_ref, ks_ref, st_ref, m_ref, stg_ref) = refs
    tb = TB_SEQ
    nch = tb // CHUNK
    n = pl.program_id(0)
    sb, nb = _seq_pos((_num_blocks(groups) - 1 - n) if rev else n, groups)

    @pl.when(sb == (nb - 1 if rev else 0))
    def _():
        st_ref[...] = jnp.zeros_like(st_ref)
        m_ref[...] = jnp.full_like(m_ref, M_INIT)

    def conv_silu(raw_ref, prev_ref, next_ref, c0):
        stg_ref[SUBLANES:SUBLANES + tb, :] = raw_ref[...]
        stg_ref[SUBLANES - 1:SUBLANES, :] = jnp.where(sb > 0, prev_ref[SUBLANES - 1:SUBLANES, :], 0.0)
        stg_ref[SUBLANES + tb:SUBLANES + tb + 1, :] = jnp.where(sb < nb - 1, next_ref[0:1, :], 0.0)
        w = cw_ref[:, c0:c0 + WIDTH]
        y = (stg_ref[SUBLANES - 1:SUBLANES - 1 + tb, :] * w[0:1, :]
             + stg_ref[SUBLANES:SUBLANES + tb, :] * w[1:2, :]
             + stg_ref[SUBLANES + 1:SUBLANES + 1 + tb, :] * w[2:3, :] + cbias_ref[:, c0:c0 + WIDTH])
        return _silu(y)

    if not rev:
        qs_ref[...] = conv_silu(q_ref, qp_ref, qn_ref, 0).astype(BF16)
        ks_ref[...] = (conv_silu(k_ref, kp_ref, kn_ref, WIDTH) * (HEAD_DIM ** -0.5)).astype(BF16)

    row = lax.broadcasted_iota(jnp.int32, (CHUNK, CHUNK), 0)
    col = lax.broadcasted_iota(jnp.int32, (CHUNK, CHUNK), 1)
    allowed = (row <= col) if rev else (row >= col)
    ones = jnp.ones((CHUNK, HEAD_DIM), BF16)

    def chunk(c, h):
        sl = pl.ds(pl.multiple_of(c * CHUNK, CHUNK), CHUNK)
        hs = slice(h * HEAD_DIM, (h + 1) * HEAD_DIM)
        q = qs_ref[sl, hs]
        kt = ks_ref[sl, hs].astype(F32).T
        v2 = jnp.concatenate([v_ref[sl, hs].astype(BF16), ones], axis=1)
        m_prev = m_ref[h, 0:1, :]
        mb = jnp.maximum(pb_ref[h, sl, :], m_prev)
        p = jnp.exp(jnp.where(allowed, gr_ref[h, 0:1, sl] - mb, -jnp.inf))
        w_inter = jnp.exp(m_prev - mb)
        floor = jnp.exp(-(cb_ref[h, sl, :] + mb))
        s = _dot(q, kt.astype(BF16)) * p
        inter = _dot(q, st_ref[h].astype(BF16))
        num2 = _dot(s.astype(BF16), v2) + jnp.concatenate([w_inter, w_inter], axis=1) * inter
        out = num2[:, 0:HEAD_DIM] / jnp.maximum(jnp.abs(num2[:, HEAD_DIM:]), floor)
        f_tot = gs_ref[h, c, 0:1, :]
        m_chunk = gs_ref[h, c, 1:2, :]
        m_new = jnp.maximum(f_tot + m_prev, m_chunk)
        kw = (kt * jnp.exp(gr_ref[h, 1:2, sl] - m_new)).astype(BF16)
        kv2 = _dot(kw, v2)
        a = jnp.exp(f_tot + m_prev - m_new)
        st_ref[h] = jnp.concatenate([a, a], axis=1) * st_ref[h] + kv2
        m_ref[h] = jnp.broadcast_to(m_new, (SUBLANES, LANES))
        return out, sl, hs

    def body(i, carry):
        c = (nch - 1 - i) if rev else i
        for h in range(HEADS):
            out, sl, hs = chunk(c, h)
            if rev:
                hh = jax.nn.sigmoid(o_ref[sl, hs]) * (out + hf_ref[sl, hs])
                y = _head_norm(hh, ng_ref[:, hs]) * _silu(z_ref[sl, hs])
                out_ref[sl, hs] = y.astype(out_ref.dtype)
            else:
                out_ref[sl, hs] = out
        return carry

    lax.fori_loop(0, nch, body, 0, unroll=True)


def _mlstm_dir(rev, u_ml, pb, cb, gr, gs, conv_w, conv_b, norm_g, fwd, groups):
    t = u_ml.shape[0]
    tb = TB_SEQ
    t8 = tb // SUBLANES
    nch = tb // CHUNK
    nblk = _num_blocks(groups)
    d = 1 if rev else 0

    def rb(n):
        return (nblk - 1 - n) if rev else n

    def col(g):
        return lambda n: (rb(n), g)

    def halo_prev(g):
        return lambda n: (jnp.maximum(rb(n) * t8 - 1, 0), g)

    def halo_next(g):
        return lambda n: (jnp.minimum((rb(n) + 1) * t8, t // SUBLANES - 1), g)

    blk = (tb, WIDTH)
    halo = (SUBLANES, WIDTH)
    gate_specs = [
        pl.BlockSpec((None, HEADS, tb, LANES), lambda n: (d, 0, rb(n), 0)),
        pl.BlockSpec((None, HEADS, tb, LANES), lambda n: (d, 0, rb(n), 0)),
        pl.BlockSpec((None, HEADS, SUBLANES, tb), lambda n: (d, 0, 0, rb(n))),
        pl.BlockSpec((None, HEADS, nch, SUBLANES, LANES), lambda n: (d, 0, rb(n), 0, 0)),
    ]
    scratch = [
        pltpu.VMEM((HEADS, HEAD_DIM, 2 * HEAD_DIM), F32),
        pltpu.VMEM((HEADS, SUBLANES, LANES), F32),
    ]
    full = jax.ShapeDtypeStruct((t, WIDTH), F32)
    if rev:
        qs, ks, h_fwd = fwd
        in_specs = [pl.BlockSpec(blk, col(0)), pl.BlockSpec(blk, col(0)), pl.BlockSpec(blk, col(2)),
                    *gate_specs,
                    pl.BlockSpec(blk, col(3)), pl.BlockSpec(blk, col(4)), pl.BlockSpec(blk, col(0)),
                    pl.BlockSpec((1, WIDTH), lambda n: (0, 0))]
        args = [qs, ks, u_ml, pb, cb, gr, gs, u_ml, u_ml, h_fwd, norm_g]
        out_specs = pl.BlockSpec(blk, col(0))
        out_shape = jax.ShapeDtypeStruct(full.shape, BF16)
    else:
        in_specs = [pl.BlockSpec(blk, col(0)), pl.BlockSpec(blk, col(1)), pl.BlockSpec(blk, col(2)),
                    pl.BlockSpec(halo, halo_prev(0)), pl.BlockSpec(halo, halo_next(0)),
                    pl.BlockSpec(halo, halo_prev(1)), pl.BlockSpec(halo, halo_next(1)),
                    *gate_specs,
                    pl.BlockSpec((3, 2 * WIDTH), lambda n: (0, 0)),
                    pl.BlockSpec((1, 2 * WIDTH), lambda n: (0, 0))]
        args = [u_ml, u_ml, u_ml, u_ml, u_ml, u_ml, u_ml, pb, cb, gr, gs, conv_w, conv_b]
        out_specs = [pl.BlockSpec(blk, col(0))] * 3
        out_shape = [full, jax.ShapeDtypeStruct((t, WIDTH), BF16), jax.ShapeDtypeStruct((t, WIDTH), BF16)]
        scratch.append(pltpu.VMEM((tb + 2 * SUBLANES, WIDTH), F32))
    return pl.pallas_call(
        functools.partial(_ml_kernel, rev, groups),
        grid=(nblk,),
        in_specs=in_specs,
        out_specs=out_specs,
        out_shape=out_shape,
        scratch_shapes=scratch,
        name="mlstm_bwd" if rev else "mlstm_fwd",
    )(*args)


def _mlstm(u_ml, pb, cb, gr, gs, conv_w, conv_b, norm_g, groups):
    h_fwd, qs, ks = _mlstm_dir(False, u_ml, pb, cb, gr, gs, conv_w, conv_b, norm_g, None, groups)
    return _mlstm_dir(True, u_ml, pb, cb, gr, gs, conv_w, conv_b, norm_g, (qs, ks, h_fwd), groups)


def _rt_kernel(rev, groups, *refs):
    if rev:
        (qs_ref, ks_ref, v_ref, dec_ref, qd_ref, kd_ref, z_ref, hf_ref, ng_ref,
         out_ref, st_ref) = refs
    else:
        (qs_ref, ks_ref, v_ref, dec_ref, qd_ref, kd_ref, out_ref, st_ref) = refs
    nch = TB_SEQ // CHUNK
    n = pl.program_id(0)
    sb, nb = _seq_pos((_num_blocks(groups) - 1 - n) if rev else n, groups)

    @pl.when(sb == (nb - 1 if rev else 0))
    def _():
        st_ref[...] = jnp.zeros_like(st_ref)

    def body(i, carry):
        c = (nch - 1 - i) if rev else i
        sl = pl.ds(pl.multiple_of(c * CHUNK, CHUNK), CHUNK)
        for h in range(HEADS):
            hs = slice(h * HEAD_DIM, (h + 1) * HEAD_DIM)
            q = qs_ref[sl, hs]
            kt = ks_ref[sl, hs].astype(F32).T
            vb = v_ref[sl, hs].astype(BF16)
            s = _dot(q, kt.astype(BF16)) * dec_ref[h]
            out = _dot(s.astype(BF16), vb) + _dot((q.astype(F32) * qd_ref[h]).astype(BF16),
                                                  st_ref[h].astype(BF16))
            st_ref[h] = kd_ref[h, 1:2, :] * st_ref[h] + _dot((kt * kd_ref[h, 0:1, :]).astype(BF16), vb)
            if rev:
                y = _head_norm(out + hf_ref[sl, hs], ng_ref[:, hs]) * _silu(z_ref[sl, hs])
                out_ref[sl, hs] = y.astype(out_ref.dtype)
            else:
                out_ref[sl, hs] = out
        return carry

    lax.fori_loop(0, nch, body, 0, unroll=not rev)


def _retention_dir(rev, u_qk, u_vz, dec, qdec, kdec, norm_g, h_fwd, groups):
    tb = TB_SEQ
    nblk = _num_blocks(groups)
    d = 1 if rev else 0

    def rb(n):
        return (nblk - 1 - n) if rev else n

    def col(g):
        return lambda n: (rb(n), g)

    blk = (tb, WIDTH)
    tbl = lambda n: (d, 0, 0, 0)
    in_specs = [
        pl.BlockSpec(blk, col(0)), pl.BlockSpec(blk, col(1)), pl.BlockSpec(blk, col(0)),
        pl.BlockSpec((None, HEADS, CHUNK, CHUNK), tbl),
        pl.BlockSpec((None, HEADS, CHUNK, HEAD_DIM), tbl),
        pl.BlockSpec((None, HEADS, SUBLANES, CHUNK), tbl),
    ]
    args = [u_qk, u_qk, u_vz, dec, qdec, kdec]
    if rev:
        in_specs += [pl.BlockSpec(blk, col(1)), pl.BlockSpec(blk, col(0)),
                     pl.BlockSpec((1, WIDTH), lambda n: (0, 0))]
        args += [u_vz, h_fwd, norm_g]
    out_shape = jax.ShapeDtypeStruct((u_qk.shape[0], WIDTH), BF16 if rev else F32)
    out_specs = pl.BlockSpec(blk, col(0))
    return pl.pallas_call(
        functools.partial(_rt_kernel, rev, groups),
        grid=(nblk,),
        in_specs=in_specs,
        out_specs=out_specs,
        out_shape=out_shape,
        scratch_shapes=[pltpu.VMEM((HEADS, HEAD_DIM, HEAD_DIM), F32)],
        name="retention_bwd" if rev else "retention_fwd",
    )(*args)


def _retention(u_qk, u_vz, tabs, norm_g, groups):
    dec, qdec, kdec = tabs
    h_fwd = _retention_dir(False, u_qk, u_vz, dec, qdec, kdec, norm_g, None, groups)
    return _retention_dir(True, u_qk, u_vz, dec, qdec, kdec, norm_g, h_fwd, groups)


def _rotary_tables(seq):
    inv = ROPE_BASE ** (-jnp.arange(0, HEAD_DIM, 2, dtype=F32) / HEAD_DIM)
    ang = jnp.arange(seq, dtype=F32)[:, None] * inv[None, :]
    cos, sin = jnp.cos(ang), jnp.sin(ang)
    return jnp.concatenate([cos, cos], axis=1), jnp.concatenate([-sin, sin], axis=1)


def _retention_tables():
    pos = np.arange(CHUNK, dtype=np.float64)
    rel = pos[:, None] - pos[None, :]
    dec = np.zeros((2, HEADS, CHUNK, CHUNK))
    qdec = np.zeros((2, HEADS, CHUNK, HEAD_DIM))
    kdec = np.zeros((2, HEADS, SUBLANES, CHUNK))
    for h in range(HEADS):
        lf, lb = RT_LOG_GAMMA_FWD[h], RT_LOG_GAMMA_BWD[h]
        dec[0, h] = np.where(rel >= 0, np.exp(np.maximum(rel, 0.0) * lf), 0.0)
        dec[1, h] = np.where(rel <= 0, np.exp(np.maximum(-rel, 0.0) * lb), 0.0)
        qdec[0, h] = np.exp((pos + 1.0) * lf)[:, None]
        qdec[1, h] = np.exp((CHUNK - pos) * lb)[:, None]
        kdec[0, h, 0] = np.exp((CHUNK - 1.0 - pos) * lf)
        kdec[1, h, 0] = np.exp(pos * lb)
        kdec[0, h, 1] = np.exp(CHUNK * lf)
        kdec[1, h, 1] = np.exp(CHUNK * lb)
    return jnp.asarray(dec, F32), jnp.asarray(qdec, F32), jnp.asarray(kdec, F32)


def _filt_kernel(ft_ref, fr_ref, f0_ref, w1_ref, b1_ref, w2_ref, b2_ref, fq_ref,
                 w3f_ref, w3b_ref, dlf_ref, dlb_ref, gb_ref, gf_ref, s_ref, hf3_ref, hr3_ref):
    lt = pl.program_id(1)
    fq = fq_ref[...]

    def hidden(ft):
        h1 = jnp.sin(fq * (_dot_f32(w1_ref[...], ft) + b1_ref[...]))
        return jnp.sin(fq * (_dot_f32(w2_ref[...], h1) + b2_ref[...]))

    @pl.when((pl.program_id(2) == 0) & (pl.program_id(3) == 0))
    def _():
        hf3_ref[...] = _split3_rows(hidden(ft_ref[...]))
        hr3_ref[...] = _split3_rows(hidden(fr_ref[...]))

    def lhs3(w):
        hi, lo = _split_hi_lo(w)
        return jnp.concatenate([hi, hi, lo], axis=1)

    ft = ft_ref[...]
    fr = fr_ref[...]
    dl_f = jnp.abs(dlf_ref[...])
    dl_b = jnp.abs(dlb_ref[...])
    fwd = _dot(lhs3(w3f_ref[...]), hf3_ref[...]) * jnp.exp(-ft[0:1, :] * dl_f)
    bwd = _dot(lhs3(w3b_ref[...]), hr3_ref[...]) * jnp.exp(-fr[0:1, :] * dl_b)
    f0 = f0_ref[...]
    bwd0 = (_dot_f32(w3b_ref[...], hidden(f0)) * jnp.exp(-f0[0:1, :] * dl_b))[:, 0:1]
    lag = lax.broadcasted_iota(jnp.int32, fwd.shape, 1) + lt * LT_FILT
    fwd = jnp.where(lag == 0, fwd + bwd0, fwd)
    bwd = jnp.where(lag == 0, 0.0, bwd)
    gb_ref[...] = bwd.reshape(gb_ref.shape)
    gf_ref[...] = fwd.reshape(gf_ref.shape)
    part = jnp.sum(jnp.abs(fwd) + jnp.abs(bwd), axis=1, keepdims=True)
    lane = lax.broadcasted_iota(jnp.int32, (LANES, LANES), 1)
    s_ref[...] = jnp.where(lane == 0, part, 0.0)


def _filters(seq, w1t, b1, w2t, b2, freq, w3t, deltas):
    t = jnp.linspace(0.0, 1.0, seq, dtype=F32)[None, :]
    w = (2.0 * math.pi / seq) * jnp.arange(seq, dtype=F32)[None, :]
    bands = jnp.linspace(1e-4, HY_BANDS - 1, HY_BANDS, dtype=F32)[:, None]
    feats = jnp.concatenate([t, jnp.cos(bands * w), -jnp.sin(bands * w),
                             jnp.zeros((FEAT_ROWS - 1 - 2 * HY_BANDS, seq), F32)], axis=0)
    feats_rev = jnp.concatenate([feats[:, :1], feats[:, :0:-1]], axis=1)
    feats0 = jnp.broadcast_to(feats[:, :1], (FEAT_ROWS, LANES))
    nlt = seq // LT_FILT
    ct = WIDTH // LANES
    lyr = lambda l, i, o, c: (l, 0, 0)
    return pl.pallas_call(
        _filt_kernel,
        grid=(DEPTH, nlt, HY_ORDER, ct),
        in_specs=[
            pl.BlockSpec((FEAT_ROWS, LT_FILT), lambda l, i, o, c: (0, i)),
            pl.BlockSpec((FEAT_ROWS, LT_FILT), lambda l, i, o, c: (0, i)),
            pl.BlockSpec((FEAT_ROWS, LANES), lambda l, i, o, c: (0, 0)),
            pl.BlockSpec((None, HY_HIDDEN, FEAT_ROWS), lyr),
            pl.BlockSpec((None, HY_HIDDEN, 1), lyr),
            pl.BlockSpec((None, HY_HIDDEN, HY_HIDDEN), lyr),
            pl.BlockSpec((None, HY_HIDDEN, 1), lyr),
            pl.BlockSpec((None, HY_HIDDEN, 1), lyr),
            pl.BlockSpec((None, LANES, HY_HIDDEN), lambda l, i, o, c: (l, o * 2 * ct + c, 0)),
            pl.BlockSpec((None, LANES, HY_HIDDEN), lambda l, i, o, c: (l, o * 2 * ct + ct + c, 0)),
            pl.BlockSpec((None, LANES, 1), lambda l, i, o, c: (l, o * 2 * ct + c, 0)),
            pl.BlockSpec((None, LANES, 1), lambda l, i, o, c: (l, o * 2 * ct + ct + c, 0)),
        ],
        out_specs=[
            pl.BlockSpec((None, None, LANES, LT_FILT // LANES, LANES), lambda l, i, o, c: (l, o, c, i, 0)),
            pl.BlockSpec((None, None, LANES, LT_FILT // LANES, LANES), lambda l, i, o, c: (l, o, c, i, 0)),
            pl.BlockSpec((None, None, LANES, LANES), lambda l, i, o, c: (l, o, c, i)),
        ],
        out_shape=[
            jax.ShapeDtypeStruct((DEPTH, HY_ORDER, WIDTH, seq // LANES, LANES), F32),
            jax.ShapeDtypeStruct((DEPTH, HY_ORDER, WIDTH, seq // LANES, LANES), F32),
            jax.ShapeDtypeStruct((DEPTH, HY_ORDER, WIDTH, nlt * LANES), F32),
        ],
        scratch_shapes=[pltpu.VMEM((3 * HY_HIDDEN, LT_FILT), BF16),
                        pltpu.VMEM((3 * HY_HIDDEN, LT_FILT), BF16)],
        name="hyena_filters",
    )(feats, feats_rev, feats0, w1t, b1, w2t, b2, freq, w3t, w3t, deltas, deltas)


def _split_hi_lo(x):
    hi = x.astype(BF16)
    lo = (x - hi.astype(F32)).astype(BF16)
    return hi, lo


def _split3_rows(x):
    hi, lo = _split_hi_lo(x)
    return jnp.concatenate([hi, lo, hi], axis=0)


def _split3_cols(x):
    hi, lo = _split_hi_lo(x)
    return jnp.concatenate([hi, lo, hi], axis=1)


def _np_split(m):
    hi = m.astype(ml_dtypes.bfloat16)
    lo = (m - hi.astype(np.float64)).astype(ml_dtypes.bfloat16)
    return hi, lo


def _dft_tables(seq):
    nj = seq // LANES
    n1 = 2 * nj
    n = n1 * LANES
    khp = n1 // 2 + KH_PAD
    k1 = np.arange(khp, dtype=np.float64)[:, None]
    valid = (k1 <= n1 // 2).astype(np.float64)
    ang = 2.0 * np.pi * k1 * np.arange(n1, dtype=np.float64)[None, :] / n1
    fa_full = np.concatenate([valid * np.cos(ang), -valid * np.sin(ang)], axis=0)

    def lhs3(m):
        hi, lo = _np_split(m)
        return jnp.asarray(np.concatenate([hi, hi, lo], axis=1))

    def rhs3(m):
        hi, lo = _np_split(m)
        return jnp.asarray(np.concatenate([hi, hi, lo], axis=0))

    n2 = np.arange(LANES, dtype=np.float64)
    ang2 = 2.0 * np.pi * n2[:, None] * n2[None, :] / LANES
    c2, s2 = np.cos(ang2), np.sin(ang2)
    fb = np.block([[c2, -s2], [s2, c2]])
    fbi = np.block([[c2, s2], [-s2, c2]])
    w = np.where((k1 == 0) | (k1 == n1 // 2), 1.0, 2.0) * valid
    th = 2.0 * np.pi * np.arange(nj, dtype=np.float64)[:, None] * k1.T / n1
    fai = np.concatenate([(w.T / n) * np.cos(th), -(w.T / n) * np.sin(th)], axis=1)
    phi = 2.0 * np.pi * k1 * n2[None, :] / n
    return dict(
        nj=nj, n1=n1, khp=khp,
        fa=lhs3(fa_full[:, :nj]), fa_full=lhs3(fa_full), fb=rhs3(fb), fbi=rhs3(fbi), fai=lhs3(fai),
        twc=jnp.asarray(valid * np.cos(phi), F32), tws=jnp.asarray(valid * np.sin(phi), F32))


def _dft_rows(fa_ref, twc, tws, khp, cols):
    rhs = cols[0] if len(cols) == 1 else jnp.concatenate(cols, axis=1)
    y = _dot(fa_ref[...], _split3_rows(rhs))
    out = []
    for i in range(len(cols)):
        yre = y[0:khp, i * LANES:(i + 1) * LANES]
        yim = y[khp:2 * khp, i * LANES:(i + 1) * LANES]
        out.append((yre * twc + yim * tws, yim * twc - yre * tws))
    return out


def _spec_kernel(khp, gf_ref, gb_ref, s_ref, fa_ref, fb_ref, twc_ref, tws_ref, h_ref, ybuf):
    twc = twc_ref[...]
    tws = tws_ref[...]

    def circular(c):
        return jnp.concatenate([gf_ref[c], gb_ref[c]], axis=0)

    for c in range(0, CB_SPEC, 2):
        res = _dft_rows(fa_ref, twc, tws, khp, [circular(c), circular(c + 1)])
        for i, (yre, yim) in enumerate(res):
            r0 = (c + i) * khp
            ybuf[r0:r0 + khp, 0:LANES] = yre
            ybuf[r0:r0 + khp, LANES:2 * LANES] = yim
    x = _dot(_split3_cols(ybuf[...]), fb_ref[...])
    for c in range(CB_SPEC):
        norm = jnp.sum(s_ref[c:c + 1, :], axis=1, keepdims=True)
        h_ref[c] = x[c * khp:(c + 1) * khp, :] / norm


def _spectrum(gf, gb, s, tab):
    nch = gf.shape[0]
    nj, khp = tab["nj"], tab["khp"]
    const = lambda a: pl.BlockSpec(a.shape, lambda i: (0, 0))
    return pl.pallas_call(
        functools.partial(_spec_kernel, khp),
        grid=(nch // CB_SPEC,),
        in_specs=[
            pl.BlockSpec((CB_SPEC, nj, LANES), lambda i: (i, 0, 0)),
            pl.BlockSpec((CB_SPEC, nj, LANES), lambda i: (i, 0, 0)),
            pl.BlockSpec((CB_SPEC, s.shape[-1]), lambda i: (i, 0)),
            const(tab["fa_full"]), const(tab["fb"]), const(tab["twc"]), const(tab["tws"]),
        ],
        out_specs=pl.BlockSpec((CB_SPEC, khp, 2 * LANES), lambda i: (i, 0, 0)),
        out_shape=jax.ShapeDtypeStruct((nch, khp, 2 * LANES), F32),
        scratch_shapes=[pltpu.VMEM((CB_SPEC * khp, 2 * LANES), F32)],
        name="hyena_spectrum",
    )(gf, gb, s, tab["fa_full"], tab["fb"], tab["twc"], tab["tws"])


def _hyfft_kernel(bsz, nj, khp, prm_ref, v_ref, x1_ref, x2_ref, z_ref, h_ref,
                  fa_ref, fb_ref, fbi_ref, fai_ref, twc_ref, tws_ref, out_ref,
                  xs_ref, ybuf, pbuf):
    ct = pl.program_id(0)
    rows = bsz * nj
    pairs = [(b, b + 1) for b in range(0, bsz, 2)] if bsz > 1 else [(0,)]
    twc = twc_ref[...]
    tws = tws_ref[...]

    lane = lax.broadcasted_iota(jnp.int32, (rows, LANES), 1)
    blk = lax.broadcasted_iota(jnp.int32, (rows, LANES), 0) & (nj - 1)
    first = (lane == 0) & (blk == 0)
    last = (lane == LANES - 1) & (blk == nj - 1)

    def conv3(x, base):
        r = pltpu.roll(x, 1, 1)
        xm1 = jnp.where(lane == 0, pltpu.roll(r, 1, 0), r)
        xm1 = jnp.where(first, 0.0, xm1)
        l = pltpu.roll(x, LANES - 1, 1)
        xp1 = jnp.where(lane == LANES - 1, pltpu.roll(l, rows - 1, 0), l)
        xp1 = jnp.where(last, 0.0, xp1)
        return prm_ref[base] * xm1 + prm_ref[base + 1] * x + prm_ref[base + 2] * xp1 + prm_ref[base + 3]

    def row0(c, b):
        return (c * bsz + b) * khp

    def forward(c, x):
        for pr in pairs:
            res = _dft_rows(fa_ref, twc, tws, khp, [x[b * nj:(b + 1) * nj, :] for b in pr])
            for b, (yre, yim) in zip(pr, res):
                r0 = row0(c, b)
                ybuf[r0:r0 + khp, 0:LANES] = yre
                ybuf[r0:r0 + khp, LANES:2 * LANES] = yim

    def spectral(order):
        for c0 in range(0, CB_HY, CG_HY):
            g0, g1 = row0(c0, 0), row0(c0 + CG_HY, 0)
            x = _dot(_split3_cols(ybuf[g0:g1, :]), fb_ref[...])
            for c in range(c0, c0 + CG_HY):
                hre = h_ref[order, c, :, 0:LANES]
                him = h_ref[order, c, :, LANES:2 * LANES]
                for b in range(bsz):
                    r0 = row0(c, b)
                    xre = x[r0 - g0:r0 - g0 + khp, 0:LANES]
                    xim = x[r0 - g0:r0 - g0 + khp, LANES:2 * LANES]
                    pbuf[r0:r0 + khp, 0:LANES] = xre * hre - xim * him
                    pbuf[r0:r0 + khp, LANES:2 * LANES] = xre * him + xim * hre
            r = _dot(_split3_cols(pbuf[g0:g1, :]), fbi_ref[...])
            for c in range(c0, c0 + CG_HY):
                for b in range(bsz):
                    r0 = row0(c, b)
                    rre = r[r0 - g0:r0 - g0 + khp, 0:LANES]
                    rim = r[r0 - g0:r0 - g0 + khp, LANES:2 * LANES]
                    ybuf[r0:r0 + khp, 0:LANES] = rre * twc - rim * tws
                    ybuf[r0:r0 + khp, LANES:2 * LANES] = rre * tws + rim * twc

    def inverse(c):
        outs = [None] * bsz
        for pr in pairs:
            rre = [ybuf[row0(c, b):row0(c, b) + khp, 0:LANES] for b in pr]
            rim = [ybuf[row0(c, b):row0(c, b) + khp, LANES:2 * LANES] for b in pr]
            if len(pr) > 1:
                rre, rim = [jnp.concatenate(rre, axis=1)], [jnp.concatenate(rim, axis=1)]
            y = _dot(fai_ref[...], _split3_rows(jnp.concatenate([rre[0], rim[0]], axis=0)))
            for i, b in enumerate(pr):
                outs[b] = y[:, i * LANES:(i + 1) * LANES]
        return outs[0] if bsz == 1 else jnp.concatenate(outs, axis=0)

    for c in range(CB_HY):
        base = (ct * CB_HY + c) * HY_PRM
        v = conv3(v_ref[c], base)
        xs_ref[0, c] = v
        xs_ref[1, c] = conv3(x1_ref[c], base + 4)
        xs_ref[2, c] = conv3(x2_ref[c], base + 8)
        forward(c, v)
    spectral(0)
    for c in range(CB_HY):
        base = (ct * CB_HY + c) * HY_PRM
        v = xs_ref[0, c]
        z1 = xs_ref[1, c] * (inverse(c) + prm_ref[base + 12] * v)
        xs_ref[0, c] = z1
        forward(c, z1)
    spectral(1)
    for c in range(CB_HY):
        base = (ct * CB_HY + c) * HY_PRM
        z1 = xs_ref[0, c]
        y2 = inverse(c) + prm_ref[base + 13] * z1
        out_ref[:, c, :] = xs_ref[2, c] * y2 * _silu(z_ref[c])


def _hyena_fft(prm, u_t, h, layer, tab, row0, bsz, seq):
    nj, khp = tab["nj"], tab["khp"]
    nct = WIDTH // CB_HY
    rows = bsz * nj
    blk = (CB_HY, rows, LANES)
    rblk = row0 // (rows * LANES)
    assert rblk * rows * LANES == row0
    const = lambda a: pl.BlockSpec(a.shape, lambda i: (0, 0))
    return pl.pallas_call(
        functools.partial(_hyfft_kernel, bsz, nj, khp),
        grid=(nct,),
        in_specs=[
            pl.BlockSpec(memory_space=pltpu.SMEM),
            pl.BlockSpec(blk, lambda i: (i, rblk, 0)),
            pl.BlockSpec(blk, lambda i: (nct + i, rblk, 0)),
            pl.BlockSpec(blk, lambda i: (2 * nct + i, rblk, 0)),
            pl.BlockSpec(blk, lambda i: (3 * nct + i, rblk, 0)),
            pl.BlockSpec((None, HY_ORDER, CB_HY, khp, 2 * LANES), lambda i: (layer, 0, i, 0, 0)),
            const(tab["fa"]), const(tab["fb"]), const(tab["fbi"]), const(tab["fai"]),
            const(tab["twc"]), const(tab["tws"]),
        ],
        out_specs=pl.BlockSpec((rows, CB_HY, LANES), lambda i: (0, i, 0)),
        out_shape=jax.ShapeDtypeStruct((rows, WIDTH, LANES), F32),
        scratch_shapes=[
            pltpu.VMEM((3, CB_HY, bsz * nj, LANES), F32),
            pltpu.VMEM((CB_HY * bsz * khp, 2 * LANES), F32),
            pltpu.VMEM((CB_HY * bsz * khp, 2 * LANES), F32),
        ],
        name="hyena_fft",
    )(prm, u_t, u_t, u_t, u_t, h, tab["fa"], tab["fb"], tab["fbi"], tab["fai"],
      tab["twc"], tab["tws"])


def _out_kernel(final, starts, nx, *refs):
    ng = len(starts)
    x_refs = refs[0:nx]
    yml_ref, yrt_ref = refs[nx:nx + 2]
    yhy_refs = refs[nx + 2:nx + 2 + ng]
    w_ref, g_ref = refs[nx + 2 + ng:nx + 4 + ng]
    out_refs = refs[nx + 4 + ng:-1]
    lhs_ref = refs[-1]
    i = pl.program_id(0)
    lhs_ref[:, 0:WIDTH] = yml_ref[...]
    lhs_ref[:, WIDTH:2 * WIDTH] = yrt_ref[...]
    jb = TM_OUT // LANES

    def fill_hyena(ref):
        for j in range(jb):
            lhs_ref[j * LANES:(j + 1) * LANES, 2 * WIDTH:] = ref[j].T.astype(BF16)

    for gi in range(ng):
        inside = i >= starts[gi]
        if gi + 1 < ng:
            inside = inside & (i < starts[gi + 1])
        pl.when(inside)(functools.partial(fill_hyena, yhy_refs[gi]))
    x = _x_tile(x_refs, starts if nx > 1 else [0]) + _dot(lhs_ref[...], w_ref[...])
    if not final:
        out_refs[0][...] = x
        return
    ms = jnp.mean(x * x, axis=-1, keepdims=True)
    x = x * lax.rsqrt(ms + RMS_EPS) * g_ref[...]
    for gi in range(ng):
        inside = i >= starts[gi]
        if gi + 1 < ng:
            inside = inside & (i < starts[gi + 1])

        @pl.when(inside)
        def _(gi=gi):
            out_refs[gi][...] = x


def _out_proj(xs, y_ml, y_rt, y_hy, w, gain, final, groups):
    t = y_ml.shape[0]
    jb = TM_OUT // LANES
    starts = [r0 // TM_OUT for (r0, _, _) in groups]
    counts = [bsz * seq // TM_OUT for (_, bsz, seq) in groups]

    def tile(gi):
        return lambda i: jnp.clip(i - starts[gi], 0, counts[gi] - 1)

    hy_specs = [pl.BlockSpec((jb, WIDTH, LANES), lambda i, f=tile(gi): (f(i), 0, 0))
                for gi in range(len(groups))]
    if final:
        out_specs = [pl.BlockSpec((TM_OUT, D_MODEL), lambda i, f=tile(gi): (f(i), 0))
                     for gi in range(len(groups))]
        out_shape = [jax.ShapeDtypeStruct((bsz * seq, D_MODEL), F32) for (_, bsz, seq) in groups]
    else:
        out_specs = [pl.BlockSpec((TM_OUT, D_MODEL), lambda i: (i, 0))]
        out_shape = [jax.ShapeDtypeStruct((t, D_MODEL), F32)]
    return pl.pallas_call(
        functools.partial(_out_kernel, final, starts, len(xs)),
        grid=(t // TM_OUT,),
        in_specs=[
            *_x_specs(xs, TM_OUT, 1),
            pl.BlockSpec((TM_OUT, WIDTH), lambda i: (i, 0)),
            pl.BlockSpec((TM_OUT, WIDTH), lambda i: (i, 0)),
            *hy_specs,
            pl.BlockSpec((3 * WIDTH, D_MODEL), lambda i: (0, 0)),
            pl.BlockSpec((1, D_MODEL), lambda i: (0, 0)),
        ],
        out_specs=out_specs,
        out_shape=out_shape,
        scratch_shapes=[pltpu.VMEM((TM_OUT, 3 * WIDTH), BF16)],
        name="out_proj",
    )(*xs, y_ml, y_rt, *y_hy, w, gain)


def kernel(x_prompt, x_sample, norm_g, w_in, ml_conv_w, ml_conv_b, ml_gate_b, ml_norm_g, rt_norm_g, hy_conv_w, hy_conv_b, hy_w1, hy_b1, hy_w2, hy_b2, hy_w3, hy_freq, hy_deltas, hy_skip, w_out, final_g):
    groups = []
    row0 = 0
    for xg in (x_prompt, x_sample):
        bsz, seq, _ = xg.shape
        groups.append((row0, bsz, seq))
        row0 += bsz * seq
    xs = [x_prompt.reshape(-1, D_MODEL), x_sample.reshape(-1, D_MODEL)]

    c_gate = ML_COLS
    c_rt = c_gate + 4 * HEADS
    c_hy = c_rt + RT_COLS
    gate_pad = jnp.zeros((DEPTH, D_MODEL, LANES - 4 * HEADS), F32)
    w_tok = jnp.concatenate([w_in[:, :, :c_gate], w_in[:, :, c_gate:c_rt], gate_pad,
                             w_in[:, :, c_rt:c_hy]], axis=2).astype(BF16)
    w_gt = jnp.swapaxes(w_in[:, :, c_gate:c_rt], 1, 2).astype(BF16)
    w_hyt = jnp.swapaxes(w_in[:, :, c_hy:], 1, 2).astype(BF16)
    w_out_b = w_out.astype(BF16)
    gate_b = ml_gate_b.reshape(DEPTH, 1, 4 * HEADS)
    gate_b_row = jnp.concatenate([gate_b, jnp.zeros((DEPTH, 1, LANES - 4 * HEADS), F32)], axis=2)
    gate_b_col = ml_gate_b.reshape(DEPTH, 4 * HEADS, 1)

    cw = hy_conv_w.reshape(DEPTH, 3, 3, WIDTH)
    cbias = hy_conv_b.reshape(DEPTH, 1, 3, WIDTH)
    taps = jnp.concatenate([cw, cbias], axis=1)
    taps = jnp.transpose(taps, (0, 3, 2, 1)).reshape(DEPTH, WIDTH, 12)
    hy_prm = jnp.concatenate([taps, jnp.transpose(hy_skip, (0, 2, 1)),
                              jnp.zeros((DEPTH, WIDTH, HY_PRM - 14), F32)],
                             axis=2).reshape(DEPTH, WIDTH * HY_PRM)

    w1t = jnp.concatenate([jnp.swapaxes(hy_w1, 1, 2),
                           jnp.zeros((DEPTH, HY_HIDDEN, FEAT_ROWS - hy_w1.shape[1]), F32)], axis=2)
    w2t = jnp.swapaxes(hy_w2, 1, 2)
    w3t = jnp.swapaxes(hy_w3, 1, 2)
    b1 = hy_b1[:, :, None]
    b2 = hy_b2[:, :, None]
    freq = hy_freq[:, :, None]
    deltas = hy_deltas.reshape(DEPTH, HY_ORDER * 2 * WIDTH, 1)

    tables, spectra = [], []
    for (_, _, seq) in groups:
        tab = _dft_tables(seq)
        gb, gf, s = _filters(seq, w1t, b1, w2t, b2, freq, w3t, deltas)
        nfil = DEPTH * HY_ORDER * WIDTH
        h = _spectrum(gf.reshape(nfil, tab["nj"], LANES), gb.reshape(nfil, tab["nj"], LANES),
                      s.reshape(nfil, -1), tab)
        tables.append(tab)
        spectra.append(h.reshape(DEPTH, HY_ORDER, WIDTH, tab["khp"], 2 * LANES))
    rot_c, rot_s = _rotary_tables(max(seq for (_, _, seq) in groups))
    rt_tabs = _retention_tables()

    for layer in range(DEPTH):
        gain = norm_g[layer][None, :]
        u_ml, u_g, u_qk, u_vz, u_gt = _in_proj(xs, gain, w_tok[layer], w_gt[layer], rot_c, rot_s, groups)
        u_hyt = _in_proj_hy(xs, gain, w_hyt[layer])
        pb, cb, gr, gs = _gates(u_g, u_gt, gate_b_row[layer], gate_b_col[layer])
        y_ml = _mlstm(u_ml, pb, cb, gr, gs, ml_conv_w[layer], ml_conv_b[layer][None, :],
                      ml_norm_g[layer][None, :], groups)
        y_rt = _retention(u_qk, u_vz, rt_tabs, rt_norm_g[layer][None, :], groups)
        y_hy = [_hyena_fft(hy_prm[layer], u_hyt, spectra[gi], layer, tables[gi], r0, bsz, seq)
                for gi, (r0, bsz, seq) in enumerate(groups)]
        outs = _out_proj(xs, y_ml, y_rt, y_hy, w_out_b[layer], final_g[None, :],
                         layer == DEPTH - 1, groups)
        xs = [outs[0]]

    return outs[0].reshape(x_prompt.shape), outs[1].reshape(x_sample.shape)
```

```python
import functools
import math

import ml_dtypes
import numpy as np
import jax
import jax.numpy as jnp
from jax import lax
from jax.experimental import pallas as pl
from jax.experimental.pallas import tpu as pltpu

F32 = jnp.float32
BF16 = jnp.bfloat16

D_MODEL = 1024
DEPTH = 4
CHUNK = 128
HEADS = 4
HEAD_DIM = 128
WIDTH = 512
HY_ORDER = 2
HY_BANDS = 16
HY_HIDDEN = 64
FEAT_ROWS = 40
HY_PRM = 16
ROPE_BASE = 10000.0
RMS_EPS = 1e-6
HEAD_NORM_EPS = 1e-5
M_INIT = -1e30
RT_LOG_GAMMA_FWD = tuple(math.log(1.0 - 2.0 ** (-5.0 - h)) for h in range(HEADS))
RT_LOG_GAMMA_BWD = tuple(math.log(1.0 - 2.0 ** (-5.5 - h)) for h in range(HEADS))

LANES = 128
SUBLANES = 8
TM_IN = 256
TM_HY = 1024
TR_HY = 1024
TM_OUT = 1024
TB_SEQ = 1024
TG = 1024
CB_HY = 8
CG_HY = 2
CB_SPEC = 16
KH_PAD = 8
LT_FILT = 2048
ML_COLS = 5 * WIDTH
RT_COLS = 4 * WIDTH
HY_ROWS = 4 * WIDTH
HIGHEST = lax.Precision.HIGHEST


def _dot(a, b):
    return jnp.dot(a, b, preferred_element_type=F32)


def _dot_nt(a, b):
    return lax.dot_general(a, b, (((1,), (1,)), ((), ())), preferred_element_type=F32)


def _dot_tn(a, b):
    return lax.dot_general(a, b, (((0,), (0,)), ((), ())), preferred_element_type=F32)


def _dot_f32(a, b):
    return jnp.dot(a, b, precision=HIGHEST, preferred_element_type=F32)


def _silu(y):
    return y * jax.nn.sigmoid(y)


def _log_sigmoid(x):
    return -(jnp.maximum(-x, 0.0) + jnp.log1p(jnp.exp(-jnp.abs(x))))


def _head_norm(h, gain):
    mu = jnp.mean(h, axis=-1, keepdims=True)
    hc = h - mu
    var = jnp.mean(hc * hc, axis=-1, keepdims=True)
    return hc * lax.rsqrt(var + HEAD_NORM_EPS) * gain


def _tile_starts(xs, tm):
    starts, s = [], 0
    for a in xs:
        starts.append(s)
        s += a.shape[0] // tm
    return starts, s


def _x_specs(xs, tm, ngrid):
    starts, _ = _tile_starts(xs, tm)
    specs = []
    for a, s in zip(xs, starts):
        cnt = a.shape[0] // tm
        if ngrid == 1:
            specs.append(pl.BlockSpec((tm, D_MODEL), lambda i, s=s, cnt=cnt: (jnp.clip(i - s, 0, cnt - 1), 0)))
        else:
            specs.append(pl.BlockSpec((tm, D_MODEL), lambda i, r, s=s, cnt=cnt: (jnp.clip(i - s, 0, cnt - 1), 0)))
    return specs


def _x_tile(x_refs, starts):
    x = x_refs[0][...]
    for ref, s in zip(x_refs[1:], starts[1:]):
        x = jnp.where(pl.program_id(0) >= s, ref[...], x)
    return x


def _rms_bf16(x, g_ref):
    ms = jnp.mean(x * x, axis=-1, keepdims=True)
    return (x * lax.rsqrt(ms + RMS_EPS) * g_ref[...]).astype(BF16)


def _in_kernel(starts, *refs):
    nx = len(starts)
    (g_ref, wtok_ref, wgt_ref, cc_ref, ss_ref,
     oml_ref, og_ref, oqk_ref, ovz_ref, ogt_ref) = refs[nx:]
    h = _rms_bf16(_x_tile(refs[:nx], starts), g_ref)
    y = _dot(h, wtok_ref[...])
    oml_ref[...] = y[:, 0:ML_COLS]
    og_ref[...] = y[:, ML_COLS:ML_COLS + LANES]
    c0 = ML_COLS + LANES
    cc = cc_ref[...]
    ss = ss_ref[...]
    for j in range(2 * HEADS):
        u = y[:, c0 + j * HEAD_DIM:c0 + (j + 1) * HEAD_DIM]
        r = u * cc + pltpu.roll(u, HEAD_DIM // 2, 1) * ss
        if j < HEADS:
            r = r * (HEAD_DIM ** -0.5)
        oqk_ref[:, j * HEAD_DIM:(j + 1) * HEAD_DIM] = r.astype(BF16)
    ovz_ref[...] = y[:, c0 + 2 * WIDTH:]
    ogt_ref[...] = _dot_nt(wgt_ref[...], h)


def _in_proj(xs, gain, w_tok, w_gt, rot_c, rot_s, groups):
    starts, nt = _tile_starts(xs, TM_IN)
    t = nt * TM_IN
    ncol = w_tok.shape[1]
    pos = lambda i: (_seq_pos(i, groups, TM_IN)[0], 0)
    return pl.pallas_call(
        functools.partial(_in_kernel, starts),
        grid=(nt,),
        in_specs=[
            *_x_specs(xs, TM_IN, 1),
            pl.BlockSpec((1, D_MODEL), lambda i: (0, 0)),
            pl.BlockSpec((D_MODEL, ncol), lambda i: (0, 0)),
            pl.BlockSpec((4 * HEADS, D_MODEL), lambda i: (0, 0)),
            pl.BlockSpec((TM_IN, HEAD_DIM), pos),
            pl.BlockSpec((TM_IN, HEAD_DIM), pos),
        ],
        out_specs=[
            pl.BlockSpec((TM_IN, ML_COLS), lambda i: (i, 0)),
            pl.BlockSpec((TM_IN, LANES), lambda i: (i, 0)),
            pl.BlockSpec((TM_IN, 2 * WIDTH), lambda i: (i, 0)),
            pl.BlockSpec((TM_IN, 2 * WIDTH), lambda i: (i, 0)),
            pl.BlockSpec((4 * HEADS, TM_IN), lambda i: (0, i)),
        ],
        out_shape=[
            jax.ShapeDtypeStruct((t, ML_COLS), F32),
            jax.ShapeDtypeStruct((t, LANES), F32),
            jax.ShapeDtypeStruct((t, 2 * WIDTH), BF16),
            jax.ShapeDtypeStruct((t, 2 * WIDTH), F32),
            jax.ShapeDtypeStruct((4 * HEADS, t), F32),
        ],
        name="in_proj",
    )(*xs, gain, w_tok, w_gt, rot_c, rot_s)


def _in_hy_kernel(starts, *refs):
    nx = len(starts)
    g_ref, w_ref, o_ref = refs[nx:]
    y = _dot_nt(w_ref[...], _rms_bf16(_x_tile(refs[:nx], starts), g_ref))
    o_ref[...] = y.reshape(o_ref.shape)


def _in_proj_hy(xs, gain, w_hyt):
    starts, nt = _tile_starts(xs, TM_HY)
    t = nt * TM_HY
    nr = HY_ROWS // TR_HY
    return pl.pallas_call(
        functools.partial(_in_hy_kernel, starts),
        grid=(nt, nr),
        in_specs=[
            *_x_specs(xs, TM_HY, 2),
            pl.BlockSpec((1, D_MODEL), lambda i, r: (0, 0)),
            pl.BlockSpec((TR_HY, D_MODEL), lambda i, r: (r, 0)),
        ],
        out_specs=pl.BlockSpec((TR_HY, TM_HY // LANES, LANES), lambda i, r: (r, i, 0)),
        out_shape=jax.ShapeDtypeStruct((HY_ROWS, t // LANES, LANES), F32),
        name="in_proj_hyena",
    )(*xs, gain, w_hyt)


def _gates_kernel(g_ref, gt_ref, b_ref, bt_ref, pb_ref, cb_ref, gr_ref, gs_ref):
    row = lax.broadcasted_iota(jnp.int32, (CHUNK, CHUNK), 0)
    col = lax.broadcasted_iota(jnp.int32, (CHUNK, CHUNK), 1)
    lower = (row >= col).astype(F32)
    upper = (row <= col).astype(F32)
    gr_ref[...] = jnp.zeros_like(gr_ref)
    gs_ref[...] = jnp.zeros_like(gs_ref)
    for c in range(TG // CHUNK):
        sl = slice(c * CHUNK, (c + 1) * CHUNK)
        lf = _log_sigmoid(g_ref[sl, :] + b_ref[...])
        cum_c = (_dot_f32(lower, lf), _dot_f32(upper, lf))
        pre_t = gt_ref[:, sl] + bt_ref[...]
        lf_t = _log_sigmoid(pre_t)
        cum_r = (_dot_f32(lf_t, upper), _dot_f32(lf_t, lower))
        for d in range(2):
            for h in range(HEADS):
                kf = (2 * d + 1) * HEADS + h
                ki = 2 * d * HEADS + h
                cum = cum_r[d][kf:kf + 1, :]
                li = pre_t[ki:ki + 1, :]
                r = li - cum
                f_tot = cum[:, 0:1] if d else cum[:, CHUNK - 1:CHUNK]
                g = f_tot - cum + li
                allowed = (row <= col) if d else (row >= col)
                pmax = jnp.max(jnp.where(allowed, r, -jnp.inf), axis=1, keepdims=True)
                pb_ref[d, h, sl, :] = jnp.broadcast_to(pmax, (CHUNK, LANES))
                cb_ref[d, h, sl, :] = jnp.broadcast_to(cum_c[d][:, kf:kf + 1], (CHUNK, LANES))
                gr_ref[d, h, 0:1, sl] = r
                gr_ref[d, h, 1:2, sl] = g
                gs_ref[d, h, c, 0:1, :] = jnp.broadcast_to(f_tot, (1, LANES))
                gs_ref[d, h, c, 1:2, :] = jnp.broadcast_to(jnp.max(g, axis=1, keepdims=True), (1, LANES))


def _gates(g, gt, bias_row, bias_col):
    t = g.shape[0]
    nc = TG // CHUNK
    return pl.pallas_call(
        _gates_kernel,
        grid=(t // TG,),
        in_specs=[
            pl.BlockSpec((TG, LANES), lambda i: (i, 0)),
            pl.BlockSpec((4 * HEADS, TG), lambda i: (0, i)),
            pl.BlockSpec((1, LANES), lambda i: (0, 0)),
            pl.BlockSpec((4 * HEADS, 1), lambda i: (0, 0)),
        ],
        out_specs=[
            pl.BlockSpec((2, HEADS, TG, LANES), lambda i: (0, 0, i, 0)),
            pl.BlockSpec((2, HEADS, TG, LANES), lambda i: (0, 0, i, 0)),
            pl.BlockSpec((2, HEADS, SUBLANES, TG), lambda i: (0, 0, 0, i)),
            pl.BlockSpec((2, HEADS, nc, SUBLANES, LANES), lambda i: (0, 0, i, 0, 0)),
        ],
        out_shape=[
            jax.ShapeDtypeStruct((2, HEADS, t, LANES), F32),
            jax.ShapeDtypeStruct((2, HEADS, t, LANES), F32),
            jax.ShapeDtypeStruct((2, HEADS, SUBLANES, t), F32),
            jax.ShapeDtypeStruct((2, HEADS, t // CHUNK, SUBLANES, LANES), F32),
        ],
        name="gates",
    )(g, gt, bias_row, bias_col)


def _seq_pos(g, groups, tb=TB_SEQ):
    pos, nbs = None, None
    start = 0
    for (_, bsz, seq) in groups:
        nb = seq // tb
        p = (g - start) % nb
        pos = p if pos is None else jnp.where(g >= start, p, pos)
        nbs = nb if nbs is None else jnp.where(g >= start, nb, nbs)
        start += bsz * nb
    return pos, nbs


def _num_blocks(groups):
    return sum(bsz * seq // TB_SEQ for (_, bsz, seq) in groups)


def _ml_kernel(rev, groups, *refs):
    if rev:
        (qs_ref, ks_ref, v_ref, pb_ref, cb_ref, gr_ref, gs_ref, o_ref, z_ref, hf_ref, ng_ref,
         out_ref, st_ref, m_ref) = refs
    else:
        (q_ref, k_ref, v_ref, qp_ref, qn_ref, kp_ref, kn_ref, pb_ref, cb_ref, gr_ref, gs_ref,
         cw_ref, cbias_ref, out_ref, qs_ref, ks_ref, st_ref, m_ref, stg_ref) = refs
    tb = TB_SEQ
    nch = tb // CHUNK
    n = pl.program_id(0)
    sb, nb = _seq_pos((_num_blocks(groups) - 1 - n) if rev else n, groups)

    @pl.when(sb == (nb - 1 if rev else 0))
    def _():
        st_ref[...] = jnp.zeros_like(st_ref)
        m_ref[...] = jnp.full_like(m_ref, M_INIT)

    def conv_silu(raw_ref, prev_ref, next_ref, c0):
        stg_ref[SUBLANES:SUBLANES + tb, :] = raw_ref[...]
        stg_ref[SUBLANES - 1:SUBLANES, :] = jnp.where(sb > 0, prev_ref[SUBLANES - 1:SUBLANES, :], 0.0)
        stg_ref[SUBLANES + tb:SUBLANES + tb + 1, :] = jnp.where(sb < nb - 1, next_ref[0:1, :], 0.0)
        w = cw_ref[:, c0:c0 + WIDTH]
        y = (stg_ref[SUBLANES - 1:SUBLANES - 1 + tb, :] * w[0:1, :]
             + stg_ref[SUBLANES:SUBLANES + tb, :] * w[1:2, :]
             + stg_ref[SUBLANES + 1:SUBLANES + 1 + tb, :] * w[2:3, :] + cbias_ref[:, c0:c0 + WIDTH])
        return _silu(y)

    if not rev:
        qs_ref[...] = conv_silu(q_ref, qp_ref, qn_ref, 0).astype(BF16)
        ks_ref[...] = (conv_silu(k_ref, kp_ref, kn_ref, WIDTH) * (HEAD_DIM ** -0.5)).astype(BF16)

    row = lax.broadcasted_iota(jnp.int32, (CHUNK, CHUNK), 0)
    col = lax.broadcasted_iota(jnp.int32, (CHUNK, CHUNK), 1)
    allowed = (row <= col) if rev else (row >= col)
    ones = jnp.ones((CHUNK, HEAD_DIM), BF16)

    def chunk(c, h):
        sl = pl.ds(pl.multiple_of(c * CHUNK, CHUNK), CHUNK)
        hs = slice(h * HEAD_DIM, (h + 1) * HEAD_DIM)
        q = qs_ref[sl, hs]
        kt = ks_ref[sl, hs].astype(F32).T
        v2 = jnp.concatenate([v_ref[sl, hs].astype(BF16), ones], axis=1)
        m_prev = m_ref[h, 0:1, :]
        mb = jnp.maximum(pb_ref[h, sl, :], m_prev)
        p = jnp.exp(jnp.where(allowed, gr_ref[h, 0:1, sl] - mb, -jnp.inf))
        w_inter = jnp.exp(m_prev - mb)
        floor = jnp.exp(-(cb_ref[h, sl, :] + mb))
        s = _dot(q, kt.astype(BF16)) * p
        inter = _dot(q, st_ref[h].astype(BF16))
        num2 = _dot(s.astype(BF16), v2) + jnp.concatenate([w_inter, w_inter], axis=1) * inter
        out = num2[:, 0:HEAD_DIM] / jnp.maximum(jnp.abs(num2[:, HEAD_DIM:]), floor)
        f_tot = gs_ref[h, c, 0:1, :]
        m_chunk = gs_ref[h, c, 1:2, :]
        m_new = jnp.maximum(f_tot + m_prev, m_chunk)
        kw = (kt * jnp.exp(gr_ref[h, 1:2, sl] - m_new)).astype(BF16)
        kv2 = _dot(kw, v2)
        a = jnp.exp(f_tot + m_prev - m_new)
        st_ref[h] = jnp.concatenate([a, a], axis=1) * st_ref[h] + kv2
        m_ref[h] = jnp.broadcast_to(m_new, (SUBLANES, LANES))
        return out, sl, hs

    def body(i, carry):
        c = (nch - 1 - i) if rev else i
        for h in range(HEADS):
            out, sl, hs = chunk(c, h)
            if rev:
                hh = jax.nn.sigmoid(o_ref[sl, hs]) * (out + hf_ref[sl, hs])
                y = _head_norm(hh, ng_ref[:, hs]) * _silu(z_ref[sl, hs])
                out_ref[sl, hs] = y.astype(out_ref.dtype)
            else:
                out_ref[sl, hs] = out
        return carry

    lax.fori_loop(0, nch, body, 0, unroll=True)


def _mlstm_dir(rev, u_ml, pb, cb, gr, gs, conv_w, conv_b, norm_g, fwd, groups):
    t = u_ml.shape[0]
    tb = TB_SEQ
    t8 = tb // SUBLANES
    nch = tb // CHUNK
    nblk = _num_blocks(groups)
    d = 1 if rev else 0

    def rb(n):
        return (nblk - 1 - n) if rev else n

    def col(g):
        return lambda n: (rb(n), g)

    def halo_prev(g):
        return lambda n: (jnp.maximum(rb(n) * t8 - 1, 0), g)

    def halo_next(g):
        return lambda n: (jnp.minimum((rb(n) + 1) * t8, t // SUBLANES - 1), g)

    blk = (tb, WIDTH)
    halo = (SUBLANES, WIDTH)
    gate_specs = [
        pl.BlockSpec((None, HEADS, tb, LANES), lambda n: (d, 0, rb(n), 0)),
        pl.BlockSpec((None, HEADS, tb, LANES), lambda n: (d, 0, rb(n), 0)),
        pl.BlockSpec((None, HEADS, SUBLANES, tb), lambda n: (d, 0, 0, rb(n))),
        pl.BlockSpec((None, HEADS, nch, SUBLANES, LANES), lambda n: (d, 0, rb(n), 0, 0)),
    ]
    scratch = [
        pltpu.VMEM((HEADS, HEAD_DIM, 2 * HEAD_DIM), F32),
        pltpu.VMEM((HEADS, SUBLANES, LANES), F32),
    ]
    full = jax.ShapeDtypeStruct((t, WIDTH), F32)
    if rev:
        qs, ks, h_fwd = fwd
        in_specs = [pl.BlockSpec(blk, col(0)), pl.BlockSpec(blk, col(0)), pl.BlockSpec(blk, col(2)),
                    *gate_specs,
                    pl.BlockSpec(blk, col(3)), pl.BlockSpec(blk, col(4)), pl.BlockSpec(blk, col(0)),
                    pl.BlockSpec((1, WIDTH), lambda n: (0, 0))]
        args = [qs, ks, u_ml, pb, cb, gr, gs, u_ml, u_ml, h_fwd, norm_g]
        out_specs = pl.BlockSpec(blk, col(0))
        out_shape = jax.ShapeDtypeStruct(full.shape, BF16)
    else:
        in_specs = [pl.BlockSpec(blk, col(0)), pl.BlockSpec(blk, col(1)), pl.BlockSpec(blk, col(2)),
                    pl.BlockSpec(halo, halo_prev(0)), pl.BlockSpec(halo, halo_next(0)),
                    pl.BlockSpec(halo, halo_prev(1)), pl.BlockSpec(halo, halo_next(1)),
                    *gate_specs,
                    pl.BlockSpec((3, 2 * WIDTH), lambda n: (0, 0)),
                    pl.BlockSpec((1, 2 * WIDTH), lambda n: (0, 0))]
        args = [u_ml, u_ml, u_ml, u_ml, u_ml, u_ml, u_ml, pb, cb, gr, gs, conv_w, conv_b]
        out_specs = [pl.BlockSpec(blk, col(0))] * 3
        out_shape = [full, jax.ShapeDtypeStruct((t, WIDTH), BF16), jax.ShapeDtypeStruct((t, WIDTH), BF16)]
        scratch.append(pltpu.VMEM((tb + 2 * SUBLANES, WIDTH), F32))
    return pl.pallas_call(
        functools.partial(_ml_kernel, rev, groups),
        grid=(nblk,),
        in_specs=in_specs,
        out_specs=out_specs,
        out_shape=out_shape,
        scratch_shapes=scratch,
        name="mlstm_bwd" if rev else "mlstm_fwd",
    )(*args)


def _mlstm(u_ml, pb, cb, gr, gs, conv_w, conv_b, norm_g, groups):
    h_fwd, qs, ks = _mlstm_dir(False, u_ml, pb, cb, gr, gs, conv_w, conv_b, norm_g, None, groups)
    return _mlstm_dir(True, u_ml, pb, cb, gr, gs, conv_w, conv_b, norm_g, (qs, ks, h_fwd), groups)


def _rt_kernel(rev, groups, *refs):
    if rev:
        (qs_ref, ks_ref, v_ref, dec_ref, qd_ref, kd_ref, z_ref, hf_ref, ng_ref,
         out_ref, st_ref) = refs
    else:
        (qs_ref, ks_ref, v_ref, dec_ref, qd_ref, kd_ref, out_ref, st_ref) = refs
    nch = TB_SEQ // CHUNK
    n = pl.program_id(0)
    sb, nb = _seq_pos((_num_blocks(groups) - 1 - n) if rev else n, groups)

    @pl.when(sb == (nb - 1 if rev else 0))
    def _():
        st_ref[...] = jnp.zeros_like(st_ref)

    def body(i, carry):
        c = (nch - 1 - i) if rev else i
        sl = pl.ds(pl.multiple_of(c * CHUNK, CHUNK), CHUNK)
        for h in range(HEADS):
            hs = slice(h * HEAD_DIM, (h + 1) * HEAD_DIM)
            q = qs_ref[sl, hs]
            kt = ks_ref[sl, hs].astype(F32).T
            vb = v_ref[sl, hs].astype(BF16)
            s = _dot(q, kt.astype(BF16)) * dec_ref[h]
            out = _dot(s.astype(BF16), vb) + _dot((q.astype(F32) * qd_ref[h]).astype(BF16),
                                                  st_ref[h].astype(BF16))
            st_ref[h] = kd_ref[h, 1:2, :] * st_ref[h] + _dot((kt * kd_ref[h, 0:1, :]).astype(BF16), vb)
            if rev:
                y = _head_norm(out + hf_ref[sl, hs], ng_ref[:, hs]) * _silu(z_ref[sl, hs])
                out_ref[sl, hs] = y.astype(out_ref.dtype)
            else:
                out_ref[sl, hs] = out
        return carry

    lax.fori_loop(0, nch, body, 0, unroll=not rev)


def _retention_dir(rev, u_qk, u_vz, dec, qdec, kdec, norm_g, h_fwd, groups):
    tb = TB_SEQ
    nblk = _num_blocks(groups)
    d = 1 if rev else 0

    def rb(n):
        return (nblk - 1 - n) if rev else n

    def col(g):
        return lambda n: (rb(n), g)

    blk = (tb, WIDTH)
    tbl = lambda n: (d, 0, 0, 0)
    in_specs = [
        pl.BlockSpec(blk, col(0)), pl.BlockSpec(blk, col(1)), pl.BlockSpec(blk, col(0)),
        pl.BlockSpec((None, HEADS, CHUNK, CHUNK), tbl),
        pl.BlockSpec((None, HEADS, CHUNK, HEAD_DIM), tbl),
        pl.BlockSpec((None, HEADS, SUBLANES, CHUNK), tbl),
    ]
    args = [u_qk, u_qk, u_vz, dec, qdec, kdec]
    if rev:
        in_specs += [pl.BlockSpec(blk, col(1)), pl.BlockSpec(blk, col(0)),
                     pl.BlockSpec((1, WIDTH), lambda n: (0, 0))]
        args += [u_vz, h_fwd, norm_g]
    out_shape = jax.ShapeDtypeStruct((u_qk.shape[0], WIDTH), BF16 if rev else F32)
    out_specs = pl.BlockSpec(blk, col(0))
    return pl.pallas_call(
        functools.partial(_rt_kernel, rev, groups),
        grid=(nblk,),
        in_specs=in_specs,
        out_specs=out_specs,
        out_shape=out_shape,
        scratch_shapes=[pltpu.VMEM((HEADS, HEAD_DIM, HEAD_DIM), F32)],
        name="retention_bwd" if rev else "retention_fwd",
    )(*args)


def _retention(u_qk, u_vz, tabs, norm_g, groups):
    dec, qdec, kdec = tabs
    h_fwd = _retention_dir(False, u_qk, u_vz, dec, qdec, kdec, norm_g, None, groups)
    return _retention_dir(True, u_qk, u_vz, dec, qdec, kdec, norm_g, h_fwd, groups)


def _rotary_tables(seq):
    inv = ROPE_BASE ** (-jnp.arange(0, HEAD_DIM, 2, dtype=F32) / HEAD_DIM)
    ang = jnp.arange(seq, dtype=F32)[:, None] * inv[None, :]
    cos, sin = jnp.cos(ang), jnp.sin(ang)
    return jnp.concatenate([cos, cos], axis=1), jnp.concatenate([-sin, sin], axis=1)


def _retention_tables():
    pos = np.arange(CHUNK, dtype=np.float64)
    rel = pos[:, None] - pos[None, :]
    dec = np.zeros((2, HEADS, CHUNK, CHUNK))
    qdec = np.zeros((2, HEADS, CHUNK, HEAD_DIM))
    kdec = np.zeros((2, HEADS, SUBLANES, CHUNK))
    for h in range(HEADS):
        lf, lb = RT_LOG_GAMMA_FWD[h], RT_LOG_GAMMA_BWD[h]
        dec[0, h] = np.where(rel >= 0, np.exp(np.maximum(rel, 0.0) * lf), 0.0)
        dec[1, h] = np.where(rel <= 0, np.exp(np.maximum(-rel, 0.0) * lb), 0.0)
        qdec[0, h] = np.exp((pos + 1.0) * lf)[:, None]
        qdec[1, h] = np.exp((CHUNK - pos) * lb)[:, None]
        kdec[0, h, 0] = np.exp((CHUNK - 1.0 - pos) * lf)
        kdec[1, h, 0] = np.exp(pos * lb)
        kdec[0, h, 1] = np.exp(CHUNK * lf)
        kdec[1, h, 1] = np.exp(CHUNK * lb)
    return jnp.asarray(dec, F32), jnp.asarray(qdec, F32), jnp.asarray(kdec, F32)


def _filt_kernel(ft_ref, fr_ref, f0_ref, w1_ref, b1_ref, w2_ref, b2_ref, fq_ref,
                 w3f_ref, w3b_ref, dlf_ref, dlb_ref, gb_ref, gf_ref, s_ref, hf3_ref, hr3_ref):
    lt = pl.program_id(1)
    fq = fq_ref[...]

    def hidden(ft):
        h1 = jnp.sin(fq * (_dot_f32(w1_ref[...], ft) + b1_ref[...]))
        return jnp.sin(fq * (_dot_f32(w2_ref[...], h1) + b2_ref[...]))

    @pl.when((pl.program_id(2) == 0) & (pl.program_id(3) == 0))
    def _():
        hf3_ref[...] = _split3_rows(hidden(ft_ref[...]))
        hr3_ref[...] = _split3_rows(hidden(fr_ref[...]))

    def lhs3(w):
        hi, lo = _split_hi_lo(w)
        return jnp.concatenate([hi, hi, lo], axis=1)

    ft = ft_ref[...]
    fr = fr_ref[...]
    dl_f = jnp.abs(dlf_ref[...])
    dl_b = jnp.abs(dlb_ref[...])
    fwd = _dot(lhs3(w3f_ref[...]), hf3_ref[...]) * jnp.exp(-ft[0:1, :] * dl_f)
    bwd = _dot(lhs3(w3b_ref[...]), hr3_ref[...]) * jnp.exp(-fr[0:1, :] * dl_b)
    f0 = f0_ref[...]
    bwd0 = (_dot_f32(w3b_ref[...], hidden(f0)) * jnp.exp(-f0[0:1, :] * dl_b))[:, 0:1]
    lag = lax.broadcasted_iota(jnp.int32, fwd.shape, 1) + lt * LT_FILT
    fwd = jnp.where(lag == 0, fwd + bwd0, fwd)
    bwd = jnp.where(lag == 0, 0.0, bwd)
    gb_ref[...] = bwd.reshape(gb_ref.shape)
    gf_ref[...] = fwd.reshape(gf_ref.shape)
    part = jnp.sum(jnp.abs(fwd) + jnp.abs(bwd), axis=1, keepdims=True)
    lane = lax.broadcasted_iota(jnp.int32, (LANES, LANES), 1)
    s_ref[...] = jnp.where(lane == 0, part, 0.0)


def _filters(seq, w1t, b1, w2t, b2, freq, w3t, deltas):
    t = jnp.linspace(0.0, 1.0, seq, dtype=F32)[None, :]
    w = (2.0 * math.pi / seq) * jnp.arange(seq, dtype=F32)[None, :]
    bands = jnp.linspace(1e-4, HY_BANDS - 1, HY_BANDS, dtype=F32)[:, None]
    feats = jnp.concatenate([t, jnp.cos(bands * w), -jnp.sin(bands * w),
                             jnp.zeros((FEAT_ROWS - 1 - 2 * HY_BANDS, seq), F32)], axis=0)
    feats_rev = jnp.concatenate([feats[:, :1], feats[:, :0:-1]], axis=1)
    feats0 = jnp.broadcast_to(feats[:, :1], (FEAT_ROWS, LANES))
    nlt = seq // LT_FILT
    ct = WIDTH // LANES
    lyr = lambda l, i, o, c: (l, 0, 0)
    return pl.pallas_call(
        _filt_kernel,
        grid=(DEPTH, nlt, HY_ORDER, ct),
        in_specs=[
            pl.BlockSpec((FEAT_ROWS, LT_FILT), lambda l, i, o, c: (0, i)),
            pl.BlockSpec((FEAT_ROWS, LT_FILT), lambda l, i, o, c: (0, i)),
            pl.BlockSpec((FEAT_ROWS, LANES), lambda l, i, o, c: (0, 0)),
            pl.BlockSpec((None, HY_HIDDEN, FEAT_ROWS), lyr),
            pl.BlockSpec((None, HY_HIDDEN, 1), lyr),
            pl.BlockSpec((None, HY_HIDDEN, HY_HIDDEN), lyr),
            pl.BlockSpec((None, HY_HIDDEN, 1), lyr),
            pl.BlockSpec((None, HY_HIDDEN, 1), lyr),
            pl.BlockSpec((None, LANES, HY_HIDDEN), lambda l, i, o, c: (l, o * 2 * ct + c, 0)),
            pl.BlockSpec((None, LANES, HY_HIDDEN), lambda l, i, o, c: (l, o * 2 * ct + ct + c, 0)),
            pl.BlockSpec((None, LANES, 1), lambda l, i, o, c: (l, o * 2 * ct + c, 0)),
            pl.BlockSpec((None, LANES, 1), lambda l, i, o, c: (l, o * 2 * ct + ct + c, 0)),
        ],
        out_specs=[
            pl.BlockSpec((None, None, LANES, LT_FILT // LANES, LANES), lambda l, i, o, c: (l, o, c, i, 0)),
            pl.BlockSpec((None, None, LANES, LT_FILT // LANES, LANES), lambda l, i, o, c: (l, o, c, i, 0)),
            pl.BlockSpec((None, None, LANES, LANES), lambda l, i, o, c: (l, o, c, i)),
        ],
        out_shape=[
            jax.ShapeDtypeStruct((DEPTH, HY_ORDER, WIDTH, seq // LANES, LANES), F32),
            jax.ShapeDtypeStruct((DEPTH, HY_ORDER, WIDTH, seq // LANES, LANES), F32),
            jax.ShapeDtypeStruct((DEPTH, HY_ORDER, WIDTH, nlt * LANES), F32),
        ],
        scratch_shapes=[pltpu.VMEM((3 * HY_HIDDEN, LT_FILT), BF16),
                        pltpu.VMEM((3 * HY_HIDDEN, LT_FILT), BF16)],
        name="hyena_filters",
    )(feats, feats_rev, feats0, w1t, b1, w2t, b2, freq, w3t, w3t, deltas, deltas)


def _split_hi_lo(x):
    hi = x.astype(BF16)
    lo = (x - hi.astype(F32)).astype(BF16)
    return hi, lo


def _split3_rows(x):
    hi, lo = _split_hi_lo(x)
    return jnp.concatenate([hi, lo, hi], axis=0)


def _split3_cols(x):
    hi, lo = _split_hi_lo(x)
    return jnp.concatenate([hi, lo, hi], axis=1)


def _np_split(m):
    hi = m.astype(ml_dtypes.bfloat16)
    lo = (m - hi.astype(np.float64)).astype(ml_dtypes.bfloat16)
    return hi, lo


def _dft_tables(seq):
    nj = seq // LANES
    n1 = 2 * nj
    n = n1 * LANES
    khp = n1 // 2 + KH_PAD
    k1 = np.arange(khp, dtype=np.float64)[:, None]
    valid = (k1 <= n1 // 2).astype(np.float64)
    ang = 2.0 * np.pi * k1 * np.arange(n1, dtype=np.float64)[None, :] / n1
    fa_full = np.concatenate([valid * np.cos(ang), -valid * np.sin(ang)], axis=0)

    def lhs3(m):
        hi, lo = _np_split(m)
        return jnp.asarray(np.concatenate([hi, hi, lo], axis=1))

    def rhs3(m):
        hi, lo = _np_split(m)
        return jnp.asarray(np.concatenate([hi, hi, lo], axis=0))

    n2 = np.arange(LANES, dtype=np.float64)
    ang2 = 2.0 * np.pi * n2[:, None] * n2[None, :] / LANES
    c2, s2 = np.cos(ang2), np.sin(ang2)
    fb = np.block([[c2, -s2], [s2, c2]])
    fbi = np.block([[c2, s2], [-s2, c2]])
    w = np.where((k1 == 0) | (k1 == n1 // 2), 1.0, 2.0) * valid
    th = 2.0 * np.pi * np.arange(nj, dtype=np.float64)[:, None] * k1.T / n1
    fai = np.concatenate([(w.T / n) * np.cos(th), -(w.T / n) * np.sin(th)], axis=1)
    phi = 2.0 * np.pi * k1 * n2[None, :] / n
    return dict(
        nj=nj, n1=n1, khp=khp,
        fa=lhs3(fa_full[:, :nj]), fa_full=lhs3(fa_full), fb=rhs3(fb), fbi=rhs3(fbi), fai=lhs3(fai),
        twc=jnp.asarray(valid * np.cos(phi), F32), tws=jnp.asarray(valid * np.sin(phi), F32))


def _dft_rows(fa_ref, twc, tws, khp, cols):
    rhs = cols[0] if len(cols) == 1 else jnp.concatenate(cols, axis=1)
    y = _dot(fa_ref[...], _split3_rows(rhs))
    out = []
    for i in range(len(cols)):
        yre = y[0:khp, i * LANES:(i + 1) * LANES]
        yim = y[khp:2 * khp, i * LANES:(i + 1) * LANES]
        out.append((yre * twc + yim * tws, yim * twc - yre * tws))
    return out


def _spec_kernel(khp, gf_ref, gb_ref, s_ref, fa_ref, fb_ref, twc_ref, tws_ref, h_ref, ybuf):
    twc = twc_ref[...]
    tws = tws_ref[...]

    def circular(c):
        return jnp.concatenate([gf_ref[c], gb_ref[c]], axis=0)

    for c in range(0, CB_SPEC, 2):
        res = _dft_rows(fa_ref, twc, tws, khp, [circular(c), circular(c + 1)])
        for i, (yre, yim) in enumerate(res):
            r0 = (c + i) * khp
            ybuf[r0:r0 + khp, 0:LANES] = yre
            ybuf[r0:r0 + khp, LANES:2 * LANES] = yim
    x = _dot(_split3_cols(ybuf[...]), fb_ref[...])
    for c in range(CB_SPEC):
        norm = jnp.sum(s_ref[c:c + 1, :], axis=1, keepdims=True)
        h_ref[c] = x[c * khp:(c + 1) * khp, :] / norm


def _spectrum(gf, gb, s, tab):
    nch = gf.shape[0]
    nj, khp = tab["nj"], tab["khp"]
    const = lambda a: pl.BlockSpec(a.shape, lambda i: (0, 0))
    return pl.pallas_call(
        functools.partial(_spec_kernel, khp),
        grid=(nch // CB_SPEC,),
        in_specs=[
            pl.BlockSpec((CB_SPEC, nj, LANES), lambda i: (i, 0, 0)),
            pl.BlockSpec((CB_SPEC, nj, LANES), lambda i: (i, 0, 0)),
            pl.BlockSpec((CB_SPEC, s.shape[-1]), lambda i: (i, 0)),
            const(tab["fa_full"]), const(tab["fb"]), const(tab["twc"]), const(tab["tws"]),
        ],
        out_specs=pl.BlockSpec((CB_SPEC, khp, 2 * LANES), lambda i: (i, 0, 0)),
        out_shape=jax.ShapeDtypeStruct((nch, khp, 2 * LANES), F32),
        scratch_shapes=[pltpu.VMEM((CB_SPEC * khp, 2 * LANES), F32)],
        name="hyena_spectrum",
    )(gf, gb, s, tab["fa_full"], tab["fb"], tab["twc"], tab["tws"])


def _hyfft_kernel(bsz, nj, khp, prm_ref, v_ref, x1_ref, x2_ref, z_ref, h_ref,
                  fa_ref, fb_ref, fbi_ref, fai_ref, twc_ref, tws_ref, out_ref,
                  xs_ref, ybuf, pbuf):
    ct = pl.program_id(0)
    rows = bsz * nj
    pairs = [(b, b + 1) for b in range(0, bsz, 2)] if bsz > 1 else [(0,)]
    twc = twc_ref[...]
    tws = tws_ref[...]

    lane = lax.broadcasted_iota(jnp.int32, (rows, LANES), 1)
    blk = lax.broadcasted_iota(jnp.int32, (rows, LANES), 0) & (nj - 1)
    first = (lane == 0) & (blk == 0)
    last = (lane == LANES - 1) & (blk == nj - 1)

    def conv3(x, base):
        r = pltpu.roll(x, 1, 1)
        xm1 = jnp.where(lane == 0, pltpu.roll(r, 1, 0), r)
        xm1 = jnp.where(first, 0.0, xm1)
        l = pltpu.roll(x, LANES - 1, 1)
        xp1 = jnp.where(lane == LANES - 1, pltpu.roll(l, rows - 1, 0), l)
        xp1 = jnp.where(last, 0.0, xp1)
        return prm_ref[base] * xm1 + prm_ref[base + 1] * x + prm_ref[base + 2] * xp1 + prm_ref[base + 3]

    def row0(c, b):
        return (c * bsz + b) * khp

    def forward(c, x):
        for pr in pairs:
            res = _dft_rows(fa_ref, twc, tws, khp, [x[b * nj:(b + 1) * nj, :] for b in pr])
            for b, (yre, yim) in zip(pr, res):
                r0 = row0(c, b)
                ybuf[r0:r0 + khp, 0:LANES] = yre
                ybuf[r0:r0 + khp, LANES:2 * LANES] = yim

    def spectral(order):
        for c0 in range(0, CB_HY, CG_HY):
            g0, g1 = row0(c0, 0), row0(c0 + CG_HY, 0)
            x = _dot(_split3_cols(ybuf[g0:g1, :]), fb_ref[...])
            for c in range(c0, c0 + CG_HY):
                hre = h_ref[order, c, :, 0:LANES]
                him = h_ref[order, c, :, LANES:2 * LANES]
                for b in range(bsz):
                    r0 = row0(c, b)
                    xre = x[r0 - g0:r0 - g0 + khp, 0:LANES]
                    xim = x[r0 - g0:r0 - g0 + khp, LANES:2 * LANES]
                    pbuf[r0:r0 + khp, 0:LANES] = xre * hre - xim * him
                    pbuf[r0:r0 + khp, LANES:2 * LANES] = xre * him + xim * hre
            r = _dot(_split3_cols(pbuf[g0:g1, :]), fbi_ref[...])
            for c in range(c0, c0 + CG_HY):
                for b in range(bsz):
                    r0 = row0(c, b)
                    rre = r[r0 - g0:r0 - g0 + khp, 0:LANES]
                    rim = r[r0 - g0:r0 - g0 + khp, LANES:2 * LANES]
                    ybuf[r0:r0 + khp, 0:LANES] = rre * twc - rim * tws
                    ybuf[r0:r0 + khp, LANES:2 * LANES] = rre * tws + rim * twc

    def inverse(c):
        outs = [None] * bsz
        for pr in pairs:
            rre = [ybuf[row0(c, b):row0(c, b) + khp, 0:LANES] for b in pr]
            rim = [ybuf[row0(c, b):row0(c, b) + khp, LANES:2 * LANES] for b in pr]
            if len(pr) > 1:
                rre, rim = [jnp.concatenate(rre, axis=1)], [jnp.concatenate(rim, axis=1)]
            y = _dot(fai_ref[...], _split3_rows(jnp.concatenate([rre[0], rim[0]], axis=0)))
            for i, b in enumerate(pr):
                outs[b] = y[:, i * LANES:(i + 1) * LANES]
        return outs[0] if bsz == 1 else jnp.concatenate(outs, axis=0)

    for c in range(CB_HY):
        base = (ct * CB_HY + c) * HY_PRM
        v = conv3(v_ref[c], base)
        xs_ref[0, c] = v
        xs_ref[1, c] = conv3(x1_ref[c], base + 4)
        xs_ref[2, c] = conv3(x2_ref[c], base + 8)
        forward(c, v)
    spectral(0)
    for c in range(CB_HY):
        base = (ct * CB_HY + c) * HY_PRM
        v = xs_ref[0, c]
        z1 = xs_ref[1, c] * (inverse(c) + prm_ref[base + 12] * v)
        xs_ref[0, c] = z1
        forward(c, z1)
    spectral(1)
    for c in range(CB_HY):
        base = (ct * CB_HY + c) * HY_PRM
        z1 = xs_ref[0, c]
        y2 = inverse(c) + prm_ref[base + 13] * z1
        out_ref[:, c, :] = xs_ref[2, c] * y2 * _silu(z_ref[c])


def _hyena_fft(prm, u_t, h, layer, tab, row0, bsz, seq):
    nj, khp = tab["nj"], tab["khp"]
    nct = WIDTH // CB_HY
    rows = bsz * nj
    blk = (CB_HY, rows, LANES)
    rblk = row0 // (rows * LANES)
    assert rblk * rows * LANES == row0
    const = lambda a: pl.BlockSpec(a.shape, lambda i: (0, 0))
    return pl.pallas_call(
        functools.partial(_hyfft_kernel, bsz, nj, khp),
        grid=(nct,),
        in_specs=[
            pl.BlockSpec(memory_space=pltpu.SMEM),
            pl.BlockSpec(blk, lambda i: (i, rblk, 0)),
            pl.BlockSpec(blk, lambda i: (nct + i, rblk, 0)),
            pl.BlockSpec(blk, lambda i: (2 * nct + i, rblk, 0)),
            pl.BlockSpec(blk, lambda i: (3 * nct + i, rblk, 0)),
            pl.BlockSpec((None, HY_ORDER, CB_HY, khp, 2 * LANES), lambda i: (layer, 0, i, 0, 0)),
            const(tab["fa"]), const(tab["fb"]), const(tab["fbi"]), const(tab["fai"]),
            const(tab["twc"]), const(tab["tws"]),
        ],
        out_specs=pl.BlockSpec((rows, CB_HY, LANES), lambda i: (0, i, 0)),
        out_shape=jax.ShapeDtypeStruct((rows, WIDTH, LANES), F32),
        scratch_shapes=[
            pltpu.VMEM((3, CB_HY, bsz * nj, LANES), F32),
            pltpu.VMEM((CB_HY * bsz * khp, 2 * LANES), F32),
            pltpu.VMEM((CB_HY * bsz * khp, 2 * LANES), F32),
        ],
        name="hyena_fft",
    )(prm, u_t, u_t, u_t, u_t, h, tab["fa"], tab["fb"], tab["fbi"], tab["fai"],
      tab["twc"], tab["tws"])


def _out_kernel(final, starts, nx, *refs):
    ng = len(starts)
    x_refs = refs[0:nx]
    yml_ref, yrt_ref = refs[nx:nx + 2]
    yhy_refs = refs[nx + 2:nx + 2 + ng]
    w_ref, g_ref = refs[nx + 2 + ng:nx + 4 + ng]
    out_refs = refs[nx + 4 + ng:-1]
    lhs_ref = refs[-1]
    i = pl.program_id(0)
    lhs_ref[:, 0:WIDTH] = yml_ref[...]
    lhs_ref[:, WIDTH:2 * WIDTH] = yrt_ref[...]
    jb = TM_OUT // LANES

    def fill_hyena(ref):
        for j in range(jb):
            lhs_ref[j * LANES:(j + 1) * LANES, 2 * WIDTH:] = ref[j].T.astype(BF16)

    for gi in range(ng):
        inside = i >= starts[gi]
        if gi + 1 < ng:
            inside = inside & (i < starts[gi + 1])
        pl.when(inside)(functools.partial(fill_hyena, yhy_refs[gi]))
    x = _x_tile(x_refs, starts if nx > 1 else [0]) + _dot(lhs_ref[...], w_ref[...])
    if not final:
        out_refs[0][...] = x
        return
    ms = jnp.mean(x * x, axis=-1, keepdims=True)
    x = x * lax.rsqrt(ms + RMS_EPS) * g_ref[...]
    for gi in range(ng):
        inside = i >= starts[gi]
        if gi + 1 < ng:
            inside = inside & (i < starts[gi + 1])

        @pl.when(inside)
        def _(gi=gi):
            out_refs[gi][...] = x


def _out_proj(xs, y_ml, y_rt, y_hy, w, gain, final, groups):
    t = y_ml.shape[0]
    jb = TM_OUT // LANES
    starts = [r0 // TM_OUT for (r0, _, _) in groups]
    counts = [bsz * seq // TM_OUT for (_, bsz, seq) in groups]

    def tile(gi):
        return lambda i: jnp.clip(i - starts[gi], 0, counts[gi] - 1)

    hy_specs = [pl.BlockSpec((jb, WIDTH, LANES), lambda i, f=tile(gi): (f(i), 0, 0))
                for gi in range(len(groups))]
    if final:
        out_specs = [pl.BlockSpec((TM_OUT, D_MODEL), lambda i, f=tile(gi): (f(i), 0))
                     for gi in range(len(groups))]
        out_shape = [jax.ShapeDtypeStruct((bsz * seq, D_MODEL), F32) for (_, bsz, seq) in groups]
    else:
        out_specs = [pl.BlockSpec((TM_OUT, D_MODEL), lambda i: (i, 0))]
        out_shape = [jax.ShapeDtypeStruct((t, D_MODEL), F32)]
    return pl.pallas_call(
        functools.partial(_out_kernel, final, starts, len(xs)),
        grid=(t // TM_OUT,),
        in_specs=[
            *_x_specs(xs, TM_OUT, 1),
            pl.BlockSpec((TM_OUT, WIDTH), lambda i: (i, 0)),
            pl.BlockSpec((TM_OUT, WIDTH), lambda i: (i, 0)),
            *hy_specs,
            pl.BlockSpec((3 * WIDTH, D_MODEL), lambda i: (0, 0)),
            pl.BlockSpec((1, D_MODEL), lambda i: (0, 0)),
        ],
        out_specs=out_specs,
        out_shape=out_shape,
        scratch_shapes=[pltpu.VMEM((TM_OUT, 3 * WIDTH), BF16)],
        name="out_proj",
    )(*xs, y_ml, y_rt, *y_hy, w, gain)


def kernel(x_prompt, x_sample, norm_g, w_in, ml_conv_w, ml_conv_b, ml_gate_b, ml_norm_g, rt_norm_g, hy_conv_w, hy_conv_b, hy_w1, hy_b1, hy_w2, hy_b2, hy_w3, hy_freq, hy_deltas, hy_skip, w_out, final_g):
    groups = []
    row0 = 0
    for xg in (x_prompt, x_sample):
        bsz, seq, _ = xg.shape
        groups.append((row0, bsz, seq))
        row0 += bsz * seq
    xs = [x_prompt.reshape(-1, D_MODEL), x_sample.reshape(-1, D_MODEL)]

    c_gate = ML_COLS
    c_rt = c_gate + 4 * HEADS
    c_hy = c_rt + RT_COLS
    gate_pad = jnp.zeros((DEPTH, D_MODEL, LANES - 4 * HEADS), F32)
    w_tok = jnp.concatenate([w_in[:, :, :c_gate], w_in[:, :, c_gate:c_rt], gate_pad,
                             w_in[:, :, c_rt:c_hy]], axis=2).astype(BF16)
    w_gt = jnp.swapaxes(w_in[:, :, c_gate:c_rt], 1, 2).astype(BF16)
    w_hyt = jnp.swapaxes(w_in[:, :, c_hy:], 1, 2).astype(BF16)
    w_out_b = w_out.astype(BF16)
    gate_b = ml_gate_b.reshape(DEPTH, 1, 4 * HEADS)
    gate_b_row = jnp.concatenate([gate_b, jnp.zeros((DEPTH, 1, LANES - 4 * HEADS), F32)], axis=2)
    gate_b_col = ml_gate_b.reshape(DEPTH, 4 * HEADS, 1)

    cw = hy_conv_w.reshape(DEPTH, 3, 3, WIDTH)
    cbias = hy_conv_b.reshape(DEPTH, 1, 3, WIDTH)
    taps = jnp.concatenate([cw, cbias], axis=1)
    taps = jnp.transpose(taps, (0, 3, 2, 1)).reshape(DEPTH, WIDTH, 12)
    hy_prm = jnp.concatenate([taps, jnp.transpose(hy_skip, (0, 2, 1)),
                              jnp.zeros((DEPTH, WIDTH, HY_PRM - 14), F32)],
                             axis=2).reshape(DEPTH, WIDTH * HY_PRM)

    w1t = jnp.concatenate([jnp.swapaxes(hy_w1, 1, 2),
                           jnp.zeros((DEPTH, HY_HIDDEN, FEAT_ROWS - hy_w1.shape[1]), F32)], axis=2)
    w2t = jnp.swapaxes(hy_w2, 1, 2)
    w3t = jnp.swapaxes(hy_w3, 1, 2)
    b1 = hy_b1[:, :, None]
    b2 = hy_b2[:, :, None]
    freq = hy_freq[:, :, None]
    deltas = hy_deltas.reshape(DEPTH, HY_ORDER * 2 * WIDTH, 1)

    tables, spectra = [], []
    for (_, _, seq) in groups:
        tab = _dft_tables(seq)
        gb, gf, s = _filters(seq, w1t, b1, w2t, b2, freq, w3t, deltas)
        nfil = DEPTH * HY_ORDER * WIDTH
        h = _spectrum(gf.reshape(nfil, tab["nj"], LANES), gb.reshape(nfil, tab["nj"], LANES),
                      s.reshape(nfil, -1), tab)
        tables.append(tab)
        spectra.append(h.reshape(DEPTH, HY_ORDER, WIDTH, tab["khp"], 2 * LANES))
    rot_c, rot_s = _rotary_tables(max(seq for (_, _, seq) in groups))
    rt_tabs = _retention_tables()

    for layer in range(DEPTH):
        gain = norm_g[layer][None, :]
        u_ml, u_g, u_qk, u_vz, u_gt = _in_proj(xs, gain, w_tok[layer], w_gt[layer], rot_c, rot_s, groups)
        u_hyt = _in_proj_hy(xs, gain, w_hyt[layer])
        pb, cb, gr, gs = _gates(u_g, u_gt, gate_b_row[layer], gate_b_col[layer])
        y_ml = _mlstm(u_ml, pb, cb, gr, gs, ml_conv_w[layer], ml_conv_b[layer][None, :],
                      ml_norm_g[layer][None, :], groups)
        y_rt = _retention(u_qk, u_vz, rt_tabs, rt_norm_g[layer][None, :], groups)
        y_hy = [_hyena_fft(hy_prm[layer], u_hyt, spectra[gi], layer, tables[gi], r0, bsz, seq)
                for gi, (r0, bsz, seq) in enumerate(groups)]
        outs = _out_proj(xs, y_ml, y_rt, y_hy, w_out_b[layer], final_g[None, :],
                         layer == DEPTH - 1, groups)
        xs = [outs[0]]

    return outs[0].reshape(x_prompt.shape), outs[1].reshape(x_sample.shape)
```

```python
import functools
import math

import ml_dtypes
import numpy as np
import jax
import jax.numpy as jnp
from jax import lax
from jax.experimental import pallas as pl
from jax.experimental.pallas import tpu as pltpu

F32 = jnp.float32
BF16 = jnp.bfloat16

D_MODEL = 1024
DEPTH = 4
CHUNK = 128
HEADS = 4
HEAD_DIM = 128
WIDTH = 512
HY_ORDER = 2
HY_BANDS = 16
HY_HIDDEN = 64
FEAT_ROWS = 40
HY_PRM = 16
ROPE_BASE = 10000.0
RMS_EPS = 1e-6
HEAD_NORM_EPS = 1e-5
M_INIT = -1e30
K_LOG_SCALE = -0.5 * math.log(HEAD_DIM)
RT_LOG_GAMMA_FWD = tuple(math.log(1.0 - 2.0 ** (-5.0 - h)) for h in range(HEADS))
RT_LOG_GAMMA_BWD = tuple(math.log(1.0 - 2.0 ** (-5.5 - h)) for h in range(HEADS))

LANES = 128
SUBLANES = 8
TM_IN = 256
TM_HY = 1024
TR_HY = 1024
TM_OUT = 1024
TB_SEQ = 1024
TG = 1024
CB_HY = 8
CG_HY = 2
CB_SPEC = 16
KH_PAD = 8
LT_FILT = 2048
ML_COLS = 5 * WIDTH
RT_COLS = 4 * WIDTH
HY_ROWS = 4 * WIDTH
HIGHEST = lax.Precision.HIGHEST


def _dot(a, b):
    return jnp.dot(a, b, preferred_element_type=F32)


def _dot_nt(a, b):
    return lax.dot_general(a, b, (((1,), (1,)), ((), ())), preferred_element_type=F32)


def _dot_tn(a, b):
    return lax.dot_general(a, b, (((0,), (0,)), ((), ())), preferred_element_type=F32)


def _dot_f32(a, b):
    return jnp.dot(a, b, precision=HIGHEST, preferred_element_type=F32)


def _silu(y):
    return y * jax.nn.sigmoid(y)


def _log_sigmoid(x):
    return -(jnp.maximum(-x, 0.0) + jnp.log1p(jnp.exp(-jnp.abs(x))))


def _head_norm(h, gain):
    mu = jnp.mean(h, axis=-1, keepdims=True)
    hc = h - mu
    var = jnp.mean(hc * hc, axis=-1, keepdims=True)
    return hc * lax.rsqrt(var + HEAD_NORM_EPS) * gain


def _tile_starts(xs, tm):
    starts, s = [], 0
    for a in xs:
        starts.append(s)
        s += a.shape[0] // tm
    return starts, s


def _x_specs(xs, tm, ngrid):
    starts, _ = _tile_starts(xs, tm)
    specs = []
    for a, s in zip(xs, starts):
        cnt = a.shape[0] // tm
        if ngrid == 1:
            specs.append(pl.BlockSpec((tm, D_MODEL), lambda i, s=s, cnt=cnt: (jnp.clip(i - s, 0, cnt - 1), 0)))
        else:
            specs.append(pl.BlockSpec((tm, D_MODEL), lambda i, r, s=s, cnt=cnt: (jnp.clip(i - s, 0, cnt - 1), 0)))
    return specs


def _x_tile(x_refs, starts):
    x = x_refs[0][...]
    for ref, s in zip(x_refs[1:], starts[1:]):
        x = jnp.where(pl.program_id(0) >= s, ref[...], x)
    return x


def _rms_bf16(x, g_ref):
    ms = jnp.mean(x * x, axis=-1, keepdims=True)
    return (x * lax.rsqrt(ms + RMS_EPS) * g_ref[...]).astype(BF16)


def _in_kernel(starts, *refs):
    nx = len(starts)
    (g_ref, wtok_ref, wgt_ref, cc_ref, ss_ref,
     oml_ref, og_ref, oqk_ref, ovz_ref, ogt_ref) = refs[nx:]
    h = _rms_bf16(_x_tile(refs[:nx], starts), g_ref)
    y = _dot(h, wtok_ref[...])
    oml_ref[...] = y[:, 0:ML_COLS]
    og_ref[...] = y[:, ML_COLS:ML_COLS + LANES]
    c0 = ML_COLS + LANES
    cc = cc_ref[...]
    ss = ss_ref[...]
    for j in range(2 * HEADS):
        u = y[:, c0 + j * HEAD_DIM:c0 + (j + 1) * HEAD_DIM]
        r = u * cc + pltpu.roll(u, HEAD_DIM // 2, 1) * ss
        if j < HEADS:
            r = r * (HEAD_DIM ** -0.5)
        oqk_ref[:, j * HEAD_DIM:(j + 1) * HEAD_DIM] = r.astype(BF16)
    ovz_ref[...] = y[:, c0 + 2 * WIDTH:]
    ogt_ref[...] = _dot_nt(wgt_ref[...], h)


def _in_proj(xs, gain, w_tok, w_gt, rot_c, rot_s, groups):
    starts, nt = _tile_starts(xs, TM_IN)
    t = nt * TM_IN
    ncol = w_tok.shape[1]
    pos = lambda i: (_seq_pos(i, groups, TM_IN)[0], 0)
    return pl.pallas_call(
        functools.partial(_in_kernel, starts),
        grid=(nt,),
        in_specs=[
            *_x_specs(xs, TM_IN, 1),
            pl.BlockSpec((1, D_MODEL), lambda i: (0, 0)),
            pl.BlockSpec((D_MODEL, ncol), lambda i: (0, 0)),
            pl.BlockSpec((4 * HEADS, D_MODEL), lambda i: (0, 0)),
            pl.BlockSpec((TM_IN, HEAD_DIM), pos),
            pl.BlockSpec((TM_IN, HEAD_DIM), pos),
        ],
        out_specs=[
            pl.BlockSpec((TM_IN, ML_COLS), lambda i: (i, 0)),
            pl.BlockSpec((TM_IN, LANES), lambda i: (i, 0)),
            pl.BlockSpec((TM_IN, 2 * WIDTH), lambda i: (i, 0)),
            pl.BlockSpec((TM_IN, 2 * WIDTH), lambda i: (i, 0)),
            pl.BlockSpec((4 * HEADS, TM_IN), lambda i: (0, i)),
        ],
        out_shape=[
            jax.ShapeDtypeStruct((t, ML_COLS), F32),
            jax.ShapeDtypeStruct((t, LANES), F32),
            jax.ShapeDtypeStruct((t, 2 * WIDTH), BF16),
            jax.ShapeDtypeStruct((t, 2 * WIDTH), F32),
            jax.ShapeDtypeStruct((4 * HEADS, t), F32),
        ],
        name="in_proj",
    )(*xs, gain, w_tok, w_gt, rot_c, rot_s)


def _in_hy_kernel(starts, *refs):
    nx = len(starts)
    g_ref, w_ref, o_ref = refs[nx:]
    y = _dot_nt(w_ref[...], _rms_bf16(_x_tile(refs[:nx], starts), g_ref))
    o_ref[...] = y.reshape(o_ref.shape)


def _in_proj_hy(xs, gain, w_hyt):
    starts, nt = _tile_starts(xs, TM_HY)
    t = nt * TM_HY
    nr = HY_ROWS // TR_HY
    return pl.pallas_call(
        functools.partial(_in_hy_kernel, starts),
        grid=(nt, nr),
        in_specs=[
            *_x_specs(xs, TM_HY, 2),
            pl.BlockSpec((1, D_MODEL), lambda i, r: (0, 0)),
            pl.BlockSpec((TR_HY, D_MODEL), lambda i, r: (r, 0)),
        ],
        out_specs=pl.BlockSpec((TR_HY, TM_HY // LANES, LANES), lambda i, r: (r, i, 0)),
        out_shape=jax.ShapeDtypeStruct((HY_ROWS, t // LANES, LANES), F32),
        name="in_proj_hyena",
    )(*xs, gain, w_hyt)


def _gates_kernel(g_ref, gt_ref, b_ref, bt_ref, pb_ref, cb_ref, gr_ref, gs_ref):
    row = lax.broadcasted_iota(jnp.int32, (CHUNK, CHUNK), 0)
    col = lax.broadcasted_iota(jnp.int32, (CHUNK, CHUNK), 1)
    lower = (row >= col).astype(F32)
    upper = (row <= col).astype(F32)
    gr_ref[...] = jnp.zeros_like(gr_ref)
    gs_ref[...] = jnp.zeros_like(gs_ref)
    for c in range(TG // CHUNK):
        sl = slice(c * CHUNK, (c + 1) * CHUNK)
        lf = _log_sigmoid(g_ref[sl, :] + b_ref[...])
        cum_c = (_dot_f32(lower, lf), _dot_f32(upper, lf))
        pre_t = gt_ref[:, sl] + bt_ref[...]
        lf_t = _log_sigmoid(pre_t)
        cum_r = (_dot_f32(lf_t, upper), _dot_f32(lf_t, lower))
        for d in range(2):
            for h in range(HEADS):
                kf = (2 * d + 1) * HEADS + h
                ki = 2 * d * HEADS + h
                cum = cum_r[d][kf:kf + 1, :]
                li = pre_t[ki:ki + 1, :] + K_LOG_SCALE
                r = li - cum
                f_tot = cum[:, 0:1] if d else cum[:, CHUNK - 1:CHUNK]
                g = f_tot - cum + li
                allowed = (row <= col) if d else (row >= col)
                pmax = jnp.max(jnp.where(allowed, r, -jnp.inf), axis=1, keepdims=True)
                pb_ref[d, h, sl, :] = jnp.broadcast_to(pmax, (CHUNK, LANES))
                cb_ref[d, h, sl, :] = jnp.broadcast_to(cum_c[d][:, kf:kf + 1], (CHUNK, LANES))
                gr_ref[d, h, 0:1, sl] = r
                gr_ref[d, h, 1:2, sl] = g
                gs_ref[d, h, c, 0:1, :] = jnp.broadcast_to(f_tot, (1, LANES))
                gs_ref[d, h, c, 1:2, :] = jnp.broadcast_to(jnp.max(g, axis=1, keepdims=True), (1, LANES))


def _gates(g, gt, bias_row, bias_col):
    t = g.shape[0]
    nc = TG // CHUNK
    return pl.pallas_call(
        _gates_kernel,
        grid=(t // TG,),
        in_specs=[
            pl.BlockSpec((TG, LANES), lambda i: (i, 0)),
            pl.BlockSpec((4 * HEADS, TG), lambda i: (0, i)),
            pl.BlockSpec((1, LANES), lambda i: (0, 0)),
            pl.BlockSpec((4 * HEADS, 1), lambda i: (0, 0)),
        ],
        out_specs=[
            pl.BlockSpec((2, HEADS, TG, LANES), lambda i: (0, 0, i, 0)),
            pl.BlockSpec((2, HEADS, TG, LANES), lambda i: (0, 0, i, 0)),
            pl.BlockSpec((2, HEADS, SUBLANES, TG), lambda i: (0, 0, 0, i)),
            pl.BlockSpec((2, HEADS, nc, SUBLANES, LANES), lambda i: (0, 0, i, 0, 0)),
        ],
        out_shape=[
            jax.ShapeDtypeStruct((2, HEADS, t, LANES), F32),
            jax.ShapeDtypeStruct((2, HEADS, t, LANES), F32),
            jax.ShapeDtypeStruct((2, HEADS, SUBLANES, t), F32),
            jax.ShapeDtypeStruct((2, HEADS, t // CHUNK, SUBLANES, LANES), F32),
        ],
        name="gates",
    )(g, gt, bias_row, bias_col)


def _seq_pos(g, groups, tb=TB_SEQ):
    pos, nbs = None, None
    start = 0
    for (_, bsz, seq) in groups:
        nb = seq // tb
        p = (g - start) % nb
        pos = p if pos is None else jnp.where(g >= start, p, pos)
        nbs = nb if nbs is None else jnp.where(g >= start, nb, nbs)
        start += bsz * nb
    return pos, nbs


def _num_blocks(groups):
    return sum(bsz * seq // TB_SEQ for (_, bsz, seq) in groups)


def _ml_kernel(rev, groups, *refs):
    if rev:
        (qs_ref, ks_ref, v_ref, pb_ref, cb_ref, gr_ref, gs_ref, o_ref, z_ref, hf_ref, ng_ref,
         out_ref, st_ref, m_ref) = refs
    else:
        (q_ref, k_ref, v_ref, qp_ref, qn_ref, kp_ref, kn_ref, pb_ref, cb_ref, gr_ref, gs_ref,
         cw_ref, cbias_ref, out_ref, qs_ref, ks_ref, st_ref, m_ref, stg_ref) = refs
    tb = TB_SEQ
    nch = tb // CHUNK
    n = pl.program_id(0)
    sb, nb = _seq_pos((_num_blocks(groups) - 1 - n) if rev else n, groups)

    @pl.when(sb == (nb - 1 if rev else 0))
    def _():
        st_ref[...] = jnp.zeros_like(st_ref)
        m_ref[...] = jnp.full_like(m_ref, M_INIT)

    def conv_silu(raw_ref, prev_ref, next_ref, c0):
        stg_ref[SUBLANES:SUBLANES + tb, :] = raw_ref[...]
        stg_ref[SUBLANES - 1:SUBLANES, :] = jnp.where(sb > 0, prev_ref[SUBLANES - 1:SUBLANES, :], 0.0)
        stg_ref[SUBLANES + tb:SUBLANES + tb + 1, :] = jnp.where(sb < nb - 1, next_ref[0:1, :], 0.0)
        w = cw_ref[:, c0:c0 + WIDTH]
        y = (stg_ref[SUBLANES - 1:SUBLANES - 1 + tb, :] * w[0:1, :]
             + stg_ref[SUBLANES:SUBLANES + tb, :] * w[1:2, :]
             + stg_ref[SUBLANES + 1:SUBLANES + 1 + tb, :] * w[2:3, :] + cbias_ref[:, c0:c0 + WIDTH])
        return _silu(y)

    if not rev:
        qs_ref[...] = conv_silu(q_ref, qp_ref, qn_ref, 0).astype(BF16)
        ks_ref[...] = conv_silu(k_ref, kp_ref, kn_ref, WIDTH).astype(BF16)

    row = lax.broadcasted_iota(jnp.int32, (CHUNK, CHUNK), 0)
    col = lax.broadcasted_iota(jnp.int32, (CHUNK, CHUNK), 1)
    allowed = (row <= col) if rev else (row >= col)
    ones = jnp.ones((CHUNK, HEAD_DIM), BF16)

    def chunk(c, h):
        sl = pl.ds(pl.multiple_of(c * CHUNK, CHUNK), CHUNK)
        hs = slice(h * HEAD_DIM, (h + 1) * HEAD_DIM)
        q = qs_ref[sl, hs]
        kt = ks_ref[sl, hs].astype(F32).T
        v2 = jnp.concatenate([v_ref[sl, hs].astype(BF16), ones], axis=1)
        m_prev = m_ref[h, 0:1, :]
        mb = jnp.maximum(pb_ref[h, sl, :], m_prev)
        p = jnp.exp(jnp.where(allowed, gr_ref[h, 0:1, sl] - mb, -jnp.inf))
        w_inter = jnp.exp(m_prev - mb)
        floor = jnp.exp(-(cb_ref[h, sl, :] + mb))
        s = _dot(q, kt.astype(BF16)) * p
        inter = _dot(q, st_ref[h].astype(BF16))
        num2 = _dot(s.astype(BF16), v2) + jnp.concatenate([w_inter, w_inter], axis=1) * inter
        out = num2[:, 0:HEAD_DIM] / jnp.maximum(jnp.abs(num2[:, HEAD_DIM:]), floor)
        f_tot = gs_ref[h, c, 0:1, :]
        m_chunk = gs_ref[h, c, 1:2, :]
        m_new = jnp.maximum(f_tot + m_prev, m_chunk)
        kw = (kt * jnp.exp(gr_ref[h, 1:2, sl] - m_new)).astype(BF16)
        kv2 = _dot(kw, v2)
        a = jnp.exp(f_tot + m_prev - m_new)
        st_ref[h] = jnp.concatenate([a, a], axis=1) * st_ref[h] + kv2
        m_ref[h] = jnp.broadcast_to(m_new, (SUBLANES, LANES))
        return out, sl, hs

    def body(i, carry):
        c = (nch - 1 - i) if rev else i
        for h in range(HEADS):
            out, sl, hs = chunk(c, h)
            if rev:
                hh = jax.nn.sigmoid(o_ref[sl, hs]) * (out + hf_ref[sl, hs])
                y = _head_norm(hh, ng_ref[:, hs]) * _silu(z_ref[sl, hs])
                out_ref[sl, hs] = y.astype(out_ref.dtype)
            else:
                out_ref[sl, hs] = out
        return carry

    lax.fori_loop(0, nch, body, 0, unroll=True)


def _mlstm_dir(rev, u_ml, pb, cb, gr, gs, conv_w, conv_b, norm_g, fwd, groups):
    t = u_ml.shape[0]
    tb = TB_SEQ
    t8 = tb // SUBLANES
    nch = tb // CHUNK
    nblk = _num_blocks(groups)
    d = 1 if rev else 0

    def rb(n):
        return (nblk - 1 - n) if rev else n

    def col(g):
        return lambda n: (rb(n), g)

    def halo_prev(g):
        return lambda n: (jnp.maximum(rb(n) * t8 - 1, 0), g)

    def halo_next(g):
        return lambda n: (jnp.minimum((rb(n) + 1) * t8, t // SUBLANES - 1), g)

    blk = (tb, WIDTH)
    halo = (SUBLANES, WIDTH)
    gate_specs = [
        pl.BlockSpec((None, HEADS, tb, LANES), lambda n: (d, 0, rb(n), 0)),
        pl.BlockSpec((None, HEADS, tb, LANES), lambda n: (d, 0, rb(n), 0)),
        pl.BlockSpec((None, HEADS, SUBLANES, tb), lambda n: (d, 0, 0, rb(n))),
        pl.BlockSpec((None, HEADS, nch, SUBLANES, LANES), lambda n: (d, 0, rb(n), 0, 0)),
    ]
    scratch = [
        pltpu.VMEM((HEADS, HEAD_DIM, 2 * HEAD_DIM), F32),
        pltpu.VMEM((HEADS, SUBLANES, LANES), F32),
    ]
    full = jax.ShapeDtypeStruct((t, WIDTH), F32)
    if rev:
        qs, ks, h_fwd = fwd
        in_specs = [pl.BlockSpec(blk, col(0)), pl.BlockSpec(blk, col(0)), pl.BlockSpec(blk, col(2)),
                    *gate_specs,
                    pl.BlockSpec(blk, col(3)), pl.BlockSpec(blk, col(4)), pl.BlockSpec(blk, col(0)),
                    pl.BlockSpec((1, WIDTH), lambda n: (0, 0))]
        args = [qs, ks, u_ml, pb, cb, gr, gs, u_ml, u_ml, h_fwd, norm_g]
        out_specs = pl.BlockSpec(blk, col(0))
        out_shape = jax.ShapeDtypeStruct(full.shape, BF16)
    else:
        in_specs = [pl.BlockSpec(blk, col(0)), pl.BlockSpec(blk, col(1)), pl.BlockSpec(blk, col(2)),
                    pl.BlockSpec(halo, halo_prev(0)), pl.BlockSpec(halo, halo_next(0)),
                    pl.BlockSpec(halo, halo_prev(1)), pl.BlockSpec(halo, halo_next(1)),
                    *gate_specs,
                    pl.BlockSpec((3, 2 * WIDTH), lambda n: (0, 0)),
                    pl.BlockSpec((1, 2 * WIDTH), lambda n: (0, 0))]
        args = [u_ml, u_ml, u_ml, u_ml, u_ml, u_ml, u_ml, pb, cb, gr, gs, conv_w, conv_b]
        out_specs = [pl.BlockSpec(blk, col(0))] * 3
        out_shape = [full, jax.ShapeDtypeStruct((t, WIDTH), BF16), jax.ShapeDtypeStruct((t, WIDTH), BF16)]
        scratch.append(pltpu.VMEM((tb + 2 * SUBLANES, WIDTH), F32))
    return pl.pallas_call(
        functools.partial(_ml_kernel, rev, groups),
        grid=(nblk,),
        in_specs=in_specs,
        out_specs=out_specs,
        out_shape=out_shape,
        scratch_shapes=scratch,
        name="mlstm_bwd" if rev else "mlstm_fwd",
    )(*args)


def _mlstm(u_ml, pb, cb, gr, gs, conv_w, conv_b, norm_g, groups):
    h_fwd, qs, ks = _mlstm_dir(False, u_ml, pb, cb, gr, gs, conv_w, conv_b, norm_g, None, groups)
    return _mlstm_dir(True, u_ml, pb, cb, gr, gs, conv_w, conv_b, norm_g, (qs, ks, h_fwd), groups)


def _rt_kernel(rev, groups, *refs):
    if rev:
        (qs_ref, ks_ref, v_ref, dec_ref, qd_ref, kd_ref, z_ref, hf_ref, ng_ref,
         out_ref, st_ref) = refs
    else:
        (qs_ref, ks_ref, v_ref, dec_ref, qd_ref, kd_ref, out_ref, st_ref) = refs
    nch = TB_SEQ // CHUNK
    n = pl.program_id(0)
    sb, nb = _seq_pos((_num_blocks(groups) - 1 - n) if rev else n, groups)

    @pl.when(sb == (nb - 1 if rev else 0))
    def _():
        st_ref[...] = jnp.zeros_like(st_ref)

    def body(i, carry):
        c = (nch - 1 - i) if rev else i
        sl = pl.ds(pl.multiple_of(c * CHUNK, CHUNK), CHUNK)
        for h in range(HEADS):
            hs = slice(h * HEAD_DIM, (h + 1) * HEAD_DIM)
            q = qs_ref[sl, hs]
            kt = ks_ref[sl, hs].astype(F32).T
            vb = v_ref[sl, hs].astype(BF16)
            s = _dot(q, kt.astype(BF16)) * dec_ref[h]
            out = _dot(s.astype(BF16), vb) + _dot((q.astype(F32) * qd_ref[h]).astype(BF16),
                                                  st_ref[h].astype(BF16))
            st_ref[h] = kd_ref[h, 1:2, :] * st_ref[h] + _dot((kt * kd_ref[h, 0:1, :]).astype(BF16), vb)
            if rev:
                y = _head_norm(out + hf_ref[sl, hs], ng_ref[:, hs]) * _silu(z_ref[sl, hs])
                out_ref[sl, hs] = y.astype(out_ref.dtype)
            else:
                out_ref[sl, hs] = out
        return carry

    lax.fori_loop(0, nch, body, 0, unroll=not rev)


def _retention_dir(rev, u_qk, u_vz, dec, qdec, kdec, norm_g, h_fwd, groups):
    tb = TB_SEQ
    nblk = _num_blocks(groups)
    d = 1 if rev else 0

    def rb(n):
        return (nblk - 1 - n) if rev else n

    def col(g):
        return lambda n: (rb(n), g)

    blk = (tb, WIDTH)
    tbl = lambda n: (d, 0, 0, 0)
    in_specs = [
        pl.BlockSpec(blk, col(0)), pl.BlockSpec(blk, col(1)), pl.BlockSpec(blk, col(0)),
        pl.BlockSpec((None, HEADS, CHUNK, CHUNK), tbl),
        pl.BlockSpec((None, HEADS, CHUNK, HEAD_DIM), tbl),
        pl.BlockSpec((None, HEADS, SUBLANES, CHUNK), tbl),
    ]
    args = [u_qk, u_qk, u_vz, dec, qdec, kdec]
    if rev:
        in_specs += [pl.BlockSpec(blk, col(1)), pl.BlockSpec(blk, col(0)),
                     pl.BlockSpec((1, WIDTH), lambda n: (0, 0))]
        args += [u_vz, h_fwd, norm_g]
    out_shape = jax.ShapeDtypeStruct((u_qk.shape[0], WIDTH), BF16 if rev else F32)
    out_specs = pl.BlockSpec(blk, col(0))
    return pl.pallas_call(
        functools.partial(_rt_kernel, rev, groups),
        grid=(nblk,),
        in_specs=in_specs,
        out_specs=out_specs,
        out_shape=out_shape,
        scratch_shapes=[pltpu.VMEM((HEADS, HEAD_DIM, HEAD_DIM), F32)],
        name="retention_bwd" if rev else "retention_fwd",
    )(*args)


def _retention(u_qk, u_vz, tabs, norm_g, groups):
    dec, qdec, kdec = tabs
    h_fwd = _retention_dir(False, u_qk, u_vz, dec, qdec, kdec, norm_g, None, groups)
    return _retention_dir(True, u_qk, u_vz, dec, qdec, kdec, norm_g, h_fwd, groups)


def _rotary_tables(seq):
    inv = ROPE_BASE ** (-jnp.arange(0, HEAD_DIM, 2, dtype=F32) / HEAD_DIM)
    ang = jnp.arange(seq, dtype=F32)[:, None] * inv[None, :]
    cos, sin = jnp.cos(ang), jnp.sin(ang)
    return jnp.concatenate([cos, cos], axis=1), jnp.concatenate([-sin, sin], axis=1)


def _retention_tables():
    pos = np.arange(CHUNK, dtype=np.float64)
    rel = pos[:, None] - pos[None, :]
    dec = np.zeros((2, HEADS, CHUNK, CHUNK))
    qdec = np.zeros((2, HEADS, CHUNK, HEAD_DIM))
    kdec = np.zeros((2, HEADS, SUBLANES, CHUNK))
    for h in range(HEADS):
        lf, lb = RT_LOG_GAMMA_FWD[h], RT_LOG_GAMMA_BWD[h]
        dec[0, h] = np.where(rel >= 0, np.exp(np.maximum(rel, 0.0) * lf), 0.0)
        dec[1, h] = np.where(rel <= 0, np.exp(np.maximum(-rel, 0.0) * lb), 0.0)
        qdec[0, h] = np.exp((pos + 1.0) * lf)[:, None]
        qdec[1, h] = np.exp((CHUNK - pos) * lb)[:, None]
        kdec[0, h, 0] = np.exp((CHUNK - 1.0 - pos) * lf)
        kdec[1, h, 0] = np.exp(pos * lb)
        kdec[0, h, 1] = np.exp(CHUNK * lf)
        kdec[1, h, 1] = np.exp(CHUNK * lb)
    return jnp.asarray(dec, F32), jnp.asarray(qdec, F32), jnp.asarray(kdec, F32)


def _filt_kernel(ft_ref, fr_ref, f0_ref, w1_ref, b1_ref, w2_ref, b2_ref, fq_ref,
                 w3f_ref, w3b_ref, dlf_ref, dlb_ref, gb_ref, gf_ref, s_ref, hf3_ref, hr3_ref):
    lt = pl.program_id(1)
    fq = fq_ref[...]

    def hidden(ft):
        h1 = jnp.sin(fq * (_dot_f32(w1_ref[...], ft) + b1_ref[...]))
        return jnp.sin(fq * (_dot_f32(w2_ref[...], h1) + b2_ref[...]))

    @pl.when((pl.program_id(2) == 0) & (pl.program_id(3) == 0))
    def _():
        hf3_ref[...] = _split3_rows(hidden(ft_ref[...]))
        hr3_ref[...] = _split3_rows(hidden(fr_ref[...]))

    def lhs3(w):
        hi, lo = _split_hi_lo(w)
        return jnp.concatenate([hi, hi, lo], axis=1)

    ft = ft_ref[...]
    fr = fr_ref[...]
    dl_f = jnp.abs(dlf_ref[...])
    dl_b = jnp.abs(dlb_ref[...])
    fwd = _dot(lhs3(w3f_ref[...]), hf3_ref[...]) * jnp.exp(-ft[0:1, :] * dl_f)
    bwd = _dot(lhs3(w3b_ref[...]), hr3_ref[...]) * jnp.exp(-fr[0:1, :] * dl_b)
    f0 = f0_ref[...]
    bwd0 = (_dot_f32(w3b_ref[...], hidden(f0)) * jnp.exp(-f0[0:1, :] * dl_b))[:, 0:1]
    lag = lax.broadcasted_iota(jnp.int32, fwd.shape, 1) + lt * LT_FILT
    fwd = jnp.where(lag == 0, fwd + bwd0, fwd)
    bwd = jnp.where(lag == 0, 0.0, bwd)
    gb_ref[...] = bwd.reshape(gb_ref.shape)
    gf_ref[...] = fwd.reshape(gf_ref.shape)
    part = jnp.sum(jnp.abs(fwd) + jnp.abs(bwd), axis=1, keepdims=True)
    lane = lax.broadcasted_iota(jnp.int32, (LANES, LANES), 1)
    s_ref[...] = jnp.where(lane == 0, part, 0.0)


def _filters(seq, w1t, b1, w2t, b2, freq, w3t, deltas):
    t = jnp.linspace(0.0, 1.0, seq, dtype=F32)[None, :]
    w = (2.0 * math.pi / seq) * jnp.arange(seq, dtype=F32)[None, :]
    bands = jnp.linspace(1e-4, HY_BANDS - 1, HY_BANDS, dtype=F32)[:, None]
    feats = jnp.concatenate([t, jnp.cos(bands * w), -jnp.sin(bands * w),
                             jnp.zeros((FEAT_ROWS - 1 - 2 * HY_BANDS, seq), F32)], axis=0)
    feats_rev = jnp.concatenate([feats[:, :1], feats[:, :0:-1]], axis=1)
    feats0 = jnp.broadcast_to(feats[:, :1], (FEAT_ROWS, LANES))
    nlt = seq // LT_FILT
    ct = WIDTH // LANES
    lyr = lambda l, i, o, c: (l, 0, 0)
    return pl.pallas_call(
        _filt_kernel,
        grid=(DEPTH, nlt, HY_ORDER, ct),
        in_specs=[
            pl.BlockSpec((FEAT_ROWS, LT_FILT), lambda l, i, o, c: (0, i)),
            pl.BlockSpec((FEAT_ROWS, LT_FILT), lambda l, i, o, c: (0, i)),
            pl.BlockSpec((FEAT_ROWS, LANES), lambda l, i, o, c: (0, 0)),
            pl.BlockSpec((None, HY_HIDDEN, FEAT_ROWS), lyr),
            pl.BlockSpec((None, HY_HIDDEN, 1), lyr),
            pl.BlockSpec((None, HY_HIDDEN, HY_HIDDEN), lyr),
            pl.BlockSpec((None, HY_HIDDEN, 1), lyr),
            pl.BlockSpec((None, HY_HIDDEN, 1), lyr),
            pl.BlockSpec((None, LANES, HY_HIDDEN), lambda l, i, o, c: (l, o * 2 * ct + c, 0)),
            pl.BlockSpec((None, LANES, HY_HIDDEN), lambda l, i, o, c: (l, o * 2 * ct + ct + c, 0)),
            pl.BlockSpec((None, LANES, 1), lambda l, i, o, c: (l, o * 2 * ct + c, 0)),
            pl.BlockSpec((None, LANES, 1), lambda l, i, o, c: (l, o * 2 * ct + ct + c, 0)),
        ],
        out_specs=[
            pl.BlockSpec((None, None, LANES, LT_FILT // LANES, LANES), lambda l, i, o, c: (l, o, c, i, 0)),
            pl.BlockSpec((None, None, LANES, LT_FILT // LANES, LANES), lambda l, i, o, c: (l, o, c, i, 0)),
            pl.BlockSpec((None, None, LANES, LANES), lambda l, i, o, c: (l, o, c, i)),
        ],
        out_shape=[
            jax.ShapeDtypeStruct((DEPTH, HY_ORDER, WIDTH, seq // LANES, LANES), F32),
            jax.ShapeDtypeStruct((DEPTH, HY_ORDER, WIDTH, seq // LANES, LANES), F32),
            jax.ShapeDtypeStruct((DEPTH, HY_ORDER, WIDTH, nlt * LANES), F32),
        ],
        scratch_shapes=[pltpu.VMEM((3 * HY_HIDDEN, LT_FILT), BF16),
                        pltpu.VMEM((3 * HY_HIDDEN, LT_FILT), BF16)],
        name="hyena_filters",
    )(feats, feats_rev, feats0, w1t, b1, w2t, b2, freq, w3t, w3t, deltas, deltas)


def _split_hi_lo(x):
    hi = x.astype(BF16)
    lo = (x - hi.astype(F32)).astype(BF16)
    return hi, lo


def _split3_rows(x):
    hi, lo = _split_hi_lo(x)
    return jnp.concatenate([hi, lo, hi], axis=0)


def _split3_cols(x):
    hi, lo = _split_hi_lo(x)
    return jnp.concatenate([hi, lo, hi], axis=1)


def _np_split(m):
    hi = m.astype(ml_dtypes.bfloat16)
    lo = (m - hi.astype(np.float64)).astype(ml_dtypes.bfloat16)
    return hi, lo


def _dft_tables(seq):
    nj = seq // LANES
    n1 = 2 * nj
    n = n1 * LANES
    khp = n1 // 2 + KH_PAD
    k1 = np.arange(khp, dtype=np.float64)[:, None]
    valid = (k1 <= n1 // 2).astype(np.float64)
    ang = 2.0 * np.pi * k1 * np.arange(n1, dtype=np.float64)[None, :] / n1
    fa_full = np.concatenate([valid * np.cos(ang), -valid * np.sin(ang)], axis=0)

    def lhs3(m):
        hi, lo = _np_split(m)
        return jnp.asarray(np.concatenate([hi, hi, lo], axis=1))

    def rhs3(m):
        hi, lo = _np_split(m)
        return jnp.asarray(np.concatenate([hi, hi, lo], axis=0))

    n2 = np.arange(LANES, dtype=np.float64)
    ang2 = 2.0 * np.pi * n2[:, None] * n2[None, :] / LANES
    c2, s2 = np.cos(ang2), np.sin(ang2)
    fb = np.block([[c2, -s2], [s2, c2]])
    fbi = np.block([[c2, s2], [-s2, c2]])
    w = np.where((k1 == 0) | (k1 == n1 // 2), 1.0, 2.0) * valid
    th = 2.0 * np.pi * np.arange(nj, dtype=np.float64)[:, None] * k1.T / n1
    fai = np.concatenate([(w.T / n) * np.cos(th), -(w.T / n) * np.sin(th)], axis=1)
    phi = 2.0 * np.pi * k1 * n2[None, :] / n
    return dict(
        nj=nj, n1=n1, khp=khp,
        fa=lhs3(fa_full[:, :nj]), fa_full=lhs3(fa_full), fb=rhs3(fb), fbi=rhs3(fbi), fai=lhs3(fai),
        twc=jnp.asarray(valid * np.cos(phi), F32), tws=jnp.asarray(valid * np.sin(phi), F32))


def _dft_rows(fa_ref, twc, tws, khp, cols):
    rhs = cols[0] if len(cols) == 1 else jnp.concatenate(cols, axis=1)
    y = _dot(fa_ref[...], _split3_rows(rhs))
    out = []
    for i in range(len(cols)):
        yre = y[0:khp, i * LANES:(i + 1) * LANES]
        yim = y[khp:2 * khp, i * LANES:(i + 1) * LANES]
        out.append((yre * twc + yim * tws, yim * twc - yre * tws))
    return out


def _spec_kernel(khp, gf_ref, gb_ref, s_ref, fa_ref, fb_ref, twc_ref, tws_ref, h_ref, ybuf):
    twc = twc_ref[...]
    tws = tws_ref[...]

    def circular(c):
        return jnp.concatenate([gf_ref[c], gb_ref[c]], axis=0)

    for c in range(0, CB_SPEC, 2):
        res = _dft_rows(fa_ref, twc, tws, khp, [circular(c), circular(c + 1)])
        for i, (yre, yim) in enumerate(res):
            r0 = (c + i) * khp
            ybuf[r0:r0 + khp, 0:LANES] = yre
            ybuf[r0:r0 + khp, LANES:2 * LANES] = yim
    x = _dot(_split3_cols(ybuf[...]), fb_ref[...])
    for c in range(CB_SPEC):
        norm = jnp.sum(s_ref[c:c + 1, :], axis=1, keepdims=True)
        h_ref[c] = x[c * khp:(c + 1) * khp, :] / norm


def _spectrum(gf, gb, s, tab):
    nch = gf.shape[0]
    nj, khp = tab["nj"], tab["khp"]
    const = lambda a: pl.BlockSpec(a.shape, lambda i: (0, 0))
    return pl.pallas_call(
        functools.partial(_spec_kernel, khp),
        grid=(nch // CB_SPEC,),
        in_specs=[
            pl.BlockSpec((CB_SPEC, nj, LANES), lambda i: (i, 0, 0)),
            pl.BlockSpec((CB_SPEC, nj, LANES), lambda i: (i, 0, 0)),
            pl.BlockSpec((CB_SPEC, s.shape[-1]), lambda i: (i, 0)),
            const(tab["fa_full"]), const(tab["fb"]), const(tab["twc"]), const(tab["tws"]),
        ],
        out_specs=pl.BlockSpec((CB_SPEC, khp, 2 * LANES), lambda i: (i, 0, 0)),
        out_shape=jax.ShapeDtypeStruct((nch, khp, 2 * LANES), F32),
        scratch_shapes=[pltpu.VMEM((CB_SPEC * khp, 2 * LANES), F32)],
        name="hyena_spectrum",
    )(gf, gb, s, tab["fa_full"], tab["fb"], tab["twc"], tab["tws"])


def _hyfft_kernel(bsz, nj, khp, prm_ref, v_ref, x1_ref, x2_ref, z_ref, h_ref,
                  fa_ref, fb_ref, fbi_ref, fai_ref, twc_ref, tws_ref, out_ref,
                  xs_ref, ybuf, pbuf):
    ct = pl.program_id(0)
    rows = bsz * nj
    pairs = [(b, b + 1) for b in range(0, bsz, 2)] if bsz > 1 else [(0,)]
    twc = twc_ref[...]
    tws = tws_ref[...]

    lane = lax.broadcasted_iota(jnp.int32, (rows, LANES), 1)
    blk = lax.broadcasted_iota(jnp.int32, (rows, LANES), 0) & (nj - 1)
    first = (lane == 0) & (blk == 0)
    last = (lane == LANES - 1) & (blk == nj - 1)

    def conv3(x, base):
        r = pltpu.roll(x, 1, 1)
        xm1 = jnp.where(lane == 0, pltpu.roll(r, 1, 0), r)
        xm1 = jnp.where(first, 0.0, xm1)
        l = pltpu.roll(x, LANES - 1, 1)
        xp1 = jnp.where(lane == LANES - 1, pltpu.roll(l, rows - 1, 0), l)
        xp1 = jnp.where(last, 0.0, xp1)
        return prm_ref[base] * xm1 + prm_ref[base + 1] * x + prm_ref[base + 2] * xp1 + prm_ref[base + 3]

    def row0(c, b):
        return (c * bsz + b) * khp

    def forward(c, x):
        for pr in pairs:
            res = _dft_rows(fa_ref, twc, tws, khp, [x[b * nj:(b + 1) * nj, :] for b in pr])
            for b, (yre, yim) in zip(pr, res):
                r0 = row0(c, b)
                ybuf[r0:r0 + khp, 0:LANES] = yre
                ybuf[r0:r0 + khp, LANES:2 * LANES] = yim

    def spectral(order):
        for c0 in range(0, CB_HY, CG_HY):
            g0, g1 = row0(c0, 0), row0(c0 + CG_HY, 0)
            x = _dot(_split3_cols(ybuf[g0:g1, :]), fb_ref[...])
            for c in range(c0, c0 + CG_HY):
                hre = h_ref[order, c, :, 0:LANES]
                him = h_ref[order, c, :, LANES:2 * LANES]
                for b in range(bsz):
                    r0 = row0(c, b)
                    xre = x[r0 - g0:r0 - g0 + khp, 0:LANES]
                    xim = x[r0 - g0:r0 - g0 + khp, LANES:2 * LANES]
                    pbuf[r0:r0 + khp, 0:LANES] = xre * hre - xim * him
                    pbuf[r0:r0 + khp, LANES:2 * LANES] = xre * him + xim * hre
            r = _dot(_split3_cols(pbuf[g0:g1, :]), fbi_ref[...])
            for c in range(c0, c0 + CG_HY):
                for b in range(bsz):
                    r0 = row0(c, b)
                    rre = r[r0 - g0:r0 - g0 + khp, 0:LANES]
                    rim = r[r0 - g0:r0 - g0 + khp, LANES:2 * LANES]
                    ybuf[r0:r0 + khp, 0:LANES] = rre * twc - rim * tws
                    ybuf[r0:r0 + khp, LANES:2 * LANES] = rre * tws + rim * twc

    def inverse(c):
        outs = [None] * bsz
        for pr in pairs:
            rre = [ybuf[row0(c, b):row0(c, b) + khp, 0:LANES] for b in pr]
            rim = [ybuf[row0(c, b):row0(c, b) + khp, LANES:2 * LANES] for b in pr]
            if len(pr) > 1:
                rre, rim = [jnp.concatenate(rre, axis=1)], [jnp.concatenate(rim, axis=1)]
            y = _dot(fai_ref[...], _split3_rows(jnp.concatenate([rre[0], rim[0]], axis=0)))
            for i, b in enumerate(pr):
                outs[b] = y[:, i * LANES:(i + 1) * LANES]
        return outs[0] if bsz == 1 else jnp.concatenate(outs, axis=0)

    for c in range(CB_HY):
        base = (ct * CB_HY + c) * HY_PRM
        v = conv3(v_ref[c], base)
        xs_ref[0, c] = v
        xs_ref[1, c] = conv3(x1_ref[c], base + 4)
        xs_ref[2, c] = conv3(x2_ref[c], base + 8)
        forward(c, v)
    spectral(0)
    for c in range(CB_HY):
        base = (ct * CB_HY + c) * HY_PRM
        v = xs_ref[0, c]
        z1 = xs_ref[1, c] * (inverse(c) + prm_ref[base + 12] * v)
        xs_ref[0, c] = z1
        forward(c, z1)
    spectral(1)
    for c in range(CB_HY):
        base = (ct * CB_HY + c) * HY_PRM
        z1 = xs_ref[0, c]
        y2 = inverse(c) + prm_ref[base + 13] * z1
        out_ref[:, c, :] = xs_ref[2, c] * y2 * _silu(z_ref[c])


def _hyena_fft(prm, u_t, h, layer, tab, row0, bsz, seq):
    nj, khp = tab["nj"], tab["khp"]
    nct = WIDTH // CB_HY
    rows = bsz * nj
    blk = (CB_HY, rows, LANES)
    rblk = row0 // (rows * LANES)
    assert rblk * rows * LANES == row0
    const = lambda a: pl.BlockSpec(a.shape, lambda i: (0, 0))
    return pl.pallas_call(
        functools.partial(_hyfft_kernel, bsz, nj, khp),
        grid=(nct,),
        in_specs=[
            pl.BlockSpec(memory_space=pltpu.SMEM),
            pl.BlockSpec(blk, lambda i: (i, rblk, 0)),
            pl.BlockSpec(blk, lambda i: (nct + i, rblk, 0)),
            pl.BlockSpec(blk, lambda i: (2 * nct + i, rblk, 0)),
            pl.BlockSpec(blk, lambda i: (3 * nct + i, rblk, 0)),
            pl.BlockSpec((None, HY_ORDER, CB_HY, khp, 2 * LANES), lambda i: (layer, 0, i, 0, 0)),
            const(tab["fa"]), const(tab["fb"]), const(tab["fbi"]), const(tab["fai"]),
            const(tab["twc"]), const(tab["tws"]),
        ],
        out_specs=pl.BlockSpec((rows, CB_HY, LANES), lambda i: (0, i, 0)),
        out_shape=jax.ShapeDtypeStruct((rows, WIDTH, LANES), F32),
        scratch_shapes=[
            pltpu.VMEM((3, CB_HY, bsz * nj, LANES), F32),
            pltpu.VMEM((CB_HY * bsz * khp, 2 * LANES), F32),
            pltpu.VMEM((CB_HY * bsz * khp, 2 * LANES), F32),
        ],
        name="hyena_fft",
    )(prm, u_t, u_t, u_t, u_t, h, tab["fa"], tab["fb"], tab["fbi"], tab["fai"],
      tab["twc"], tab["tws"])


def _out_kernel(final, starts, nx, *refs):
    ng = len(starts)
    x_refs = refs[0:nx]
    yml_ref, yrt_ref = refs[nx:nx + 2]
    yhy_refs = refs[nx + 2:nx + 2 + ng]
    w_ref, g_ref = refs[nx + 2 + ng:nx + 4 + ng]
    out_refs = refs[nx + 4 + ng:-1]
    lhs_ref = refs[-1]
    i = pl.program_id(0)
    lhs_ref[:, 0:WIDTH] = yml_ref[...]
    lhs_ref[:, WIDTH:2 * WIDTH] = yrt_ref[...]
    jb = TM_OUT // LANES

    def fill_hyena(ref):
        for j in range(jb):
            lhs_ref[j * LANES:(j + 1) * LANES, 2 * WIDTH:] = ref[j].T.astype(BF16)

    for gi in range(ng):
        inside = i >= starts[gi]
        if gi + 1 < ng:
            inside = inside & (i < starts[gi + 1])
        pl.when(inside)(functools.partial(fill_hyena, yhy_refs[gi]))
    x = _x_tile(x_refs, starts if nx > 1 else [0]) + _dot(lhs_ref[...], w_ref[...])
    if not final:
        out_refs[0][...] = x
        return
    ms = jnp.mean(x * x, axis=-1, keepdims=True)
    x = x * lax.rsqrt(ms + RMS_EPS) * g_ref[...]
    for gi in range(ng):
        inside = i >= starts[gi]
        if gi + 1 < ng:
            inside = inside & (i < starts[gi + 1])

        @pl.when(inside)
        def _(gi=gi):
            out_refs[gi][...] = x


def _out_proj(xs, y_ml, y_rt, y_hy, w, gain, final, groups):
    t = y_ml.shape[0]
    jb = TM_OUT // LANES
    starts = [r0 // TM_OUT for (r0, _, _) in groups]
    counts = [bsz * seq // TM_OUT for (_, bsz, seq) in groups]

    def tile(gi):
        return lambda i: jnp.clip(i - starts[gi], 0, counts[gi] - 1)

    hy_specs = [pl.BlockSpec((jb, WIDTH, LANES), lambda i, f=tile(gi): (f(i), 0, 0))
                for gi in range(len(groups))]
    if final:
        out_specs = [pl.BlockSpec((TM_OUT, D_MODEL), lambda i, f=tile(gi): (f(i), 0))
                     for gi in range(len(groups))]
        out_shape = [jax.ShapeDtypeStruct((bsz * seq, D_MODEL), F32) for (_, bsz, seq) in groups]
    else:
        out_specs = [pl.BlockSpec((TM_OUT, D_MODEL), lambda i: (i, 0))]
        out_shape = [jax.ShapeDtypeStruct((t, D_MODEL), F32)]
    return pl.pallas_call(
        functools.partial(_out_kernel, final, starts, len(xs)),
        grid=(t // TM_OUT,),
        in_specs=[
            *_x_specs(xs, TM_OUT, 1),
            pl.BlockSpec((TM_OUT, WIDTH), lambda i: (i, 0)),
            pl.BlockSpec((TM_OUT, WIDTH), lambda i: (i, 0)),
            *hy_specs,
            pl.BlockSpec((3 * WIDTH, D_MODEL), lambda i: (0, 0)),
            pl.BlockSpec((1, D_MODEL), lambda i: (0, 0)),
        ],
        out_specs=out_specs,
        out_shape=out_shape,
        scratch_shapes=[pltpu.VMEM((TM_OUT, 3 * WIDTH), BF16)],
        name="out_proj",
    )(*xs, y_ml, y_rt, *y_hy, w, gain)


def kernel(x_prompt, x_sample, norm_g, w_in, ml_conv_w, ml_conv_b, ml_gate_b, ml_norm_g, rt_norm_g, hy_conv_w, hy_conv_b, hy_w1, hy_b1, hy_w2, hy_b2, hy_w3, hy_freq, hy_deltas, hy_skip, w_out, final_g):
    groups = []
    row0 = 0
    for xg in (x_prompt, x_sample):
        bsz, seq, _ = xg.shape
        groups.append((row0, bsz, seq))
        row0 += bsz * seq
    xs = [x_prompt.reshape(-1, D_MODEL), x_sample.reshape(-1, D_MODEL)]

    c_gate = ML_COLS
    c_rt = c_gate + 4 * HEADS
    c_hy = c_rt + RT_COLS
    gate_pad = jnp.zeros((DEPTH, D_MODEL, LANES - 4 * HEADS), F32)
    w_tok = jnp.concatenate([w_in[:, :, :c_gate], w_in[:, :, c_gate:c_rt], gate_pad,
                             w_in[:, :, c_rt:c_hy]], axis=2).astype(BF16)
    w_gt = jnp.swapaxes(w_in[:, :, c_gate:c_rt], 1, 2).astype(BF16)
    w_hyt = jnp.swapaxes(w_in[:, :, c_hy:], 1, 2).astype(BF16)
    w_out_b = w_out.astype(BF16)
    gate_b = ml_gate_b.reshape(DEPTH, 1, 4 * HEADS)
    gate_b_row = jnp.concatenate([gate_b, jnp.zeros((DEPTH, 1, LANES - 4 * HEADS), F32)], axis=2)
    gate_b_col = ml_gate_b.reshape(DEPTH, 4 * HEADS, 1)

    cw = hy_conv_w.reshape(DEPTH, 3, 3, WIDTH)
    cbias = hy_conv_b.reshape(DEPTH, 1, 3, WIDTH)
    taps = jnp.concatenate([cw, cbias], axis=1)
    taps = jnp.transpose(taps, (0, 3, 2, 1)).reshape(DEPTH, WIDTH, 12)
    hy_prm = jnp.concatenate([taps, jnp.transpose(hy_skip, (0, 2, 1)),
                              jnp.zeros((DEPTH, WIDTH, HY_PRM - 14), F32)],
                             axis=2).reshape(DEPTH, WIDTH * HY_PRM)

    w1t = jnp.concatenate([jnp.swapaxes(hy_w1, 1, 2),
                           jnp.zeros((DEPTH, HY_HIDDEN, FEAT_ROWS - hy_w1.shape[1]), F32)], axis=2)
    w2t = jnp.swapaxes(hy_w2, 1, 2)
    w3t = jnp.swapaxes(hy_w3, 1, 2)
    b1 = hy_b1[:, :, None]
    b2 = hy_b2[:, :, None]
    freq = hy_freq[:, :, None]
    deltas = hy_deltas.reshape(DEPTH, HY_ORDER * 2 * WIDTH, 1)

    tables, spectra = [], []
    for (_, _, seq) in groups:
        tab = _dft_tables(seq)
        gb, gf, s = _filters(seq, w1t, b1, w2t, b2, freq, w3t, deltas)
        nfil = DEPTH * HY_ORDER * WIDTH
        h = _spectrum(gf.reshape(nfil, tab["nj"], LANES), gb.reshape(nfil, tab["nj"], LANES),
                      s.reshape(nfil, -1), tab)
        tables.append(tab)
        spectra.append(h.reshape(DEPTH, HY_ORDER, WIDTH, tab["khp"], 2 * LANES))
    rot_c, rot_s = _rotary_tables(max(seq for (_, _, seq) in groups))
    rt_tabs = _retention_tables()

    for layer in range(DEPTH):
        gain = norm_g[layer][None, :]
        u_ml, u_g, u_qk, u_vz, u_gt = _in_proj(xs, gain, w_tok[layer], w_gt[layer], rot_c, rot_s, groups)
        u_hyt = _in_proj_hy(xs, gain, w_hyt[layer])
        pb, cb, gr, gs = _gates(u_g, u_gt, gate_b_row[layer], gate_b_col[layer])
        y_ml = _mlstm(u_ml, pb, cb, gr, gs, ml_conv_w[layer], ml_conv_b[layer][None, :],
                      ml_norm_g[layer][None, :], groups)
        y_rt = _retention(u_qk, u_vz, rt_tabs, rt_norm_g[layer][None, :], groups)
        y_hy = [_hyena_fft(hy_prm[layer], u_hyt, spectra[gi], layer, tables[gi], r0, bsz, seq)
                for gi, (r0, bsz, seq) in enumerate(groups)]
        outs = _out_proj(xs, y_ml, y_rt, y_hy, w_out_b[layer], final_g[None, :],
                         layer == DEPTH - 1, groups)
        xs = [outs[0]]

    return outs[0].reshape(x_prompt.shape), outs[1].reshape(x_sample.shape)
```

```python
import functools
import math

import ml_dtypes
import numpy as np
import jax
import jax.numpy as jnp
from jax import lax
from jax.experimental import pallas as pl
from jax.experimental.pallas import tpu as pltpu

F32 = jnp.float32
BF16 = jnp.bfloat16

D_MODEL = 1024
DEPTH = 4
CHUNK = 128
HEADS = 4
HEAD_DIM = 128
WIDTH = 512
HY_ORDER = 2
HY_BANDS = 16
HY_HIDDEN = 64
FEAT_ROWS = 40
HY_PRM = 16
ROPE_BASE = 10000.0
RMS_EPS = 1e-6
HEAD_NORM_EPS = 1e-5
M_INIT = -1e30
RT_LOG_GAMMA_FWD = tuple(math.log(1.0 - 2.0 ** (-5.0 - h)) for h in range(HEADS))
RT_LOG_GAMMA_BWD = tuple(math.log(1.0 - 2.0 ** (-5.5 - h)) for h in range(HEADS))

LANES = 128
SUBLANES = 8
TM_IN = 256
TM_HY = 1024
TR_HY = 1024
TM_OUT = 1024
TB_SEQ = 1024
TG = 1024
CB_HY = 8
CG_HY = 2
CB_SPEC = 16
KH_PAD = 8
LT_FILT = 2048
ML_COLS = 5 * WIDTH
RT_COLS = 4 * WIDTH
HY_ROWS = 4 * WIDTH
HIGHEST = lax.Precision.HIGHEST


def _dot(a, b):
    return jnp.dot(a, b, preferred_element_type=F32)


def _dot_nt(a, b):
    return lax.dot_general(a, b, (((1,), (1,)), ((), ())), preferred_element_type=F32)


def _dot_tn(a, b):
    return lax.dot_general(a, b, (((0,), (0,)), ((), ())), preferred_element_type=F32)


def _dot_f32(a, b):
    return jnp.dot(a, b, precision=HIGHEST, preferred_element_type=F32)


def _silu(y):
    return y * jax.nn.sigmoid(y)


def _log_sigmoid(x):
    return -(jnp.maximum(-x, 0.0) + jnp.log1p(jnp.exp(-jnp.abs(x))))


def _head_norm(h, gain):
    mu = jnp.mean(h, axis=-1, keepdims=True)
    hc = h - mu
    var = jnp.mean(hc * hc, axis=-1, keepdims=True)
    return hc * lax.rsqrt(var + HEAD_NORM_EPS) * gain


def _tile_starts(xs, tm):
    starts, s = [], 0
    for a in xs:
        starts.append(s)
        s += a.shape[0] // tm
    return starts, s


def _x_specs(xs, tm, ngrid):
    starts, _ = _tile_starts(xs, tm)
    specs = []
    for a, s in zip(xs, starts):
        cnt = a.shape[0] // tm
        if ngrid == 1:
            specs.append(pl.BlockSpec((tm, D_MODEL), lambda i, s=s, cnt=cnt: (jnp.clip(i - s, 0, cnt - 1), 0)))
        else:
            specs.append(pl.BlockSpec((tm, D_MODEL), lambda i, r, s=s, cnt=cnt: (jnp.clip(i - s, 0, cnt - 1), 0)))
    return specs


def _x_tile(x_refs, starts):
    x = x_refs[0][...]
    for ref, s in zip(x_refs[1:], starts[1:]):
        x = jnp.where(pl.program_id(0) >= s, ref[...], x)
    return x


def _rms_bf16(x, g_ref):
    ms = jnp.mean(x * x, axis=-1, keepdims=True)
    return (x * lax.rsqrt(ms + RMS_EPS) * g_ref[...]).astype(BF16)


def _in_kernel(starts, *refs):
    nx = len(starts)
    (g_ref, wtok_ref, wgt_ref, cc_ref, ss_ref,
     oml_ref, og_ref, oqk_ref, ovz_ref, ogt_ref) = refs[nx:]
    h = _rms_bf16(_x_tile(refs[:nx], starts), g_ref)
    y = _dot(h, wtok_ref[...])
    oml_ref[...] = y[:, 0:ML_COLS]
    og_ref[...] = y[:, ML_COLS:ML_COLS + LANES]
    c0 = ML_COLS + LANES
    cc = cc_ref[...]
    ss = ss_ref[...]
    for j in range(2 * HEADS):
        u = y[:, c0 + j * HEAD_DIM:c0 + (j + 1) * HEAD_DIM]
        r = u * cc + pltpu.roll(u, HEAD_DIM // 2, 1) * ss
        if j < HEADS:
            r = r * (HEAD_DIM ** -0.5)
        oqk_ref[:, j * HEAD_DIM:(j + 1) * HEAD_DIM] = r.astype(BF16)
    ovz_ref[...] = y[:, c0 + 2 * WIDTH:]
    ogt_ref[...] = _dot_nt(wgt_ref[...], h)


def _in_proj(xs, gain, w_tok, w_gt, rot_c, rot_s, groups):
    starts, nt = _tile_starts(xs, TM_IN)
    t = nt * TM_IN
    ncol = w_tok.shape[1]
    pos = lambda i: (_seq_pos(i, groups, TM_IN)[0], 0)
    return pl.pallas_call(
        functools.partial(_in_kernel, starts),
        grid=(nt,),
        in_specs=[
            *_x_specs(xs, TM_IN, 1),
            pl.BlockSpec((1, D_MODEL), lambda i: (0, 0)),
            pl.BlockSpec((D_MODEL, ncol), lambda i: (0, 0)),
            pl.BlockSpec((4 * HEADS, D_MODEL), lambda i: (0, 0)),
            pl.BlockSpec((TM_IN, HEAD_DIM), pos),
            pl.BlockSpec((TM_IN, HEAD_DIM), pos),
        ],
        out_specs=[
            pl.BlockSpec((TM_IN, ML_COLS), lambda i: (i, 0)),
            pl.BlockSpec((TM_IN, LANES), lambda i: (i, 0)),
            pl.BlockSpec((TM_IN, 2 * WIDTH), lambda i: (i, 0)),
            pl.BlockSpec((TM_IN, 2 * WIDTH), lambda i: (i, 0)),
            pl.BlockSpec((4 * HEADS, TM_IN), lambda i: (0, i)),
        ],
        out_shape=[
            jax.ShapeDtypeStruct((t, ML_COLS), F32),
            jax.ShapeDtypeStruct((t, LANES), F32),
            jax.ShapeDtypeStruct((t, 2 * WIDTH), BF16),
            jax.ShapeDtypeStruct((t, 2 * WIDTH), F32),
            jax.ShapeDtypeStruct((4 * HEADS, t), F32),
        ],
        name="in_proj",
    )(*xs, gain, w_tok, w_gt, rot_c, rot_s)


def _in_hy_kernel(starts, *refs):
    nx = len(starts)
    g_ref, w_ref, o_ref = refs[nx:]
    y = _dot_nt(w_ref[...], _rms_bf16(_x_tile(refs[:nx], starts), g_ref))
    o_ref[...] = y.reshape(o_ref.shape)


def _in_proj_hy(xs, gain, w_hyt):
    starts, nt = _tile_starts(xs, TM_HY)
    t = nt * TM_HY
    nr = HY_ROWS // TR_HY
    return pl.pallas_call(
        functools.partial(_in_hy_kernel, starts),
        grid=(nt, nr),
        in_specs=[
            *_x_specs(xs, TM_HY, 2),
            pl.BlockSpec((1, D_MODEL), lambda i, r: (0, 0)),
            pl.BlockSpec((TR_HY, D_MODEL), lambda i, r: (r, 0)),
        ],
        out_specs=pl.BlockSpec((TR_HY, TM_HY // LANES, LANES), lambda i, r: (r, i, 0)),
        out_shape=jax.ShapeDtypeStruct((HY_ROWS, t // LANES, LANES), F32),
        name="in_proj_hyena",
    )(*xs, gain, w_hyt)


def _gates_kernel(g_ref, gt_ref, b_ref, bt_ref, pb_ref, cb_ref, gr_ref, gs_ref):
    row = lax.broadcasted_iota(jnp.int32, (CHUNK, CHUNK), 0)
    col = lax.broadcasted_iota(jnp.int32, (CHUNK, CHUNK), 1)
    lower = (row >= col).astype(F32)
    upper = (row <= col).astype(F32)
    gr_ref[...] = jnp.zeros_like(gr_ref)
    gs_ref[...] = jnp.zeros_like(gs_ref)
    for c in range(TG // CHUNK):
        sl = slice(c * CHUNK, (c + 1) * CHUNK)
        lf = _log_sigmoid(g_ref[sl, :] + b_ref[...])
        cum_c = (_dot_f32(lower, lf), _dot_f32(upper, lf))
        pre_t = gt_ref[:, sl] + bt_ref[...]
        lf_t = _log_sigmoid(pre_t)
        cum_r = (_dot_f32(lf_t, upper), _dot_f32(lf_t, lower))
        for d in range(2):
            for h in range(HEADS):
                kf = (2 * d + 1) * HEADS + h
                ki = 2 * d * HEADS + h
                cum = cum_r[d][kf:kf + 1, :]
                li = pre_t[ki:ki + 1, :]
                r = li - cum
                f_tot = cum[:, 0:1] if d else cum[:, CHUNK - 1:CHUNK]
                g = f_tot - cum + li
                allowed = (row <= col) if d else (row >= col)
                pmax = jnp.max(jnp.where(allowed, r, -jnp.inf), axis=1, keepdims=True)
                pb_ref[d, h, sl, :] = jnp.broadcast_to(pmax, (CHUNK, LANES))
                cb_ref[d, h, sl, :] = jnp.broadcast_to(cum_c[d][:, kf:kf + 1], (CHUNK, LANES))
                gr_ref[d, h, 0:1, sl] = r
                gr_ref[d, h, 1:2, sl] = g
                gs_ref[d, h, c, 0:1, :] = jnp.broadcast_to(f_tot, (1, LANES))
                gs_ref[d, h, c, 1:2, :] = jnp.broadcast_to(jnp.max(g, axis=1, keepdims=True), (1, LANES))


def _gates(g, gt, bias_row, bias_col):
    t = g.shape[0]
    nc = TG // CHUNK
    return pl.pallas_call(
        _gates_kernel,
        grid=(t // TG,),
        in_specs=[
            pl.BlockSpec((TG, LANES), lambda i: (i, 0)),
            pl.BlockSpec((4 * HEADS, TG), lambda i: (0, i)),
            pl.BlockSpec((1, LANES), lambda i: (0, 0)),
            pl.BlockSpec((4 * HEADS, 1), lambda i: (0, 0)),
        ],
        out_specs=[
            pl.BlockSpec((2, HEADS, TG, LANES), lambda i: (0, 0, i, 0)),
            pl.BlockSpec((2, HEADS, TG, LANES), lambda i: (0, 0, i, 0)),
            pl.BlockSpec((2, HEADS, SUBLANES, TG), lambda i: (0, 0, 0, i)),
            pl.BlockSpec((2, HEADS, nc, SUBLANES, LANES), lambda i: (0, 0, i, 0, 0)),
        ],
        out_shape=[
            jax.ShapeDtypeStruct((2, HEADS, t, LANES), F32),
            jax.ShapeDtypeStruct((2, HEADS, t, LANES), F32),
            jax.ShapeDtypeStruct((2, HEADS, SUBLANES, t), F32),
            jax.ShapeDtypeStruct((2, HEADS, t // CHUNK, SUBLANES, LANES), F32),
        ],
        name="gates",
    )(g, gt, bias_row, bias_col)


def _seq_pos(g, groups, tb=TB_SEQ):
    pos, nbs = None, None
    start = 0
    for (_, bsz, seq) in groups:
        nb = seq // tb
        p = (g - start) % nb
        pos = p if pos is None else jnp.where(g >= start, p, pos)
        nbs = nb if nbs is None else jnp.where(g >= start, nb, nbs)
        start += bsz * nb
    return pos, nbs


def _num_blocks(groups):
    return sum(bsz * seq // TB_SEQ for (_, bsz, seq) in groups)


def _ml_kernel(rev, groups, *refs):
    if rev:
        (qs_ref, ks_ref, v_ref, pb_ref, cb_ref, gr_ref, gs_ref, o_ref, z_ref, hf_ref, ng_ref,
         out_ref, st_ref, m_ref) = refs
    else:
        (q_ref, k_ref, v_ref, qp_ref, qn_ref, kp_ref, kn_ref, pb_ref, cb_ref, gr_ref, gs_ref,
         cw_ref, cbias_ref, out_ref, qs_ref, ks_ref, st_ref, m_ref, stg_ref) = refs
    tb = TB_SEQ
    nch = tb // CHUNK
    n = pl.program_id(0)
    sb, nb = _seq_pos((_num_blocks(groups) - 1 - n) if rev else n, groups)

    @pl.when(sb == (nb - 1 if rev else 0))
    def _():
        st_ref[...] = jnp.zeros_like(st_ref)
        m_ref[...] = jnp.full_like(m_ref, M_INIT)

    def conv_silu(raw_ref, prev_ref, next_ref, c0):
        stg_ref[SUBLANES:SUBLANES + tb, :] = raw_ref[...]
        stg_ref[SUBLANES - 1:SUBLANES, :] = jnp.where(sb > 0, prev_ref[SUBLANES - 1:SUBLANES, :], 0.0)
        stg_ref[SUBLANES + tb:SUBLANES + tb + 1, :] = jnp.where(sb < nb - 1, next_ref[0:1, :], 0.0)
        w = cw_ref[:, c0:c0 + WIDTH]
        y = (stg_ref[SUBLANES - 1:SUBLANES - 1 + tb, :] * w[0:1, :]
             + stg_ref[SUBLANES:SUBLANES + tb, :] * w[1:2, :]
             + stg_ref[SUBLANES + 1:SUBLANES + 1 + tb, :] * w[2:3, :] + cbias_ref[:, c0:c0 + WIDTH])
        return _silu(y)

    if not rev:
        qs_ref[...] = conv_silu(q_ref, qp_ref, qn_ref, 0).astype(BF16)
        ks_ref[...] = (conv_silu(k_ref, kp_ref, kn_ref, WIDTH) * (HEAD_DIM ** -0.5)).astype(BF16)

    row = lax.broadcasted_iota(jnp.int32, (CHUNK, CHUNK), 0)
    col = lax.broadcasted_iota(jnp.int32, (CHUNK, CHUNK), 1)
    allowed = (row <= col) if rev else (row >= col)
    ones = jnp.ones((CHUNK, HEAD_DIM), BF16)

    def chunk(c, h):
        sl = pl.ds(pl.multiple_of(c * CHUNK, CHUNK), CHUNK)
        hs = slice(h * HEAD_DIM, (h + 1) * HEAD_DIM)
        q = qs_ref[sl, hs]
        kt = ks_ref[sl, hs].astype(F32).T
        v2 = jnp.concatenate([v_ref[sl, hs].astype(BF16), ones], axis=1)
        m_prev = m_ref[h, 0:1, :]
        mb = jnp.maximum(pb_ref[h, sl, :], m_prev)
        p = jnp.exp(jnp.where(allowed, gr_ref[h, 0:1, sl] - mb, -jnp.inf))
        w_inter = jnp.exp(m_prev - mb)
        floor = jnp.exp(-(cb_ref[h, sl, :] + mb))
        s = _dot(q, kt.astype(BF16)) * p
        inter = _dot(q, st_ref[h].astype(BF16))
        num2 = _dot(s.astype(BF16), v2) + jnp.concatenate([w_inter, w_inter], axis=1) * inter
        out = num2[:, 0:HEAD_DIM] / jnp.maximum(jnp.abs(num2[:, HEAD_DIM:]), floor)
        f_tot = gs_ref[h, c, 0:1, :]
        m_chunk = gs_ref[h, c, 1:2, :]
        m_new = jnp.maximum(f_tot + m_prev, m_chunk)
        kw = (kt * jnp.exp(gr_ref[h, 1:2, sl] - m_new)).astype(BF16)
        kv2 = _dot(kw, v2)
        a = jnp.exp(f_tot + m_prev - m_new)
        st_ref[h] = jnp.concatenate([a, a], axis=1) * st_ref[h] + kv2
        m_ref[h] = jnp.broadcast_to(m_new, (SUBLANES, LANES))
        return out, sl, hs

    def body(i, carry):
        c = (nch - 1 - i) if rev else i
        for h in range(HEADS):
            out, sl, hs = chunk(c, h)
            if rev:
                hh = jax.nn.sigmoid(o_ref[sl, hs]) * (out + hf_ref[sl, hs])
                y = _head_norm(hh, ng_ref[:, hs]) * _silu(z_ref[sl, hs])
                out_ref[sl, hs] = y.astype(out_ref.dtype)
            else:
                out_ref[sl, hs] = out
        return carry

    lax.fori_loop(0, nch, body, 0, unroll=True)


def _mlstm_dir(rev, u_ml, pb, cb, gr, gs, conv_w, conv_b, norm_g, fwd, groups):
    t = u_ml.shape[0]
    tb = TB_SEQ
    t8 = tb // SUBLANES
    nch = tb // CHUNK
    nblk = _num_blocks(groups)
    d = 1 if rev else 0

    def rb(n):
        return (nblk - 1 - n) if rev else n

    def col(g):
        return lambda n: (rb(n), g)

    def halo_prev(g):
        return lambda n: (jnp.maximum(rb(n) * t8 - 1, 0), g)

    def halo_next(g):
        return lambda n: (jnp.minimum((rb(n) + 1) * t8, t // SUBLANES - 1), g)

    blk = (tb, WIDTH)
    halo = (SUBLANES, WIDTH)
    gate_specs = [
        pl.BlockSpec((None, HEADS, tb, LANES), lambda n: (d, 0, rb(n), 0)),
        pl.BlockSpec((None, HEADS, tb, LANES), lambda n: (d, 0, rb(n), 0)),
        pl.BlockSpec((None, HEADS, SUBLANES, tb), lambda n: (d, 0, 0, rb(n))),
        pl.BlockSpec((None, HEADS, nch, SUBLANES, LANES), lambda n: (d, 0, rb(n), 0, 0)),
    ]
    scratch = [
        pltpu.VMEM((HEADS, HEAD_DIM, 2 * HEAD_DIM), F32),
        pltpu.VMEM((HEADS, SUBLANES, LANES), F32),
    ]
    full = jax.ShapeDtypeStruct((t, WIDTH), F32)
    if rev:
        qs, ks, h_fwd = fwd
        in_specs = [pl.BlockSpec(blk, col(0)), pl.BlockSpec(blk, col(0)), pl.BlockSpec(blk, col(2)),
                    *gate_specs,
                    pl.BlockSpec(blk, col(3)), pl.BlockSpec(blk, col(4)), pl.BlockSpec(blk, col(0)),
                    pl.BlockSpec((1, WIDTH), lambda n: (0, 0))]
        args = [qs, ks, u_ml, pb, cb, gr, gs, u_ml, u_ml, h_fwd, norm_g]
        out_specs = pl.BlockSpec(blk, col(0))
        out_shape = jax.ShapeDtypeStruct(full.shape, BF16)
    else:
        in_specs = [pl.BlockSpec(blk, col(0)), pl.BlockSpec(blk, col(1)), pl.BlockSpec(blk, col(2)),
                    pl.BlockSpec(halo, halo_prev(0)), pl.BlockSpec(halo, halo_next(0)),
                    pl.BlockSpec(halo, halo_prev(1)), pl.BlockSpec(halo, halo_next(1)),
                    *gate_specs,
                    pl.BlockSpec((3, 2 * WIDTH), lambda n: (0, 0)),
                    pl.BlockSpec((1, 2 * WIDTH), lambda n: (0, 0))]
        args = [u_ml, u_ml, u_ml, u_ml, u_ml, u_ml, u_ml, pb, cb, gr, gs, conv_w, conv_b]
        out_specs = [pl.BlockSpec(blk, col(0))] * 3
        out_shape = [full, jax.ShapeDtypeStruct((t, WIDTH), BF16), jax.ShapeDtypeStruct((t, WIDTH), BF16)]
        scratch.append(pltpu.VMEM((tb + 2 * SUBLANES, WIDTH), F32))
    return pl.pallas_call(
        functools.partial(_ml_kernel, rev, groups),
        grid=(nblk,),
        in_specs=in_specs,
        out_specs=out_specs,
        out_shape=out_shape,
        scratch_shapes=scratch,
        name="mlstm_bwd" if rev else "mlstm_fwd",
    )(*args)


def _mlstm(u_ml, pb, cb, gr, gs, conv_w, conv_b, norm_g, groups):
    h_fwd, qs, ks = _mlstm_dir(False, u_ml, pb, cb, gr, gs, conv_w, conv_b, norm_g, None, groups)
    return _mlstm_dir(True, u_ml, pb, cb, gr, gs, conv_w, conv_b, norm_g, (qs, ks, h_fwd), groups)


def _rt_kernel(rev, groups, *refs):
    if rev:
        (qs_ref, ks_ref, v_ref, dec_ref, qd_ref, kd_ref, z_ref, hf_ref, ng_ref,
         out_ref, st_ref) = refs
    else:
        (qs_ref, ks_ref, v_ref, dec_ref, qd_ref, kd_ref, out_ref, st_ref) = refs
    nch = TB_SEQ // CHUNK
    n = pl.program_id(0)
    sb, nb = _seq_pos((_num_blocks(groups) - 1 - n) if rev else n, groups)

    @pl.when(sb == (nb - 1 if rev else 0))
    def _():
        st_ref[...] = jnp.zeros_like(st_ref)

    def body(i, carry):
        c = (nch - 1 - i) if rev else i
        sl = pl.ds(pl.multiple_of(c * CHUNK, CHUNK), CHUNK)
        for h in range(HEADS):
            hs = slice(h * HEAD_DIM, (h + 1) * HEAD_DIM)
            q = qs_ref[sl, hs]
            kt = ks_ref[sl, hs].astype(F32).T
            vb = v_ref[sl, hs].astype(BF16)
            s = _dot(q, kt.astype(BF16)) * dec_ref[h]
            out = _dot(s.astype(BF16), vb) + _dot((q.astype(F32) * qd_ref[h]).astype(BF16),
                                                  st_ref[h].astype(BF16))
            st_ref[h] = kd_ref[h, 1:2, :] * st_ref[h] + _dot((kt * kd_ref[h, 0:1, :]).astype(BF16), vb)
            if rev:
                y = _head_norm(out + hf_ref[sl, hs], ng_ref[:, hs]) * _silu(z_ref[sl, hs])
                out_ref[sl, hs] = y.astype(out_ref.dtype)
            else:
                out_ref[sl, hs] = out
        return carry

    lax.fori_loop(0, nch, body, 0, unroll=2 if rev else True)


def _retention_dir(rev, u_qk, u_vz, dec, qdec, kdec, norm_g, h_fwd, groups):
    tb = TB_SEQ
    nblk = _num_blocks(groups)
    d = 1 if rev else 0

    def rb(n):
        return (nblk - 1 - n) if rev else n

    def col(g):
        return lambda n: (rb(n), g)

    blk = (tb, WIDTH)
    tbl = lambda n: (d, 0, 0, 0)
    in_specs = [
        pl.BlockSpec(blk, col(0)), pl.BlockSpec(blk, col(1)), pl.BlockSpec(blk, col(0)),
        pl.BlockSpec((None, HEADS, CHUNK, CHUNK), tbl),
        pl.BlockSpec((None, HEADS, CHUNK, HEAD_DIM), tbl),
        pl.BlockSpec((None, HEADS, SUBLANES, CHUNK), tbl),
    ]
    args = [u_qk, u_qk, u_vz, dec, qdec, kdec]
    if rev:
        in_specs += [pl.BlockSpec(blk, col(1)), pl.BlockSpec(blk, col(0)),
                     pl.BlockSpec((1, WIDTH), lambda n: (0, 0))]
        args += [u_vz, h_fwd, norm_g]
    out_shape = jax.ShapeDtypeStruct((u_qk.shape[0], WIDTH), BF16 if rev else F32)
    out_specs = pl.BlockSpec(blk, col(0))
    return pl.pallas_call(
        functools.partial(_rt_kernel, rev, groups),
        grid=(nblk,),
        in_specs=in_specs,
        out_specs=out_specs,
        out_shape=out_shape,
        scratch_shapes=[pltpu.VMEM((HEADS, HEAD_DIM, HEAD_DIM), F32)],
        name="retention_bwd" if rev else "retention_fwd",
    )(*args)


def _retention(u_qk, u_vz, tabs, norm_g, groups):
    dec, qdec, kdec = tabs
    h_fwd = _retention_dir(False, u_qk, u_vz, dec, qdec, kdec, norm_g, None, groups)
    return _retention_dir(True, u_qk, u_vz, dec, qdec, kdec, norm_g, h_fwd, groups)


def _rotary_tables(seq):
    inv = ROPE_BASE ** (-jnp.arange(0, HEAD_DIM, 2, dtype=F32) / HEAD_DIM)
    ang = jnp.arange(seq, dtype=F32)[:, None] * inv[None, :]
    cos, sin = jnp.cos(ang), jnp.sin(ang)
    return jnp.concatenate([cos, cos], axis=1), jnp.concatenate([-sin, sin], axis=1)


def _retention_tables():
    pos = np.arange(CHUNK, dtype=np.float64)
    rel = pos[:, None] - pos[None, :]
    dec = np.zeros((2, HEADS, CHUNK, CHUNK))
    qdec = np.zeros((2, HEADS, CHUNK, HEAD_DIM))
    kdec = np.zeros((2, HEADS, SUBLANES, CHUNK))
    for h in range(HEADS):
        lf, lb = RT_LOG_GAMMA_FWD[h], RT_LOG_GAMMA_BWD[h]
        dec[0, h] = np.where(rel >= 0, np.exp(np.maximum(rel, 0.0) * lf), 0.0)
        dec[1, h] = np.where(rel <= 0, np.exp(np.maximum(-rel, 0.0) * lb), 0.0)
        qdec[0, h] = np.exp((pos + 1.0) * lf)[:, None]
        qdec[1, h] = np.exp((CHUNK - pos) * lb)[:, None]
        kdec[0, h, 0] = np.exp((CHUNK - 1.0 - pos) * lf)
        kdec[1, h, 0] = np.exp(pos * lb)
        kdec[0, h, 1] = np.exp(CHUNK * lf)
        kdec[1, h, 1] = np.exp(CHUNK * lb)
    return jnp.asarray(dec, F32), jnp.asarray(qdec, F32), jnp.asarray(kdec, F32)


def _filt_kernel(ft_ref, fr_ref, f0_ref, w1_ref, b1_ref, w2_ref, b2_ref, fq_ref,
                 w3f_ref, w3b_ref, dlf_ref, dlb_ref, gb_ref, gf_ref, s_ref, hf3_ref, hr3_ref):
    lt = pl.program_id(1)
    fq = fq_ref[...]

    def hidden(ft):
        h1 = jnp.sin(fq * (_dot_f32(w1_ref[...], ft) + b1_ref[...]))
        return jnp.sin(fq * (_dot_f32(w2_ref[...], h1) + b2_ref[...]))

    @pl.when((pl.program_id(2) == 0) & (pl.program_id(3) == 0))
    def _():
        hf3_ref[...] = _split3_rows(hidden(ft_ref[...]))
        hr3_ref[...] = _split3_rows(hidden(fr_ref[...]))

    def lhs3(w):
        hi, lo = _split_hi_lo(w)
        return jnp.concatenate([hi, hi, lo], axis=1)

    ft = ft_ref[...]
    fr = fr_ref[...]
    dl_f = jnp.abs(dlf_ref[...])
    dl_b = jnp.abs(dlb_ref[...])
    fwd = _dot(lhs3(w3f_ref[...]), hf3_ref[...]) * jnp.exp(-ft[0:1, :] * dl_f)
    bwd = _dot(lhs3(w3b_ref[...]), hr3_ref[...]) * jnp.exp(-fr[0:1, :] * dl_b)
    f0 = f0_ref[...]
    bwd0 = (_dot_f32(w3b_ref[...], hidden(f0)) * jnp.exp(-f0[0:1, :] * dl_b))[:, 0:1]
    lag = lax.broadcasted_iota(jnp.int32, fwd.shape, 1) + lt * LT_FILT
    fwd = jnp.where(lag == 0, fwd + bwd0, fwd)
    bwd = jnp.where(lag == 0, 0.0, bwd)
    gb_ref[...] = bwd.reshape(gb_ref.shape)
    gf_ref[...] = fwd.reshape(gf_ref.shape)
    part = jnp.sum(jnp.abs(fwd) + jnp.abs(bwd), axis=1, keepdims=True)
    lane = lax.broadcasted_iota(jnp.int32, (LANES, LANES), 1)
    s_ref[...] = jnp.where(lane == 0, part, 0.0)


def _filters(seq, w1t, b1, w2t, b2, freq, w3t, deltas):
    t = jnp.linspace(0.0, 1.0, seq, dtype=F32)[None, :]
    w = (2.0 * math.pi / seq) * jnp.arange(seq, dtype=F32)[None, :]
    bands = jnp.linspace(1e-4, HY_BANDS - 1, HY_BANDS, dtype=F32)[:, None]
    feats = jnp.concatenate([t, jnp.cos(bands * w), -jnp.sin(bands * w),
                             jnp.zeros((FEAT_ROWS - 1 - 2 * HY_BANDS, seq), F32)], axis=0)
    feats_rev = jnp.concatenate([feats[:, :1], feats[:, :0:-1]], axis=1)
    feats0 = jnp.broadcast_to(feats[:, :1], (FEAT_ROWS, LANES))
    nlt = seq // LT_FILT
    ct = WIDTH // LANES
    lyr = lambda l, i, o, c: (l, 0, 0)
    return pl.pallas_call(
        _filt_kernel,
        grid=(DEPTH, nlt, HY_ORDER, ct),
        in_specs=[
            pl.BlockSpec((FEAT_ROWS, LT_FILT), lambda l, i, o, c: (0, i)),
            pl.BlockSpec((FEAT_ROWS, LT_FILT), lambda l, i, o, c: (0, i)),
            pl.BlockSpec((FEAT_ROWS, LANES), lambda l, i, o, c: (0, 0)),
            pl.BlockSpec((None, HY_HIDDEN, FEAT_ROWS), lyr),
            pl.BlockSpec((None, HY_HIDDEN, 1), lyr),
            pl.BlockSpec((None, HY_HIDDEN, HY_HIDDEN), lyr),
            pl.BlockSpec((None, HY_HIDDEN, 1), lyr),
            pl.BlockSpec((None, HY_HIDDEN, 1), lyr),
            pl.BlockSpec((None, LANES, HY_HIDDEN), lambda l, i, o, c: (l, o * 2 * ct + c, 0)),
            pl.BlockSpec((None, LANES, HY_HIDDEN), lambda l, i, o, c: (l, o * 2 * ct + ct + c, 0)),
            pl.BlockSpec((None, LANES, 1), lambda l, i, o, c: (l, o * 2 * ct + c, 0)),
            pl.BlockSpec((None, LANES, 1), lambda l, i, o, c: (l, o * 2 * ct + ct + c, 0)),
        ],
        out_specs=[
            pl.BlockSpec((None, None, LANES, LT_FILT // LANES, LANES), lambda l, i, o, c: (l, o, c, i, 0)),
            pl.BlockSpec((None, None, LANES, LT_FILT // LANES, LANES), lambda l, i, o, c: (l, o, c, i, 0)),
            pl.BlockSpec((None, None, LANES, LANES), lambda l, i, o, c: (l, o, c, i)),
        ],
        out_shape=[
            jax.ShapeDtypeStruct((DEPTH, HY_ORDER, WIDTH, seq // LANES, LANES), F32),
            jax.ShapeDtypeStruct((DEPTH, HY_ORDER, WIDTH, seq // LANES, LANES), F32),
            jax.ShapeDtypeStruct((DEPTH, HY_ORDER, WIDTH, nlt * LANES), F32),
        ],
        scratch_shapes=[pltpu.VMEM((3 * HY_HIDDEN, LT_FILT), BF16),
                        pltpu.VMEM((3 * HY_HIDDEN, LT_FILT), BF16)],
        name="hyena_filters",
    )(feats, feats_rev, feats0, w1t, b1, w2t, b2, freq, w3t, w3t, deltas, deltas)


def _split_hi_lo(x):
    hi = x.astype(BF16)
    lo = (x - hi.astype(F32)).astype(BF16)
    return hi, lo


def _split3_rows(x):
    hi, lo = _split_hi_lo(x)
    return jnp.concatenate([hi, lo, hi], axis=0)


def _split3_cols(x):
    hi, lo = _split_hi_lo(x)
    return jnp.concatenate([hi, lo, hi], axis=1)


def _np_split(m):
    hi = m.astype(ml_dtypes.bfloat16)
    lo = (m - hi.astype(np.float64)).astype(ml_dtypes.bfloat16)
    return hi, lo


def _dft_tables(seq):
    nj = seq // LANES
    n1 = 2 * nj
    n = n1 * LANES
    khp = n1 // 2 + KH_PAD
    k1 = np.arange(khp, dtype=np.float64)[:, None]
    valid = (k1 <= n1 // 2).astype(np.float64)
    ang = 2.0 * np.pi * k1 * np.arange(n1, dtype=np.float64)[None, :] / n1
    fa_full = np.concatenate([valid * np.cos(ang), -valid * np.sin(ang)], axis=0)

    def lhs3(m):
        hi, lo = _np_split(m)
        return jnp.asarray(np.concatenate([hi, hi, lo], axis=1))

    def rhs3(m):
        hi, lo = _np_split(m)
        return jnp.asarray(np.concatenate([hi, hi, lo], axis=0))

    n2 = np.arange(LANES, dtype=np.float64)
    ang2 = 2.0 * np.pi * n2[:, None] * n2[None, :] / LANES
    c2, s2 = np.cos(ang2), np.sin(ang2)
    fb = np.block([[c2, -s2], [s2, c2]])
    fbi = np.block([[c2, s2], [-s2, c2]])
    w = np.where((k1 == 0) | (k1 == n1 // 2), 1.0, 2.0) * valid
    th = 2.0 * np.pi * np.arange(nj, dtype=np.float64)[:, None] * k1.T / n1
    fai = np.concatenate([(w.T / n) * np.cos(th), -(w.T / n) * np.sin(th)], axis=1)
    phi = 2.0 * np.pi * k1 * n2[None, :] / n
    return dict(
        nj=nj, n1=n1, khp=khp,
        fa=lhs3(fa_full[:, :nj]), fa_full=lhs3(fa_full), fb=rhs3(fb), fbi=rhs3(fbi), fai=lhs3(fai),
        twc=jnp.asarray(valid * np.cos(phi), F32), tws=jnp.asarray(valid * np.sin(phi), F32))


def _dft_rows(fa_ref, twc, tws, khp, cols):
    rhs = cols[0] if len(cols) == 1 else jnp.concatenate(cols, axis=1)
    y = _dot(fa_ref[...], _split3_rows(rhs))
    out = []
    for i in range(len(cols)):
        yre = y[0:khp, i * LANES:(i + 1) * LANES]
        yim = y[khp:2 * khp, i * LANES:(i + 1) * LANES]
        out.append((yre * twc + yim * tws, yim * twc - yre * tws))
    return out


def _spec_kernel(khp, gf_ref, gb_ref, s_ref, fa_ref, fb_ref, twc_ref, tws_ref, h_ref, ybuf):
    twc = twc_ref[...]
    tws = tws_ref[...]

    def circular(c):
        return jnp.concatenate([gf_ref[c], gb_ref[c]], axis=0)

    for c in range(0, CB_SPEC, 2):
        res = _dft_rows(fa_ref, twc, tws, khp, [circular(c), circular(c + 1)])
        for i, (yre, yim) in enumerate(res):
            r0 = (c + i) * khp
            ybuf[r0:r0 + khp, 0:LANES] = yre
            ybuf[r0:r0 + khp, LANES:2 * LANES] = yim
    x = _dot(_split3_cols(ybuf[...]), fb_ref[...])
    for c in range(CB_SPEC):
        norm = jnp.sum(s_ref[c:c + 1, :], axis=1, keepdims=True)
        h_ref[c] = x[c * khp:(c + 1) * khp, :] / norm


def _spectrum(gf, gb, s, tab):
    nch = gf.shape[0]
    nj, khp = tab["nj"], tab["khp"]
    const = lambda a: pl.BlockSpec(a.shape, lambda i: (0, 0))
    return pl.pallas_call(
        functools.partial(_spec_kernel, khp),
        grid=(nch // CB_SPEC,),
        in_specs=[
            pl.BlockSpec((CB_SPEC, nj, LANES), lambda i: (i, 0, 0)),
            pl.BlockSpec((CB_SPEC, nj, LANES), lambda i: (i, 0, 0)),
            pl.BlockSpec((CB_SPEC, s.shape[-1]), lambda i: (i, 0)),
            const(tab["fa_full"]), const(tab["fb"]), const(tab["twc"]), const(tab["tws"]),
        ],
        out_specs=pl.BlockSpec((CB_SPEC, khp, 2 * LANES), lambda i: (i, 0, 0)),
        out_shape=jax.ShapeDtypeStruct((nch, khp, 2 * LANES), F32),
        scratch_shapes=[pltpu.VMEM((CB_SPEC * khp, 2 * LANES), F32)],
        name="hyena_spectrum",
    )(gf, gb, s, tab["fa_full"], tab["fb"], tab["twc"], tab["tws"])


def _hyfft_kernel(bsz, nj, khp, prm_ref, v_ref, x1_ref, x2_ref, z_ref, h_ref,
                  fa_ref, fb_ref, fbi_ref, fai_ref, twc_ref, tws_ref, out_ref,
                  xs_ref, ybuf, pbuf):
    ct = pl.program_id(0)
    rows = bsz * nj
    pairs = [(b, b + 1) for b in range(0, bsz, 2)] if bsz > 1 else [(0,)]
    twc = twc_ref[...]
    tws = tws_ref[...]

    lane = lax.broadcasted_iota(jnp.int32, (rows, LANES), 1)
    blk = lax.broadcasted_iota(jnp.int32, (rows, LANES), 0) & (nj - 1)
    first = (lane == 0) & (blk == 0)
    last = (lane == LANES - 1) & (blk == nj - 1)

    def conv3(x, base):
        r = pltpu.roll(x, 1, 1)
        xm1 = jnp.where(lane == 0, pltpu.roll(r, 1, 0), r)
        xm1 = jnp.where(first, 0.0, xm1)
        l = pltpu.roll(x, LANES - 1, 1)
        xp1 = jnp.where(lane == LANES - 1, pltpu.roll(l, rows - 1, 0), l)
        xp1 = jnp.where(last, 0.0, xp1)
        return prm_ref[base] * xm1 + prm_ref[base + 1] * x + prm_ref[base + 2] * xp1 + prm_ref[base + 3]

    def row0(c, b):
        return (c * bsz + b) * khp

    def forward(c, x):
        for pr in pairs:
            res = _dft_rows(fa_ref, twc, tws, khp, [x[b * nj:(b + 1) * nj, :] for b in pr])
            for b, (yre, yim) in zip(pr, res):
                r0 = row0(c, b)
                ybuf[r0:r0 + khp, 0:LANES] = yre
                ybuf[r0:r0 + khp, LANES:2 * LANES] = yim

    def spectral(order):
        for c0 in range(0, CB_HY, CG_HY):
            g0, g1 = row0(c0, 0), row0(c0 + CG_HY, 0)
            x = _dot(_split3_cols(ybuf[g0:g1, :]), fb_ref[...])
            for c in range(c0, c0 + CG_HY):
                hre = h_ref[order, c, :, 0:LANES]
                him = h_ref[order, c, :, LANES:2 * LANES]
                for b in range(bsz):
                    r0 = row0(c, b)
                    xre = x[r0 - g0:r0 - g0 + khp, 0:LANES]
                    xim = x[r0 - g0:r0 - g0 + khp, LANES:2 * LANES]
                    pbuf[r0:r0 + khp, 0:LANES] = xre * hre - xim * him
                    pbuf[r0:r0 + khp, LANES:2 * LANES] = xre * him + xim * hre
            r = _dot(_split3_cols(pbuf[g0:g1, :]), fbi_ref[...])
            for c in range(c0, c0 + CG_HY):
                for b in range(bsz):
                    r0 = row0(c, b)
                    rre = r[r0 - g0:r0 - g0 + khp, 0:LANES]
                    rim = r[r0 - g0:r0 - g0 + khp, LANES:2 * LANES]
                    ybuf[r0:r0 + khp, 0:LANES] = rre * twc - rim * tws
                    ybuf[r0:r0 + khp, LANES:2 * LANES] = rre * tws + rim * twc

    def inverse(c):
        outs = [None] * bsz
        for pr in pairs:
            rre = [ybuf[row0(c, b):row0(c, b) + khp, 0:LANES] for b in pr]
            rim = [ybuf[row0(c, b):row0(c, b) + khp, LANES:2 * LANES] for b in pr]
            if len(pr) > 1:
                rre, rim = [jnp.concatenate(rre, axis=1)], [jnp.concatenate(rim, axis=1)]
            y = _dot(fai_ref[...], _split3_rows(jnp.concatenate([rre[0], rim[0]], axis=0)))
            for i, b in enumerate(pr):
                outs[b] = y[:, i * LANES:(i + 1) * LANES]
        return outs[0] if bsz == 1 else jnp.concatenate(outs, axis=0)

    for c in range(CB_HY):
        base = (ct * CB_HY + c) * HY_PRM
        v = conv3(v_ref[c], base)
        xs_ref[0, c] = v
        xs_ref[1, c] = conv3(x1_ref[c], base + 4)
        xs_ref[2, c] = conv3(x2_ref[c], base + 8)
        forward(c, v)
    spectral(0)
    for c in range(CB_HY):
        base = (ct * CB_HY + c) * HY_PRM
        v = xs_ref[0, c]
        z1 = xs_ref[1, c] * (inverse(c) + prm_ref[base + 12] * v)
        xs_ref[0, c] = z1
        forward(c, z1)
    spectral(1)
    for c in range(CB_HY):
        base = (ct * CB_HY + c) * HY_PRM
        z1 = xs_ref[0, c]
        y2 = inverse(c) + prm_ref[base + 13] * z1
        out_ref[:, c, :] = xs_ref[2, c] * y2 * _silu(z_ref[c])


def _hyena_fft(prm, u_t, h, layer, tab, row0, bsz, seq):
    nj, khp = tab["nj"], tab["khp"]
    nct = WIDTH // CB_HY
    rows = bsz * nj
    blk = (CB_HY, rows, LANES)
    rblk = row0 // (rows * LANES)
    assert rblk * rows * LANES == row0
    const = lambda a: pl.BlockSpec(a.shape, lambda i: (0, 0))
    return pl.pallas_call(
        functools.partial(_hyfft_kernel, bsz, nj, khp),
        grid=(nct,),
        in_specs=[
            pl.BlockSpec(memory_space=pltpu.SMEM),
            pl.BlockSpec(blk, lambda i: (i, rblk, 0)),
            pl.BlockSpec(blk, lambda i: (nct + i, rblk, 0)),
            pl.BlockSpec(blk, lambda i: (2 * nct + i, rblk, 0)),
            pl.BlockSpec(blk, lambda i: (3 * nct + i, rblk, 0)),
            pl.BlockSpec((None, HY_ORDER, CB_HY, khp, 2 * LANES), lambda i: (layer, 0, i, 0, 0)),
            const(tab["fa"]), const(tab["fb"]), const(tab["fbi"]), const(tab["fai"]),
            const(tab["twc"]), const(tab["tws"]),
        ],
        out_specs=pl.BlockSpec((rows, CB_HY, LANES), lambda i: (0, i, 0)),
        out_shape=jax.ShapeDtypeStruct((rows, WIDTH, LANES), F32),
        scratch_shapes=[
            pltpu.VMEM((3, CB_HY, bsz * nj, LANES), F32),
            pltpu.VMEM((CB_HY * bsz * khp, 2 * LANES), F32),
            pltpu.VMEM((CB_HY * bsz * khp, 2 * LANES), F32),
        ],
        name="hyena_fft",
    )(prm, u_t, u_t, u_t, u_t, h, tab["fa"], tab["fb"], tab["fbi"], tab["fai"],
      tab["twc"], tab["tws"])


def _out_kernel(final, starts, nx, *refs):
    ng = len(starts)
    x_refs = refs[0:nx]
    yml_ref, yrt_ref = refs[nx:nx + 2]
    yhy_refs = refs[nx + 2:nx + 2 + ng]
    w_ref, g_ref = refs[nx + 2 + ng:nx + 4 + ng]
    out_refs = refs[nx + 4 + ng:-1]
    lhs_ref = refs[-1]
    i = pl.program_id(0)
    lhs_ref[:, 0:WIDTH] = yml_ref[...]
    lhs_ref[:, WIDTH:2 * WIDTH] = yrt_ref[...]
    jb = TM_OUT // LANES

    def fill_hyena(ref):
        for j in range(jb):
            lhs_ref[j * LANES:(j + 1) * LANES, 2 * WIDTH:] = ref[j].T.astype(BF16)

    for gi in range(ng):
        inside = i >= starts[gi]
        if gi + 1 < ng:
            inside = inside & (i < starts[gi + 1])
        pl.when(inside)(functools.partial(fill_hyena, yhy_refs[gi]))
    x = _x_tile(x_refs, starts if nx > 1 else [0]) + _dot(lhs_ref[...], w_ref[...])
    if not final:
        out_refs[0][...] = x
        return
    ms = jnp.mean(x * x, axis=-1, keepdims=True)
    x = x * lax.rsqrt(ms + RMS_EPS) * g_ref[...]
    for gi in range(ng):
        inside = i >= starts[gi]
        if gi + 1 < ng:
            inside = inside & (i < starts[gi + 1])

        @pl.when(inside)
        def _(gi=gi):
            out_refs[gi][...] = x


def _out_proj(xs, y_ml, y_rt, y_hy, w, gain, final, groups):
    t = y_ml.shape[0]
    jb = TM_OUT // LANES
    starts = [r0 // TM_OUT for (r0, _, _) in groups]
    counts = [bsz * seq // TM_OUT for (_, bsz, seq) in groups]

    def tile(gi):
        return lambda i: jnp.clip(i - starts[gi], 0, counts[gi] - 1)

    hy_specs = [pl.BlockSpec((jb, WIDTH, LANES), lambda i, f=tile(gi): (f(i), 0, 0))
                for gi in range(len(groups))]
    if final:
        out_specs = [pl.BlockSpec((TM_OUT, D_MODEL), lambda i, f=tile(gi): (f(i), 0))
                     for gi in range(len(groups))]
        out_shape = [jax.ShapeDtypeStruct((bsz * seq, D_MODEL), F32) for (_, bsz, seq) in groups]
    else:
        out_specs = [pl.BlockSpec((TM_OUT, D_MODEL), lambda i: (i, 0))]
        out_shape = [jax.ShapeDtypeStruct((t, D_MODEL), F32)]
    return pl.pallas_call(
        functools.partial(_out_kernel, final, starts, len(xs)),
        grid=(t // TM_OUT,),
        in_specs=[
            *_x_specs(xs, TM_OUT, 1),
            pl.BlockSpec((TM_OUT, WIDTH), lambda i: (i, 0)),
            pl.BlockSpec((TM_OUT, WIDTH), lambda i: (i, 0)),
            *hy_specs,
            pl.BlockSpec((3 * WIDTH, D_MODEL), lambda i: (0, 0)),
            pl.BlockSpec((1, D_MODEL), lambda i: (0, 0)),
        ],
        out_specs=out_specs,
        out_shape=out_shape,
        scratch_shapes=[pltpu.VMEM((TM_OUT, 3 * WIDTH), BF16)],
        name="out_proj",
    )(*xs, y_ml, y_rt, *y_hy, w, gain)


def kernel(x_prompt, x_sample, norm_g, w_in, ml_conv_w, ml_conv_b, ml_gate_b, ml_norm_g, rt_norm_g, hy_conv_w, hy_conv_b, hy_w1, hy_b1, hy_w2, hy_b2, hy_w3, hy_freq, hy_deltas, hy_skip, w_out, final_g):
    groups = []
    row0 = 0
    for xg in (x_prompt, x_sample):
        bsz, seq, _ = xg.shape
        groups.append((row0, bsz, seq))
        row0 += bsz * seq
    xs = [x_prompt.reshape(-1, D_MODEL), x_sample.reshape(-1, D_MODEL)]

    c_gate = ML_COLS
    c_rt = c_gate + 4 * HEADS
    c_hy = c_rt + RT_COLS
    gate_pad = jnp.zeros((DEPTH, D_MODEL, LANES - 4 * HEADS), F32)
    w_tok = jnp.concatenate([w_in[:, :, :c_gate], w_in[:, :, c_gate:c_rt], gate_pad,
                             w_in[:, :, c_rt:c_hy]], axis=2).astype(BF16)
    w_gt = jnp.swapaxes(w_in[:, :, c_gate:c_rt], 1, 2).astype(BF16)
    w_hyt = jnp.swapaxes(w_in[:, :, c_hy:], 1, 2).astype(BF16)
    w_out_b = w_out.astype(BF16)
    gate_b = ml_gate_b.reshape(DEPTH, 1, 4 * HEADS)
    gate_b_row = jnp.concatenate([gate_b, jnp.zeros((DEPTH, 1, LANES - 4 * HEADS), F32)], axis=2)
    gate_b_col = ml_gate_b.reshape(DEPTH, 4 * HEADS, 1)

    cw = hy_conv_w.reshape(DEPTH, 3, 3, WIDTH)
    cbias = hy_conv_b.reshape(DEPTH, 1, 3, WIDTH)
    taps = jnp.concatenate([cw, cbias], axis=1)
    taps = jnp.transpose(taps, (0, 3, 2, 1)).reshape(DEPTH, WIDTH, 12)
    hy_prm = jnp.concatenate([taps, jnp.transpose(hy_skip, (0, 2, 1)),
                              jnp.zeros((DEPTH, WIDTH, HY_PRM - 14), F32)],
                             axis=2).reshape(DEPTH, WIDTH * HY_PRM)

    w1t = jnp.concatenate([jnp.swapaxes(hy_w1, 1, 2),
                           jnp.zeros((DEPTH, HY_HIDDEN, FEAT_ROWS - hy_w1.shape[1]), F32)], axis=2)
    w2t = jnp.swapaxes(hy_w2, 1, 2)
    w3t = jnp.swapaxes(hy_w3, 1, 2)
    b1 = hy_b1[:, :, None]
    b2 = hy_b2[:, :, None]
    freq = hy_freq[:, :, None]
    deltas = hy_deltas.reshape(DEPTH, HY_ORDER * 2 * WIDTH, 1)

    tables, spectra = [], []
    for (_, _, seq) in groups:
        tab = _dft_tables(seq)
        gb, gf, s = _filters(seq, w1t, b1, w2t, b2, freq, w3t, deltas)
        nfil = DEPTH * HY_ORDER * WIDTH
        h = _spectrum(gf.reshape(nfil, tab["nj"], LANES), gb.reshape(nfil, tab["nj"], LANES),
                      s.reshape(nfil, -1), tab)
        tables.append(tab)
        spectra.append(h.reshape(DEPTH, HY_ORDER, WIDTH, tab["khp"], 2 * LANES))
    rot_c, rot_s = _rotary_tables(max(seq for (_, _, seq) in groups))
    rt_tabs = _retention_tables()

    for layer in range(DEPTH):
        gain = norm_g[layer][None, :]
        u_ml, u_g, u_qk, u_vz, u_gt = _in_proj(xs, gain, w_tok[layer], w_gt[layer], rot_c, rot_s, groups)
        u_hyt = _in_proj_hy(xs, gain, w_hyt[layer])
        pb, cb, gr, gs = _gates(u_g, u_gt, gate_b_row[layer], gate_b_col[layer])
        y_ml = _mlstm(u_ml, pb, cb, gr, gs, ml_conv_w[layer], ml_conv_b[layer][None, :],
                      ml_norm_g[layer][None, :], groups)
        y_rt = _retention(u_qk, u_vz, rt_tabs, rt_norm_g[layer][None, :], groups)
        y_hy = [_hyena_fft(hy_prm[layer], u_hyt, spectra[gi], layer, tables[gi], r0, bsz, seq)
                for gi, (r0, bsz, seq) in enumerate(groups)]
        outs = _out_proj(xs, y_ml, y_rt, y_hy, w_out_b[layer], final_g[None, :],
                         layer == DEPTH - 1, groups)
        xs = [outs[0]]

    return outs[0].reshape(x_prompt.shape), outs[1].reshape(x_sample.shape)
```
